```python
import jax, jax.numpy as jnp
from jax import lax
import numpy as np


D_MODEL = 1024
BATCH = 8
SEQ = 8192
DEPTH = 1
DEC_BATCH = 8
DEC_SEQ = 16
PAST_LEN = 4096

CHUNK = 64
Q_BLOCK = 128
N_HEADS = 8
QK_NOPE = 64
QK_ROPE = 32
V_HEAD = 64
Q_LORA = 256
KV_LORA = 128
ROPE_THETA = 10000.0
MLA_SCALE = (QK_NOPE + QK_ROPE) ** -0.5
CONV_WIDTH = 512
CONV_K = 3
MIX_WIDTH = N_HEADS * V_HEAD + CONV_WIDTH
IN_WIDTH = Q_LORA + KV_LORA + QK_ROPE + 3 * CONV_WIDTH
SPLITS = [Q_LORA, Q_LORA + KV_LORA, Q_LORA + KV_LORA + QK_ROPE,
          Q_LORA + KV_LORA + QK_ROPE + CONV_WIDTH, Q_LORA + KV_LORA + QK_ROPE + 2 * CONV_WIDTH]
N_MEM = 256
X_HEADS = 4
X_HEAD_DIM = D_MODEL // X_HEADS
X_SCALE = X_HEAD_DIM ** -0.5
N_GROUPS = 4
EXPERTS_PER_GROUP = 8
N_EXPERTS = N_GROUPS * EXPERTS_PER_GROUP
TOP_K = 2
D_EXPERT = 512
MOE_BLOCK = 128
ALPHA = (2 * DEPTH) ** 0.25
BETA = (8 * DEPTH) ** -0.25
LN_EPS = 1e-5
RMS_EPS = 1e-6

kernel_name = "hybrid_mla_shortconv_hiermoe_stream_step"


def layer_norm(x, g, b):
    xf = x.astype(jnp.float32)
    mu = xf.mean(-1, keepdims=True)
    var = jnp.square(xf - mu).mean(-1, keepdims=True)
    return ((xf - mu) * lax.rsqrt(var + LN_EPS) * g.astype(jnp.float32) + b.astype(jnp.float32)).astype(x.dtype)


def rms_norm(x, g):
    xf = x.astype(jnp.float32)
    return (xf * lax.rsqrt(jnp.mean(xf * xf, -1, keepdims=True) + RMS_EPS) * g.astype(jnp.float32)).astype(x.dtype)


def rope(x, pos):
    half = QK_ROPE // 2
    inv = ROPE_THETA ** (-jnp.arange(half, dtype=jnp.float32) / half)
    ang = pos.astype(jnp.float32)[:, None] * inv[None, :]
    ang = ang.reshape(ang.shape[0], *([1] * (x.ndim - 3)), half)
    cos, sin = jnp.cos(ang), jnp.sin(ang)
    xf = x.astype(jnp.float32)
    x1, x2 = xf[..., :half], xf[..., half:]
    return jnp.concatenate([x1 * cos - x2 * sin, x1 * sin + x2 * cos], -1).astype(x.dtype)


def mla_scores(qn, qr, kn, kr):
    s = jnp.einsum('bqhd,bkhd->bhqk', qn, kn) + jnp.einsum('bqhr,bkr->bhqk', qr, kr)
    return s.astype(jnp.float32) * MLA_SCALE


def mla_attend_prompt(qn, qr, kn, kr, v):
    B, S = qn.shape[:2]
    key_chunk = jnp.arange(S) // CHUNK

    def block(i):
        s0 = i * Q_BLOCK
        qn_b = lax.dynamic_slice_in_dim(qn, s0, Q_BLOCK, axis=1)
        qr_b = lax.dynamic_slice_in_dim(qr, s0, Q_BLOCK, axis=1)
        sc = mla_scores(qn_b, qr_b, kn, kr)
        q_chunk = (s0 + jnp.arange(Q_BLOCK)) // CHUNK
        sc = jnp.where(key_chunk[None, :] <= q_chunk[:, None], sc, -jnp.inf)
        p = jax.nn.softmax(sc, axis=-1).astype(v.dtype)
        return jnp.einsum('bhqk,bkhd->bqhd', p, v)

    out = lax.map(block, jnp.arange(S // Q_BLOCK))
    return out.transpose(1, 0, 2, 3, 4).reshape(B, S, N_HEADS * V_HEAD)


def mla_attend_all(qn, qr, kn, kr, v):
    B, S = qn.shape[:2]
    p = jax.nn.softmax(mla_scores(qn, qr, kn, kr), axis=-1).astype(v.dtype)
    return jnp.einsum('bhqk,bkhd->bqhd', p, v).reshape(B, S, N_HEADS * V_HEAD)


def token_mixer(x, pos, lat_past, kr_past, conv_past, lp):
    B, S, _ = x.shape
    proj = x @ lp['w_in']
    cq, ckv, kr, gb, gc, gv = jnp.split(proj, SPLITS, axis=-1)
    cq = rms_norm(cq, lp['q_norm_g'])
    ckv = rms_norm(ckv, lp['kv_norm_g'])
    kr = rope(kr, pos)
    q = (cq @ lp['w_uq']).reshape(B, S, N_HEADS, QK_NOPE + QK_ROPE)
    qn, qr = q[..., :QK_NOPE], rope(q[..., QK_NOPE:], pos)
    if lat_past is None:
        lat, kr_all = ckv, kr
    else:
        lat = jnp.concatenate([lat_past, ckv], axis=1)
        kr_all = jnp.concatenate([kr_past, kr], axis=1)
    kv = (lat @ lp['w_ukv']).reshape(B, lat.shape[1], N_HEADS, QK_NOPE + V_HEAD)
    kn, v = kv[..., :QK_NOPE], kv[..., QK_NOPE:]
    if lat_past is None:
        attn = mla_attend_prompt(qn, qr, kn, kr_all, v)
    else:
        attn = mla_attend_all(qn, qr, kn, kr_all, v)
    u = gc * gv
    if conv_past is None:
        u_all = jnp.pad(u, ((0, 0), (CONV_K - 1, 0), (0, 0)))
    else:
        u_all = jnp.concatenate([conv_past, u], axis=1)
    cw = lp['conv_w']
    conv = cw[0] * u_all[:, 0:S]
    for k in range(1, CONV_K):
        conv = conv + cw[k] * u_all[:, k:k + S]
    y_conv = gb * conv
    out = jnp.concatenate([attn, y_conv], axis=-1) @ lp['w_out']
    return out, ckv, kr, u_all[:, -(CONV_K - 1):]


def memory_kv(mem, w_xk, w_xv):
    B = mem.shape[0]
    mk = (mem @ w_xk).reshape(B, N_MEM, X_HEADS, X_HEAD_DIM)
    mv = (mem @ w_xv).reshape(B, N_MEM, X_HEADS, X_HEAD_DIM)
    return mk, mv


def cross_attend(x, mk, mv, w_xq, w_xo):
    B, S, _ = x.shape
    q = (x @ w_xq).reshape(B, S, X_HEADS, X_HEAD_DIM)
    sc = jnp.einsum('bqhd,bmhd->bhqm', q, mk).astype(jnp.float32) * X_SCALE
    p = jax.nn.softmax(sc, axis=-1).astype(mv.dtype)
    o = jnp.einsum('bhqm,bmhd->bqhd', p, mv).reshape(B, S, D_MODEL)
    return o @ w_xo


def hier_moe(x, lp):
    shp = x.shape
    xt = x.reshape(-1, shp[-1])
    T = xt.shape[0]
    TK = T * TOP_K
    logits_g = (xt @ lp['w_router_group']).astype(jnp.float32) + lp['b_router_group'].astype(jnp.float32)
    p_g = jax.nn.softmax(logits_g, axis=-1)
    g = jnp.argmax(logits_g, axis=-1)
    rows = jnp.arange(T)
    pg = p_g[rows, g]
    logits_e = (xt @ lp['w_router_expert']).astype(jnp.float32).reshape(T, N_GROUPS, EXPERTS_PER_GROUP)
    logits_e = logits_e + lp['b_router_expert'].astype(jnp.float32)
    le = logits_e[rows, g]
    top_v, top_i = lax.top_k(le, TOP_K)
    gate = (jax.nn.softmax(top_v, axis=-1) * pg[:, None]).reshape(-1)
    eid = (g[:, None] * EXPERTS_PER_GROUP + top_i).reshape(-1)
    tok = jnp.repeat(rows, TOP_K)
    order = jnp.argsort(eid, stable=True)
    se, stok, sgate = eid[order], tok[order], gate[order]
    counts = jnp.zeros((N_EXPERTS,), jnp.int32).at[eid].add(1)
    starts = jnp.cumsum(counts) - counts
    padded = (counts + MOE_BLOCK - 1) // MOE_BLOCK * MOE_BLOCK
    pends = jnp.cumsum(padded)
    pstarts = pends - padded
    dest = pstarts[se] + jnp.arange(TK) - starts[se]
    n_blocks = -(-TK // MOE_BLOCK) + N_EXPERTS
    P = n_blocks * MOE_BLOCK
    buf_tok = jnp.zeros((P,), jnp.int32).at[dest].set(stok)
    buf_gate = jnp.zeros((P,), jnp.float32).at[dest].set(sgate)
    block_e = jnp.minimum(jnp.searchsorted(pends, jnp.arange(n_blocks) * MOE_BLOCK, side='right'), N_EXPERTS - 1)
    w_g, w_u, w_d = lp['w_exp_gate'], lp['w_exp_up'], lp['w_exp_down']

    def expert_block(args):
        e, toks, wts = args
        xb = xt[toks]
        h = jax.nn.silu(xb @ w_g[e]) * (xb @ w_u[e])
        return (h @ w_d[e]) * wts[:, None].astype(xt.dtype)

    yb = lax.map(expert_block, (block_e, buf_tok.reshape(n_blocks, MOE_BLOCK), buf_gate.reshape(n_blocks, MOE_BLOCK)))
    out = jnp.zeros_like(xt).at[buf_tok].add(yb.reshape(P, shp[-1]))
    return out.reshape(shp)


def encoder_layer(x, pos, mk, mv, lat_past, kr_past, conv_past, lp):
    a, lat_new, kr_new, conv_new = token_mixer(x, pos, lat_past, kr_past, conv_past, lp)
    x = layer_norm(ALPHA * x + a, lp['ln1_g'], lp['ln1_b'])
    x = layer_norm(ALPHA * x + cross_attend(x, mk, mv, lp['w_xq'], lp['w_xo']), lp['ln2_g'], lp['ln2_b'])
    x = layer_norm(ALPHA * x + hier_moe(x, lp), lp['ln3_g'], lp['ln3_b'])
    return x, lat_new, kr_new, conv_new


def setup_inputs(seed: int = 0) -> dict:
    key = jax.random.key(seed)
    ks = iter(jax.random.split(key, 40))

    def nrm(shape, scale=1.0):
        return jax.random.normal(next(ks), shape, jnp.float32) * scale

    L = DEPTH
    D = D_MODEL
    return {
        'x_prompt': nrm((BATCH, SEQ, D)),
        'x_sample': nrm((DEC_BATCH, DEC_SEQ, D)),
        'cache_kv_latent': nrm((L, DEC_BATCH, PAST_LEN, KV_LORA)),
        'cache_k_rope': nrm((L, DEC_BATCH, PAST_LEN, QK_ROPE)),
        'cache_conv': nrm((L, DEC_BATCH, CONV_K - 1, CONV_WIDTH)),
        'cache_mem_k': nrm((L, DEC_BATCH, N_MEM, X_HEADS, X_HEAD_DIM)),
        'cache_mem_v': nrm((L, DEC_BATCH, N_MEM, X_HEADS, X_HEAD_DIM), BETA),
        'mem_prompt': nrm((BATCH, N_MEM, D)),
        'w_in': nrm((L, D, IN_WIDTH), D ** -0.5),
        'q_norm_g': 1.0 + nrm((L, Q_LORA), 0.02),
        'kv_norm_g': 1.0 + nrm((L, KV_LORA), 0.02),
        'w_uq': nrm((L, Q_LORA, N_HEADS * (QK_NOPE + QK_ROPE)), Q_LORA ** -0.5),
        'w_ukv': nrm((L, KV_LORA, N_HEADS * (QK_NOPE + V_HEAD)), KV_LORA ** -0.5),
        'conv_w': nrm((L, CONV_K, CONV_WIDTH), CONV_K ** -0.5),
        'w_out': nrm((L, MIX_WIDTH, D), MIX_WIDTH ** -0.5 * BETA),
        'ln1_g': 1.0 + nrm((L, D), 0.02),
        'ln1_b': nrm((L, D), 0.02),
        'w_xq': nrm((L, D, D), D ** -0.5),
        'w_xk': nrm((L, D, D), D ** -0.5),
        'w_xv': nrm((L, D, D), D ** -0.5 * BETA),
        'w_xo': nrm((L, D, D), D ** -0.5 * BETA),
        'ln2_g': 1.0 + nrm((L, D), 0.02),
        'ln2_b': nrm((L, D), 0.02),
        'w_router_group': nrm((L, D, N_GROUPS), D ** -0.5),
        'b_router_group': nrm((L, N_GROUPS), 0.01),
        'w_router_expert': nrm((L, D, N_GROUPS * EXPERTS_PER_GROUP), D ** -0.5),
        'b_router_expert': nrm((L, N_GROUPS, EXPERTS_PER_GROUP), 0.01),
        'w_exp_gate': nrm((L, N_EXPERTS, D, D_EXPERT), D ** -0.5),
        'w_exp_up': nrm((L, N_EXPERTS, D, D_EXPERT), D ** -0.5),
        'w_exp_down': nrm((L, N_EXPERTS, D_EXPERT, D), D_EXPERT ** -0.5 * BETA),
        'ln3_g': 1.0 + nrm((L, D), 0.02),
        'ln3_b': nrm((L, D), 0.02),
    }


def reference(x_prompt, x_sample, cache_kv_latent, cache_k_rope, cache_conv, cache_mem_k, cache_mem_v,
              mem_prompt, w_in, q_norm_g, kv_norm_g, w_uq, w_ukv, conv_w, w_out, ln1_g, ln1_b,
              w_xq, w_xk, w_xv, w_xo, ln2_g, ln2_b, w_router_group, b_router_group,
              w_router_expert, b_router_expert, w_exp_gate, w_exp_up, w_exp_down, ln3_g, ln3_b):
    pos_p = jnp.arange(x_prompt.shape[1])
    pos_s = cache_kv_latent.shape[2] + jnp.arange(x_sample.shape[1])
    xp, xs = x_prompt, x_sample
    lat_p, kr_p, conv_p, mk_p, mv_p = [], [], [], [], []
    lat_s, kr_s, conv_s = [], [], []
    for l in range(DEPTH):
        lp = dict(w_in=w_in[l], q_norm_g=q_norm_g[l], kv_norm_g=kv_norm_g[l], w_uq=w_uq[l], w_ukv=w_ukv[l],
                  conv_w=conv_w[l], w_out=w_out[l], ln1_g=ln1_g[l], ln1_b=ln1_b[l], w_xq=w_xq[l], w_xo=w_xo[l],
                  ln2_g=ln2_g[l], ln2_b=ln2_b[l], w_router_group=w_router_group[l],
                  b_router_group=b_router_group[l], w_router_expert=w_router_expert[l],
                  b_router_expert=b_router_expert[l], w_exp_gate=w_exp_gate[l], w_exp_up=w_exp_up[l],
                  w_exp_down=w_exp_down[l], ln3_g=ln3_g[l], ln3_b=ln3_b[l])
        mk, mv = memory_kv(mem_prompt, w_xk[l], w_xv[l])
        xp, lat_new, kr_new, conv_new = encoder_layer(xp, pos_p, mk, mv, None, None, None, lp)
        lat_p.append(lat_new); kr_p.append(kr_new); conv_p.append(conv_new); mk_p.append(mk); mv_p.append(mv)
        xs, lat_new, kr_new, conv_new = encoder_layer(xs, pos_s, cache_mem_k[l], cache_mem_v[l],
                                                      cache_kv_latent[l], cache_k_rope[l], cache_conv[l], lp)
        lat_s.append(lat_new); kr_s.append(kr_new); conv_s.append(conv_new)
    return (xp, xs, jnp.stack(lat_p), jnp.stack(kr_p), jnp.stack(conv_p), jnp.stack(mk_p), jnp.stack(mv_p),
            jnp.stack(lat_s), jnp.stack(kr_s), jnp.stack(conv_s))
```

```python
import functools
import math

import numpy as np
import jax
import jax.numpy as jnp
from jax import lax
from jax.experimental import pallas as pl
from jax.experimental.pallas import tpu as pltpu

F32 = jnp.float32
BF16 = jnp.bfloat16
I32 = jnp.int32

D_MODEL = 1024
CHUNK = 64
N_HEADS = 8
QK_NOPE = 64
QK_ROPE = 32
V_HEAD = 64
Q_LORA = 256
KV_LORA = 128
ROPE_THETA = 10000.0
MLA_SCALE = (QK_NOPE + QK_ROPE) ** -0.5
CONV_WIDTH = 512
CONV_K = 3
N_MEM = 256
X_HEADS = 4
X_HEAD_DIM = D_MODEL // X_HEADS
X_SCALE = X_HEAD_DIM ** -0.5
N_GROUPS = 4
EXPERTS_PER_GROUP = 8
N_EXPERTS = N_GROUPS * EXPERTS_PER_GROUP
D_EXPERT = 512
LN_EPS = 1e-5
RMS_EPS = 1e-6

LANES = 128
SUBLANES = 8
ROW_CHUNKS = D_MODEL // LANES
HEAD_PAD = LANES
ROPE_LANE0 = QK_NOPE
ROUTER_LANE0 = N_GROUPS
VMEM_LIMIT = 56 * 1024 * 1024
MOE_ROWS = 256


def _cp(sem, vmem=VMEM_LIMIT):
    return pltpu.CompilerParams(dimension_semantics=sem, vmem_limit_bytes=vmem)


def _pick_tile(n, pref):
    t = min(n, pref)
    while n % t:
        t //= 2
    return t


def _inproj_body(x_ref, cinit_ref, cos_ref, sin_ref, win_ref, qg_ref, kvg_ref, wuq_ref, wukv_ref, cw_ref,
                 q_ref, k_ref, v_ref, yc_ref, lat_ref, kr_ref, cst_ref, u_scr, *, tt, nj):
    j = pl.program_id(1)

    @pl.when(j == 0)
    def _():
        u_scr[6:8, :] = cinit_ref[0]

    x = x_ref[0].astype(BF16)
    proj = jnp.dot(x, win_ref[...], preferred_element_type=F32)
    cq = proj[:, 0:256]
    ckv = proj[:, 256:384]
    gb = proj[:, 384:896]
    gc = proj[:, 896:1408]
    gv = proj[:, 1408:1920]
    kr_a = proj[:, 1920:2048]
    kr_b = proj[:, 2048:2176]
    cos_t = cos_ref[...]
    sin_t = sin_ref[...]
    cqn = cq * lax.rsqrt(jnp.mean(cq * cq, -1, keepdims=True) + RMS_EPS) * qg_ref[...]
    ckvn = ckv * lax.rsqrt(jnp.mean(ckv * ckv, -1, keepdims=True) + RMS_EPS) * kvg_ref[...]
    lat_ref[0] = ckvn
    kr_p = kr_a * cos_t + kr_b * sin_t
    kr_ref[0] = kr_p[:, ROPE_LANE0:ROPE_LANE0 + QK_ROPE]
    qq = jnp.dot(cqn.astype(BF16), wuq_ref[...], preferred_element_type=F32)
    kv = jnp.dot(ckvn.astype(BF16), wukv_ref[...], preferred_element_type=F32)
    hw = N_HEADS * HEAD_PAD
    for h in range(N_HEADS):
        sl = slice(h * HEAD_PAD, (h + 1) * HEAD_PAD)
        sl_b = slice(hw + h * HEAD_PAD, hw + (h + 1) * HEAD_PAD)
        q_ref[0, :, sl] = (qq[:, sl] * cos_t + qq[:, sl_b] * sin_t).astype(BF16)
        k_ref[0, :, sl] = (kv[:, sl] + kr_p).astype(BF16)
    v_ref[0] = kv[:, hw:hw + N_HEADS * V_HEAD].astype(BF16)
    u = gc * gv
    u_scr[8:8 + tt, :] = u
    conv = cw_ref[0:1, :] * u_scr[6:6 + tt, :] + cw_ref[1:2, :] * u_scr[7:7 + tt, :] + cw_ref[2:3, :] * u
    yc_ref[0] = (gb * conv).astype(BF16)
    last2 = u_scr[tt + 6:tt + 8, :]
    u_scr[6:8, :] = last2

    @pl.when(j == nj - 1)
    def _():
        cst_ref[0] = last2


def _inproj(x, conv_init, cos_t, sin_t, w):
    b, s, _ = x.shape
    tt = _pick_tile(s, 512)
    nj = s // tt
    wn = w['w_in'].shape[1]
    full = lambda shape: pl.BlockSpec(shape, lambda bi, ji: (0,) * len(shape))
    out_shapes = (
        jax.ShapeDtypeStruct((b, s, N_HEADS * HEAD_PAD), BF16),
        jax.ShapeDtypeStruct((b, s, N_HEADS * HEAD_PAD), BF16),
        jax.ShapeDtypeStruct((b, s, N_HEADS * V_HEAD), BF16),
        jax.ShapeDtypeStruct((b, s, CONV_WIDTH), BF16),
        jax.ShapeDtypeStruct((b, s, KV_LORA), F32),
        jax.ShapeDtypeStruct((b, s, QK_ROPE), F32),
        jax.ShapeDtypeStruct((b, CONV_K - 1, CONV_WIDTH), F32),
    )
    row = lambda width: pl.BlockSpec((1, tt, width), lambda bi, ji: (bi, ji, 0))
    return pl.pallas_call(
        functools.partial(_inproj_body, tt=tt, nj=nj),
        grid=(b, nj),
        in_specs=[
            row(D_MODEL),
            pl.BlockSpec((1, CONV_K - 1, CONV_WIDTH), lambda bi, ji: (bi, 0, 0)),
            pl.BlockSpec((tt, LANES), lambda bi, ji: (ji, 0)),
            pl.BlockSpec((tt, LANES), lambda bi, ji: (ji, 0)),
            full((D_MODEL, wn)),
            full((1, Q_LORA)),
            full((1, KV_LORA)),
            full(w['w_uq'].shape),
            full(w['w_ukv'].shape),
            full((CONV_K, CONV_WIDTH)),
        ],
        out_specs=(
            row(N_HEADS * HEAD_PAD), row(N_HEADS * HEAD_PAD), row(N_HEADS * V_HEAD), row(CONV_WIDTH),
            row(KV_LORA), row(QK_ROPE),
            pl.BlockSpec((1, CONV_K - 1, CONV_WIDTH), lambda bi, ji: (bi, 0, 0)),
        ),
        out_shape=out_shapes,
        scratch_shapes=[pltpu.VMEM((tt + 8, CONV_WIDTH), F32)],
        compiler_params=_cp(("parallel", "arbitrary")),
        name="inproj",
    )(x, conv_init, cos_t, sin_t, w['w_in'], w['q_norm_g'], w['kv_norm_g'], w['w_uq'], w['w_ukv'], w['conv_w'])


def _kvup_body(lat_ref, krp_ref, wukv_ref, k_ref, v_ref):
    kv = jnp.dot(lat_ref[0].astype(BF16), wukv_ref[...], preferred_element_type=F32)
    kr_p = krp_ref[0]
    hw = N_HEADS * HEAD_PAD
    for h in range(N_HEADS):
        sl = slice(h * HEAD_PAD, (h + 1) * HEAD_PAD)
        k_ref[0, :, sl] = (kv[:, sl] + kr_p).astype(BF16)
    v_ref[0] = kv[:, hw:hw + N_HEADS * V_HEAD].astype(BF16)


def _kvup(lat, kr_padded, w_ukv):
    b, s, _ = lat.shape
    tt = _pick_tile(s, 512)
    row = lambda width: pl.BlockSpec((1, tt, width), lambda bi, ji: (bi, ji, 0))
    return pl.pallas_call(
        _kvup_body,
        grid=(b, s // tt),
        in_specs=[row(KV_LORA), row(LANES), pl.BlockSpec(w_ukv.shape, lambda bi, ji: (0, 0))],
        out_specs=(row(N_HEADS * HEAD_PAD), row(N_HEADS * V_HEAD)),
        out_shape=(jax.ShapeDtypeStruct((b, s, N_HEADS * HEAD_PAD), BF16),
                   jax.ShapeDtypeStruct((b, s, N_HEADS * V_HEAD), BF16)),
        compiler_params=_cp(("parallel", "parallel")),
        name="kvup",
    )(lat, kr_padded, w_ukv)


def _attn_body(qi_ref, ki_ref, fl_ref, q_ref, k_ref, v_ref, o_ref, m_scr, l_scr, acc_scr,
               *, tq, tk, q_pos0, n_valid):
    step = pl.program_id(1)
    qi = qi_ref[step]
    ki = ki_ref[step]
    flags = fl_ref[step]
    c_exp = MLA_SCALE * math.log2(math.e)

    @pl.when((flags & 1) != 0)
    def _():
        m_scr[...] = jnp.full(m_scr.shape, -jnp.inf, F32)
        l_scr[...] = jnp.zeros(l_scr.shape, F32)
        acc_scr[...] = jnp.zeros(acc_scr.shape, F32)

    def tile(masked):
        if masked:
            qpos = q_pos0 + qi * tq + lax.broadcasted_iota(I32, (tq, tk), 0)
            kpos = ki * tk + lax.broadcasted_iota(I32, (tq, tk), 1)
            mask = (kpos >> 6) <= (qpos >> 6)
            if n_valid is not None:
                mask = mask & (kpos < n_valid)
        for h in range(N_HEADS):
            q_h = q_ref[0, :, h * HEAD_PAD:(h + 1) * HEAD_PAD]
            k_h = k_ref[0, :, h * HEAD_PAD:(h + 1) * HEAD_PAD]
            s = lax.dot_general(q_h, k_h, (((1,), (1,)), ((), ())), preferred_element_type=F32)
            if masked:
                s = jnp.where(mask, s, -jnp.inf)
            m_old = m_scr[h]
            m_new = jnp.maximum(m_old, jnp.max(s, axis=-1, keepdims=True))
            alpha = jnp.exp2((m_old - m_new) * c_exp)
            p = jnp.exp2((s - m_new[:, 0:1]) * c_exp)
            l_scr[h] = alpha * l_scr[h] + jnp.sum(p, axis=-1, keepdims=True)
            pv = jnp.dot(p.astype(BF16), v_ref[0, :, h * V_HEAD:(h + 1) * V_HEAD], preferred_element_type=F32)
            acc_scr[h] = alpha[:, 0:V_HEAD] * acc_scr[h] + pv
            m_scr[h] = m_new

    @pl.when((flags & 4) != 0)
    def _():
        tile(True)

    @pl.when((flags & 4) == 0)
    def _():
        tile(False)

    @pl.when((flags & 2) != 0)
    def _():
        for h in range(N_HEADS):
            o_ref[0, :, h * V_HEAD:(h + 1) * V_HEAD] = (acc_scr[h] / l_scr[h][:, 0:V_HEAD]).astype(BF16)


def _attn_tables(nq, nk, tq, tk, q_pos0, n_valid):
    qi_l, ki_l, fl_l = [], [], []
    for qi in range(nq):
        q_lo = q_pos0 + qi * tq
        q_hi = q_lo + tq - 1
        k_last_pos = (q_hi // CHUNK) * CHUNK + CHUNK - 1
        if n_valid is not None:
            k_last_pos = min(k_last_pos, n_valid - 1)
        k_last = min(nk - 1, k_last_pos // tk)
        for ki in range(k_last + 1):
            k_hi = ki * tk + tk - 1
            masked = (k_hi // CHUNK) > (q_lo // CHUNK) or (n_valid is not None and k_hi >= n_valid)
            fl = (1 if ki == 0 else 0) | (2 if ki == k_last else 0) | (4 if masked else 0)
            qi_l.append(qi); ki_l.append(ki); fl_l.append(fl)
    return (jnp.asarray(np.array(qi_l, np.int32)), jnp.asarray(np.array(ki_l, np.int32)),
            jnp.asarray(np.array(fl_l, np.int32)))


def _attention(q, k, v, q_pos0, n_valid, tq_pref=512, tk_pref=512):
    b, sq, _ = q.shape
    sk = k.shape[1]
    tq = _pick_tile(sq, tq_pref)
    tk = _pick_tile(sk, tk_pref)
    qi_t, ki_t, fl_t = _attn_tables(sq // tq, sk // tk, tq, tk, q_pos0, n_valid)
    n_steps = int(qi_t.shape[0])
    grid_spec = pltpu.PrefetchScalarGridSpec(
        num_scalar_prefetch=3,
        grid=(b, n_steps),
        in_specs=[
            pl.BlockSpec((1, tq, N_HEADS * HEAD_PAD), lambda bi, si, qt, kt, ft: (bi, qt[si], 0)),
            pl.BlockSpec((1, tk, N_HEADS * HEAD_PAD), lambda bi, si, qt, kt, ft: (bi, kt[si], 0)),
            pl.BlockSpec((1, tk, N_HEADS * V_HEAD), lambda bi, si, qt, kt, ft: (bi, kt[si], 0)),
        ],
        out_specs=pl.BlockSpec((1, tq, N_HEADS * V_HEAD), lambda bi, si, qt, kt, ft: (bi, qt[si], 0)),
        scratch_shapes=[
            pltpu.VMEM((N_HEADS, tq, LANES), F32),
            pltpu.VMEM((N_HEADS, tq, LANES), F32),
            pltpu.VMEM((N_HEADS, tq, V_HEAD), F32),
        ],
    )
    return pl.pallas_call(
        functools.partial(_attn_body, tq=tq, tk=tk, q_pos0=q_pos0, n_valid=n_valid),
        grid_spec=grid_spec,
        out_shape=jax.ShapeDtypeStruct((b, sq, N_HEADS * V_HEAD), BF16),
        compiler_params=_cp(("parallel", "arbitrary")),
        name="mla_attn",
    )(qi_t, ki_t, fl_t, q, k, v)


def _memkv_body(mem_ref, wk_ref, wv_ref, mk_ref, mv_ref, mkb_ref, mvb_ref):
    m = mem_ref[...].astype(BF16)
    mk = jnp.dot(m, wk_ref[...], preferred_element_type=F32)
    mv = jnp.dot(m, wv_ref[...], preferred_element_type=F32)
    mk_ref[...] = mk
    mv_ref[...] = mv
    mkb_ref[...] = mk.astype(BF16)
    mvb_ref[...] = mv.astype(BF16)


def _memkv(mem2d, w_xk, w_xv):
    n = mem2d.shape[0]
    tt = _pick_tile(n, 256)
    row = pl.BlockSpec((tt, D_MODEL), lambda i: (i, 0))
    wspec = pl.BlockSpec((D_MODEL, D_MODEL), lambda i: (0, 0))
    return pl.pallas_call(
        _memkv_body,
        grid=(n // tt,),
        in_specs=[row, wspec, wspec],
        out_specs=(row, row, row, row),
        out_shape=(jax.ShapeDtypeStruct((n, D_MODEL), F32), jax.ShapeDtypeStruct((n, D_MODEL), F32),
                   jax.ShapeDtypeStruct((n, D_MODEL), BF16), jax.ShapeDtypeStruct((n, D_MODEL), BF16)),
        compiler_params=_cp(("parallel",)),
        name="memkv",
    )(mem2d, w_xk, w_xv)


def _layer_norm(x, g, b):
    mu = jnp.mean(x, -1, keepdims=True)
    xc = x - mu
    var = jnp.mean(xc * xc, -1, keepdims=True)
    return xc * lax.rsqrt(var + LN_EPS) * g + b


def _mid_body(x_ref, at_ref, yc_ref, mk_ref, mv_ref, cnt0_ref, wo_ref, g1_ref, b1_ref, wq_ref, wxo_ref,
              g2_ref, b2_ref, wr_ref, br_ref, x2_ref, ri_ref, cnt_ref, cnt_scr, *, tt, alpha):
    first = (pl.program_id(0) == 0) & (pl.program_id(1) == 0)

    @pl.when(first)
    def _():
        cnt_scr[...] = cnt0_ref[...]

    mix = jnp.concatenate([at_ref[0], yc_ref[0]], axis=-1)
    a = jnp.dot(mix, wo_ref[...], preferred_element_type=F32)
    x1 = _layer_norm(alpha * x_ref[0] + a, g1_ref[...], b1_ref[...])
    q = jnp.dot(x1.astype(BF16), wq_ref[...], preferred_element_type=F32).astype(BF16)
    outs = []
    for h in range(X_HEADS):
        sl = slice(h * X_HEAD_DIM, (h + 1) * X_HEAD_DIM)
        sc = lax.dot_general(q[:, sl], mk_ref[0, :, sl], (((1,), (1,)), ((), ())),
                             preferred_element_type=F32) * X_SCALE
        sc = sc - jnp.max(sc, -1, keepdims=True)
        e = jnp.exp(sc)
        p = e / jnp.sum(e, -1, keepdims=True)
        outs.append(jnp.dot(p.astype(BF16), mv_ref[0, :, sl], preferred_element_type=F32).astype(BF16))
    o = jnp.concatenate(outs, axis=-1)
    c = jnp.dot(o, wxo_ref[...], preferred_element_type=F32)
    x2 = _layer_norm(alpha * x1 + c, g2_ref[...], b2_ref[...])
    x2_ref[0] = x2

    logits = jnp.dot(x2, wr_ref[...], preferred_element_type=F32, precision=lax.Precision.HIGHEST) + br_ref[...]
    lane = lax.broadcasted_iota(I32, (tt, LANES), 1)
    neg = -jnp.inf
    is_g = lane < N_GROUPS
    lg = jnp.where(is_g, logits, neg)
    mg = jnp.max(lg, -1, keepdims=True)
    g_idx = jnp.min(jnp.where(lg == mg, lane, LANES), -1, keepdims=True)
    pg = 1.0 / jnp.sum(jnp.where(is_g, jnp.exp(logits - mg), 0.0), -1, keepdims=True)
    in_grp = (lane >= ROUTER_LANE0) & (lane < ROUTER_LANE0 + N_EXPERTS) & (((lane - ROUTER_LANE0) >> 3) == g_idx)
    le = jnp.where(in_grp, logits, neg)
    v1 = jnp.max(le, -1, keepdims=True)
    i1 = jnp.min(jnp.where(le == v1, lane, LANES), -1, keepdims=True)
    le2 = jnp.where(lane == i1, neg, le)
    v2 = jnp.max(le2, -1, keepdims=True)
    i2 = jnp.min(jnp.where(le2 == v2, lane, LANES), -1, keepdims=True)
    e2 = jnp.exp(v2 - v1)
    den = 1.0 + e2
    gate1 = (1.0 / den) * pg
    gate2 = (e2 / den) * pg
    oh1 = (lane == i1).astype(F32)
    oh2 = (lane == i2).astype(F32)
    oh = oh1 + oh2
    r_i = lax.broadcasted_iota(I32, (tt, tt), 0)
    c_i = lax.broadcasted_iota(I32, (tt, tt), 1)
    lower = jnp.where(c_i < r_i, 1.0, 0.0).astype(BF16)
    base = cnt_scr[...] + jnp.dot(lower, oh.astype(BF16), preferred_element_type=F32)
    rank1 = jnp.sum(oh1 * base, -1, keepdims=True)
    rank2 = jnp.sum(oh2 * base, -1, keepdims=True)
    cnt_new = cnt_scr[...] + jnp.sum(oh, 0, keepdims=True)
    cnt_scr[...] = cnt_new
    cnt_ref[...] = cnt_new
    e1f = (i1 - ROUTER_LANE0).astype(F32)
    e2f = (i2 - ROUTER_LANE0).astype(F32)
    ri = jnp.where(lane == 0, e1f, jnp.where(lane == 1, e2f, jnp.where(lane == 2, rank1, jnp.where(
        lane == 3, rank2, jnp.where(lane == 4, gate1, jnp.where(lane == 5, gate2, 0.0))))))
    ri_ref[0] = ri


def _mid(x, attn, yconv, mk_b, mv_b, cnt0, w, alpha):
    b, s, _ = x.shape
    tt = _pick_tile(s, 512)
    row = lambda width: pl.BlockSpec((1, tt, width), lambda bi, ji: (bi, ji, 0))
    full = lambda shape: pl.BlockSpec(shape, lambda bi, ji: (0,) * len(shape))
    mem = pl.BlockSpec((1, N_MEM, D_MODEL), lambda bi, ji: (bi, 0, 0))
    vec = full((1, D_MODEL))
    return pl.pallas_call(
        functools.partial(_mid_body, tt=tt, alpha=alpha),
        grid=(b, s // tt),
        in_specs=[row(D_MODEL), row(N_HEADS * V_HEAD), row(CONV_WIDTH), mem, mem, full((1, LANES)),
                  full((D_MODEL, D_MODEL)), vec, vec, full((D_MODEL, D_MODEL)), full((D_MODEL, D_MODEL)),
                  vec, vec, full((D_MODEL, LANES)), full((1, LANES))],
        out_specs=(row(D_MODEL), row(LANES), full((1, LANES))),
        out_shape=(jax.ShapeDtypeStruct((b, s, D_MODEL), F32), jax.ShapeDtypeStruct((b, s, LANES), F32),
                   jax.ShapeDtypeStruct((1, LANES), F32)),
        scratch_shapes=[pltpu.VMEM((1, LANES), F32)],
        compiler_params=_cp(("arbitrary", "arbitrary")),
        name="mid",
    )(x, attn, yconv, mk_b, mv_b, cnt0, w['w_out'], w['ln1_g'], w['ln1_b'], w['w_xq'], w['w_xo'],
      w['ln2_g'], w['ln2_b'], w['w_router'], w['b_router'])


def _row_slice(ref, row):
    return ref.at[pl.ds(pl.multiple_of(row * SUBLANES, SUBLANES), SUBLANES), :]


def _dispatch_body(dest_ref, x_ref, xs_in_ref, xs_ref, buf0, buf1, sem0, sem1, *, tt, nt):
    del xs_in_ref
    i = pl.program_id(0)

    def wait_all(buf, sem):
        for _ in range(2):
            pltpu.make_async_copy(buf, xs_ref.at[pl.ds(0, tt * SUBLANES), :], sem).wait()

    def run(buf, sem, obuf, osem):
        for c in range(ROW_CHUNKS):
            buf[pl.ds(c, tt, stride=SUBLANES), :] = x_ref[:, c * LANES:(c + 1) * LANES]

        def issue(t, carry):
            src = _row_slice(buf, t)
            for kk in range(2):
                d = dest_ref[0, 0, 2 * t + kk]
                pltpu.make_async_copy(src, _row_slice(xs_ref, d), sem).start()
            return carry

        lax.fori_loop(0, tt, issue, 0)

        @pl.when(i > 0)
        def _():
            wait_all(obuf, osem)

        @pl.when(i == nt - 1)
        def _():
            wait_all(buf, sem)

    @pl.when(i % 2 == 0)
    def _():
        run(buf0, sem0, buf1, sem1)

    @pl.when(i % 2 == 1)
    def _():
        run(buf1, sem1, buf0, sem0)


def _dispatch(x2d, dest, xs):
    n = x2d.shape[0]
    tt = _pick_tile(n, 256)
    nt = n // tt
    dest3 = dest.reshape(nt, 1, 2 * tt)
    return pl.pallas_call(
        functools.partial(_dispatch_body, tt=tt, nt=nt),
        grid=(nt,),
        in_specs=[
            pl.BlockSpec((1, 1, 2 * tt), lambda i: (i, 0, 0), memory_space=pltpu.SMEM),
            pl.BlockSpec((tt, D_MODEL), lambda i: (i, 0)),
            pl.BlockSpec(memory_space=pl.ANY),
        ],
        out_specs=pl.BlockSpec(memory_space=pl.ANY),
        out_shape=jax.ShapeDtypeStruct(xs.shape, xs.dtype),
        scratch_shapes=[pltpu.VMEM((tt * SUBLANES, LANES), F32), pltpu.VMEM((tt * SUBLANES, LANES), F32),
                        pltpu.SemaphoreType.DMA, pltpu.SemaphoreType.DMA],
        input_output_aliases={2: 0},
        compiler_params=_cp(("arbitrary",)),
        name="moe_dispatch",
    )(dest3, x2d, xs)


def _experts_body(be_ref, nu_ref, x_ref, wg_ref, wu_ref, wd_ref, y_ref, *, rows):
    i = pl.program_id(0)

    @pl.when(i < nu_ref[0])
    def _():
        xb = jnp.concatenate([x_ref[pl.ds(c, rows, stride=SUBLANES), :] for c in range(ROW_CHUNKS)],
                             axis=-1).astype(BF16)
        g = jnp.dot(xb, wg_ref[0], preferred_element_type=F32)
        u = jnp.dot(xb, wu_ref[0], preferred_element_type=F32)
        hdn = (g * jax.nn.sigmoid(g)) * u
        y = jnp.dot(hdn.astype(BF16), wd_ref[0], preferred_element_type=F32)
        for c in range(ROW_CHUNKS):
            y_ref[pl.ds(c, rows, stride=SUBLANES), :] = y[:, c * LANES:(c + 1) * LANES]

    @pl.when(i >= nu_ref[0])
    def _():
        y_ref[...] = jnp.zeros(y_ref.shape, F32)


def _experts(xs, block_e, n_used, wg, wu, wd, rows):
    nblk = xs.shape[0] // (rows * SUBLANES)
    clamp = lambda i, nu: jnp.minimum(i, nu[0] - 1)
    grid_spec = pltpu.PrefetchScalarGridSpec(
        num_scalar_prefetch=2,
        grid=(nblk,),
        in_specs=[
            pl.BlockSpec((rows * SUBLANES, LANES), lambda i, be, nu: (clamp(i, nu), 0)),
            pl.BlockSpec((1, D_MODEL, D_EXPERT), lambda i, be, nu: (be[clamp(i, nu)], 0, 0)),
            pl.BlockSpec((1, D_MODEL, D_EXPERT), lambda i, be, nu: (be[clamp(i, nu)], 0, 0)),
            pl.BlockSpec((1, D_EXPERT, D_MODEL), lambda i, be, nu: (be[clamp(i, nu)], 0, 0)),
        ],
        out_specs=pl.BlockSpec((rows * SUBLANES, LANES), lambda i, be, nu: (i, 0)),
    )
    return pl.pallas_call(
        functools.partial(_experts_body, rows=rows),
        grid_spec=grid_spec,
        out_shape=jax.ShapeDtypeStruct(xs.shape, F32),
        compiler_params=_cp(("arbitrary",)),
        name="moe_experts",
    )(block_e, n_used, xs, wg, wu, wd)


def _combine_body(dest_ref, x_ref, ri_ref, g3_ref, b3_ref, ys_ref, o_ref, buf0, buf1, sem, *, tt, alpha):
    bufs = (buf0, buf1)

    def issue(t, carry):
        for kk in range(2):
            d = dest_ref[0, 0, 2 * t + kk]
            pltpu.make_async_copy(_row_slice(ys_ref, d), _row_slice(bufs[kk], t), sem).start()
        return carry

    lax.fori_loop(0, tt, issue, 0)
    for kk in range(2):
        pltpu.make_async_copy(ys_ref.at[pl.ds(0, tt * SUBLANES), :], bufs[kk], sem).wait()
    ri = ri_ref[...]
    y0 = jnp.concatenate([buf0[pl.ds(c, tt, stride=SUBLANES), :] for c in range(ROW_CHUNKS)], axis=-1)
    y1 = jnp.concatenate([buf1[pl.ds(c, tt, stride=SUBLANES), :] for c in range(ROW_CHUNKS)], axis=-1)
    moe = y0 * ri[:, 4:5] + y1 * ri[:, 5:6]
    o_ref[...] = _layer_norm(alpha * x_ref[...] + moe, g3_ref[...], b3_ref[...])


def _combine(x2d, rinfo, dest, ys, g3, b3, alpha):
    n = x2d.shape[0]
    tt = _pick_tile(n, 256)
    nt = n // tt
    dest3 = dest.reshape(nt, 1, 2 * tt)
    vec = pl.BlockSpec((1, D_MODEL), lambda i: (0, 0))
    return pl.pallas_call(
        functools.partial(_combine_body, tt=tt, alpha=alpha),
        grid=(nt,),
        in_specs=[
            pl.BlockSpec((1, 1, 2 * tt), lambda i: (i, 0, 0), memory_space=pltpu.SMEM),
            pl.BlockSpec((tt, D_MODEL), lambda i: (i, 0)),
            pl.BlockSpec((tt, LANES), lambda i: (i, 0)),
            vec, vec,
            pl.BlockSpec(memory_space=pl.ANY),
        ],
        out_specs=pl.BlockSpec((tt, D_MODEL), lambda i: (i, 0)),
        out_shape=jax.ShapeDtypeStruct((n, D_MODEL), F32),
        scratch_shapes=[pltpu.VMEM((tt * SUBLANES, LANES), F32), pltpu.VMEM((tt * SUBLANES, LANES), F32),
                        pltpu.SemaphoreType.DMA],
        compiler_params=_cp(("arbitrary",)),
        name="moe_combine",
    )(dest3, x2d, rinfo, g3, b3, ys)


def _rope_tables(pos):
    half = QK_ROPE // 2
    inv = ROPE_THETA ** (-jnp.arange(half, dtype=F32) / half)
    ang = pos.astype(F32)[:, None] * inv[None, :]
    cos, sin = jnp.cos(ang), jnp.sin(ang)
    n = pos.shape[0]
    pad_r = LANES - ROPE_LANE0 - QK_ROPE
    cos_t = jnp.concatenate([jnp.ones((n, ROPE_LANE0), F32), cos, cos, jnp.ones((n, pad_r), F32)], -1)
    sin_t = jnp.concatenate([jnp.zeros((n, ROPE_LANE0), F32), sin, sin, jnp.zeros((n, pad_r), F32)], -1)
    return cos_t, sin_t


def _swap_neg(wr):
    half = QK_ROPE // 2
    return jnp.concatenate([-wr[:, half:], wr[:, :half]], axis=1)


def _prep_weights(l, w_in, q_norm_g, kv_norm_g, w_uq, w_ukv, conv_w, w_out, ln1_g, ln1_b, w_xq, w_xk, w_xv, w_xo,
                  ln2_g, ln2_b, w_router_group, b_router_group, w_router_expert, b_router_expert,
                  w_exp_gate, w_exp_up, w_exp_down, ln3_g, ln3_b):
    wi = w_in[l]
    c0 = Q_LORA + KV_LORA
    w_kr = wi[:, c0:c0 + QK_ROPE]
    zl = jnp.zeros((D_MODEL, ROPE_LANE0), F32)
    zr = jnp.zeros((D_MODEL, LANES - ROPE_LANE0 - QK_ROPE), F32)
    w_in_p = jnp.concatenate([wi[:, :c0], wi[:, c0 + QK_ROPE:], zl, w_kr, zr, zl, _swap_neg(w_kr), zr], axis=1)
    wq = w_uq[l].reshape(Q_LORA, N_HEADS, QK_NOPE + QK_ROPE)
    zq = jnp.zeros((Q_LORA, N_HEADS, HEAD_PAD - QK_NOPE - QK_ROPE), F32)
    wq_a = jnp.concatenate([wq, zq], axis=2).reshape(Q_LORA, N_HEADS * HEAD_PAD)
    wq_rot = jnp.concatenate([-wq[:, :, QK_NOPE + QK_ROPE // 2:], wq[:, :, QK_NOPE:QK_NOPE + QK_ROPE // 2]], axis=2)
    wq_b = jnp.concatenate([jnp.zeros((Q_LORA, N_HEADS, QK_NOPE), F32), wq_rot, zq], axis=2)
    wq_b = wq_b.reshape(Q_LORA, N_HEADS * HEAD_PAD)
    wkv = w_ukv[l].reshape(KV_LORA, N_HEADS, QK_NOPE + V_HEAD)
    wk_p = jnp.concatenate([wkv[:, :, :QK_NOPE], jnp.zeros((KV_LORA, N_HEADS, HEAD_PAD - QK_NOPE), F32)], axis=2)
    wk_p = wk_p.reshape(KV_LORA, N_HEADS * HEAD_PAD)
    wv_p = wkv[:, :, QK_NOPE:].reshape(KV_LORA, N_HEADS * V_HEAD)
    w_router = jnp.concatenate([w_router_group[l], w_router_expert[l],
                                jnp.zeros((D_MODEL, LANES - N_GROUPS - N_EXPERTS), F32)], axis=1)
    b_router = jnp.concatenate([b_router_group[l], b_router_expert[l].reshape(-1),
                                jnp.zeros((LANES - N_GROUPS - N_EXPERTS,), F32)]).reshape(1, LANES)
    return dict(
        w_in=w_in_p.astype(BF16),
        q_norm_g=q_norm_g[l].reshape(1, Q_LORA), kv_norm_g=kv_norm_g[l].reshape(1, KV_LORA),
        w_uq=jnp.concatenate([wq_a, wq_b], axis=1).astype(BF16),
        w_ukv=jnp.concatenate([wk_p, wv_p], axis=1).astype(BF16),
        conv_w=conv_w[l],
        w_out=w_out[l].astype(BF16), ln1_g=ln1_g[l].reshape(1, -1), ln1_b=ln1_b[l].reshape(1, -1),
        w_xq=w_xq[l].astype(BF16), w_xk=w_xk[l].astype(BF16), w_xv=w_xv[l].astype(BF16),
        w_xo=w_xo[l].astype(BF16), ln2_g=ln2_g[l].reshape(1, -1), ln2_b=ln2_b[l].reshape(1, -1),
        w_router=w_router, b_router=b_router,
        w_exp_gate=w_exp_gate[l].astype(BF16), w_exp_up=w_exp_up[l].astype(BF16),
        w_exp_down=w_exp_down[l].astype(BF16),
        ln3_g=ln3_g[l].reshape(1, -1), ln3_b=ln3_b[l].reshape(1, -1),
    )


def _route_cols(rinfo2d):
    eid = rinfo2d[:, 0:2].astype(I32)
    rank = rinfo2d[:, 2:4].astype(I32)
    return eid, rank


def _layer(l, depth, xp, xs, lat_past, kr_past, conv_past, mk_s, mv_s, mem_prompt, w):
    alpha = (2 * depth) ** 0.25
    b, s, _ = xp.shape
    bs, ss, _ = xs.shape
    past = lat_past.shape[1]

    cos_p, sin_p = _rope_tables(jnp.arange(s))
    q_p, k_p, v_p, yc_p, lat_p, kr_p, cst_p = _inproj(
        xp, jnp.zeros((b, CONV_K - 1, CONV_WIDTH), F32), cos_p, sin_p, w)
    attn_p = _attention(q_p, k_p, v_p, 0, None)
    mk, mv, mk_b, mv_b = _memkv(mem_prompt.reshape(b * N_MEM, D_MODEL), w['w_xk'], w['w_xv'])
    cnt0 = jnp.zeros((1, LANES), F32)
    x2_p, ri_p, cnt_p = _mid(xp, attn_p, yc_p, mk_b.reshape(b, N_MEM, D_MODEL), mv_b.reshape(b, N_MEM, D_MODEL),
                             cnt0, w, alpha)

    cos_s, sin_s = _rope_tables(past + jnp.arange(ss))
    q_s, _, _, yc_s, lat_s, kr_s, cst_s = _inproj(xs, conv_past, cos_s, sin_s, w)
    n_keys = past + ss
    sk = -(-n_keys // 512) * 512
    lat_all = jnp.concatenate([lat_past, lat_s, jnp.zeros((bs, sk - n_keys, KV_LORA), F32)], axis=1)
    kr_all = jnp.concatenate([kr_past, kr_s, jnp.zeros((bs, sk - n_keys, QK_ROPE), F32)], axis=1)
    kr_all = jnp.pad(kr_all, ((0, 0), (0, 0), (ROPE_LANE0, LANES - ROPE_LANE0 - QK_ROPE)))
    k_s, v_s = _kvup(lat_all, kr_all, w['w_ukv'])
    attn_s = _attention(q_s, k_s, v_s, past, n_keys)
    mk_sb = mk_s.reshape(bs, N_MEM, D_MODEL).astype(BF16)
    mv_sb = mv_s.reshape(bs, N_MEM, D_MODEL).astype(BF16)
    x2_s, ri_s, cnt = _mid(xs, attn_s, yc_s, mk_sb, mv_sb, cnt_p, w, alpha)

    n_p, n_s = b * s, bs * ss
    counts = cnt[0, ROUTER_LANE0:ROUTER_LANE0 + N_EXPERTS].astype(I32)
    padded = (counts + MOE_ROWS - 1) // MOE_ROWS * MOE_ROWS
    pends = jnp.cumsum(padded)
    pstarts = pends - padded
    nblk = -(-2 * (n_p + n_s) // MOE_ROWS) + N_EXPERTS
    block_e = jnp.minimum(jnp.searchsorted(pends, jnp.arange(nblk, dtype=I32) * MOE_ROWS, side='right'),
                          N_EXPERTS - 1).astype(I32)
    n_used = (pends[-1] // MOE_ROWS).astype(I32).reshape(1)
    ri_p2, ri_s2 = ri_p.reshape(n_p, LANES), ri_s.reshape(n_s, LANES)
    eid_p, rank_p = _route_cols(ri_p2)
    eid_s, rank_s = _route_cols(ri_s2)
    dest_p = (pstarts[eid_p] + rank_p).reshape(-1)
    dest_s = (pstarts[eid_s] + rank_s).reshape(-1)
    x2_p2, x2_s2 = x2_p.reshape(n_p, D_MODEL), x2_s.reshape(n_s, D_MODEL)
    slots = jnp.zeros((nblk * MOE_ROWS * SUBLANES, LANES), F32)
    slots = _dispatch(x2_p2, dest_p, slots)
    slots = _dispatch(x2_s2, dest_s, slots)
    ys = _experts(slots, block_e, n_used, w['w_exp_gate'], w['w_exp_up'], w['w_exp_down'], MOE_ROWS)
    y_p = _combine(x2_p2, ri_p2, dest_p, ys, w['ln3_g'], w['ln3_b'], alpha).reshape(b, s, D_MODEL)
    y_s = _combine(x2_s2, ri_s2, dest_s, ys, w['ln3_g'], w['ln3_b'], alpha).reshape(bs, ss, D_MODEL)
    return (y_p, y_s, lat_p, kr_p, cst_p, mk.reshape(b, N_MEM, X_HEADS, X_HEAD_DIM),
            mv.reshape(b, N_MEM, X_HEADS, X_HEAD_DIM), lat_s, kr_s, cst_s)


def kernel(x_prompt, x_sample, cache_kv_latent, cache_k_rope, cache_conv, cache_mem_k, cache_mem_v, mem_prompt,
           w_in, q_norm_g, kv_norm_g, w_uq, w_ukv, conv_w, w_out, ln1_g, ln1_b, w_xq, w_xk, w_xv, w_xo, ln2_g,
           ln2_b, w_router_group, b_router_group, w_router_expert, b_router_expert, w_exp_gate, w_exp_up,
           w_exp_down, ln3_g, ln3_b):
    depth = w_in.shape[0]
    xp, xs = x_prompt, x_sample
    outs = [[] for _ in range(8)]
    for l in range(depth):
        w = _prep_weights(l, w_in, q_norm_g, kv_norm_g, w_uq, w_ukv, conv_w, w_out, ln1_g, ln1_b, w_xq, w_xk, w_xv,
                          w_xo, ln2_g, ln2_b, w_router_group, b_router_group, w_router_expert, b_router_expert,
                          w_exp_gate, w_exp_up, w_exp_down, ln3_g, ln3_b)
        res = _layer(l, depth, xp, xs, cache_kv_latent[l], cache_k_rope[l], cache_conv[l], cache_mem_k[l],
                     cache_mem_v[l], mem_prompt, w)
        xp, xs = res[0], res[1]
        for acc, r in zip(outs, res[2:]):
            acc.append(r)
    return (xp, xs) + tuple(jnp.stack(o) for o in outs)
```

```python
import functools
import math

import numpy as np
import jax
import jax.numpy as jnp
from jax import lax
from jax.experimental import pallas as pl
from jax.experimental.pallas import tpu as pltpu

F32 = jnp.float32
BF16 = jnp.bfloat16
I32 = jnp.int32

D_MODEL = 1024
CHUNK = 64
N_HEADS = 8
QK_NOPE = 64
QK_ROPE = 32
V_HEAD = 64
Q_LORA = 256
KV_LORA = 128
ROPE_THETA = 10000.0
MLA_SCALE = (QK_NOPE + QK_ROPE) ** -0.5
CONV_WIDTH = 512
CONV_K = 3
N_MEM = 256
X_HEADS = 4
X_HEAD_DIM = D_MODEL // X_HEADS
X_SCALE = X_HEAD_DIM ** -0.5
N_GROUPS = 4
EXPERTS_PER_GROUP = 8
N_EXPERTS = N_GROUPS * EXPERTS_PER_GROUP
D_EXPERT = 512
LN_EPS = 1e-5
RMS_EPS = 1e-6

LANES = 128
SUBLANES = 8
ROW_CHUNKS = D_MODEL // LANES
HEAD_PAD = LANES
ROPE_LANE0 = QK_NOPE
ROUTER_LANE0 = N_GROUPS
VMEM_LIMIT = 56 * 1024 * 1024
MOE_ROWS = 256
ATTN_ROW_SLAB = 64
LN_ROW_SLAB = 16
XATTN_ROW_SLAB = 64
MID_PARTS = 2


def _cp(sem, vmem=VMEM_LIMIT):
    return pltpu.CompilerParams(dimension_semantics=sem, vmem_limit_bytes=vmem)


def _pick_tile(n, pref):
    t = min(n, pref)
    while n % t:
        t //= 2
    return t


def _with_ones_lane(v):
    lane = lax.broadcasted_iota(I32, v.shape, 1)
    return jnp.where((lane & (HEAD_PAD - 1)) == V_HEAD, 1.0, v)


def _inproj_body(x_ref, cinit_ref, cos_ref, sin_ref, win_ref, qg_ref, kvg_ref, wuq_ref, wukv_ref, cw_ref,
                 q_ref, k_ref, v_ref, yc_ref, lat_ref, kr_ref, cst_ref, u_scr, *, tt, nj):
    j = pl.program_id(1)

    @pl.when(j == 0)
    def _():
        u_scr[6:8, :] = cinit_ref[0]

    x = x_ref[0].astype(BF16)
    proj = jnp.dot(x, win_ref[...], preferred_element_type=F32)
    cq = proj[:, 0:256]
    ckv = proj[:, 256:384]
    gb = proj[:, 384:896]
    gc = proj[:, 896:1408]
    gv = proj[:, 1408:1920]
    kr_a = proj[:, 1920:2048]
    kr_b = proj[:, 2048:2176]
    cos_t = cos_ref[...]
    sin_t = sin_ref[...]
    cqn = cq * lax.rsqrt(jnp.mean(cq * cq, -1, keepdims=True) + RMS_EPS) * qg_ref[...]
    ckvn = ckv * lax.rsqrt(jnp.mean(ckv * ckv, -1, keepdims=True) + RMS_EPS) * kvg_ref[...]
    lat_ref[0] = ckvn
    kr_p = kr_a * cos_t + kr_b * sin_t
    kr_ref[0] = kr_p[:, ROPE_LANE0:ROPE_LANE0 + QK_ROPE]
    qq = jnp.dot(cqn.astype(BF16), wuq_ref[...], preferred_element_type=F32)
    kv = jnp.dot(ckvn.astype(BF16), wukv_ref[...], preferred_element_type=F32)
    hw = N_HEADS * HEAD_PAD
    for h in range(N_HEADS):
        sl = slice(h * HEAD_PAD, (h + 1) * HEAD_PAD)
        sl_b = slice(hw + h * HEAD_PAD, hw + (h + 1) * HEAD_PAD)
        q_ref[0, :, sl] = (qq[:, sl] * cos_t + qq[:, sl_b] * sin_t).astype(BF16)
        k_ref[0, :, sl] = (kv[:, sl] + kr_p).astype(BF16)
    v_ref[0] = _with_ones_lane(kv[:, hw:2 * hw]).astype(BF16)
    u = gc * gv
    u_scr[8:8 + tt, :] = u
    conv = cw_ref[0:1, :] * u_scr[6:6 + tt, :] + cw_ref[1:2, :] * u_scr[7:7 + tt, :] + cw_ref[2:3, :] * u
    yc_ref[0] = (gb * conv).astype(BF16)
    last2 = u_scr[tt + 6:tt + 8, :]
    u_scr[6:8, :] = last2

    @pl.when(j == nj - 1)
    def _():
        cst_ref[0] = last2


def _inproj(x, conv_init, cos_t, sin_t, w):
    b, s, _ = x.shape
    tt = _pick_tile(s, 512)
    nj = s // tt
    wn = w['w_in'].shape[1]
    full = lambda shape: pl.BlockSpec(shape, lambda bi, ji: (0,) * len(shape))
    out_shapes = (
        jax.ShapeDtypeStruct((b, s, N_HEADS * HEAD_PAD), BF16),
        jax.ShapeDtypeStruct((b, s, N_HEADS * HEAD_PAD), BF16),
        jax.ShapeDtypeStruct((b, s, N_HEADS * HEAD_PAD), BF16),
        jax.ShapeDtypeStruct((b, s, CONV_WIDTH), BF16),
        jax.ShapeDtypeStruct((b, s, KV_LORA), F32),
        jax.ShapeDtypeStruct((b, s, QK_ROPE), F32),
        jax.ShapeDtypeStruct((b, CONV_K - 1, CONV_WIDTH), F32),
    )
    row = lambda width: pl.BlockSpec((1, tt, width), lambda bi, ji: (bi, ji, 0))
    return pl.pallas_call(
        functools.partial(_inproj_body, tt=tt, nj=nj),
        grid=(b, nj),
        in_specs=[
            row(D_MODEL),
            pl.BlockSpec((1, CONV_K - 1, CONV_WIDTH), lambda bi, ji: (bi, 0, 0)),
            pl.BlockSpec((tt, LANES), lambda bi, ji: (ji, 0)),
            pl.BlockSpec((tt, LANES), lambda bi, ji: (ji, 0)),
            full((D_MODEL, wn)),
            full((1, Q_LORA)),
            full((1, KV_LORA)),
            full(w['w_uq'].shape),
            full(w['w_ukv'].shape),
            full((CONV_K, CONV_WIDTH)),
        ],
        out_specs=(
            row(N_HEADS * HEAD_PAD), row(N_HEADS * HEAD_PAD), row(N_HEADS * HEAD_PAD), row(CONV_WIDTH),
            row(KV_LORA), row(QK_ROPE),
            pl.BlockSpec((1, CONV_K - 1, CONV_WIDTH), lambda bi, ji: (bi, 0, 0)),
        ),
        out_shape=out_shapes,
        scratch_shapes=[pltpu.VMEM((tt + 8, CONV_WIDTH), F32)],
        compiler_params=_cp(("parallel", "arbitrary")),
        name="inproj",
    )(x, conv_init, cos_t, sin_t, w['w_in'], w['q_norm_g'], w['kv_norm_g'], w['w_uq'], w['w_ukv'], w['conv_w'])


def _kvup_body(lat_ref, krp_ref, wukv_ref, k_ref, v_ref):
    kv = jnp.dot(lat_ref[0].astype(BF16), wukv_ref[...], preferred_element_type=F32)
    kr_p = krp_ref[0]
    hw = N_HEADS * HEAD_PAD
    for h in range(N_HEADS):
        sl = slice(h * HEAD_PAD, (h + 1) * HEAD_PAD)
        k_ref[0, :, sl] = (kv[:, sl] + kr_p).astype(BF16)
    v_ref[0] = _with_ones_lane(kv[:, hw:2 * hw]).astype(BF16)


def _kvup(lat, kr_padded, w_ukv):
    b, s, _ = lat.shape
    tt = _pick_tile(s, 512)
    row = lambda width: pl.BlockSpec((1, tt, width), lambda bi, ji: (bi, ji, 0))
    return pl.pallas_call(
        _kvup_body,
        grid=(b, s // tt),
        in_specs=[row(KV_LORA), row(LANES), pl.BlockSpec(w_ukv.shape, lambda bi, ji: (0, 0))],
        out_specs=(row(N_HEADS * HEAD_PAD), row(N_HEADS * HEAD_PAD)),
        out_shape=(jax.ShapeDtypeStruct((b, s, N_HEADS * HEAD_PAD), BF16),
                   jax.ShapeDtypeStruct((b, s, N_HEADS * HEAD_PAD), BF16)),
        compiler_params=_cp(("parallel", "parallel")),
        name="kvup",
    )(lat, kr_padded, w_ukv)


def _attn_body(qi_ref, ki_ref, fl_ref, q_ref, k_ref, v_ref, o_ref, m_scr, acc_scr, s_scr, p_scr, a_scr,
               *, tq, tk, rs, q_pos0, n_valid):
    step = pl.program_id(1)
    qi = qi_ref[step]
    ki = ki_ref[step]
    flags = fl_ref[step]
    c_exp = MLA_SCALE * math.log2(math.e)
    reps = tk // LANES

    @pl.when((flags & 1) != 0)
    def _():
        m_scr[...] = jnp.full(m_scr.shape, -jnp.inf, F32)
        acc_scr[...] = jnp.zeros(acc_scr.shape, F32)

    def scores(h):
        hs = slice(h * HEAD_PAD, (h + 1) * HEAD_PAD)
        s_scr[h % 2] = lax.dot_general(q_ref[0, :, hs], k_ref[0, :, hs], (((1,), (1,)), ((), ())),
                                       preferred_element_type=F32)

    def tile(masked):
        scores(0)
        for h in range(N_HEADS):
            hs = slice(h * HEAD_PAD, (h + 1) * HEAD_PAD)
            b2 = h % 2
            if h + 1 < N_HEADS:
                scores(h + 1)
            for r in range(tq // rs):
                rows = slice(r * rs, (r + 1) * rs)
                s_r = s_scr[b2, rows, :]
                if masked:
                    qpos = q_pos0 + qi * tq + r * rs + lax.broadcasted_iota(I32, (rs, tk), 0)
                    kpos = ki * tk + lax.broadcasted_iota(I32, (rs, tk), 1)
                    mask = (kpos >> 6) <= (qpos >> 6)
                    if n_valid is not None:
                        mask = mask & (kpos < n_valid)
                    s_r = jnp.where(mask, s_r, -jnp.inf)
                m_old = m_scr[h, rows, :]
                m_new = jnp.maximum(m_old, jnp.max(s_r, axis=-1, keepdims=True))
                a_scr[b2, rows, :] = jnp.exp2((m_old - m_new) * c_exp)
                m_rep = jnp.concatenate([m_new] * reps, axis=1)
                p_scr[b2, rows, :] = jnp.exp2((s_r - m_rep) * c_exp).astype(BF16)
                m_scr[h, rows, :] = m_new
            pv = jnp.dot(p_scr[b2], v_ref[0, :, hs], preferred_element_type=F32)
            acc_scr[h] = a_scr[b2] * acc_scr[h] + pv

    @pl.when((flags & 4) != 0)
    def _():
        tile(True)

    @pl.when((flags & 4) == 0)
    def _():
        tile(False)

    @pl.when((flags & 2) != 0)
    def _():
        for h in range(N_HEADS):
            acc = acc_scr[h]
            o_ref[0, :, h * V_HEAD:(h + 1) * V_HEAD] = (acc[:, 0:V_HEAD] / acc[:, V_HEAD:V_HEAD + 1]).astype(BF16)


def _attn_tables(nq, nk, tq, tk, q_pos0, n_valid):
    qi_l, ki_l, fl_l = [], [], []
    for qi in range(nq):
        q_lo = q_pos0 + qi * tq
        q_hi = q_lo + tq - 1
        k_last_pos = (q_hi // CHUNK) * CHUNK + CHUNK - 1
        if n_valid is not None:
            k_last_pos = min(k_last_pos, n_valid - 1)
        k_last = min(nk - 1, k_last_pos // tk)
        for ki in range(k_last + 1):
            k_hi = ki * tk + tk - 1
            masked = (k_hi // CHUNK) > (q_lo // CHUNK) or (n_valid is not None and k_hi >= n_valid)
            fl = (1 if ki == 0 else 0) | (2 if ki == k_last else 0) | (4 if masked else 0)
            qi_l.append(qi); ki_l.append(ki); fl_l.append(fl)
    return (jnp.asarray(np.array(qi_l, np.int32)), jnp.asarray(np.array(ki_l, np.int32)),
            jnp.asarray(np.array(fl_l, np.int32)))


def _attention(q, k, v, q_pos0, n_valid, tq_pref=512, tk_pref=512):
    b, sq, _ = q.shape
    sk = k.shape[1]
    tq = _pick_tile(sq, tq_pref)
    tk = _pick_tile(sk, tk_pref)
    qi_t, ki_t, fl_t = _attn_tables(sq // tq, sk // tk, tq, tk, q_pos0, n_valid)
    n_steps = int(qi_t.shape[0])
    grid_spec = pltpu.PrefetchScalarGridSpec(
        num_scalar_prefetch=3,
        grid=(b, n_steps),
        in_specs=[
            pl.BlockSpec((1, tq, N_HEADS * HEAD_PAD), lambda bi, si, qt, kt, ft: (bi, qt[si], 0)),
            pl.BlockSpec((1, tk, N_HEADS * HEAD_PAD), lambda bi, si, qt, kt, ft: (bi, kt[si], 0)),
            pl.BlockSpec((1, tk, N_HEADS * HEAD_PAD), lambda bi, si, qt, kt, ft: (bi, kt[si], 0)),
        ],
        out_specs=pl.BlockSpec((1, tq, N_HEADS * V_HEAD), lambda bi, si, qt, kt, ft: (bi, qt[si], 0)),
        scratch_shapes=[
            pltpu.VMEM((N_HEADS, tq, LANES), F32),
            pltpu.VMEM((N_HEADS, tq, LANES), F32),
            pltpu.VMEM((2, tq, tk), F32),
            pltpu.VMEM((2, tq, tk), BF16),
            pltpu.VMEM((2, tq, LANES), F32),
        ],
    )
    return pl.pallas_call(
        functools.partial(_attn_body, tq=tq, tk=tk, rs=min(tq, ATTN_ROW_SLAB), q_pos0=q_pos0, n_valid=n_valid),
        grid_spec=grid_spec,
        out_shape=jax.ShapeDtypeStruct((b, sq, N_HEADS * V_HEAD), BF16),
        compiler_params=_cp(("parallel", "arbitrary")),
        name="mla_attn",
    )(qi_t, ki_t, fl_t, q, k, v)


def _memkv_body(mem_ref, wk_ref, wv_ref, mk_ref, mv_ref, mkb_ref, mvb_ref):
    m = mem_ref[...].astype(BF16)
    mk = jnp.dot(m, wk_ref[...], preferred_element_type=F32)
    mv = jnp.dot(m, wv_ref[...], preferred_element_type=F32)
    mk_ref[...] = mk
    mv_ref[...] = mv
    mkb_ref[...] = mk.astype(BF16)
    mvb_ref[...] = mv.astype(BF16)


def _memkv(mem2d, w_xk, w_xv):
    n = mem2d.shape[0]
    tt = _pick_tile(n, 256)
    row = pl.BlockSpec((tt, D_MODEL), lambda i: (i, 0))
    wspec = pl.BlockSpec((D_MODEL, D_MODEL), lambda i: (0, 0))
    return pl.pallas_call(
        _memkv_body,
        grid=(n // tt,),
        in_specs=[row, wspec, wspec],
        out_specs=(row, row, row, row),
        out_shape=(jax.ShapeDtypeStruct((n, D_MODEL), F32), jax.ShapeDtypeStruct((n, D_MODEL), F32),
                   jax.ShapeDtypeStruct((n, D_MODEL), BF16), jax.ShapeDtypeStruct((n, D_MODEL), BF16)),
        compiler_params=_cp(("parallel",)),
        name="memkv",
    )(mem2d, w_xk, w_xv)


def _layer_norm(x, g, b):
    mu = jnp.mean(x, -1, keepdims=True)
    xc = x - mu
    var = jnp.mean(xc * xc, -1, keepdims=True)
    return xc * lax.rsqrt(var + LN_EPS) * g + b


def _mid_body(x_ref, at_ref, yc_ref, mk_ref, mv_ref, cnt0_ref, low_ref, wo_ref, g1_ref, b1_ref, wq_ref, wxo_ref,
              g2_ref, b2_ref, wr_ref, br_ref, x2_ref, ri_ref, cnt_ref,
              cnt_scr, a_scr, x1_scr, xb_scr, q_scr, sc_scr, p_scr, o_scr, lg_scr, *, tt, alpha):
    first = (pl.program_id(0) == 0) & (pl.program_id(1) == 0)

    @pl.when(first)
    def _():
        cnt_scr[...] = cnt0_ref[...]

    n_part = MID_PARTS if tt % (MID_PARTS * SUBLANES * 2) == 0 else 1
    pr = tt // n_part
    ln_rs = min(pr, LN_ROW_SLAB)
    sm_rs = min(pr, XATTN_ROW_SLAB)
    c_exp = X_SCALE * math.log2(math.e)
    lane = lax.broadcasted_iota(I32, (pr, LANES), 1)

    def part_rows(k):
        return slice(k * pr, (k + 1) * pr)

    def out_proj(k):
        rp = part_rows(k)
        mix = jnp.concatenate([at_ref[0, rp, :], yc_ref[0, rp, :]], axis=-1)
        a_scr[rp, :] = jnp.dot(mix, wo_ref[...], preferred_element_type=F32)

    def norm1(k):
        for r in range(pr // ln_rs):
            rows = slice(k * pr + r * ln_rs, k * pr + (r + 1) * ln_rs)
            x1 = _layer_norm(alpha * x_ref[0, rows, :] + a_scr[rows, :], g1_ref[...], b1_ref[...])
            x1_scr[rows, :] = x1
            xb_scr[rows, :] = x1.astype(BF16)

    def q_proj(k):
        rp = part_rows(k)
        q_scr[rp, :] = jnp.dot(xb_scr[rp, :], wq_ref[...], preferred_element_type=F32).astype(BF16)

    def cross_attn(k):
        rp = part_rows(k)
        for h in range(X_HEADS):
            sl = slice(h * X_HEAD_DIM, (h + 1) * X_HEAD_DIM)
            b2 = h % 2
            sc_scr[k, b2] = lax.dot_general(q_scr[rp, sl], mk_ref[0, :, sl], (((1,), (1,)), ((), ())),
                                            preferred_element_type=F32)
            for r in range(pr // sm_rs):
                rows = slice(r * sm_rs, (r + 1) * sm_rs)
                s_r = sc_scr[k, b2, rows, :]
                e = jnp.exp2((s_r - jnp.max(s_r, -1, keepdims=True)) * c_exp)
                p_scr[k, b2, rows, :] = (e / jnp.sum(e, -1, keepdims=True)).astype(BF16)
            o_scr[rp, sl] = jnp.dot(p_scr[k, b2], mv_ref[0, :, sl], preferred_element_type=F32).astype(BF16)

    def x_out_proj(k):
        rp = part_rows(k)
        a_scr[rp, :] = jnp.dot(o_scr[rp, :], wxo_ref[...], preferred_element_type=F32)

    def norm2(k):
        for r in range(pr // ln_rs):
            rows = slice(k * pr + r * ln_rs, k * pr + (r + 1) * ln_rs)
            x2 = _layer_norm(alpha * x1_scr[rows, :] + a_scr[rows, :], g2_ref[...], b2_ref[...])
            x2_ref[0, rows, :] = x2
            xb_scr[rows, :] = x2.astype(BF16)

    def router_logits(k):
        rp = part_rows(k)
        lg_scr[rp, :] = jnp.dot(xb_scr[rp, :], wr_ref[...], preferred_element_type=F32) + br_ref[...]

    def route(k):
        rp = part_rows(k)
        logits = lg_scr[rp, :]
        neg = -jnp.inf
        is_g = lane < N_GROUPS
        lg = jnp.where(is_g, logits, neg)
        mg = jnp.max(lg, -1, keepdims=True)
        g_idx = jnp.min(jnp.where(lg == mg, lane, LANES), -1, keepdims=True)
        pg = 1.0 / jnp.sum(jnp.where(is_g, jnp.exp(logits - mg), 0.0), -1, keepdims=True)
        in_grp = ((lane >= ROUTER_LANE0) & (lane < ROUTER_LANE0 + N_EXPERTS)
                  & (((lane - ROUTER_LANE0) >> 3) == g_idx))
        le = jnp.where(in_grp, logits, neg)
        v1 = jnp.max(le, -1, keepdims=True)
        i1 = jnp.min(jnp.where(le == v1, lane, LANES), -1, keepdims=True)
        le2 = jnp.where(lane == i1, neg, le)
        v2 = jnp.max(le2, -1, keepdims=True)
        i2 = jnp.min(jnp.where(le2 == v2, lane, LANES), -1, keepdims=True)
        e2 = jnp.exp(v2 - v1)
        den = 1.0 + e2
        gate1 = (1.0 / den) * pg
        gate2 = (e2 / den) * pg
        oh1 = (lane == i1).astype(F32)
        oh2 = (lane == i2).astype(F32)
        oh = oh1 + oh2
        base = cnt_scr[...] + jnp.dot(low_ref[...], oh.astype(BF16), preferred_element_type=F32)
        rank1 = jnp.sum(oh1 * base, -1, keepdims=True)
        rank2 = jnp.sum(oh2 * base, -1, keepdims=True)
        cnt_scr[...] = cnt_scr[...] + jnp.sum(oh, 0, keepdims=True)
        e1f = (i1 - ROUTER_LANE0).astype(F32)
        e2f = (i2 - ROUTER_LANE0).astype(F32)
        ri_ref[0, rp, :] = jnp.where(lane == 0, e1f, jnp.where(lane == 1, e2f, jnp.where(
            lane == 2, rank1, jnp.where(lane == 3, rank2, jnp.where(lane == 4, gate1, jnp.where(
                lane == 5, gate2, 0.0))))))

    stages = (out_proj, norm1, q_proj, cross_attn, x_out_proj, norm2, router_logits, route)
    matmul_stages = (out_proj, q_proj, x_out_proj, router_logits)
    for t in range(len(stages) + n_part - 1):
        todo = [(stages[t - k], k) for k in range(n_part) if 0 <= t - k < len(stages)]
        for fn, k in sorted(todo, key=lambda fk: fk[0] not in matmul_stages):
            fn(k)
    cnt_ref[...] = cnt_scr[...]


def _mid(x, attn, yconv, mk_b, mv_b, cnt0, w, alpha):
    b, s, _ = x.shape
    tt = _pick_tile(s, 512)
    row = lambda width: pl.BlockSpec((1, tt, width), lambda bi, ji: (bi, ji, 0))
    full = lambda shape: pl.BlockSpec(shape, lambda bi, ji: (0,) * len(shape))
    mem = pl.BlockSpec((1, N_MEM, D_MODEL), lambda bi, ji: (bi, 0, 0))
    vec = full((1, D_MODEL))
    n_part = MID_PARTS if tt % (MID_PARTS * SUBLANES * 2) == 0 else 1
    pr = tt // n_part
    lower = jnp.tril(jnp.ones((pr, pr), F32), -1).astype(BF16)
    return pl.pallas_call(
        functools.partial(_mid_body, tt=tt, alpha=alpha),
        grid=(b, s // tt),
        in_specs=[row(D_MODEL), row(N_HEADS * V_HEAD), row(CONV_WIDTH), mem, mem, full((1, LANES)), full((pr, pr)),
                  full((D_MODEL, D_MODEL)), vec, vec, full((D_MODEL, D_MODEL)), full((D_MODEL, D_MODEL)),
                  vec, vec, full((D_MODEL, LANES)), full((1, LANES))],
        out_specs=(row(D_MODEL), row(LANES), full((1, LANES))),
        out_shape=(jax.ShapeDtypeStruct((b, s, D_MODEL), F32), jax.ShapeDtypeStruct((b, s, LANES), F32),
                   jax.ShapeDtypeStruct((1, LANES), F32)),
        scratch_shapes=[
            pltpu.VMEM((1, LANES), F32),
            pltpu.VMEM((tt, D_MODEL), F32),
            pltpu.VMEM((tt, D_MODEL), F32),
            pltpu.VMEM((tt, D_MODEL), BF16),
            pltpu.VMEM((tt, D_MODEL), BF16),
            pltpu.VMEM((n_part, 2, pr, N_MEM), F32),
            pltpu.VMEM((n_part, 2, pr, N_MEM), BF16),
            pltpu.VMEM((tt, D_MODEL), BF16),
            pltpu.VMEM((tt, LANES), F32),
        ],
        compiler_params=_cp(("arbitrary", "arbitrary")),
        name="mid",
    )(x, attn, yconv, mk_b, mv_b, cnt0, lower, w['w_out'], w['ln1_g'], w['ln1_b'], w['w_xq'], w['w_xo'],
      w['ln2_g'], w['ln2_b'], w['w_router'], w['b_router'])


def _row_slice(ref, row):
    return ref.at[pl.ds(pl.multiple_of(row * SUBLANES, SUBLANES), SUBLANES), :]


def _dispatch_body(dest_ref, x_ref, xs_in_ref, xs_ref, buf0, buf1, sem0, sem1, *, tt, nt):
    del xs_in_ref
    i = pl.program_id(0)

    def wait_all(buf, sem):
        for _ in range(2):
            pltpu.make_async_copy(buf, xs_ref.at[pl.ds(0, tt * SUBLANES), :], sem).wait()

    def run(buf, sem, obuf, osem):
        for c in range(ROW_CHUNKS):
            buf[pl.ds(c, tt, stride=SUBLANES), :] = x_ref[:, c * LANES:(c + 1) * LANES]

        def issue(t, carry):
            src = _row_slice(buf, t)
            for kk in range(2):
                d = dest_ref[0, 0, 2 * t + kk]
                pltpu.make_async_copy(src, _row_slice(xs_ref, d), sem).start()
            return carry

        lax.fori_loop(0, tt, issue, 0)

        @pl.when(i > 0)
        def _():
            wait_all(obuf, osem)

        @pl.when(i == nt - 1)
        def _():
            wait_all(buf, sem)

    @pl.when(i % 2 == 0)
    def _():
        run(buf0, sem0, buf1, sem1)

    @pl.when(i % 2 == 1)
    def _():
        run(buf1, sem1, buf0, sem0)


def _dispatch(x2d, dest, xs):
    n = x2d.shape[0]
    tt = _pick_tile(n, 256)
    nt = n // tt
    dest3 = dest.reshape(nt, 1, 2 * tt)
    return pl.pallas_call(
        functools.partial(_dispatch_body, tt=tt, nt=nt),
        grid=(nt,),
        in_specs=[
            pl.BlockSpec((1, 1, 2 * tt), lambda i: (i, 0, 0), memory_space=pltpu.SMEM),
            pl.BlockSpec((tt, D_MODEL), lambda i: (i, 0)),
            pl.BlockSpec(memory_space=pl.ANY),
        ],
        out_specs=pl.BlockSpec(memory_space=pl.ANY),
        out_shape=jax.ShapeDtypeStruct(xs.shape, xs.dtype),
        scratch_shapes=[pltpu.VMEM((tt * SUBLANES, LANES), F32), pltpu.VMEM((tt * SUBLANES, LANES), F32),
                        pltpu.SemaphoreType.DMA, pltpu.SemaphoreType.DMA],
        input_output_aliases={2: 0},
        compiler_params=_cp(("arbitrary",)),
        name="moe_dispatch",
    )(dest3, x2d, xs)


def _experts_body(be_ref, nu_ref, x_ref, wg_ref, wu_ref, wd_ref, y_ref, *, rows):
    i = pl.program_id(0)

    @pl.when(i < nu_ref[0])
    def _():
        xb = jnp.concatenate([x_ref[pl.ds(c, rows, stride=SUBLANES), :] for c in range(ROW_CHUNKS)],
                             axis=-1).astype(BF16)
        g = jnp.dot(xb, wg_ref[0], preferred_element_type=F32)
        u = jnp.dot(xb, wu_ref[0], preferred_element_type=F32)
        hdn = (g * jax.nn.sigmoid(g)) * u
        y = jnp.dot(hdn.astype(BF16), wd_ref[0], preferred_element_type=F32)
        for c in range(ROW_CHUNKS):
            y_ref[pl.ds(c, rows, stride=SUBLANES), :] = y[:, c * LANES:(c + 1) * LANES]

    @pl.when(i >= nu_ref[0])
    def _():
        y_ref[...] = jnp.zeros(y_ref.shape, F32)


def _experts(xs, block_e, n_used, wg, wu, wd, rows):
    nblk = xs.shape[0] // (rows * SUBLANES)
    clamp = lambda i, nu: jnp.minimum(i, nu[0] - 1)
    grid_spec = pltpu.PrefetchScalarGridSpec(
        num_scalar_prefetch=2,
        grid=(nblk,),
        in_specs=[
            pl.BlockSpec((rows * SUBLANES, LANES), lambda i, be, nu: (clamp(i, nu), 0)),
            pl.BlockSpec((1, D_MODEL, D_EXPERT), lambda i, be, nu: (be[clamp(i, nu)], 0, 0)),
            pl.BlockSpec((1, D_MODEL, D_EXPERT), lambda i, be, nu: (be[clamp(i, nu)], 0, 0)),
            pl.BlockSpec((1, D_EXPERT, D_MODEL), lambda i, be, nu: (be[clamp(i, nu)], 0, 0)),
        ],
        out_specs=pl.BlockSpec((rows * SUBLANES, LANES), lambda i, be, nu: (i, 0)),
    )
    return pl.pallas_call(
        functools.partial(_experts_body, rows=rows),
        grid_spec=grid_spec,
        out_shape=jax.ShapeDtypeStruct(xs.shape, F32),
        compiler_params=_cp(("arbitrary",)),
        name="moe_experts",
    )(block_e, n_used, xs, wg, wu, wd)


def _combine_body(dest_ref, x_ref, ri_ref, g3_ref, b3_ref, ys_ref, o_ref, buf0, buf1, sem, *, tt, alpha):
    bufs = (buf0, buf1)

    def issue(t, carry):
        for kk in range(2):
            d = dest_ref[0, 0, 2 * t + kk]
            pltpu.make_async_copy(_row_slice(ys_ref, d), _row_slice(bufs[kk], t), sem).start()
        return carry

    lax.fori_loop(0, tt, issue, 0)
    for kk in range(2):
        pltpu.make_async_copy(ys_ref.at[pl.ds(0, tt * SUBLANES), :], bufs[kk], sem).wait()
    ri = ri_ref[...]
    y0 = jnp.concatenate([buf0[pl.ds(c, tt, stride=SUBLANES), :] for c in range(ROW_CHUNKS)], axis=-1)
    y1 = jnp.concatenate([buf1[pl.ds(c, tt, stride=SUBLANES), :] for c in range(ROW_CHUNKS)], axis=-1)
    moe = y0 * ri[:, 4:5] + y1 * ri[:, 5:6]
    o_ref[...] = _layer_norm(alpha * x_ref[...] + moe, g3_ref[...], b3_ref[...])


def _combine(x2d, rinfo, dest, ys, g3, b3, alpha):
    n = x2d.shape[0]
    tt = _pick_tile(n, 256)
    nt = n // tt
    dest3 = dest.reshape(nt, 1, 2 * tt)
    vec = pl.BlockSpec((1, D_MODEL), lambda i: (0, 0))
    return pl.pallas_call(
        functools.partial(_combine_body, tt=tt, alpha=alpha),
        grid=(nt,),
        in_specs=[
            pl.BlockSpec((1, 1, 2 * tt), lambda i: (i, 0, 0), memory_space=pltpu.SMEM),
            pl.BlockSpec((tt, D_MODEL), lambda i: (i, 0)),
            pl.BlockSpec((tt, LANES), lambda i: (i, 0)),
            vec, vec,
            pl.BlockSpec(memory_space=pl.ANY),
        ],
        out_specs=pl.BlockSpec((tt, D_MODEL), lambda i: (i, 0)),
        out_shape=jax.ShapeDtypeStruct((n, D_MODEL), F32),
        scratch_shapes=[pltpu.VMEM((tt * SUBLANES, LANES), F32), pltpu.VMEM((tt * SUBLANES, LANES), F32),
                        pltpu.SemaphoreType.DMA],
        compiler_params=_cp(("arbitrary",)),
        name="moe_combine",
    )(dest3, x2d, rinfo, g3, b3, ys)


def _rope_tables(pos):
    half = QK_ROPE // 2
    inv = ROPE_THETA ** (-jnp.arange(half, dtype=F32) / half)
    ang = pos.astype(F32)[:, None] * inv[None, :]
    cos, sin = jnp.cos(ang), jnp.sin(ang)
    n = pos.shape[0]
    pad_r = LANES - ROPE_LANE0 - QK_ROPE
    cos_t = jnp.concatenate([jnp.ones((n, ROPE_LANE0), F32), cos, cos, jnp.ones((n, pad_r), F32)], -1)
    sin_t = jnp.concatenate([jnp.zeros((n, ROPE_LANE0), F32), sin, sin, jnp.zeros((n, pad_r), F32)], -1)
    return cos_t, sin_t


def _swap_neg(wr):
    half = QK_ROPE // 2
    return jnp.concatenate([-wr[:, half:], wr[:, :half]], axis=1)


def _prep_weights(l, w_in, q_norm_g, kv_norm_g, w_uq, w_ukv, conv_w, w_out, ln1_g, ln1_b, w_xq, w_xk, w_xv, w_xo,
                  ln2_g, ln2_b, w_router_group, b_router_group, w_router_expert, b_router_expert,
                  w_exp_gate, w_exp_up, w_exp_down, ln3_g, ln3_b):
    wi = w_in[l]
    c0 = Q_LORA + KV_LORA
    w_kr = wi[:, c0:c0 + QK_ROPE]
    zl = jnp.zeros((D_MODEL, ROPE_LANE0), F32)
    zr = jnp.zeros((D_MODEL, LANES - ROPE_LANE0 - QK_ROPE), F32)
    w_in_p = jnp.concatenate([wi[:, :c0], wi[:, c0 + QK_ROPE:], zl, w_kr, zr, zl, _swap_neg(w_kr), zr], axis=1)
    wq = w_uq[l].reshape(Q_LORA, N_HEADS, QK_NOPE + QK_ROPE)
    zq = jnp.zeros((Q_LORA, N_HEADS, HEAD_PAD - QK_NOPE - QK_ROPE), F32)
    wq_a = jnp.concatenate([wq, zq], axis=2).reshape(Q_LORA, N_HEADS * HEAD_PAD)
    wq_rot = jnp.concatenate([-wq[:, :, QK_NOPE + QK_ROPE // 2:], wq[:, :, QK_NOPE:QK_NOPE + QK_ROPE // 2]], axis=2)
    wq_b = jnp.concatenate([jnp.zeros((Q_LORA, N_HEADS, QK_NOPE), F32), wq_rot, zq], axis=2)
    wq_b = wq_b.reshape(Q_LORA, N_HEADS * HEAD_PAD)
    wkv = w_ukv[l].reshape(KV_LORA, N_HEADS, QK_NOPE + V_HEAD)
    wk_p = jnp.concatenate([wkv[:, :, :QK_NOPE], jnp.zeros((KV_LORA, N_HEADS, HEAD_PAD - QK_NOPE), F32)], axis=2)
    wk_p = wk_p.reshape(KV_LORA, N_HEADS * HEAD_PAD)
    wv_p = jnp.concatenate([wkv[:, :, QK_NOPE:], jnp.zeros((KV_LORA, N_HEADS, HEAD_PAD - V_HEAD), F32)], axis=2)
    wv_p = wv_p.reshape(KV_LORA, N_HEADS * HEAD_PAD)
    w_router = jnp.concatenate([w_router_group[l], w_router_expert[l],
                                jnp.zeros((D_MODEL, LANES - N_GROUPS - N_EXPERTS), F32)], axis=1)
    b_router = jnp.concatenate([b_router_group[l], b_router_expert[l].reshape(-1),
                                jnp.zeros((LANES - N_GROUPS - N_EXPERTS,), F32)]).reshape(1, LANES)
    return dict(
        w_in=w_in_p.astype(BF16),
        q_norm_g=q_norm_g[l].reshape(1, Q_LORA), kv_norm_g=kv_norm_g[l].reshape(1, KV_LORA),
        w_uq=jnp.concatenate([wq_a, wq_b], axis=1).astype(BF16),
        w_ukv=jnp.concatenate([wk_p, wv_p], axis=1).astype(BF16),
        conv_w=conv_w[l],
        w_out=w_out[l].astype(BF16), ln1_g=ln1_g[l].reshape(1, -1), ln1_b=ln1_b[l].reshape(1, -1),
        w_xq=w_xq[l].astype(BF16), w_xk=w_xk[l].astype(BF16), w_xv=w_xv[l].astype(BF16),
        w_xo=w_xo[l].astype(BF16), ln2_g=ln2_g[l].reshape(1, -1), ln2_b=ln2_b[l].reshape(1, -1),
        w_router=w_router.astype(BF16), b_router=b_router,
        w_exp_gate=w_exp_gate[l].astype(BF16), w_exp_up=w_exp_up[l].astype(BF16),
        w_exp_down=w_exp_down[l].astype(BF16),
        ln3_g=ln3_g[l].reshape(1, -1), ln3_b=ln3_b[l].reshape(1, -1),
    )


def _route_cols(rinfo2d):
    eid = rinfo2d[:, 0:2].astype(I32)
    rank = rinfo2d[:, 2:4].astype(I32)
    return eid, rank


def _layer(l, depth, xp, xs, lat_past, kr_past, conv_past, mk_s, mv_s, mem_prompt, w):
    alpha = (2 * depth) ** 0.25
    b, s, _ = xp.shape
    bs, ss, _ = xs.shape
    past = lat_past.shape[1]

    cos_p, sin_p = _rope_tables(jnp.arange(s))
    q_p, k_p, v_p, yc_p, lat_p, kr_p, cst_p = _inproj(
        xp, jnp.zeros((b, CONV_K - 1, CONV_WIDTH), F32), cos_p, sin_p, w)
    attn_p = _attention(q_p, k_p, v_p, 0, None)
    mk, mv, mk_b, mv_b = _memkv(mem_prompt.reshape(b * N_MEM, D_MODEL), w['w_xk'], w['w_xv'])
    cnt0 = jnp.zeros((1, LANES), F32)
    x2_p, ri_p, cnt_p = _mid(xp, attn_p, yc_p, mk_b.reshape(b, N_MEM, D_MODEL), mv_b.reshape(b, N_MEM, D_MODEL),
                             cnt0, w, alpha)

    cos_s, sin_s = _rope_tables(past + jnp.arange(ss))
    q_s, _, _, yc_s, lat_s, kr_s, cst_s = _inproj(xs, conv_past, cos_s, sin_s, w)
    n_keys = past + ss
    sk = -(-n_keys // 512) * 512
    lat_all = jnp.concatenate([lat_past, lat_s, jnp.zeros((bs, sk - n_keys, KV_LORA), F32)], axis=1)
    kr_all = jnp.concatenate([kr_past, kr_s, jnp.zeros((bs, sk - n_keys, QK_ROPE), F32)], axis=1)
    kr_all = jnp.pad(kr_all, ((0, 0), (0, 0), (ROPE_LANE0, LANES - ROPE_LANE0 - QK_ROPE)))
    k_s, v_s = _kvup(lat_all, kr_all, w['w_ukv'])
    attn_s = _attention(q_s, k_s, v_s, past, n_keys)
    mk_sb = mk_s.reshape(bs, N_MEM, D_MODEL).astype(BF16)
    mv_sb = mv_s.reshape(bs, N_MEM, D_MODEL).astype(BF16)
    x2_s, ri_s, cnt = _mid(xs, attn_s, yc_s, mk_sb, mv_sb, cnt_p, w, alpha)

    n_p, n_s = b * s, bs * ss
    counts = cnt[0, ROUTER_LANE0:ROUTER_LANE0 + N_EXPERTS].astype(I32)
    padded = (counts + MOE_ROWS - 1) // MOE_ROWS * MOE_ROWS
    pends = jnp.cumsum(padded)
    pstarts = pends - padded
    nblk = -(-2 * (n_p + n_s) // MOE_ROWS) + N_EXPERTS
    blk_start = jnp.arange(nblk, dtype=I32) * MOE_ROWS
    block_e = jnp.minimum(jnp.sum((pends[None, :] <= blk_start[:, None]).astype(I32), axis=1), N_EXPERTS - 1)
    n_used = (pends[-1] // MOE_ROWS).astype(I32).reshape(1)
    ri_p2, ri_s2 = ri_p.reshape(n_p, LANES), ri_s.reshape(n_s, LANES)
    eid_p, rank_p = _route_cols(ri_p2)
    eid_s, rank_s = _route_cols(ri_s2)
    dest_p = (pstarts[eid_p] + rank_p).reshape(-1)
    dest_s = (pstarts[eid_s] + rank_s).reshape(-1)
    x2_p2, x2_s2 = x2_p.reshape(n_p, D_MODEL), x2_s.reshape(n_s, D_MODEL)
    slots = jnp.zeros((nblk * MOE_ROWS * SUBLANES, LANES), F32)
    slots = _dispatch(x2_p2, dest_p, slots)
    slots = _dispatch(x2_s2, dest_s, slots)
    ys = _experts(slots, block_e, n_used, w['w_exp_gate'], w['w_exp_up'], w['w_exp_down'], MOE_ROWS)
    y_p = _combine(x2_p2, ri_p2, dest_p, ys, w['ln3_g'], w['ln3_b'], alpha).reshape(b, s, D_MODEL)
    y_s = _combine(x2_s2, ri_s2, dest_s, ys, w['ln3_g'], w['ln3_b'], alpha).reshape(bs, ss, D_MODEL)
    return (y_p, y_s, lat_p, kr_p, cst_p, mk.reshape(b, N_MEM, X_HEADS, X_HEAD_DIM),
            mv.reshape(b, N_MEM, X_HEADS, X_HEAD_DIM), lat_s, kr_s, cst_s)


def kernel(x_prompt, x_sample, cache_kv_latent, cache_k_rope, cache_conv, cache_mem_k, cache_mem_v, mem_prompt,
           w_in, q_norm_g, kv_norm_g, w_uq, w_ukv, conv_w, w_out, ln1_g, ln1_b, w_xq, w_xk, w_xv, w_xo, ln2_g,
           ln2_b, w_router_group, b_router_group, w_router_expert, b_router_expert, w_exp_gate, w_exp_up,
           w_exp_down, ln3_g, ln3_b):
    depth = w_in.shape[0]
    xp, xs = x_prompt, x_sample
    outs = [[] for _ in range(8)]
    for l in range(depth):
        w = _prep_weights(l, w_in, q_norm_g, kv_norm_g, w_uq, w_ukv, conv_w, w_out, ln1_g, ln1_b, w_xq, w_xk, w_xv,
                          w_xo, ln2_g, ln2_b, w_router_group, b_router_group, w_router_expert, b_router_expert,
                          w_exp_gate, w_exp_up, w_exp_down, ln3_g, ln3_b)
        res = _layer(l, depth, xp, xs, cache_kv_latent[l], cache_k_rope[l], cache_conv[l], cache_mem_k[l],
                     cache_mem_v[l], mem_prompt, w)
        xp, xs = res[0], res[1]
        for acc, r in zip(outs, res[2:]):
            acc.append(r)
    return (xp, xs) + tuple(jnp.stack(o) for o in outs)
```

```python
import functools
import math

import numpy as np
import jax
import jax.numpy as jnp
from jax import lax
from jax.experimental import pallas as pl
from jax.experimental.pallas import tpu as pltpu

F32 = jnp.float32
BF16 = jnp.bfloat16
I32 = jnp.int32

D_MODEL = 1024
CHUNK = 64
N_HEADS = 8
QK_NOPE = 64
QK_ROPE = 32
V_HEAD = 64
Q_LORA = 256
KV_LORA = 128
ROPE_THETA = 10000.0
MLA_SCALE = (QK_NOPE + QK_ROPE) ** -0.5
CONV_WIDTH = 512
CONV_K = 3
N_MEM = 256
X_HEADS = 4
X_HEAD_DIM = D_MODEL // X_HEADS
X_SCALE = X_HEAD_DIM ** -0.5
N_GROUPS = 4
EXPERTS_PER_GROUP = 8
N_EXPERTS = N_GROUPS * EXPERTS_PER_GROUP
D_EXPERT = 512
LN_EPS = 1e-5
RMS_EPS = 1e-6

LANES = 128
SUBLANES = 8
ROW_CHUNKS = D_MODEL // LANES
HEAD_PAD = LANES
ROPE_LANE0 = QK_NOPE
ROUTER_LANE0 = N_GROUPS
VMEM_LIMIT = 56 * 1024 * 1024
MOE_ROWS = 512
MOE_PARTS = 2
ACT_ROW_SLAB = 32
GATHER_UNROLL = 8
ATTN_ROW_SLAB = 64
LN_ROW_SLAB = 16
XATTN_ROW_SLAB = 64
MID_PARTS = 2


def _cp(sem, vmem=VMEM_LIMIT):
    return pltpu.CompilerParams(dimension_semantics=sem, vmem_limit_bytes=vmem)


def _pick_tile(n, pref):
    t = min(n, pref)
    while n % t:
        t //= 2
    return t


def _with_ones_lane(v):
    lane = lax.broadcasted_iota(I32, v.shape, 1)
    return jnp.where((lane & (HEAD_PAD - 1)) == V_HEAD, 1.0, v)


def _inproj_body(x_ref, cinit_ref, cos_ref, sin_ref, win_ref, qg_ref, kvg_ref, wuq_ref, wukv_ref, cw_ref,
                 q_ref, k_ref, v_ref, yc_ref, lat_ref, kr_ref, cst_ref, u_scr, *, tt, nj):
    j = pl.program_id(1)

    @pl.when(j == 0)
    def _():
        u_scr[6:8, :] = cinit_ref[0]

    x = x_ref[0].astype(BF16)
    proj = jnp.dot(x, win_ref[...], preferred_element_type=F32)
    cq = proj[:, 0:256]
    ckv = proj[:, 256:384]
    gb = proj[:, 384:896]
    gc = proj[:, 896:1408]
    gv = proj[:, 1408:1920]
    kr_a = proj[:, 1920:2048]
    kr_b = proj[:, 2048:2176]
    cos_t = cos_ref[...]
    sin_t = sin_ref[...]
    cqn = cq * lax.rsqrt(jnp.mean(cq * cq, -1, keepdims=True) + RMS_EPS) * qg_ref[...]
    ckvn = ckv * lax.rsqrt(jnp.mean(ckv * ckv, -1, keepdims=True) + RMS_EPS) * kvg_ref[...]
    lat_ref[0] = ckvn
    kr_p = kr_a * cos_t + kr_b * sin_t
    kr_ref[0] = kr_p[:, ROPE_LANE0:ROPE_LANE0 + QK_ROPE]
    qq = jnp.dot(cqn.astype(BF16), wuq_ref[...], preferred_element_type=F32)
    kv = jnp.dot(ckvn.astype(BF16), wukv_ref[...], preferred_element_type=F32)
    hw = N_HEADS * HEAD_PAD
    for h in range(N_HEADS):
        sl = slice(h * HEAD_PAD, (h + 1) * HEAD_PAD)
        sl_b = slice(hw + h * HEAD_PAD, hw + (h + 1) * HEAD_PAD)
        q_ref[0, :, sl] = (qq[:, sl] * cos_t + qq[:, sl_b] * sin_t).astype(BF16)
        k_ref[0, :, sl] = (kv[:, sl] + kr_p).astype(BF16)
    v_ref[0] = _with_ones_lane(kv[:, hw:2 * hw]).astype(BF16)
    u = gc * gv
    u_scr[8:8 + tt, :] = u
    conv = cw_ref[0:1, :] * u_scr[6:6 + tt, :] + cw_ref[1:2, :] * u_scr[7:7 + tt, :] + cw_ref[2:3, :] * u
    yc_ref[0] = (gb * conv).astype(BF16)
    last2 = u_scr[tt + 6:tt + 8, :]
    u_scr[6:8, :] = last2

    @pl.when(j == nj - 1)
    def _():
        cst_ref[0] = last2


def _inproj(x, conv_init, cos_t, sin_t, w):
    b, s, _ = x.shape
    tt = _pick_tile(s, 512)
    nj = s // tt
    wn = w['w_in'].shape[1]
    full = lambda shape: pl.BlockSpec(shape, lambda bi, ji: (0,) * len(shape))
    out_shapes = (
        jax.ShapeDtypeStruct((b, s, N_HEADS * HEAD_PAD), BF16),
        jax.ShapeDtypeStruct((b, s, N_HEADS * HEAD_PAD), BF16),
        jax.ShapeDtypeStruct((b, s, N_HEADS * HEAD_PAD), BF16),
        jax.ShapeDtypeStruct((b, s, CONV_WIDTH), BF16),
        jax.ShapeDtypeStruct((b, s, KV_LORA), F32),
        jax.ShapeDtypeStruct((b, s, QK_ROPE), F32),
        jax.ShapeDtypeStruct((b, CONV_K - 1, CONV_WIDTH), F32),
    )
    row = lambda width: pl.BlockSpec((1, tt, width), lambda bi, ji: (bi, ji, 0))
    return pl.pallas_call(
        functools.partial(_inproj_body, tt=tt, nj=nj),
        grid=(b, nj),
        in_specs=[
            row(D_MODEL),
            pl.BlockSpec((1, CONV_K - 1, CONV_WIDTH), lambda bi, ji: (bi, 0, 0)),
            pl.BlockSpec((tt, LANES), lambda bi, ji: (ji, 0)),
            pl.BlockSpec((tt, LANES), lambda bi, ji: (ji, 0)),
            full((D_MODEL, wn)),
            full((1, Q_LORA)),
            full((1, KV_LORA)),
            full(w['w_uq'].shape),
            full(w['w_ukv'].shape),
            full((CONV_K, CONV_WIDTH)),
        ],
        out_specs=(
            row(N_HEADS * HEAD_PAD), row(N_HEADS * HEAD_PAD), row(N_HEADS * HEAD_PAD), row(CONV_WIDTH),
            row(KV_LORA), row(QK_ROPE),
            pl.BlockSpec((1, CONV_K - 1, CONV_WIDTH), lambda bi, ji: (bi, 0, 0)),
        ),
        out_shape=out_shapes,
        scratch_shapes=[pltpu.VMEM((tt + 8, CONV_WIDTH), F32)],
        compiler_params=_cp(("parallel", "arbitrary")),
        name="inproj",
    )(x, conv_init, cos_t, sin_t, w['w_in'], w['q_norm_g'], w['kv_norm_g'], w['w_uq'], w['w_ukv'], w['conv_w'])


def _kvup_body(lat_ref, krp_ref, wukv_ref, k_ref, v_ref):
    kv = jnp.dot(lat_ref[0].astype(BF16), wukv_ref[...], preferred_element_type=F32)
    kr_p = krp_ref[0]
    hw = N_HEADS * HEAD_PAD
    for h in range(N_HEADS):
        sl = slice(h * HEAD_PAD, (h + 1) * HEAD_PAD)
        k_ref[0, :, sl] = (kv[:, sl] + kr_p).astype(BF16)
    v_ref[0] = _with_ones_lane(kv[:, hw:2 * hw]).astype(BF16)


def _kvup(lat, kr_padded, w_ukv):
    b, s, _ = lat.shape
    tt = _pick_tile(s, 512)
    row = lambda width: pl.BlockSpec((1, tt, width), lambda bi, ji: (bi, ji, 0))
    return pl.pallas_call(
        _kvup_body,
        grid=(b, s // tt),
        in_specs=[row(KV_LORA), row(LANES), pl.BlockSpec(w_ukv.shape, lambda bi, ji: (0, 0))],
        out_specs=(row(N_HEADS * HEAD_PAD), row(N_HEADS * HEAD_PAD)),
        out_shape=(jax.ShapeDtypeStruct((b, s, N_HEADS * HEAD_PAD), BF16),
                   jax.ShapeDtypeStruct((b, s, N_HEADS * HEAD_PAD), BF16)),
        compiler_params=_cp(("parallel", "parallel")),
        name="kvup",
    )(lat, kr_padded, w_ukv)


def _attn_body(qi_ref, ki_ref, fl_ref, q_ref, k_ref, v_ref, o_ref, m_scr, acc_scr, s_scr, p_scr, a_scr,
               *, tq, tk, rs, q_pos0, n_valid):
    step = pl.program_id(1)
    qi = qi_ref[step]
    ki = ki_ref[step]
    flags = fl_ref[step]
    c_exp = MLA_SCALE * math.log2(math.e)
    reps = tk // LANES

    @pl.when((flags & 1) != 0)
    def _():
        m_scr[...] = jnp.full(m_scr.shape, -jnp.inf, F32)
        acc_scr[...] = jnp.zeros(acc_scr.shape, F32)

    def scores(h):
        hs = slice(h * HEAD_PAD, (h + 1) * HEAD_PAD)
        s_scr[h % 2] = lax.dot_general(q_ref[0, :, hs], k_ref[0, :, hs], (((1,), (1,)), ((), ())),
                                       preferred_element_type=F32)

    def tile(masked):
        scores(0)
        for h in range(N_HEADS):
            hs = slice(h * HEAD_PAD, (h + 1) * HEAD_PAD)
            b2 = h % 2
            if h + 1 < N_HEADS:
                scores(h + 1)
            for r in range(tq // rs):
                rows = slice(r * rs, (r + 1) * rs)
                s_r = s_scr[b2, rows, :]
                if masked:
                    qpos = q_pos0 + qi * tq + r * rs + lax.broadcasted_iota(I32, (rs, tk), 0)
                    kpos = ki * tk + lax.broadcasted_iota(I32, (rs, tk), 1)
                    mask = (kpos >> 6) <= (qpos >> 6)
                    if n_valid is not None:
                        mask = mask & (kpos < n_valid)
                    s_r = jnp.where(mask, s_r, -jnp.inf)
                m_old = m_scr[h, rows, :]
                m_new = jnp.maximum(m_old, jnp.max(s_r, axis=-1, keepdims=True))
                a_scr[b2, rows, :] = jnp.exp2((m_old - m_new) * c_exp)
                m_rep = jnp.concatenate([m_new] * reps, axis=1)
                p_scr[b2, rows, :] = jnp.exp2((s_r - m_rep) * c_exp).astype(BF16)
                m_scr[h, rows, :] = m_new
            pv = jnp.dot(p_scr[b2], v_ref[0, :, hs], preferred_element_type=F32)
            acc_scr[h] = a_scr[b2] * acc_scr[h] + pv

    @pl.when((flags & 4) != 0)
    def _():
        tile(True)

    @pl.when((flags & 4) == 0)
    def _():
        tile(False)

    @pl.when((flags & 2) != 0)
    def _():
        for h in range(N_HEADS):
            acc = acc_scr[h]
            o_ref[0, :, h * V_HEAD:(h + 1) * V_HEAD] = (acc[:, 0:V_HEAD] / acc[:, V_HEAD:V_HEAD + 1]).astype(BF16)


def _attn_tables(nq, nk, tq, tk, q_pos0, n_valid):
    qi_l, ki_l, fl_l = [], [], []
    for qi in range(nq):
        q_lo = q_pos0 + qi * tq
        q_hi = q_lo + tq - 1
        k_last_pos = (q_hi // CHUNK) * CHUNK + CHUNK - 1
        if n_valid is not None:
            k_last_pos = min(k_last_pos, n_valid - 1)
        k_last = min(nk - 1, k_last_pos // tk)
        for ki in range(k_last + 1):
            k_hi = ki * tk + tk - 1
            masked = (k_hi // CHUNK) > (q_lo // CHUNK) or (n_valid is not None and k_hi >= n_valid)
            fl = (1 if ki == 0 else 0) | (2 if ki == k_last else 0) | (4 if masked else 0)
            qi_l.append(qi); ki_l.append(ki); fl_l.append(fl)
    return (jnp.asarray(np.array(qi_l, np.int32)), jnp.asarray(np.array(ki_l, np.int32)),
            jnp.asarray(np.array(fl_l, np.int32)))


def _attention(q, k, v, q_pos0, n_valid, tq_pref=512, tk_pref=512):
    b, sq, _ = q.shape
    sk = k.shape[1]
    tq = _pick_tile(sq, tq_pref)
    tk = _pick_tile(sk, tk_pref)
    qi_t, ki_t, fl_t = _attn_tables(sq // tq, sk // tk, tq, tk, q_pos0, n_valid)
    n_steps = int(qi_t.shape[0])
    grid_spec = pltpu.PrefetchScalarGridSpec(
        num_scalar_prefetch=3,
        grid=(b, n_steps),
        in_specs=[
            pl.BlockSpec((1, tq, N_HEADS * HEAD_PAD), lambda bi, si, qt, kt, ft: (bi, qt[si], 0)),
            pl.BlockSpec((1, tk, N_HEADS * HEAD_PAD), lambda bi, si, qt, kt, ft: (bi, kt[si], 0)),
            pl.BlockSpec((1, tk, N_HEADS * HEAD_PAD), lambda bi, si, qt, kt, ft: (bi, kt[si], 0)),
        ],
        out_specs=pl.BlockSpec((1, tq, N_HEADS * V_HEAD), lambda bi, si, qt, kt, ft: (bi, qt[si], 0)),
        scratch_shapes=[
            pltpu.VMEM((N_HEADS, tq, LANES), F32),
            pltpu.VMEM((N_HEADS, tq, LANES), F32),
            pltpu.VMEM((2, tq, tk), F32),
            pltpu.VMEM((2, tq, tk), BF16),
            pltpu.VMEM((2, tq, LANES), F32),
        ],
    )
    return pl.pallas_call(
        functools.partial(_attn_body, tq=tq, tk=tk, rs=min(tq, ATTN_ROW_SLAB), q_pos0=q_pos0, n_valid=n_valid),
        grid_spec=grid_spec,
        out_shape=jax.ShapeDtypeStruct((b, sq, N_HEADS * V_HEAD), BF16),
        compiler_params=_cp(("parallel", "arbitrary")),
        name="mla_attn",
    )(qi_t, ki_t, fl_t, q, k, v)


def _memkv_body(mem_ref, wk_ref, wv_ref, mk_ref, mv_ref, mkb_ref, mvb_ref):
    m = mem_ref[...].astype(BF16)
    mk = jnp.dot(m, wk_ref[...], preferred_element_type=F32)
    mv = jnp.dot(m, wv_ref[...], preferred_element_type=F32)
    mk_ref[...] = mk
    mv_ref[...] = mv
    mkb_ref[...] = mk.astype(BF16)
    mvb_ref[...] = mv.astype(BF16)


def _memkv(mem2d, w_xk, w_xv):
    n = mem2d.shape[0]
    tt = _pick_tile(n, 256)
    row = pl.BlockSpec((tt, D_MODEL), lambda i: (i, 0))
    wspec = pl.BlockSpec((D_MODEL, D_MODEL), lambda i: (0, 0))
    return pl.pallas_call(
        _memkv_body,
        grid=(n // tt,),
        in_specs=[row, wspec, wspec],
        out_specs=(row, row, row, row),
        out_shape=(jax.ShapeDtypeStruct((n, D_MODEL), F32), jax.ShapeDtypeStruct((n, D_MODEL), F32),
                   jax.ShapeDtypeStruct((n, D_MODEL), BF16), jax.ShapeDtypeStruct((n, D_MODEL), BF16)),
        compiler_params=_cp(("parallel",)),
        name="memkv",
    )(mem2d, w_xk, w_xv)


def _layer_norm(x, g, b):
    mu = jnp.mean(x, -1, keepdims=True)
    xc = x - mu
    var = jnp.mean(xc * xc, -1, keepdims=True)
    return xc * lax.rsqrt(var + LN_EPS) * g + b


def _mid_body(x_ref, at_ref, yc_ref, mk_ref, mv_ref, cnt0_ref, low_ref, wo_ref, g1_ref, b1_ref, wq_ref, wxo_ref,
              g2_ref, b2_ref, wr_ref, br_ref, x2_ref, ri_ref, cnt_ref,
              cnt_scr, a_scr, x1_scr, xb_scr, q_scr, sc_scr, p_scr, o_scr, lg_scr, *, tt, alpha):
    first = (pl.program_id(0) == 0) & (pl.program_id(1) == 0)

    @pl.when(first)
    def _():
        cnt_scr[...] = cnt0_ref[...]

    n_part = MID_PARTS if tt % (MID_PARTS * SUBLANES * 2) == 0 else 1
    pr = tt // n_part
    ln_rs = min(pr, LN_ROW_SLAB)
    sm_rs = min(pr, XATTN_ROW_SLAB)
    c_exp = X_SCALE * math.log2(math.e)
    lane = lax.broadcasted_iota(I32, (pr, LANES), 1)

    def part_rows(k):
        return slice(k * pr, (k + 1) * pr)

    def out_proj(k):
        rp = part_rows(k)
        mix = jnp.concatenate([at_ref[0, rp, :], yc_ref[0, rp, :]], axis=-1)
        a_scr[rp, :] = jnp.dot(mix, wo_ref[...], preferred_element_type=F32)

    def norm1(k):
        for r in range(pr // ln_rs):
            rows = slice(k * pr + r * ln_rs, k * pr + (r + 1) * ln_rs)
            x1 = _layer_norm(alpha * x_ref[0, rows, :] + a_scr[rows, :], g1_ref[...], b1_ref[...])
            x1_scr[rows, :] = x1
            xb_scr[rows, :] = x1.astype(BF16)

    def q_proj(k):
        rp = part_rows(k)
        q_scr[rp, :] = jnp.dot(xb_scr[rp, :], wq_ref[...], preferred_element_type=F32).astype(BF16)

    def cross_attn(k):
        rp = part_rows(k)
        for h in range(X_HEADS):
            sl = slice(h * X_HEAD_DIM, (h + 1) * X_HEAD_DIM)
            b2 = h % 2
            sc_scr[k, b2] = lax.dot_general(q_scr[rp, sl], mk_ref[0, :, sl], (((1,), (1,)), ((), ())),
                                            preferred_element_type=F32)
            for r in range(pr // sm_rs):
                rows = slice(r * sm_rs, (r + 1) * sm_rs)
                s_r = sc_scr[k, b2, rows, :]
                e = jnp.exp2((s_r - jnp.max(s_r, -1, keepdims=True)) * c_exp)
                p_scr[k, b2, rows, :] = (e / jnp.sum(e, -1, keepdims=True)).astype(BF16)
            o_scr[rp, sl] = jnp.dot(p_scr[k, b2], mv_ref[0, :, sl], preferred_element_type=F32).astype(BF16)

    def x_out_proj(k):
        rp = part_rows(k)
        a_scr[rp, :] = jnp.dot(o_scr[rp, :], wxo_ref[...], preferred_element_type=F32)

    def norm2(k):
        for r in range(pr // ln_rs):
            rows = slice(k * pr + r * ln_rs, k * pr + (r + 1) * ln_rs)
            x2 = _layer_norm(alpha * x1_scr[rows, :] + a_scr[rows, :], g2_ref[...], b2_ref[...])
            x2_ref[0, rows, :] = x2
            xb_scr[rows, :] = x2.astype(BF16)

    def router_logits(k):
        rp = part_rows(k)
        lg_scr[rp, :] = jnp.dot(xb_scr[rp, :], wr_ref[...], preferred_element_type=F32) + br_ref[...]

    def route(k):
        rp = part_rows(k)
        logits = lg_scr[rp, :]
        neg = -jnp.inf
        is_g = lane < N_GROUPS
        lg = jnp.where(is_g, logits, neg)
        mg = jnp.max(lg, -1, keepdims=True)
        g_idx = jnp.min(jnp.where(lg == mg, lane, LANES), -1, keepdims=True)
        pg = 1.0 / jnp.sum(jnp.where(is_g, jnp.exp(logits - mg), 0.0), -1, keepdims=True)
        in_grp = ((lane >= ROUTER_LANE0) & (lane < ROUTER_LANE0 + N_EXPERTS)
                  & (((lane - ROUTER_LANE0) >> 3) == g_idx))
        le = jnp.where(in_grp, logits, neg)
        v1 = jnp.max(le, -1, keepdims=True)
        i1 = jnp.min(jnp.where(le == v1, lane, LANES), -1, keepdims=True)
        le2 = jnp.where(lane == i1, neg, le)
        v2 = jnp.max(le2, -1, keepdims=True)
        i2 = jnp.min(jnp.where(le2 == v2, lane, LANES), -1, keepdims=True)
        e2 = jnp.exp(v2 - v1)
        den = 1.0 + e2
        gate1 = (1.0 / den) * pg
        gate2 = (e2 / den) * pg
        oh1 = (lane == i1).astype(F32)
        oh2 = (lane == i2).astype(F32)
        oh = oh1 + oh2
        base = cnt_scr[...] + jnp.dot(low_ref[...], oh.astype(BF16), preferred_element_type=F32)
        rank1 = jnp.sum(oh1 * base, -1, keepdims=True)
        rank2 = jnp.sum(oh2 * base, -1, keepdims=True)
        cnt_scr[...] = cnt_scr[...] + jnp.sum(oh, 0, keepdims=True)
        e1f = (i1 - ROUTER_LANE0).astype(F32)
        e2f = (i2 - ROUTER_LANE0).astype(F32)
        ri_ref[0, rp, :] = jnp.where(lane == 0, e1f, jnp.where(lane == 1, e2f, jnp.where(
            lane == 2, rank1, jnp.where(lane == 3, rank2, jnp.where(lane == 4, gate1, jnp.where(
                lane == 5, gate2, 0.0))))))

    stages = (out_proj, norm1, q_proj, cross_attn, x_out_proj, norm2, router_logits, route)
    matmul_stages = (out_proj, q_proj, x_out_proj, router_logits)
    for t in range(len(stages) + n_part - 1):
        todo = [(stages[t - k], k) for k in range(n_part) if 0 <= t - k < len(stages)]
        for fn, k in sorted(todo, key=lambda fk: fk[0] not in matmul_stages):
            fn(k)
    cnt_ref[...] = cnt_scr[...]


def _mid(x, attn, yconv, mk_b, mv_b, cnt0, w, alpha):
    b, s, _ = x.shape
    tt = _pick_tile(s, 512)
    row = lambda width: pl.BlockSpec((1, tt, width), lambda bi, ji: (bi, ji, 0))
    full = lambda shape: pl.BlockSpec(shape, lambda bi, ji: (0,) * len(shape))
    mem = pl.BlockSpec((1, N_MEM, D_MODEL), lambda bi, ji: (bi, 0, 0))
    vec = full((1, D_MODEL))
    n_part = MID_PARTS if tt % (MID_PARTS * SUBLANES * 2) == 0 else 1
    pr = tt // n_part
    lower = jnp.tril(jnp.ones((pr, pr), F32), -1).astype(BF16)
    return pl.pallas_call(
        functools.partial(_mid_body, tt=tt, alpha=alpha),
        grid=(b, s // tt),
        in_specs=[row(D_MODEL), row(N_HEADS * V_HEAD), row(CONV_WIDTH), mem, mem, full((1, LANES)), full((pr, pr)),
                  full((D_MODEL, D_MODEL)), vec, vec, full((D_MODEL, D_MODEL)), full((D_MODEL, D_MODEL)),
                  vec, vec, full((D_MODEL, LANES)), full((1, LANES))],
        out_specs=(row(D_MODEL), row(LANES), full((1, LANES))),
        out_shape=(jax.ShapeDtypeStruct((b, s, D_MODEL), F32), jax.ShapeDtypeStruct((b, s, LANES), F32),
                   jax.ShapeDtypeStruct((1, LANES), F32)),
        scratch_shapes=[
            pltpu.VMEM((1, LANES), F32),
            pltpu.VMEM((tt, D_MODEL), F32),
            pltpu.VMEM((tt, D_MODEL), F32),
            pltpu.VMEM((tt, D_MODEL), BF16),
            pltpu.VMEM((tt, D_MODEL), BF16),
            pltpu.VMEM((n_part, 2, pr, N_MEM), F32),
            pltpu.VMEM((n_part, 2, pr, N_MEM), BF16),
            pltpu.VMEM((tt, D_MODEL), BF16),
            pltpu.VMEM((tt, LANES), F32),
        ],
        compiler_params=_cp(("arbitrary", "arbitrary")),
        name="mid",
    )(x, attn, yconv, mk_b, mv_b, cnt0, lower, w['w_out'], w['ln1_g'], w['ln1_b'], w['w_xq'], w['w_xo'],
      w['ln2_g'], w['ln2_b'], w['w_router'], w['b_router'])


def _row_slice(ref, row):
    return ref.at[pl.ds(pl.multiple_of(row * SUBLANES, SUBLANES), SUBLANES), :]


def _dispatch_body(dest_ref, x_ref, xs_in_ref, xs_ref, buf0, buf1, sem0, sem1, *, tt, nt):
    del xs_in_ref
    i = pl.program_id(0)

    def wait_all(buf, sem):
        for _ in range(2):
            pltpu.make_async_copy(buf, xs_ref.at[pl.ds(0, tt * SUBLANES), :], sem).wait()

    def run(buf, sem, obuf, osem):
        for c in range(ROW_CHUNKS):
            buf[pl.ds(c, tt, stride=SUBLANES), :] = x_ref[:, c * LANES:(c + 1) * LANES]

        unroll = min(tt, GATHER_UNROLL)

        def issue(j, carry):
            for u in range(unroll):
                t = j * unroll + u
                src = _row_slice(buf, t)
                for kk in range(2):
                    d = dest_ref[0, 0, 2 * t + kk]
                    pltpu.make_async_copy(src, _row_slice(xs_ref, d), sem).start()
            return carry

        lax.fori_loop(0, tt // unroll, issue, 0)

        @pl.when(i > 0)
        def _():
            wait_all(obuf, osem)

        @pl.when(i == nt - 1)
        def _():
            wait_all(buf, sem)

    @pl.when(i % 2 == 0)
    def _():
        run(buf0, sem0, buf1, sem1)

    @pl.when(i % 2 == 1)
    def _():
        run(buf1, sem1, buf0, sem0)


def _dispatch(x2d, dest, xs):
    n = x2d.shape[0]
    tt = _pick_tile(n, 256)
    nt = n // tt
    dest3 = dest.reshape(nt, 1, 2 * tt)
    return pl.pallas_call(
        functools.partial(_dispatch_body, tt=tt, nt=nt),
        grid=(nt,),
        in_specs=[
            pl.BlockSpec((1, 1, 2 * tt), lambda i: (i, 0, 0), memory_space=pltpu.SMEM),
            pl.BlockSpec((tt, D_MODEL), lambda i: (i, 0)),
            pl.BlockSpec(memory_space=pl.ANY),
        ],
        out_specs=pl.BlockSpec(memory_space=pl.ANY),
        out_shape=jax.ShapeDtypeStruct(xs.shape, xs.dtype),
        scratch_shapes=[pltpu.VMEM((tt * SUBLANES, LANES), F32), pltpu.VMEM((tt * SUBLANES, LANES), F32),
                        pltpu.SemaphoreType.DMA, pltpu.SemaphoreType.DMA],
        input_output_aliases={2: 0},
        compiler_params=_cp(("arbitrary",)),
        name="moe_dispatch",
    )(dest3, x2d, xs)


def _experts_body(be_ref, nu_ref, x_ref, wg_ref, wu_ref, wd_ref, y_ref, xb_scr, g_scr, u_scr, h_scr, *, rows):
    i = pl.program_id(0)
    n_part = MOE_PARTS
    pr = rows // n_part
    act_rs = min(pr, ACT_ROW_SLAB)

    def load(k):
        base = k * pr * SUBLANES
        xb_scr[k] = jnp.concatenate([x_ref[pl.ds(base + c, pr, stride=SUBLANES), :] for c in range(ROW_CHUNKS)],
                                    axis=-1).astype(BF16)

    def gate_up(k):
        g_scr[k] = jnp.dot(xb_scr[k], wg_ref[0], preferred_element_type=F32)
        u_scr[k] = jnp.dot(xb_scr[k], wu_ref[0], preferred_element_type=F32)

    def act(k):
        for r in range(pr // act_rs):
            rows_r = slice(r * act_rs, (r + 1) * act_rs)
            g = g_scr[k, rows_r, :]
            h_scr[k, rows_r, :] = ((g * jax.nn.sigmoid(g)) * u_scr[k, rows_r, :]).astype(BF16)

    def down(k):
        y = jnp.dot(h_scr[k], wd_ref[0], preferred_element_type=F32)
        base = k * pr * SUBLANES
        for c in range(ROW_CHUNKS):
            y_ref[pl.ds(base + c, pr, stride=SUBLANES), :] = y[:, c * LANES:(c + 1) * LANES]

    @pl.when(i < nu_ref[0])
    def _():
        stages = (load, gate_up, act, down)
        matmul_stages = (gate_up, down)
        for t in range(len(stages) + n_part - 1):
            todo = [(stages[t - k], k) for k in range(n_part) if 0 <= t - k < len(stages)]
            for fn, k in sorted(todo, key=lambda fk: fk[0] not in matmul_stages):
                fn(k)

    @pl.when(i >= nu_ref[0])
    def _():
        y_ref[...] = jnp.zeros(y_ref.shape, F32)


def _experts(xs, block_e, n_used, wg, wu, wd, rows):
    nblk = xs.shape[0] // (rows * SUBLANES)
    clamp = lambda i, nu: jnp.minimum(i, nu[0] - 1)
    grid_spec = pltpu.PrefetchScalarGridSpec(
        num_scalar_prefetch=2,
        grid=(nblk,),
        in_specs=[
            pl.BlockSpec((rows * SUBLANES, LANES), lambda i, be, nu: (clamp(i, nu), 0)),
            pl.BlockSpec((1, D_MODEL, D_EXPERT), lambda i, be, nu: (be[clamp(i, nu)], 0, 0)),
            pl.BlockSpec((1, D_MODEL, D_EXPERT), lambda i, be, nu: (be[clamp(i, nu)], 0, 0)),
            pl.BlockSpec((1, D_EXPERT, D_MODEL), lambda i, be, nu: (be[clamp(i, nu)], 0, 0)),
        ],
        out_specs=pl.BlockSpec((rows * SUBLANES, LANES), lambda i, be, nu: (i, 0)),
        scratch_shapes=[
            pltpu.VMEM((MOE_PARTS, rows // MOE_PARTS, D_MODEL), BF16),
            pltpu.VMEM((MOE_PARTS, rows // MOE_PARTS, D_EXPERT), F32),
            pltpu.VMEM((MOE_PARTS, rows // MOE_PARTS, D_EXPERT), F32),
            pltpu.VMEM((MOE_PARTS, rows // MOE_PARTS, D_EXPERT), BF16),
        ],
    )
    return pl.pallas_call(
        functools.partial(_experts_body, rows=rows),
        grid_spec=grid_spec,
        out_shape=jax.ShapeDtypeStruct(xs.shape, F32),
        compiler_params=_cp(("arbitrary",)),
        name="moe_experts",
    )(block_e, n_used, xs, wg, wu, wd)


def _combine_body(dest_ref, destn_ref, x_ref, ri_ref, g3_ref, b3_ref, ys_ref, o_ref,
                  b00, b01, b10, b11, sem0, sem1, *, tt, nt, alpha):
    i = pl.program_id(0)
    bufs = ((b00, b01), (b10, b11))
    sems = (sem0, sem1)
    rs = min(tt, LN_ROW_SLAB)
    unroll = min(tt, GATHER_UNROLL)

    def gather_rows(dref, slot):
        def copy(t, kk):
            d = dref[0, 0, 2 * t + kk]
            return pltpu.make_async_copy(_row_slice(ys_ref, d), _row_slice(bufs[slot][kk], t), sems[slot])

        def body(j, carry):
            for u in range(unroll):
                for kk in range(2):
                    copy(j * unroll + u, kk).start()
            return carry

        lax.fori_loop(0, tt // unroll, body, 0)

    def wait_rows(slot):
        for kk in range(2):
            pltpu.make_async_copy(ys_ref.at[pl.ds(0, tt * SUBLANES), :], bufs[slot][kk], sems[slot]).wait()

    def run(slot):
        if slot == 0:
            @pl.when(i == 0)
            def _():
                gather_rows(dest_ref, 0)

        @pl.when(i + 1 < nt)
        def _():
            gather_rows(destn_ref, 1 - slot)

        wait_rows(slot)
        for r in range(tt // rs):
            rows = slice(r * rs, (r + 1) * rs)
            base = r * rs * SUBLANES
            y0 = jnp.concatenate([bufs[slot][0][pl.ds(base + c, rs, stride=SUBLANES), :]
                                  for c in range(ROW_CHUNKS)], axis=-1)
            y1 = jnp.concatenate([bufs[slot][1][pl.ds(base + c, rs, stride=SUBLANES), :]
                                  for c in range(ROW_CHUNKS)], axis=-1)
            ri = ri_ref[rows, :]
            moe = y0 * ri[:, 4:5] + y1 * ri[:, 5:6]
            o_ref[rows, :] = _layer_norm(alpha * x_ref[rows, :] + moe, g3_ref[...], b3_ref[...])

    @pl.when(i % 2 == 0)
    def _():
        run(0)

    @pl.when(i % 2 == 1)
    def _():
        run(1)


def _combine(x2d, rinfo, dest, ys, g3, b3, alpha):
    n = x2d.shape[0]
    tt = _pick_tile(n, 256)
    nt = n // tt
    dest3 = dest.reshape(nt, 1, 2 * tt)
    vec = pl.BlockSpec((1, D_MODEL), lambda i: (0, 0))
    stage = pltpu.VMEM((tt * SUBLANES, LANES), F32)
    return pl.pallas_call(
        functools.partial(_combine_body, tt=tt, nt=nt, alpha=alpha),
        grid=(nt,),
        in_specs=[
            pl.BlockSpec((1, 1, 2 * tt), lambda i: (i, 0, 0), memory_space=pltpu.SMEM),
            pl.BlockSpec((1, 1, 2 * tt), lambda i: (jnp.minimum(i + 1, nt - 1), 0, 0), memory_space=pltpu.SMEM),
            pl.BlockSpec((tt, D_MODEL), lambda i: (i, 0)),
            pl.BlockSpec((tt, LANES), lambda i: (i, 0)),
            vec, vec,
            pl.BlockSpec(memory_space=pl.ANY),
        ],
        out_specs=pl.BlockSpec((tt, D_MODEL), lambda i: (i, 0)),
        out_shape=jax.ShapeDtypeStruct((n, D_MODEL), F32),
        scratch_shapes=[stage, stage, stage, stage, pltpu.SemaphoreType.DMA, pltpu.SemaphoreType.DMA],
        compiler_params=_cp(("arbitrary",)),
        name="moe_combine",
    )(dest3, dest3, x2d, rinfo, g3, b3, ys)


def _rope_tables(pos):
    half = QK_ROPE // 2
    inv = ROPE_THETA ** (-jnp.arange(half, dtype=F32) / half)
    ang = pos.astype(F32)[:, None] * inv[None, :]
    cos, sin = jnp.cos(ang), jnp.sin(ang)
    n = pos.shape[0]
    pad_r = LANES - ROPE_LANE0 - QK_ROPE
    cos_t = jnp.concatenate([jnp.ones((n, ROPE_LANE0), F32), cos, cos, jnp.ones((n, pad_r), F32)], -1)
    sin_t = jnp.concatenate([jnp.zeros((n, ROPE_LANE0), F32), sin, sin, jnp.zeros((n, pad_r), F32)], -1)
    return cos_t, sin_t


def _swap_neg(wr):
    half = QK_ROPE // 2
    return jnp.concatenate([-wr[:, half:], wr[:, :half]], axis=1)


def _prep_weights(l, w_in, q_norm_g, kv_norm_g, w_uq, w_ukv, conv_w, w_out, ln1_g, ln1_b, w_xq, w_xk, w_xv, w_xo,
                  ln2_g, ln2_b, w_router_group, b_router_group, w_router_expert, b_router_expert,
                  w_exp_gate, w_exp_up, w_exp_down, ln3_g, ln3_b):
    wi = w_in[l]
    c0 = Q_LORA + KV_LORA
    w_kr = wi[:, c0:c0 + QK_ROPE]
    zl = jnp.zeros((D_MODEL, ROPE_LANE0), F32)
    zr = jnp.zeros((D_MODEL, LANES - ROPE_LANE0 - QK_ROPE), F32)
    w_in_p = jnp.concatenate([wi[:, :c0], wi[:, c0 + QK_ROPE:], zl, w_kr, zr, zl, _swap_neg(w_kr), zr], axis=1)
    wq = w_uq[l].reshape(Q_LORA, N_HEADS, QK_NOPE + QK_ROPE)
    zq = jnp.zeros((Q_LORA, N_HEADS, HEAD_PAD - QK_NOPE - QK_ROPE), F32)
    wq_a = jnp.concatenate([wq, zq], axis=2).reshape(Q_LORA, N_HEADS * HEAD_PAD)
    wq_rot = jnp.concatenate([-wq[:, :, QK_NOPE + QK_ROPE // 2:], wq[:, :, QK_NOPE:QK_NOPE + QK_ROPE // 2]], axis=2)
    wq_b = jnp.concatenate([jnp.zeros((Q_LORA, N_HEADS, QK_NOPE), F32), wq_rot, zq], axis=2)
    wq_b = wq_b.reshape(Q_LORA, N_HEADS * HEAD_PAD)
    wkv = w_ukv[l].reshape(KV_LORA, N_HEADS, QK_NOPE + V_HEAD)
    wk_p = jnp.concatenate([wkv[:, :, :QK_NOPE], jnp.zeros((KV_LORA, N_HEADS, HEAD_PAD - QK_NOPE), F32)], axis=2)
    wk_p = wk_p.reshape(KV_LORA, N_HEADS * HEAD_PAD)
    wv_p = jnp.concatenate([wkv[:, :, QK_NOPE:], jnp.zeros((KV_LORA, N_HEADS, HEAD_PAD - V_HEAD), F32)], axis=2)
    wv_p = wv_p.reshape(KV_LORA, N_HEADS * HEAD_PAD)
    w_router = jnp.concatenate([w_router_group[l], w_router_expert[l],
                                jnp.zeros((D_MODEL, LANES - N_GROUPS - N_EXPERTS), F32)], axis=1)
    b_router = jnp.concatenate([b_router_group[l], b_router_expert[l].reshape(-1),
                                jnp.zeros((LANES - N_GROUPS - N_EXPERTS,), F32)]).reshape(1, LANES)
    return dict(
        w_in=w_in_p.astype(BF16),
        q_norm_g=q_norm_g[l].reshape(1, Q_LORA), kv_norm_g=kv_norm_g[l].reshape(1, KV_LORA),
        w_uq=jnp.concatenate([wq_a, wq_b], axis=1).astype(BF16),
        w_ukv=jnp.concatenate([wk_p, wv_p], axis=1).astype(BF16),
        conv_w=conv_w[l],
        w_out=w_out[l].astype(BF16), ln1_g=ln1_g[l].reshape(1, -1), ln1_b=ln1_b[l].reshape(1, -1),
        w_xq=w_xq[l].astype(BF16), w_xk=w_xk[l].astype(BF16), w_xv=w_xv[l].astype(BF16),
        w_xo=w_xo[l].astype(BF16), ln2_g=ln2_g[l].reshape(1, -1), ln2_b=ln2_b[l].reshape(1, -1),
        w_router=w_router.astype(BF16), b_router=b_router,
        w_exp_gate=w_exp_gate[l].astype(BF16), w_exp_up=w_exp_up[l].astype(BF16),
        w_exp_down=w_exp_down[l].astype(BF16),
        ln3_g=ln3_g[l].reshape(1, -1), ln3_b=ln3_b[l].reshape(1, -1),
    )


def _route_cols(rinfo2d):
    eid = rinfo2d[:, 0:2].astype(I32)
    rank = rinfo2d[:, 2:4].astype(I32)
    return eid, rank


def _layer(l, depth, xp, xs, lat_past, kr_past, conv_past, mk_s, mv_s, mem_prompt, w):
    alpha = (2 * depth) ** 0.25
    b, s, _ = xp.shape
    bs, ss, _ = xs.shape
    past = lat_past.shape[1]

    cos_p, sin_p = _rope_tables(jnp.arange(s))
    q_p, k_p, v_p, yc_p, lat_p, kr_p, cst_p = _inproj(
        xp, jnp.zeros((b, CONV_K - 1, CONV_WIDTH), F32), cos_p, sin_p, w)
    attn_p = _attention(q_p, k_p, v_p, 0, None)
    mk, mv, mk_b, mv_b = _memkv(mem_prompt.reshape(b * N_MEM, D_MODEL), w['w_xk'], w['w_xv'])
    cnt0 = jnp.zeros((1, LANES), F32)
    x2_p, ri_p, cnt_p = _mid(xp, attn_p, yc_p, mk_b.reshape(b, N_MEM, D_MODEL), mv_b.reshape(b, N_MEM, D_MODEL),
                             cnt0, w, alpha)

    cos_s, sin_s = _rope_tables(past + jnp.arange(ss))
    q_s, _, _, yc_s, lat_s, kr_s, cst_s = _inproj(xs, conv_past, cos_s, sin_s, w)
    n_keys = past + ss
    sk = -(-n_keys // 512) * 512
    lat_all = jnp.concatenate([lat_past, lat_s, jnp.zeros((bs, sk - n_keys, KV_LORA), F32)], axis=1)
    kr_all = jnp.concatenate([kr_past, kr_s, jnp.zeros((bs, sk - n_keys, QK_ROPE), F32)], axis=1)
    kr_all = jnp.pad(kr_all, ((0, 0), (0, 0), (ROPE_LANE0, LANES - ROPE_LANE0 - QK_ROPE)))
    k_s, v_s = _kvup(lat_all, kr_all, w['w_ukv'])
    attn_s = _attention(q_s, k_s, v_s, past, n_keys)
    mk_sb = mk_s.reshape(bs, N_MEM, D_MODEL).astype(BF16)
    mv_sb = mv_s.reshape(bs, N_MEM, D_MODEL).astype(BF16)
    x2_s, ri_s, cnt = _mid(xs, attn_s, yc_s, mk_sb, mv_sb, cnt_p, w, alpha)

    n_p, n_s = b * s, bs * ss
    counts = cnt[0, ROUTER_LANE0:ROUTER_LANE0 + N_EXPERTS].astype(I32)
    padded = (counts + MOE_ROWS - 1) // MOE_ROWS * MOE_ROWS
    pends = jnp.cumsum(padded)
    pstarts = pends - padded
    nblk = -(-2 * (n_p + n_s) // MOE_ROWS) + N_EXPERTS
    blk_start = jnp.arange(nblk, dtype=I32) * MOE_ROWS
    block_e = jnp.minimum(jnp.sum((pends[None, :] <= blk_start[:, None]).astype(I32), axis=1), N_EXPERTS - 1)
    n_used = (pends[-1] // MOE_ROWS).astype(I32).reshape(1)
    ri_p2, ri_s2 = ri_p.reshape(n_p, LANES), ri_s.reshape(n_s, LANES)
    eid_p, rank_p = _route_cols(ri_p2)
    eid_s, rank_s = _route_cols(ri_s2)
    dest_p = (pstarts[eid_p] + rank_p).reshape(-1)
    dest_s = (pstarts[eid_s] + rank_s).reshape(-1)
    x2_p2, x2_s2 = x2_p.reshape(n_p, D_MODEL), x2_s.reshape(n_s, D_MODEL)
    slots = jnp.zeros((nblk * MOE_ROWS * SUBLANES, LANES), F32)
    slots = _dispatch(x2_p2, dest_p, slots)
    slots = _dispatch(x2_s2, dest_s, slots)
    ys = _experts(slots, block_e, n_used, w['w_exp_gate'], w['w_exp_up'], w['w_exp_down'], MOE_ROWS)
    y_p = _combine(x2_p2, ri_p2, dest_p, ys, w['ln3_g'], w['ln3_b'], alpha).reshape(b, s, D_MODEL)
    y_s = _combine(x2_s2, ri_s2, dest_s, ys, w['ln3_g'], w['ln3_b'], alpha).reshape(bs, ss, D_MODEL)
    return (y_p, y_s, lat_p, kr_p, cst_p, mk.reshape(b, N_MEM, X_HEADS, X_HEAD_DIM),
            mv.reshape(b, N_MEM, X_HEADS, X_HEAD_DIM), lat_s, kr_s, cst_s)


def kernel(x_prompt, x_sample, cache_kv_latent, cache_k_rope, cache_conv, cache_mem_k, cache_mem_v, mem_prompt,
           w_in, q_norm_g, kv_norm_g, w_uq, w_ukv, conv_w, w_out, ln1_g, ln1_b, w_xq, w_xk, w_xv, w_xo, ln2_g,
           ln2_b, w_router_group, b_router_group, w_router_expert, b_router_expert, w_exp_gate, w_exp_up,
           w_exp_down, ln3_g, ln3_b):
    depth = w_in.shape[0]
    xp, xs = x_prompt, x_sample
    outs = [[] for _ in range(8)]
    for l in range(depth):
        w = _prep_weights(l, w_in, q_norm_g, kv_norm_g, w_uq, w_ukv, conv_w, w_out, ln1_g, ln1_b, w_xq, w_xk, w_xv,
                          w_xo, ln2_g, ln2_b, w_router_group, b_router_group, w_router_expert, b_router_expert,
                          w_exp_gate, w_exp_up, w_exp_down, ln3_g, ln3_b)
        res = _layer(l, depth, xp, xs, cache_kv_latent[l], cache_k_rope[l], cache_conv[l], cache_mem_k[l],
                     cache_mem_v[l], mem_prompt, w)
        xp, xs = res[0], res[1]
        for acc, r in zip(outs, res[2:]):
            acc.append(r)
    return (xp, xs) + tuple(jnp.stack(o) for o in outs)
```

```python
import functools
import math

import numpy as np
import jax
import jax.numpy as jnp
from jax import lax
from jax.experimental import pallas as pl
from jax.experimental.pallas import tpu as pltpu

F32 = jnp.float32
BF16 = jnp.bfloat16
I32 = jnp.int32

D_MODEL = 1024
CHUNK = 64
N_HEADS = 8
QK_NOPE = 64
QK_ROPE = 32
V_HEAD = 64
Q_LORA = 256
KV_LORA = 128
ROPE_THETA = 10000.0
MLA_SCALE = (QK_NOPE + QK_ROPE) ** -0.5
CONV_WIDTH = 512
CONV_K = 3
N_MEM = 256
X_HEADS = 4
X_HEAD_DIM = D_MODEL // X_HEADS
X_SCALE = X_HEAD_DIM ** -0.5
N_GROUPS = 4
EXPERTS_PER_GROUP = 8
N_EXPERTS = N_GROUPS * EXPERTS_PER_GROUP
D_EXPERT = 512
LN_EPS = 1e-5
RMS_EPS = 1e-6

LANES = 128
SUBLANES = 8
ROW_CHUNKS = D_MODEL // LANES
HEAD_PAD = LANES
ROPE_LANE0 = QK_NOPE
ROUTER_LANE0 = N_GROUPS
VMEM_LIMIT = 56 * 1024 * 1024
MOE_ROWS = 512
MOE_PARTS = 2
ACT_ROW_SLAB = 32
GATHER_UNROLL = 8
WCAST_ROWS = 64
ATTN_TQ = 1024
ATTN_TK = 512
ATTN_SUB = 2
ATTN_ROW_SLAB = 64
LN_ROW_SLAB = 16
XATTN_ROW_SLAB = 64
MID_PARTS = 2


def _cp(sem, vmem=VMEM_LIMIT):
    return pltpu.CompilerParams(dimension_semantics=sem, vmem_limit_bytes=vmem)


def _pick_tile(n, pref):
    t = min(n, pref)
    while n % t:
        t //= 2
    return t


def _with_ones_lane(v):
    lane = lax.broadcasted_iota(I32, v.shape, 1)
    return jnp.where((lane & (HEAD_PAD - 1)) == V_HEAD, 1.0, v)


def _inproj_body(x_ref, cinit_ref, cos_ref, sin_ref, win_ref, qg_ref, kvg_ref, wuq_ref, wukv_ref, cw_ref,
                 q_ref, k_ref, v_ref, yc_ref, lat_ref, kr_ref, cst_ref, u_scr, *, tt, nj):
    j = pl.program_id(1)

    @pl.when(j == 0)
    def _():
        u_scr[6:8, :] = cinit_ref[0]

    x = x_ref[0].astype(BF16)
    proj = jnp.dot(x, win_ref[...], preferred_element_type=F32)
    cq = proj[:, 0:256]
    ckv = proj[:, 256:384]
    gb = proj[:, 384:896]
    gc = proj[:, 896:1408]
    gv = proj[:, 1408:1920]
    kr_a = proj[:, 1920:2048]
    kr_b = proj[:, 2048:2176]
    cos_t = cos_ref[...]
    sin_t = sin_ref[...]
    cqn = cq * lax.rsqrt(jnp.mean(cq * cq, -1, keepdims=True) + RMS_EPS) * qg_ref[...]
    ckvn = ckv * lax.rsqrt(jnp.mean(ckv * ckv, -1, keepdims=True) + RMS_EPS) * kvg_ref[...]
    lat_ref[0] = ckvn
    kr_p = kr_a * cos_t + kr_b * sin_t
    kr_ref[0] = kr_p[:, ROPE_LANE0:ROPE_LANE0 + QK_ROPE]
    qq = jnp.dot(cqn.astype(BF16), wuq_ref[...], preferred_element_type=F32)
    kv = jnp.dot(ckvn.astype(BF16), wukv_ref[...], preferred_element_type=F32)
    hw = N_HEADS * HEAD_PAD
    for h in range(N_HEADS):
        sl = slice(h * HEAD_PAD, (h + 1) * HEAD_PAD)
        sl_b = slice(hw + h * HEAD_PAD, hw + (h + 1) * HEAD_PAD)
        q_ref[0, :, sl] = (qq[:, sl] * cos_t + qq[:, sl_b] * sin_t).astype(BF16)
        k_ref[0, :, sl] = (kv[:, sl] + kr_p).astype(BF16)
    v_ref[0] = _with_ones_lane(kv[:, hw:2 * hw]).astype(BF16)
    u = gc * gv
    u_scr[8:8 + tt, :] = u
    conv = cw_ref[0:1, :] * u_scr[6:6 + tt, :] + cw_ref[1:2, :] * u_scr[7:7 + tt, :] + cw_ref[2:3, :] * u
    yc_ref[0] = (gb * conv).astype(BF16)
    last2 = u_scr[tt + 6:tt + 8, :]
    u_scr[6:8, :] = last2

    @pl.when(j == nj - 1)
    def _():
        cst_ref[0] = last2


def _inproj(x, conv_init, cos_t, sin_t, w):
    b, s, _ = x.shape
    tt = _pick_tile(s, 512)
    nj = s // tt
    wn = w['w_in'].shape[1]
    full = lambda shape: pl.BlockSpec(shape, lambda bi, ji: (0,) * len(shape))
    out_shapes = (
        jax.ShapeDtypeStruct((b, s, N_HEADS * HEAD_PAD), BF16),
        jax.ShapeDtypeStruct((b, s, N_HEADS * HEAD_PAD), BF16),
        jax.ShapeDtypeStruct((b, s, N_HEADS * HEAD_PAD), BF16),
        jax.ShapeDtypeStruct((b, s, CONV_WIDTH), BF16),
        jax.ShapeDtypeStruct((b, s, KV_LORA), F32),
        jax.ShapeDtypeStruct((b, s, QK_ROPE), F32),
        jax.ShapeDtypeStruct((b, CONV_K - 1, CONV_WIDTH), F32),
    )
    row = lambda width: pl.BlockSpec((1, tt, width), lambda bi, ji: (bi, ji, 0))
    return pl.pallas_call(
        functools.partial(_inproj_body, tt=tt, nj=nj),
        grid=(b, nj),
        in_specs=[
            row(D_MODEL),
            pl.BlockSpec((1, CONV_K - 1, CONV_WIDTH), lambda bi, ji: (bi, 0, 0)),
            pl.BlockSpec((tt, LANES), lambda bi, ji: (ji, 0)),
            pl.BlockSpec((tt, LANES), lambda bi, ji: (ji, 0)),
            full((D_MODEL, wn)),
            full((1, Q_LORA)),
            full((1, KV_LORA)),
            full(w['w_uq'].shape),
            full(w['w_ukv'].shape),
            full((CONV_K, CONV_WIDTH)),
        ],
        out_specs=(
            row(N_HEADS * HEAD_PAD), row(N_HEADS * HEAD_PAD), row(N_HEADS * HEAD_PAD), row(CONV_WIDTH),
            row(KV_LORA), row(QK_ROPE),
            pl.BlockSpec((1, CONV_K - 1, CONV_WIDTH), lambda bi, ji: (bi, 0, 0)),
        ),
        out_shape=out_shapes,
        scratch_shapes=[pltpu.VMEM((tt + 8, CONV_WIDTH), F32)],
        compiler_params=_cp(("parallel", "arbitrary")),
        name="inproj",
    )(x, conv_init, cos_t, sin_t, w['w_in'], w['q_norm_g'], w['kv_norm_g'], w['w_uq'], w['w_ukv'], w['conv_w'])


def _kvup_body(lat_ref, krp_ref, wukv_ref, k_ref, v_ref):
    kv = jnp.dot(lat_ref[0].astype(BF16), wukv_ref[...], preferred_element_type=F32)
    kr_p = krp_ref[0]
    hw = N_HEADS * HEAD_PAD
    for h in range(N_HEADS):
        sl = slice(h * HEAD_PAD, (h + 1) * HEAD_PAD)
        k_ref[0, :, sl] = (kv[:, sl] + kr_p).astype(BF16)
    v_ref[0] = _with_ones_lane(kv[:, hw:2 * hw]).astype(BF16)


def _kvup(lat, kr_padded, w_ukv):
    b, s, _ = lat.shape
    tt = _pick_tile(s, 512)
    row = lambda width: pl.BlockSpec((1, tt, width), lambda bi, ji: (bi, ji, 0))
    return pl.pallas_call(
        _kvup_body,
        grid=(b, s // tt),
        in_specs=[row(KV_LORA), row(LANES), pl.BlockSpec(w_ukv.shape, lambda bi, ji: (0, 0))],
        out_specs=(row(N_HEADS * HEAD_PAD), row(N_HEADS * HEAD_PAD)),
        out_shape=(jax.ShapeDtypeStruct((b, s, N_HEADS * HEAD_PAD), BF16),
                   jax.ShapeDtypeStruct((b, s, N_HEADS * HEAD_PAD), BF16)),
        compiler_params=_cp(("parallel", "parallel")),
        name="kvup",
    )(lat, kr_padded, w_ukv)


def _attn_body(qi_ref, ki_ref, fl_ref, q_ref, k_ref, v_ref, o_ref, m_scr, acc_scr, s_scr, p_scr, a_scr,
               *, tq, tk, n_sub, rs, q_pos0, n_valid, combos):
    step = pl.program_id(1)
    qi = qi_ref[step]
    ki = ki_ref[step]
    flags = fl_ref[step]
    c_exp = MLA_SCALE * math.log2(math.e)
    reps = tk // LANES
    th = tq // n_sub

    @pl.when((flags & 1) != 0)
    def _():
        m_scr[...] = jnp.full(m_scr.shape, -jnp.inf, F32)
        acc_scr[...] = jnp.zeros(acc_scr.shape, F32)

    def scores(item, buf):
        sub, h, _ = item
        hs = slice(h * HEAD_PAD, (h + 1) * HEAD_PAD)
        s_scr[buf] = lax.dot_general(q_ref[0, sub * th:(sub + 1) * th, hs], k_ref[0, :, hs],
                                     (((1,), (1,)), ((), ())), preferred_element_type=F32)

    def softmax_pv(item, buf):
        sub, h, masked = item
        hs = slice(h * HEAD_PAD, (h + 1) * HEAD_PAD)
        for r in range(th // rs):
            rows = slice(r * rs, (r + 1) * rs)
            arows = slice(sub * th + r * rs, sub * th + (r + 1) * rs)
            s_r = s_scr[buf, rows, :]
            if masked:
                qpos = q_pos0 + qi * tq + sub * th + r * rs + lax.broadcasted_iota(I32, (rs, tk), 0)
                kpos = ki * tk + lax.broadcasted_iota(I32, (rs, tk), 1)
                mask = (kpos >> 6) <= (qpos >> 6)
                if n_valid is not None:
                    mask = mask & (kpos < n_valid)
                s_r = jnp.where(mask, s_r, -jnp.inf)
            m_old = m_scr[h, arows, :]
            m_new = jnp.maximum(m_old, jnp.max(s_r, axis=-1, keepdims=True))
            a_scr[buf, rows, :] = jnp.exp2((m_old - m_new) * c_exp)
            m_rep = jnp.concatenate([m_new] * reps, axis=1)
            p_scr[buf, rows, :] = jnp.exp2((s_r - m_rep) * c_exp).astype(BF16)
            m_scr[h, arows, :] = m_new
        pv = jnp.dot(p_scr[buf], v_ref[0, :, hs], preferred_element_type=F32)
        srows = slice(sub * th, (sub + 1) * th)
        acc_scr[h, srows, :] = a_scr[buf] * acc_scr[h, srows, :] + pv

    def run(modes):
        items = [(sub, h, mode == 2) for sub, mode in enumerate(modes) if mode != 0 for h in range(N_HEADS)]
        scores(items[0], 0)
        for n, item in enumerate(items):
            if n + 1 < len(items):
                scores(items[n + 1], (n + 1) % 2)
            softmax_pv(item, n % 2)

    for code, modes in combos:
        @pl.when((flags >> 2) == code)
        def _(modes=modes):
            run(modes)

    @pl.when((flags & 2) != 0)
    def _():
        for h in range(N_HEADS):
            acc = acc_scr[h]
            o_ref[0, :, h * V_HEAD:(h + 1) * V_HEAD] = (acc[:, 0:V_HEAD] / acc[:, V_HEAD:V_HEAD + 1]).astype(BF16)


def _attn_tables(nq, nk, tq, tk, n_sub, q_pos0, n_valid):
    th = tq // n_sub
    qi_l, ki_l, fl_l, combos = [], [], [], {}
    for qi in range(nq):
        sub_lo = [q_pos0 + qi * tq + j * th for j in range(n_sub)]
        sub_last = []
        for lo in sub_lo:
            last_pos = ((lo + th - 1) // CHUNK) * CHUNK + CHUNK - 1
            if n_valid is not None:
                last_pos = min(last_pos, n_valid - 1)
            sub_last.append(min(nk - 1, last_pos // tk))
        k_last = max(sub_last)
        for ki in range(k_last + 1):
            k_hi = ki * tk + tk - 1
            modes = []
            for lo, last in zip(sub_lo, sub_last):
                if ki > last:
                    modes.append(0)
                elif (k_hi // CHUNK) > (lo // CHUNK) or (n_valid is not None and k_hi >= n_valid):
                    modes.append(2)
                else:
                    modes.append(1)
            code = sum(m * 3 ** j for j, m in enumerate(modes))
            combos[code] = tuple(modes)
            qi_l.append(qi); ki_l.append(ki)
            fl_l.append((1 if ki == 0 else 0) | (2 if ki == k_last else 0) | (code << 2))
    to_arr = lambda vals: jnp.asarray(np.array(vals, np.int32))
    return to_arr(qi_l), to_arr(ki_l), to_arr(fl_l), tuple(sorted(combos.items()))


def _attention(q, k, v, q_pos0, n_valid):
    b, sq, _ = q.shape
    sk = k.shape[1]
    tq = _pick_tile(sq, ATTN_TQ)
    tk = _pick_tile(sk, ATTN_TK)
    n_sub = ATTN_SUB if tq % (ATTN_SUB * 2 * SUBLANES) == 0 else 1
    th = tq // n_sub
    qi_t, ki_t, fl_t, combos = _attn_tables(sq // tq, sk // tk, tq, tk, n_sub, q_pos0, n_valid)
    n_steps = int(qi_t.shape[0])
    grid_spec = pltpu.PrefetchScalarGridSpec(
        num_scalar_prefetch=3,
        grid=(b, n_steps),
        in_specs=[
            pl.BlockSpec((1, tq, N_HEADS * HEAD_PAD), lambda bi, si, qt, kt, ft: (bi, qt[si], 0)),
            pl.BlockSpec((1, tk, N_HEADS * HEAD_PAD), lambda bi, si, qt, kt, ft: (bi, kt[si], 0)),
            pl.BlockSpec((1, tk, N_HEADS * HEAD_PAD), lambda bi, si, qt, kt, ft: (bi, kt[si], 0)),
        ],
        out_specs=pl.BlockSpec((1, tq, N_HEADS * V_HEAD), lambda bi, si, qt, kt, ft: (bi, qt[si], 0)),
        scratch_shapes=[
            pltpu.VMEM((N_HEADS, tq, LANES), F32),
            pltpu.VMEM((N_HEADS, tq, LANES), F32),
            pltpu.VMEM((2, th, tk), F32),
            pltpu.VMEM((2, th, tk), BF16),
            pltpu.VMEM((2, th, LANES), F32),
        ],
    )
    return pl.pallas_call(
        functools.partial(_attn_body, tq=tq, tk=tk, n_sub=n_sub, rs=min(th, ATTN_ROW_SLAB), q_pos0=q_pos0,
                          n_valid=n_valid, combos=combos),
        grid_spec=grid_spec,
        out_shape=jax.ShapeDtypeStruct((b, sq, N_HEADS * V_HEAD), BF16),
        compiler_params=_cp(("parallel", "arbitrary")),
        name="mla_attn",
    )(qi_t, ki_t, fl_t, q, k, v)


def _memkv_body(mem_ref, wk_ref, wv_ref, mk_ref, mv_ref, mkb_ref, mvb_ref):
    m = mem_ref[...].astype(BF16)
    mk = jnp.dot(m, wk_ref[...], preferred_element_type=F32)
    mv = jnp.dot(m, wv_ref[...], preferred_element_type=F32)
    mk_ref[...] = mk
    mv_ref[...] = mv
    mkb_ref[...] = mk.astype(BF16)
    mvb_ref[...] = mv.astype(BF16)


def _memkv(mem2d, w_xk, w_xv):
    n = mem2d.shape[0]
    tt = _pick_tile(n, 256)
    row = pl.BlockSpec((tt, D_MODEL), lambda i: (i, 0))
    wspec = pl.BlockSpec((D_MODEL, D_MODEL), lambda i: (0, 0))
    return pl.pallas_call(
        _memkv_body,
        grid=(n // tt,),
        in_specs=[row, wspec, wspec],
        out_specs=(row, row, row, row),
        out_shape=(jax.ShapeDtypeStruct((n, D_MODEL), F32), jax.ShapeDtypeStruct((n, D_MODEL), F32),
                   jax.ShapeDtypeStruct((n, D_MODEL), BF16), jax.ShapeDtypeStruct((n, D_MODEL), BF16)),
        compiler_params=_cp(("parallel",)),
        name="memkv",
    )(mem2d, w_xk, w_xv)


def _layer_norm(x, g, b):
    mu = jnp.mean(x, -1, keepdims=True)
    xc = x - mu
    var = jnp.mean(xc * xc, -1, keepdims=True)
    return xc * lax.rsqrt(var + LN_EPS) * g + b


def _mid_body(x_ref, at_ref, yc_ref, mk_ref, mv_ref, cnt0_ref, low_ref, wo_ref, g1_ref, b1_ref, wq_ref, wxo_ref,
              g2_ref, b2_ref, wr_ref, br_ref, x2_ref, ri_ref, cnt_ref,
              cnt_scr, a_scr, x1_scr, xb_scr, q_scr, sc_scr, p_scr, o_scr, lg_scr, *, tt, alpha):
    first = (pl.program_id(0) == 0) & (pl.program_id(1) == 0)

    @pl.when(first)
    def _():
        cnt_scr[...] = cnt0_ref[...]

    n_part = MID_PARTS if tt % (MID_PARTS * SUBLANES * 2) == 0 else 1
    pr = tt // n_part
    ln_rs = min(pr, LN_ROW_SLAB)
    sm_rs = min(pr, XATTN_ROW_SLAB)
    c_exp = X_SCALE * math.log2(math.e)
    lane = lax.broadcasted_iota(I32, (pr, LANES), 1)

    def part_rows(k):
        return slice(k * pr, (k + 1) * pr)

    def out_proj(k):
        rp = part_rows(k)
        mix = jnp.concatenate([at_ref[0, rp, :], yc_ref[0, rp, :]], axis=-1)
        a_scr[rp, :] = jnp.dot(mix, wo_ref[...], preferred_element_type=F32)

    def norm1(k):
        for r in range(pr // ln_rs):
            rows = slice(k * pr + r * ln_rs, k * pr + (r + 1) * ln_rs)
            x1 = _layer_norm(alpha * x_ref[0, rows, :] + a_scr[rows, :], g1_ref[...], b1_ref[...])
            x1_scr[rows, :] = x1
            xb_scr[rows, :] = x1.astype(BF16)

    def q_proj(k):
        rp = part_rows(k)
        q_scr[rp, :] = jnp.dot(xb_scr[rp, :], wq_ref[...], preferred_element_type=F32).astype(BF16)

    def cross_attn(k):
        rp = part_rows(k)
        for h in range(X_HEADS):
            sl = slice(h * X_HEAD_DIM, (h + 1) * X_HEAD_DIM)
            b2 = h % 2
            sc_scr[k, b2] = lax.dot_general(q_scr[rp, sl], mk_ref[0, :, sl], (((1,), (1,)), ((), ())),
                                            preferred_element_type=F32)
            for r in range(pr // sm_rs):
                rows = slice(r * sm_rs, (r + 1) * sm_rs)
                s_r = sc_scr[k, b2, rows, :]
                e = jnp.exp2((s_r - jnp.max(s_r, -1, keepdims=True)) * c_exp)
                p_scr[k, b2, rows, :] = (e / jnp.sum(e, -1, keepdims=True)).astype(BF16)
            o_scr[rp, sl] = jnp.dot(p_scr[k, b2], mv_ref[0, :, sl], preferred_element_type=F32).astype(BF16)

    def x_out_proj(k):
        rp = part_rows(k)
        a_scr[rp, :] = jnp.dot(o_scr[rp, :], wxo_ref[...], preferred_element_type=F32)

    def norm2(k):
        for r in range(pr // ln_rs):
            rows = slice(k * pr + r * ln_rs, k * pr + (r + 1) * ln_rs)
            x2 = _layer_norm(alpha * x1_scr[rows, :] + a_scr[rows, :], g2_ref[...], b2_ref[...])
            x2_ref[0, rows, :] = x2
            xb_scr[rows, :] = x2.astype(BF16)

    def router_logits(k):
        rp = part_rows(k)
        lg_scr[rp, :] = jnp.dot(xb_scr[rp, :], wr_ref[...], preferred_element_type=F32) + br_ref[...]

    def route(k):
        rp = part_rows(k)
        logits = lg_scr[rp, :]
        neg = -jnp.inf
        is_g = lane < N_GROUPS
        lg = jnp.where(is_g, logits, neg)
        mg = jnp.max(lg, -1, keepdims=True)
        g_idx = jnp.min(jnp.where(lg == mg, lane, LANES), -1, keepdims=True)
        pg = 1.0 / jnp.sum(jnp.where(is_g, jnp.exp(logits - mg), 0.0), -1, keepdims=True)
        in_grp = ((lane >= ROUTER_LANE0) & (lane < ROUTER_LANE0 + N_EXPERTS)
                  & (((lane - ROUTER_LANE0) >> 3) == g_idx))
        le = jnp.where(in_grp, logits, neg)
        v1 = jnp.max(le, -1, keepdims=True)
        i1 = jnp.min(jnp.where(le == v1, lane, LANES), -1, keepdims=True)
        le2 = jnp.where(lane == i1, neg, le)
        v2 = jnp.max(le2, -1, keepdims=True)
        i2 = jnp.min(jnp.where(le2 == v2, lane, LANES), -1, keepdims=True)
        e2 = jnp.exp(v2 - v1)
        den = 1.0 + e2
        gate1 = (1.0 / den) * pg
        gate2 = (e2 / den) * pg
        oh1 = (lane == i1).astype(F32)
        oh2 = (lane == i2).astype(F32)
        oh = oh1 + oh2
        base = cnt_scr[...] + jnp.dot(low_ref[...], oh.astype(BF16), preferred_element_type=F32)
        rank1 = jnp.sum(oh1 * base, -1, keepdims=True)
        rank2 = jnp.sum(oh2 * base, -1, keepdims=True)
        cnt_scr[...] = cnt_scr[...] + jnp.sum(oh, 0, keepdims=True)
        e1f = (i1 - ROUTER_LANE0).astype(F32)
        e2f = (i2 - ROUTER_LANE0).astype(F32)
        ri_ref[0, rp, :] = jnp.where(lane == 0, e1f, jnp.where(lane == 1, e2f, jnp.where(
            lane == 2, rank1, jnp.where(lane == 3, rank2, jnp.where(lane == 4, gate1, jnp.where(
                lane == 5, gate2, 0.0))))))

    stages = (out_proj, norm1, q_proj, cross_attn, x_out_proj, norm2, router_logits, route)
    matmul_stages = (out_proj, q_proj, x_out_proj, router_logits)
    for t in range(len(stages) + n_part - 1):
        todo = [(stages[t - k], k) for k in range(n_part) if 0 <= t - k < len(stages)]
        for fn, k in sorted(todo, key=lambda fk: fk[0] not in matmul_stages):
            fn(k)
    cnt_ref[...] = cnt_scr[...]


def _mid(x, attn, yconv, mk_b, mv_b, cnt0, w, alpha):
    b, s, _ = x.shape
    tt = _pick_tile(s, 512)
    row = lambda width: pl.BlockSpec((1, tt, width), lambda bi, ji: (bi, ji, 0))
    full = lambda shape: pl.BlockSpec(shape, lambda bi, ji: (0,) * len(shape))
    mem = pl.BlockSpec((1, N_MEM, D_MODEL), lambda bi, ji: (bi, 0, 0))
    vec = full((1, D_MODEL))
    n_part = MID_PARTS if tt % (MID_PARTS * SUBLANES * 2) == 0 else 1
    pr = tt // n_part
    lower = jnp.tril(jnp.ones((pr, pr), F32), -1).astype(BF16)
    return pl.pallas_call(
        functools.partial(_mid_body, tt=tt, alpha=alpha),
        grid=(b, s // tt),
        in_specs=[row(D_MODEL), row(N_HEADS * V_HEAD), row(CONV_WIDTH), mem, mem, full((1, LANES)), full((pr, pr)),
                  full((D_MODEL, D_MODEL)), vec, vec, full((D_MODEL, D_MODEL)), full((D_MODEL, D_MODEL)),
                  vec, vec, full((D_MODEL, LANES)), full((1, LANES))],
        out_specs=(row(D_MODEL), row(LANES), full((1, LANES))),
        out_shape=(jax.ShapeDtypeStruct((b, s, D_MODEL), F32), jax.ShapeDtypeStruct((b, s, LANES), F32),
                   jax.ShapeDtypeStruct((1, LANES), F32)),
        scratch_shapes=[
            pltpu.VMEM((1, LANES), F32),
            pltpu.VMEM((tt, D_MODEL), F32),
            pltpu.VMEM((tt, D_MODEL), F32),
            pltpu.VMEM((tt, D_MODEL), BF16),
            pltpu.VMEM((tt, D_MODEL), BF16),
            pltpu.VMEM((n_part, 2, pr, N_MEM), F32),
            pltpu.VMEM((n_part, 2, pr, N_MEM), BF16),
            pltpu.VMEM((tt, D_MODEL), BF16),
            pltpu.VMEM((tt, LANES), F32),
        ],
        compiler_params=_cp(("arbitrary", "arbitrary")),
        name="mid",
    )(x, attn, yconv, mk_b, mv_b, cnt0, lower, w['w_out'], w['ln1_g'], w['ln1_b'], w['w_xq'], w['w_xo'],
      w['ln2_g'], w['ln2_b'], w['w_router'], w['b_router'])


def _row_slice(ref, row):
    return ref.at[pl.ds(pl.multiple_of(row * SUBLANES, SUBLANES), SUBLANES), :]


def _rows_slice(ref, row, n_rows):
    return ref.at[pl.ds(pl.multiple_of(row * SUBLANES, SUBLANES), n_rows * SUBLANES), :]


def _dispatch_body(pad_ref, dest_ref, dest_s_ref, x_ref, x_s_ref, xs_ref, buf0, buf1, zbuf, sem0, sem1, zsem,
                   *, tt, ts, nt, nblk, rows):
    i = pl.program_id(0)
    bufs = (buf0, buf1)
    sems = (sem0, sem1)
    len_bits = rows.bit_length() - 1

    def scatter_rows(buf, sem, src_ref, d_ref, n):
        for c in range(ROW_CHUNKS):
            buf[pl.ds(c, n, stride=SUBLANES), :] = src_ref[:, c * LANES:(c + 1) * LANES]
        unroll = min(n, GATHER_UNROLL)

        def issue(j, carry):
            for u in range(unroll):
                t = j * unroll + u
                src = _row_slice(buf, t)
                for kk in range(2):
                    d = d_ref[0, 0, 2 * t + kk]
                    pltpu.make_async_copy(src, _row_slice(xs_ref, d), sem).start()
            return carry

        lax.fori_loop(0, n // unroll, issue, 0)

    def wait_rows(buf, sem, n):
        for _ in range(2):
            pltpu.make_async_copy(_rows_slice(buf, 0, n), _rows_slice(xs_ref, 0, n), sem).wait()

    def zero_fill(wait):
        def fire(copy):
            if wait:
                copy.wait()
            else:
                copy.start()

        def per_expert(e, carry):
            first = pad_ref[e]
            n_pad = pad_ref[N_EXPERTS + e]
            for bit in range(len_bits):
                size = 1 << bit
                off = (n_pad >> (bit + 1)) << (bit + 1)

                @pl.when(((n_pad >> bit) & 1) == 1)
                def _():
                    fire(pltpu.make_async_copy(_rows_slice(zbuf, 0, size), _rows_slice(xs_ref, first + off, size),
                                               zsem))
            return carry

        lax.fori_loop(0, N_EXPERTS, per_expert, 0)

        def per_block(j, carry):
            fire(pltpu.make_async_copy(zbuf, _rows_slice(xs_ref, j * rows, rows), zsem))
            return carry

        lax.fori_loop(pad_ref[2 * N_EXPERTS], nblk, per_block, 0)

    def run(slot):
        @pl.when(i < nt)
        def _():
            scatter_rows(bufs[slot], sems[slot], x_ref, dest_ref, tt)

        @pl.when(i == nt)
        def _():
            zbuf[...] = jnp.zeros(zbuf.shape, F32)
            scatter_rows(bufs[slot], sems[slot], x_s_ref, dest_s_ref, ts)
            zero_fill(False)

        @pl.when(i > 0)
        def _():
            wait_rows(bufs[1 - slot], sems[1 - slot], tt)

        @pl.when(i == nt)
        def _():
            wait_rows(bufs[slot], sems[slot], ts)
            zero_fill(True)

    @pl.when(i % 2 == 0)
    def _():
        run(0)

    @pl.when(i % 2 == 1)
    def _():
        run(1)


def _dispatch(x_p, dest_p, x_s, dest_s, pad_info, nblk, rows):
    n_p, n_s = x_p.shape[0], x_s.shape[0]
    tt = _pick_tile(n_p, 256)
    nt = n_p // tt
    assert n_s <= tt
    last = lambda i, pad: jnp.minimum(i, nt - 1)
    grid_spec = pltpu.PrefetchScalarGridSpec(
        num_scalar_prefetch=1,
        grid=(nt + 1,),
        in_specs=[
            pl.BlockSpec((1, 1, 2 * tt), lambda i, pad: (last(i, pad), 0, 0), memory_space=pltpu.SMEM),
            pl.BlockSpec((1, 1, 2 * n_s), lambda i, pad: (0, 0, 0), memory_space=pltpu.SMEM),
            pl.BlockSpec((tt, D_MODEL), lambda i, pad: (last(i, pad), 0)),
            pl.BlockSpec((n_s, D_MODEL), lambda i, pad: (0, 0)),
        ],
        out_specs=pl.BlockSpec(memory_space=pl.ANY),
        scratch_shapes=[pltpu.VMEM((tt * SUBLANES, LANES), F32), pltpu.VMEM((tt * SUBLANES, LANES), F32),
                        pltpu.VMEM((rows * SUBLANES, LANES), F32),
                        pltpu.SemaphoreType.DMA, pltpu.SemaphoreType.DMA, pltpu.SemaphoreType.DMA],
    )
    return pl.pallas_call(
        functools.partial(_dispatch_body, tt=tt, ts=n_s, nt=nt, nblk=nblk, rows=rows),
        grid_spec=grid_spec,
        out_shape=jax.ShapeDtypeStruct((nblk * rows * SUBLANES, LANES), F32),
        compiler_params=_cp(("arbitrary",)),
        name="moe_dispatch",
    )(pad_info, dest_p.reshape(nt, 1, 2 * tt), dest_s.reshape(1, 1, 2 * n_s), x_p, x_s)


def _experts_body(be_ref, nu_ref, x_ref, wg_ref, wu_ref, wd_ref, y_ref, xb_scr, g_scr, u_scr, h_scr,
                  wgb_scr, wub_scr, wdb_scr, *, rows):
    i = pl.program_id(0)
    cur = jnp.minimum(i, nu_ref[0] - 1)
    new_expert = (i == 0) | (be_ref[cur] != be_ref[jnp.maximum(cur - 1, 0)])

    @pl.when((i < nu_ref[0]) & new_expert)
    def _():
        for r in range(0, D_MODEL, WCAST_ROWS):
            wgb_scr[r:r + WCAST_ROWS, :] = wg_ref[0, r:r + WCAST_ROWS, :].astype(BF16)
            wub_scr[r:r + WCAST_ROWS, :] = wu_ref[0, r:r + WCAST_ROWS, :].astype(BF16)
        for r in range(0, D_EXPERT, WCAST_ROWS // 2):
            wdb_scr[r:r + WCAST_ROWS // 2, :] = wd_ref[0, r:r + WCAST_ROWS // 2, :].astype(BF16)
    n_part = MOE_PARTS
    pr = rows // n_part
    act_rs = min(pr, ACT_ROW_SLAB)

    def load(k):
        base = k * pr * SUBLANES
        xb_scr[k] = jnp.concatenate([x_ref[pl.ds(base + c, pr, stride=SUBLANES), :] for c in range(ROW_CHUNKS)],
                                    axis=-1).astype(BF16)

    def gate_up(k):
        g_scr[k] = jnp.dot(xb_scr[k], wgb_scr[...], preferred_element_type=F32)
        u_scr[k] = jnp.dot(xb_scr[k], wub_scr[...], preferred_element_type=F32)

    def act(k):
        for r in range(pr // act_rs):
            rows_r = slice(r * act_rs, (r + 1) * act_rs)
            g = g_scr[k, rows_r, :]
            h_scr[k, rows_r, :] = ((g * jax.nn.sigmoid(g)) * u_scr[k, rows_r, :]).astype(BF16)

    def down(k):
        y = jnp.dot(h_scr[k], wdb_scr[...], preferred_element_type=F32)
        base = k * pr * SUBLANES
        for c in range(ROW_CHUNKS):
            y_ref[pl.ds(base + c, pr, stride=SUBLANES), :] = y[:, c * LANES:(c + 1) * LANES]

    @pl.when(i < nu_ref[0])
    def _():
        stages = (load, gate_up, act, down)
        matmul_stages = (gate_up, down)
        for t in range(len(stages) + n_part - 1):
            todo = [(stages[t - k], k) for k in range(n_part) if 0 <= t - k < len(stages)]
            for fn, k in sorted(todo, key=lambda fk: fk[0] not in matmul_stages):
                fn(k)

    @pl.when(i >= nu_ref[0])
    def _():
        y_ref[...] = jnp.zeros(y_ref.shape, F32)


def _experts(xs, block_e, n_used, wg, wu, wd, rows):
    nblk = xs.shape[0] // (rows * SUBLANES)
    clamp = lambda i, nu: jnp.minimum(i, nu[0] - 1)
    grid_spec = pltpu.PrefetchScalarGridSpec(
        num_scalar_prefetch=2,
        grid=(nblk,),
        in_specs=[
            pl.BlockSpec((rows * SUBLANES, LANES), lambda i, be, nu: (clamp(i, nu), 0)),
            pl.BlockSpec((1, D_MODEL, D_EXPERT), lambda i, be, nu: (be[clamp(i, nu)], 0, 0)),
            pl.BlockSpec((1, D_MODEL, D_EXPERT), lambda i, be, nu: (be[clamp(i, nu)], 0, 0)),
            pl.BlockSpec((1, D_EXPERT, D_MODEL), lambda i, be, nu: (be[clamp(i, nu)], 0, 0)),
        ],
        out_specs=pl.BlockSpec((rows * SUBLANES, LANES), lambda i, be, nu: (i, 0)),
        scratch_shapes=[
            pltpu.VMEM((MOE_PARTS, rows // MOE_PARTS, D_MODEL), BF16),
            pltpu.VMEM((MOE_PARTS, rows // MOE_PARTS, D_EXPERT), F32),
            pltpu.VMEM((MOE_PARTS, rows // MOE_PARTS, D_EXPERT), F32),
            pltpu.VMEM((MOE_PARTS, rows // MOE_PARTS, D_EXPERT), BF16),
            pltpu.VMEM((D_MODEL, D_EXPERT), BF16),
            pltpu.VMEM((D_MODEL, D_EXPERT), BF16),
            pltpu.VMEM((D_EXPERT, D_MODEL), BF16),
        ],
    )
    return pl.pallas_call(
        functools.partial(_experts_body, rows=rows),
        grid_spec=grid_spec,
        out_shape=jax.ShapeDtypeStruct(xs.shape, F32),
        compiler_params=_cp(("arbitrary",)),
        name="moe_experts",
    )(block_e, n_used, xs, wg, wu, wd)


def _combine_body(dest_ref, destn_ref, x_ref, ri_ref, g3_ref, b3_ref, ys_ref, o_ref,
                  b00, b01, b10, b11, sem0, sem1, *, tt, nt, alpha):
    i = pl.program_id(0)
    bufs = ((b00, b01), (b10, b11))
    sems = (sem0, sem1)
    rs = min(tt, LN_ROW_SLAB)
    unroll = min(tt, GATHER_UNROLL)

    def gather_rows(dref, slot):
        def copy(t, kk):
            d = dref[0, 0, 2 * t + kk]
            return pltpu.make_async_copy(_row_slice(ys_ref, d), _row_slice(bufs[slot][kk], t), sems[slot])

        def body(j, carry):
            for u in range(unroll):
                for kk in range(2):
                    copy(j * unroll + u, kk).start()
            return carry

        lax.fori_loop(0, tt // unroll, body, 0)

    def wait_rows(slot):
        for kk in range(2):
            pltpu.make_async_copy(ys_ref.at[pl.ds(0, tt * SUBLANES), :], bufs[slot][kk], sems[slot]).wait()

    def run(slot):
        if slot == 0:
            @pl.when(i == 0)
            def _():
                gather_rows(dest_ref, 0)

        @pl.when(i + 1 < nt)
        def _():
            gather_rows(destn_ref, 1 - slot)

        wait_rows(slot)
        for r in range(tt // rs):
            rows = slice(r * rs, (r + 1) * rs)
            base = r * rs * SUBLANES
            y0 = jnp.concatenate([bufs[slot][0][pl.ds(base + c, rs, stride=SUBLANES), :]
                                  for c in range(ROW_CHUNKS)], axis=-1)
            y1 = jnp.concatenate([bufs[slot][1][pl.ds(base + c, rs, stride=SUBLANES), :]
                                  for c in range(ROW_CHUNKS)], axis=-1)
            ri = ri_ref[rows, :]
            moe = y0 * ri[:, 4:5] + y1 * ri[:, 5:6]
            o_ref[rows, :] = _layer_norm(alpha * x_ref[rows, :] + moe, g3_ref[...], b3_ref[...])

    @pl.when(i % 2 == 0)
    def _():
        run(0)

    @pl.when(i % 2 == 1)
    def _():
        run(1)


def _combine(x2d, rinfo, dest, ys, g3, b3, alpha):
    n = x2d.shape[0]
    tt = _pick_tile(n, 256)
    nt = n // tt
    dest3 = dest.reshape(nt, 1, 2 * tt)
    vec = pl.BlockSpec((1, D_MODEL), lambda i: (0, 0))
    stage = pltpu.VMEM((tt * SUBLANES, LANES), F32)
    return pl.pallas_call(
        functools.partial(_combine_body, tt=tt, nt=nt, alpha=alpha),
        grid=(nt,),
        in_specs=[
            pl.BlockSpec((1, 1, 2 * tt), lambda i: (i, 0, 0), memory_space=pltpu.SMEM),
            pl.BlockSpec((1, 1, 2 * tt), lambda i: (jnp.minimum(i + 1, nt - 1), 0, 0), memory_space=pltpu.SMEM),
            pl.BlockSpec((tt, D_MODEL), lambda i: (i, 0)),
            pl.BlockSpec((tt, LANES), lambda i: (i, 0)),
            vec, vec,
            pl.BlockSpec(memory_space=pl.ANY),
        ],
        out_specs=pl.BlockSpec((tt, D_MODEL), lambda i: (i, 0)),
        out_shape=jax.ShapeDtypeStruct((n, D_MODEL), F32),
        scratch_shapes=[stage, stage, stage, stage, pltpu.SemaphoreType.DMA, pltpu.SemaphoreType.DMA],
        compiler_params=_cp(("arbitrary",)),
        name="moe_combine",
    )(dest3, dest3, x2d, rinfo, g3, b3, ys)


def _rope_tables(pos):
    half = QK_ROPE // 2
    inv = ROPE_THETA ** (-jnp.arange(half, dtype=F32) / half)
    ang = pos.astype(F32)[:, None] * inv[None, :]
    cos, sin = jnp.cos(ang), jnp.sin(ang)
    n = pos.shape[0]
    pad_r = LANES - ROPE_LANE0 - QK_ROPE
    cos_t = jnp.concatenate([jnp.ones((n, ROPE_LANE0), F32), cos, cos, jnp.ones((n, pad_r), F32)], -1)
    sin_t = jnp.concatenate([jnp.zeros((n, ROPE_LANE0), F32), sin, sin, jnp.zeros((n, pad_r), F32)], -1)
    return cos_t, sin_t


def _swap_neg(wr):
    half = QK_ROPE // 2
    return jnp.concatenate([-wr[:, half:], wr[:, :half]], axis=1)


def _prep_weights(l, w_in, q_norm_g, kv_norm_g, w_uq, w_ukv, conv_w, w_out, ln1_g, ln1_b, w_xq, w_xk, w_xv, w_xo,
                  ln2_g, ln2_b, w_router_group, b_router_group, w_router_expert, b_router_expert,
                  w_exp_gate, w_exp_up, w_exp_down, ln3_g, ln3_b):
    wi = w_in[l]
    c0 = Q_LORA + KV_LORA
    w_kr = wi[:, c0:c0 + QK_ROPE]
    zl = jnp.zeros((D_MODEL, ROPE_LANE0), F32)
    zr = jnp.zeros((D_MODEL, LANES - ROPE_LANE0 - QK_ROPE), F32)
    w_in_p = jnp.concatenate([wi[:, :c0], wi[:, c0 + QK_ROPE:], zl, w_kr, zr, zl, _swap_neg(w_kr), zr], axis=1)
    wq = w_uq[l].reshape(Q_LORA, N_HEADS, QK_NOPE + QK_ROPE)
    zq = jnp.zeros((Q_LORA, N_HEADS, HEAD_PAD - QK_NOPE - QK_ROPE), F32)
    wq_a = jnp.concatenate([wq, zq], axis=2).reshape(Q_LORA, N_HEADS * HEAD_PAD)
    wq_rot = jnp.concatenate([-wq[:, :, QK_NOPE + QK_ROPE // 2:], wq[:, :, QK_NOPE:QK_NOPE + QK_ROPE // 2]], axis=2)
    wq_b = jnp.concatenate([jnp.zeros((Q_LORA, N_HEADS, QK_NOPE), F32), wq_rot, zq], axis=2)
    wq_b = wq_b.reshape(Q_LORA, N_HEADS * HEAD_PAD)
    wkv = w_ukv[l].reshape(KV_LORA, N_HEADS, QK_NOPE + V_HEAD)
    wk_p = jnp.concatenate([wkv[:, :, :QK_NOPE], jnp.zeros((KV_LORA, N_HEADS, HEAD_PAD - QK_NOPE), F32)], axis=2)
    wk_p = wk_p.reshape(KV_LORA, N_HEADS * HEAD_PAD)
    wv_p = jnp.concatenate([wkv[:, :, QK_NOPE:], jnp.zeros((KV_LORA, N_HEADS, HEAD_PAD - V_HEAD), F32)], axis=2)
    wv_p = wv_p.reshape(KV_LORA, N_HEADS * HEAD_PAD)
    w_router = jnp.concatenate([w_router_group[l], w_router_expert[l],
                                jnp.zeros((D_MODEL, LANES - N_GROUPS - N_EXPERTS), F32)], axis=1)
    b_router = jnp.concatenate([b_router_group[l], b_router_expert[l].reshape(-1),
                                jnp.zeros((LANES - N_GROUPS - N_EXPERTS,), F32)]).reshape(1, LANES)
    return dict(
        w_in=w_in_p.astype(BF16),
        q_norm_g=q_norm_g[l].reshape(1, Q_LORA), kv_norm_g=kv_norm_g[l].reshape(1, KV_LORA),
        w_uq=jnp.concatenate([wq_a, wq_b], axis=1).astype(BF16),
        w_ukv=jnp.concatenate([wk_p, wv_p], axis=1).astype(BF16),
        conv_w=conv_w[l],
        w_out=w_out[l].astype(BF16), ln1_g=ln1_g[l].reshape(1, -1), ln1_b=ln1_b[l].reshape(1, -1),
        w_xq=w_xq[l].astype(BF16), w_xk=w_xk[l].astype(BF16), w_xv=w_xv[l].astype(BF16),
        w_xo=w_xo[l].astype(BF16), ln2_g=ln2_g[l].reshape(1, -1), ln2_b=ln2_b[l].reshape(1, -1),
        w_router=w_router.astype(BF16), b_router=b_router,
        w_exp_gate=w_exp_gate[l], w_exp_up=w_exp_up[l], w_exp_down=w_exp_down[l],
        ln3_g=ln3_g[l].reshape(1, -1), ln3_b=ln3_b[l].reshape(1, -1),
    )


def _route_cols(rinfo2d):
    eid = rinfo2d[:, 0:2].astype(I32)
    rank = rinfo2d[:, 2:4].astype(I32)
    return eid, rank


def _layer(l, depth, xp, xs, lat_past, kr_past, conv_past, mk_s, mv_s, mem_prompt, w):
    alpha = (2 * depth) ** 0.25
    b, s, _ = xp.shape
    bs, ss, _ = xs.shape
    past = lat_past.shape[1]

    cos_p, sin_p = _rope_tables(jnp.arange(s))
    q_p, k_p, v_p, yc_p, lat_p, kr_p, cst_p = _inproj(
        xp, jnp.zeros((b, CONV_K - 1, CONV_WIDTH), F32), cos_p, sin_p, w)
    attn_p = _attention(q_p, k_p, v_p, 0, None)
    mk, mv, mk_b, mv_b = _memkv(mem_prompt.reshape(b * N_MEM, D_MODEL), w['w_xk'], w['w_xv'])
    cnt0 = jnp.zeros((1, LANES), F32)
    x2_p, ri_p, cnt_p = _mid(xp, attn_p, yc_p, mk_b.reshape(b, N_MEM, D_MODEL), mv_b.reshape(b, N_MEM, D_MODEL),
                             cnt0, w, alpha)

    cos_s, sin_s = _rope_tables(past + jnp.arange(ss))
    q_s, _, _, yc_s, lat_s, kr_s, cst_s = _inproj(xs, conv_past, cos_s, sin_s, w)
    n_keys = past + ss
    sk = -(-n_keys // 512) * 512
    lat_all = jnp.concatenate([lat_past, lat_s, jnp.zeros((bs, sk - n_keys, KV_LORA), F32)], axis=1)
    kr_all = jnp.concatenate([kr_past, kr_s, jnp.zeros((bs, sk - n_keys, QK_ROPE), F32)], axis=1)
    kr_all = jnp.pad(kr_all, ((0, 0), (0, 0), (ROPE_LANE0, LANES - ROPE_LANE0 - QK_ROPE)))
    k_s, v_s = _kvup(lat_all, kr_all, w['w_ukv'])
    attn_s = _attention(q_s, k_s, v_s, past, n_keys)
    mk_sb = mk_s.reshape(bs, N_MEM, D_MODEL).astype(BF16)
    mv_sb = mv_s.reshape(bs, N_MEM, D_MODEL).astype(BF16)
    x2_s, ri_s, cnt = _mid(xs, attn_s, yc_s, mk_sb, mv_sb, cnt_p, w, alpha)

    n_p, n_s = b * s, bs * ss
    counts = cnt[0, ROUTER_LANE0:ROUTER_LANE0 + N_EXPERTS].astype(I32)
    padded = (counts + MOE_ROWS - 1) // MOE_ROWS * MOE_ROWS
    pends = jnp.cumsum(padded)
    pstarts = pends - padded
    nblk = -(-2 * (n_p + n_s) // MOE_ROWS) + N_EXPERTS
    blk_start = jnp.arange(nblk, dtype=I32) * MOE_ROWS
    block_e = jnp.minimum(jnp.sum((pends[None, :] <= blk_start[:, None]).astype(I32), axis=1), N_EXPERTS - 1)
    n_used = (pends[-1] // MOE_ROWS).astype(I32).reshape(1)
    ri_p2, ri_s2 = ri_p.reshape(n_p, LANES), ri_s.reshape(n_s, LANES)
    eid_p, rank_p = _route_cols(ri_p2)
    eid_s, rank_s = _route_cols(ri_s2)
    dest_p = (pstarts[eid_p] + rank_p).reshape(-1)
    dest_s = (pstarts[eid_s] + rank_s).reshape(-1)
    x2_p2, x2_s2 = x2_p.reshape(n_p, D_MODEL), x2_s.reshape(n_s, D_MODEL)
    pad_info = jnp.concatenate([pstarts + counts, padded - counts, n_used]).astype(I32)
    slots = _dispatch(x2_p2, dest_p, x2_s2, dest_s, pad_info, nblk, MOE_ROWS)
    ys = _experts(slots, block_e, n_used, w['w_exp_gate'], w['w_exp_up'], w['w_exp_down'], MOE_ROWS)
    y_p = _combine(x2_p2, ri_p2, dest_p, ys, w['ln3_g'], w['ln3_b'], alpha).reshape(b, s, D_MODEL)
    y_s = _combine(x2_s2, ri_s2, dest_s, ys, w['ln3_g'], w['ln3_b'], alpha).reshape(bs, ss, D_MODEL)
    return (y_p, y_s, lat_p, kr_p, cst_p, mk.reshape(b, N_MEM, X_HEADS, X_HEAD_DIM),
            mv.reshape(b, N_MEM, X_HEADS, X_HEAD_DIM), lat_s, kr_s, cst_s)


def kernel(x_prompt, x_sample, cache_kv_latent, cache_k_rope, cache_conv, cache_mem_k, cache_mem_v, mem_prompt,
           w_in, q_norm_g, kv_norm_g, w_uq, w_ukv, conv_w, w_out, ln1_g, ln1_b, w_xq, w_xk, w_xv, w_xo, ln2_g,
           ln2_b, w_router_group, b_router_group, w_router_expert, b_router_expert, w_exp_gate, w_exp_up,
           w_exp_down, ln3_g, ln3_b):
    depth = w_in.shape[0]
    xp, xs = x_prompt, x_sample
    outs = [[] for _ in range(8)]
    for l in range(depth):
        w = _prep_weights(l, w_in, q_norm_g, kv_norm_g, w_uq, w_ukv, conv_w, w_out, ln1_g, ln1_b, w_xq, w_xk, w_xv,
                          w_xo, ln2_g, ln2_b, w_router_group, b_router_group, w_router_expert, b_router_expert,
                          w_exp_gate, w_exp_up, w_exp_down, ln3_g, ln3_b)
        res = _layer(l, depth, xp, xs, cache_kv_latent[l], cache_k_rope[l], cache_conv[l], cache_mem_k[l],
                     cache_mem_v[l], mem_prompt, w)
        xp, xs = res[0], res[1]
        for acc, r in zip(outs, res[2:]):
            acc.append(r)
    return (xp, xs) + tuple(jnp.stack(o) for o in outs)
```

```python
import functools
import math

import numpy as np
import jax
import jax.numpy as jnp
from jax import lax
from jax.experimental import pallas as pl
from jax.experimental.pallas import tpu as pltpu

F32 = jnp.float32
BF16 = jnp.bfloat16
I32 = jnp.int32

D_MODEL = 1024
CHUNK = 64
N_HEADS = 8
QK_NOPE = 64
QK_ROPE = 32
V_HEAD = 64
Q_LORA = 256
KV_LORA = 128
ROPE_THETA = 10000.0
MLA_SCALE = (QK_NOPE + QK_ROPE) ** -0.5
Q_PRESCALE = MLA_SCALE * math.log2(math.e)
CONV_WIDTH = 512
CONV_K = 3
N_MEM = 256
X_HEADS = 4
X_HEAD_DIM = D_MODEL // X_HEADS
X_SCALE = X_HEAD_DIM ** -0.5
N_GROUPS = 4
EXPERTS_PER_GROUP = 8
N_EXPERTS = N_GROUPS * EXPERTS_PER_GROUP
D_EXPERT = 512
LN_EPS = 1e-5
RMS_EPS = 1e-6

LANES = 128
SUBLANES = 8
ROW_CHUNKS = D_MODEL // LANES
HEAD_PAD = LANES
ROPE_LANE0 = QK_NOPE
ROUTER_LANE0 = N_GROUPS
VMEM_LIMIT = 56 * 1024 * 1024
MOE_ROWS = 512
MOE_PARTS = 2
ACT_ROW_SLAB = 32
GATHER_UNROLL = 8
KVUP_ROWS = 1536
WCAST_ROWS = 64
ATTN_TQ = 1024
ATTN_TK = 512
ATTN_SUB = 2
ATTN_ROW_SLAB = 64
LN_ROW_SLAB = 16
XATTN_ROW_SLAB = 64
MID_PARTS = 2


def _cp(sem, vmem=VMEM_LIMIT):
    return pltpu.CompilerParams(dimension_semantics=sem, vmem_limit_bytes=vmem)


def _pick_tile(n, pref):
    t = min(n, pref)
    while n % t:
        t //= 2
    return t


def _with_ones_lane(v):
    lane = lax.broadcasted_iota(I32, v.shape, 1)
    return jnp.where((lane & (HEAD_PAD - 1)) == V_HEAD, 1.0, v)


def _inproj_body(x_ref, cinit_ref, cos_ref, sin_ref, win_ref, qg_ref, kvg_ref, wuq_ref, wukv_ref, cw_ref,
                 q_ref, k_ref, v_ref, yc_ref, lat_ref, kr_ref, cst_ref, u_scr, *, tt, nj):
    j = pl.program_id(1)

    @pl.when(j == 0)
    def _():
        u_scr[6:8, :] = cinit_ref[0]

    x = x_ref[0].astype(BF16)
    proj = jnp.dot(x, win_ref[...], preferred_element_type=F32)
    cq = proj[:, 0:256]
    ckv = proj[:, 256:384]
    gb = proj[:, 384:896]
    gc = proj[:, 896:1408]
    gv = proj[:, 1408:1920]
    kr_a = proj[:, 1920:2048]
    kr_b = proj[:, 2048:2176]
    cos_t = cos_ref[...]
    sin_t = sin_ref[...]
    cqn = cq * lax.rsqrt(jnp.mean(cq * cq, -1, keepdims=True) + RMS_EPS) * qg_ref[...]
    ckvn = ckv * lax.rsqrt(jnp.mean(ckv * ckv, -1, keepdims=True) + RMS_EPS) * kvg_ref[...]
    lat_ref[0] = ckvn
    kr_p = kr_a * cos_t + kr_b * sin_t
    kr_ref[0] = kr_p[:, ROPE_LANE0:ROPE_LANE0 + QK_ROPE]
    qq = jnp.dot(cqn.astype(BF16), wuq_ref[...], preferred_element_type=F32)
    kv = jnp.dot(ckvn.astype(BF16), wukv_ref[...], preferred_element_type=F32)
    hw = N_HEADS * HEAD_PAD
    for h in range(N_HEADS):
        sl = slice(h * HEAD_PAD, (h + 1) * HEAD_PAD)
        sl_b = slice(hw + h * HEAD_PAD, hw + (h + 1) * HEAD_PAD)
        q_ref[0, :, sl] = ((qq[:, sl] * cos_t + qq[:, sl_b] * sin_t) * Q_PRESCALE).astype(BF16)
        k_ref[0, :, sl] = (kv[:, sl] + kr_p).astype(BF16)
    v_ref[0] = _with_ones_lane(kv[:, hw:2 * hw]).astype(BF16)
    u = gc * gv
    u_scr[8:8 + tt, :] = u
    conv = cw_ref[0:1, :] * u_scr[6:6 + tt, :] + cw_ref[1:2, :] * u_scr[7:7 + tt, :] + cw_ref[2:3, :] * u
    yc_ref[0] = (gb * conv).astype(BF16)
    last2 = u_scr[tt + 6:tt + 8, :]
    u_scr[6:8, :] = last2

    @pl.when(j == nj - 1)
    def _():
        cst_ref[0] = last2


def _inproj(x, conv_init, cos_t, sin_t, w):
    b, s, _ = x.shape
    tt = _pick_tile(s, 512)
    nj = s // tt
    wn = w['w_in'].shape[1]
    full = lambda shape: pl.BlockSpec(shape, lambda bi, ji: (0,) * len(shape))
    out_shapes = (
        jax.ShapeDtypeStruct((b, s, N_HEADS * HEAD_PAD), BF16),
        jax.ShapeDtypeStruct((b, s, N_HEADS * HEAD_PAD), BF16),
        jax.ShapeDtypeStruct((b, s, N_HEADS * HEAD_PAD), BF16),
        jax.ShapeDtypeStruct((b, s, CONV_WIDTH), BF16),
        jax.ShapeDtypeStruct((b, s, KV_LORA), F32),
        jax.ShapeDtypeStruct((b, s, QK_ROPE), F32),
        jax.ShapeDtypeStruct((b, CONV_K - 1, CONV_WIDTH), F32),
    )
    row = lambda width: pl.BlockSpec((1, tt, width), lambda bi, ji: (bi, ji, 0))
    return pl.pallas_call(
        functools.partial(_inproj_body, tt=tt, nj=nj),
        grid=(b, nj),
        in_specs=[
            row(D_MODEL),
            pl.BlockSpec((1, CONV_K - 1, CONV_WIDTH), lambda bi, ji: (bi, 0, 0)),
            pl.BlockSpec((tt, LANES), lambda bi, ji: (ji, 0)),
            pl.BlockSpec((tt, LANES), lambda bi, ji: (ji, 0)),
            full((D_MODEL, wn)),
            full((1, Q_LORA)),
            full((1, KV_LORA)),
            full(w['w_uq'].shape),
            full(w['w_ukv'].shape),
            full((CONV_K, CONV_WIDTH)),
        ],
        out_specs=(
            row(N_HEADS * HEAD_PAD), row(N_HEADS * HEAD_PAD), row(N_HEADS * HEAD_PAD), row(CONV_WIDTH),
            row(KV_LORA), row(QK_ROPE),
            pl.BlockSpec((1, CONV_K - 1, CONV_WIDTH), lambda bi, ji: (bi, 0, 0)),
        ),
        out_shape=out_shapes,
        scratch_shapes=[pltpu.VMEM((tt + 8, CONV_WIDTH), F32)],
        compiler_params=_cp(("parallel", "arbitrary")),
        name="inproj",
    )(x, conv_init, cos_t, sin_t, w['w_in'], w['q_norm_g'], w['kv_norm_g'], w['w_uq'], w['w_ukv'], w['conv_w'])


def _kvup_body(lat_ref, krp_ref, wukv_ref, k_ref, v_ref):
    kv = jnp.dot(lat_ref[0].astype(BF16), wukv_ref[...], preferred_element_type=F32)
    kr_p = krp_ref[0]
    hw = N_HEADS * HEAD_PAD
    for h in range(N_HEADS):
        sl = slice(h * HEAD_PAD, (h + 1) * HEAD_PAD)
        k_ref[0, :, sl] = (kv[:, sl] + kr_p).astype(BF16)
    v_ref[0] = _with_ones_lane(kv[:, hw:2 * hw]).astype(BF16)


def _kvup(lat, kr_padded, w_ukv):
    b, s, _ = lat.shape
    tt = _pick_tile(s, KVUP_ROWS)
    row = lambda width: pl.BlockSpec((1, tt, width), lambda bi, ji: (bi, ji, 0))
    return pl.pallas_call(
        _kvup_body,
        grid=(b, s // tt),
        in_specs=[row(KV_LORA), row(LANES), pl.BlockSpec(w_ukv.shape, lambda bi, ji: (0, 0))],
        out_specs=(row(N_HEADS * HEAD_PAD), row(N_HEADS * HEAD_PAD)),
        out_shape=(jax.ShapeDtypeStruct((b, s, N_HEADS * HEAD_PAD), BF16),
                   jax.ShapeDtypeStruct((b, s, N_HEADS * HEAD_PAD), BF16)),
        compiler_params=_cp(("parallel", "parallel")),
        name="kvup",
    )(lat, kr_padded, w_ukv)


def _attn_body(qi_ref, ki_ref, fl_ref, q_ref, k_ref, v_ref, o_ref, m_scr, acc_scr, s_scr, p_scr, a_scr,
               *, tq, tk, n_sub, rs, q_pos0, n_valid, combos):
    step = pl.program_id(1)
    qi = qi_ref[step]
    ki = ki_ref[step]
    flags = fl_ref[step]
    reps = tk // LANES
    th = tq // n_sub

    @pl.when((flags & 1) != 0)
    def _():
        m_scr[...] = jnp.full(m_scr.shape, -jnp.inf, F32)
        acc_scr[...] = jnp.zeros(acc_scr.shape, F32)

    def scores(item, buf):
        sub, h, _ = item
        hs = slice(h * HEAD_PAD, (h + 1) * HEAD_PAD)
        s_scr[buf] = lax.dot_general(q_ref[0, sub * th:(sub + 1) * th, hs], k_ref[0, :, hs],
                                     (((1,), (1,)), ((), ())), preferred_element_type=F32)

    def softmax_pv(item, buf):
        sub, h, masked = item
        hs = slice(h * HEAD_PAD, (h + 1) * HEAD_PAD)
        for r in range(th // rs):
            rows = slice(r * rs, (r + 1) * rs)
            arows = slice(sub * th + r * rs, sub * th + (r + 1) * rs)
            s_r = s_scr[buf, rows, :]
            if masked:
                qpos = q_pos0 + qi * tq + sub * th + r * rs + lax.broadcasted_iota(I32, (rs, tk), 0)
                kpos = ki * tk + lax.broadcasted_iota(I32, (rs, tk), 1)
                mask = (kpos >> 6) <= (qpos >> 6)
                if n_valid is not None:
                    mask = mask & (kpos < n_valid)
                s_r = jnp.where(mask, s_r, -jnp.inf)
            m_old = m_scr[h, arows, :]
            m_new = jnp.maximum(m_old, jnp.max(s_r, axis=-1, keepdims=True))
            a_scr[buf, rows, :] = jnp.exp2(m_old - m_new)
            m_rep = jnp.concatenate([m_new] * reps, axis=1)
            p_scr[buf, rows, :] = jnp.exp2(s_r - m_rep).astype(BF16)
            m_scr[h, arows, :] = m_new
        pv = jnp.dot(p_scr[buf], v_ref[0, :, hs], preferred_element_type=F32)
        srows = slice(sub * th, (sub + 1) * th)
        acc_scr[h, srows, :] = a_scr[buf] * acc_scr[h, srows, :] + pv

    def run(modes):
        items = [(sub, h, mode == 2) for sub, mode in enumerate(modes) if mode != 0 for h in range(N_HEADS)]
        scores(items[0], 0)
        for n, item in enumerate(items):
            if n + 1 < len(items):
                scores(items[n + 1], (n + 1) % 2)
            softmax_pv(item, n % 2)

    for code, modes in combos:
        @pl.when((flags >> 2) == code)
        def _(modes=modes):
            run(modes)

    @pl.when((flags & 2) != 0)
    def _():
        for h in range(N_HEADS):
            acc = acc_scr[h]
            o_ref[0, :, h * V_HEAD:(h + 1) * V_HEAD] = (acc[:, 0:V_HEAD] / acc[:, V_HEAD:V_HEAD + 1]).astype(BF16)


def _attn_tables(nq, nk, tq, tk, n_sub, q_pos0, n_valid):
    th = tq // n_sub
    qi_l, ki_l, fl_l, combos = [], [], [], {}
    for qi in range(nq):
        sub_lo = [q_pos0 + qi * tq + j * th for j in range(n_sub)]
        sub_last = []
        for lo in sub_lo:
            last_pos = ((lo + th - 1) // CHUNK) * CHUNK + CHUNK - 1
            if n_valid is not None:
                last_pos = min(last_pos, n_valid - 1)
            sub_last.append(min(nk - 1, last_pos // tk))
        k_last = max(sub_last)
        for ki in range(k_last + 1):
            k_hi = ki * tk + tk - 1
            modes = []
            for lo, last in zip(sub_lo, sub_last):
                if ki > last:
                    modes.append(0)
                elif (k_hi // CHUNK) > (lo // CHUNK) or (n_valid is not None and k_hi >= n_valid):
                    modes.append(2)
                else:
                    modes.append(1)
            code = sum(m * 3 ** j for j, m in enumerate(modes))
            combos[code] = tuple(modes)
            qi_l.append(qi); ki_l.append(ki)
            fl_l.append((1 if ki == 0 else 0) | (2 if ki == k_last else 0) | (code << 2))
    to_arr = lambda vals: jnp.asarray(np.array(vals, np.int32))
    return to_arr(qi_l), to_arr(ki_l), to_arr(fl_l), tuple(sorted(combos.items()))


def _attention(q, k, v, q_pos0, n_valid):
    b, sq, _ = q.shape
    sk = k.shape[1]
    tq = _pick_tile(sq, ATTN_TQ)
    tk = _pick_tile(sk, ATTN_TK)
    n_sub = ATTN_SUB if tq % (ATTN_SUB * 2 * SUBLANES) == 0 else 1
    th = tq // n_sub
    qi_t, ki_t, fl_t, combos = _attn_tables(sq // tq, sk // tk, tq, tk, n_sub, q_pos0, n_valid)
    n_steps = int(qi_t.shape[0])
    grid_spec = pltpu.PrefetchScalarGridSpec(
        num_scalar_prefetch=3,
        grid=(b, n_steps),
        in_specs=[
            pl.BlockSpec((1, tq, N_HEADS * HEAD_PAD), lambda bi, si, qt, kt, ft: (bi, qt[si], 0)),
            pl.BlockSpec((1, tk, N_HEADS * HEAD_PAD), lambda bi, si, qt, kt, ft: (bi, kt[si], 0)),
            pl.BlockSpec((1, tk, N_HEADS * HEAD_PAD), lambda bi, si, qt, kt, ft: (bi, kt[si], 0)),
        ],
        out_specs=pl.BlockSpec((1, tq, N_HEADS * V_HEAD), lambda bi, si, qt, kt, ft: (bi, qt[si], 0)),
        scratch_shapes=[
            pltpu.VMEM((N_HEADS, tq, LANES), F32),
            pltpu.VMEM((N_HEADS, tq, LANES), F32),
            pltpu.VMEM((2, th, tk), F32),
            pltpu.VMEM((2, th, tk), BF16),
            pltpu.VMEM((2, th, LANES), F32),
        ],
    )
    return pl.pallas_call(
        functools.partial(_attn_body, tq=tq, tk=tk, n_sub=n_sub, rs=min(th, ATTN_ROW_SLAB), q_pos0=q_pos0,
                          n_valid=n_valid, combos=combos),
        grid_spec=grid_spec,
        out_shape=jax.ShapeDtypeStruct((b, sq, N_HEADS * V_HEAD), BF16),
        compiler_params=_cp(("parallel", "arbitrary")),
        name="mla_attn",
    )(qi_t, ki_t, fl_t, q, k, v)


def _memkv_body(mem_ref, wk_ref, wv_ref, mk_ref, mv_ref, mkb_ref, mvb_ref):
    m = mem_ref[...].astype(BF16)
    mk = jnp.dot(m, wk_ref[...], preferred_element_type=F32)
    mv = jnp.dot(m, wv_ref[...], preferred_element_type=F32)
    mk_ref[...] = mk
    mv_ref[...] = mv
    mkb_ref[...] = mk.astype(BF16)
    mvb_ref[...] = mv.astype(BF16)


def _memkv(mem2d, w_xk, w_xv):
    n = mem2d.shape[0]
    tt = _pick_tile(n, 256)
    row = pl.BlockSpec((tt, D_MODEL), lambda i: (i, 0))
    wspec = pl.BlockSpec((D_MODEL, D_MODEL), lambda i: (0, 0))
    return pl.pallas_call(
        _memkv_body,
        grid=(n // tt,),
        in_specs=[row, wspec, wspec],
        out_specs=(row, row, row, row),
        out_shape=(jax.ShapeDtypeStruct((n, D_MODEL), F32), jax.ShapeDtypeStruct((n, D_MODEL), F32),
                   jax.ShapeDtypeStruct((n, D_MODEL), BF16), jax.ShapeDtypeStruct((n, D_MODEL), BF16)),
        compiler_params=_cp(("parallel",)),
        name="memkv",
    )(mem2d, w_xk, w_xv)


def _layer_norm(x, g, b):
    mu = jnp.mean(x, -1, keepdims=True)
    xc = x - mu
    var = jnp.mean(xc * xc, -1, keepdims=True)
    return xc * lax.rsqrt(var + LN_EPS) * g + b


def _mid_body(x_ref, at_ref, yc_ref, mk_ref, mv_ref, cnt0_ref, low_ref, wo_ref, g1_ref, b1_ref, wq_ref, wxo_ref,
              g2_ref, b2_ref, wr_ref, br_ref, x2_ref, ri_ref, cnt_ref, rt_ref,
              cnt_scr, a_scr, x1_scr, xb_scr, q_scr, sc_scr, p_scr, o_scr, lg_scr, *, tt, alpha):
    first = (pl.program_id(0) == 0) & (pl.program_id(1) == 0)

    @pl.when(first)
    def _():
        cnt_scr[...] = cnt0_ref[...]

    n_part = MID_PARTS if tt % (MID_PARTS * SUBLANES * 2) == 0 else 1
    pr = tt // n_part
    ln_rs = min(pr, LN_ROW_SLAB)
    sm_rs = min(pr, XATTN_ROW_SLAB)
    c_exp = X_SCALE * math.log2(math.e)
    lane = lax.broadcasted_iota(I32, (pr, LANES), 1)

    def part_rows(k):
        return slice(k * pr, (k + 1) * pr)

    def out_proj(k):
        rp = part_rows(k)
        mix = jnp.concatenate([at_ref[0, rp, :], yc_ref[0, rp, :]], axis=-1)
        a_scr[rp, :] = jnp.dot(mix, wo_ref[...], preferred_element_type=F32)

    def norm1(k):
        for r in range(pr // ln_rs):
            rows = slice(k * pr + r * ln_rs, k * pr + (r + 1) * ln_rs)
            x1 = _layer_norm(alpha * x_ref[0, rows, :] + a_scr[rows, :], g1_ref[...], b1_ref[...])
            x1_scr[rows, :] = x1
            xb_scr[rows, :] = x1.astype(BF16)

    def q_proj(k):
        rp = part_rows(k)
        q_scr[rp, :] = jnp.dot(xb_scr[rp, :], wq_ref[...], preferred_element_type=F32).astype(BF16)

    def cross_attn(k):
        rp = part_rows(k)
        for h in range(X_HEADS):
            sl = slice(h * X_HEAD_DIM, (h + 1) * X_HEAD_DIM)
            b2 = h % 2
            sc_scr[k, b2] = lax.dot_general(q_scr[rp, sl], mk_ref[0, :, sl], (((1,), (1,)), ((), ())),
                                            preferred_element_type=F32)
            for r in range(pr // sm_rs):
                rows = slice(r * sm_rs, (r + 1) * sm_rs)
                s_r = sc_scr[k, b2, rows, :]
                e = jnp.exp2((s_r - jnp.max(s_r, -1, keepdims=True)) * c_exp)
                p_scr[k, b2, rows, :] = (e / jnp.sum(e, -1, keepdims=True)).astype(BF16)
            o_scr[rp, sl] = jnp.dot(p_scr[k, b2], mv_ref[0, :, sl], preferred_element_type=F32).astype(BF16)

    def x_out_proj(k):
        rp = part_rows(k)
        a_scr[rp, :] = jnp.dot(o_scr[rp, :], wxo_ref[...], preferred_element_type=F32)

    def norm2(k):
        for r in range(pr // ln_rs):
            rows = slice(k * pr + r * ln_rs, k * pr + (r + 1) * ln_rs)
            x2 = _layer_norm(alpha * x1_scr[rows, :] + a_scr[rows, :], g2_ref[...], b2_ref[...])
            x2_ref[0, rows, :] = x2
            xb_scr[rows, :] = x2.astype(BF16)

    def router_logits(k):
        rp = part_rows(k)
        lg_scr[rp, :] = jnp.dot(xb_scr[rp, :], wr_ref[...], preferred_element_type=F32) + br_ref[...]

    def route(k):
        rp = part_rows(k)
        logits = lg_scr[rp, :]
        neg = -jnp.inf
        is_g = lane < N_GROUPS
        lg = jnp.where(is_g, logits, neg)
        mg = jnp.max(lg, -1, keepdims=True)
        g_idx = jnp.min(jnp.where(lg == mg, lane, LANES), -1, keepdims=True)
        pg = 1.0 / jnp.sum(jnp.where(is_g, jnp.exp(logits - mg), 0.0), -1, keepdims=True)
        in_grp = ((lane >= ROUTER_LANE0) & (lane < ROUTER_LANE0 + N_EXPERTS)
                  & (((lane - ROUTER_LANE0) >> 3) == g_idx))
        le = jnp.where(in_grp, logits, neg)
        v1 = jnp.max(le, -1, keepdims=True)
        i1 = jnp.min(jnp.where(le == v1, lane, LANES), -1, keepdims=True)
        le2 = jnp.where(lane == i1, neg, le)
        v2 = jnp.max(le2, -1, keepdims=True)
        i2 = jnp.min(jnp.where(le2 == v2, lane, LANES), -1, keepdims=True)
        e2 = jnp.exp(v2 - v1)
        den = 1.0 + e2
        gate1 = (1.0 / den) * pg
        gate2 = (e2 / den) * pg
        oh1 = (lane == i1).astype(F32)
        oh2 = (lane == i2).astype(F32)
        oh = oh1 + oh2
        base = cnt_scr[...] + jnp.dot(low_ref[...], oh.astype(BF16), preferred_element_type=F32)
        rank1 = jnp.sum(oh1 * base, -1, keepdims=True)
        rank2 = jnp.sum(oh2 * base, -1, keepdims=True)
        cnt_scr[...] = cnt_scr[...] + jnp.sum(oh, 0, keepdims=True)
        e1f = (i1 - ROUTER_LANE0).astype(F32)
        e2f = (i2 - ROUTER_LANE0).astype(F32)
        ri = jnp.where(lane == 0, e1f, jnp.where(lane == 1, e2f, jnp.where(
            lane == 2, rank1, jnp.where(lane == 3, rank2, jnp.where(lane == 4, gate1, jnp.where(
                lane == 5, gate2, 0.0))))))
        ri_ref[0, rp, :] = ri
        rt_ref[0, k] = jnp.transpose(ri)[0:SUBLANES, :]

    stages = (out_proj, norm1, q_proj, cross_attn, x_out_proj, norm2, router_logits, route)
    matmul_stages = (out_proj, q_proj, x_out_proj, router_logits)
    for t in range(len(stages) + n_part - 1):
        todo = [(stages[t - k], k) for k in range(n_part) if 0 <= t - k < len(stages)]
        for fn, k in sorted(todo, key=lambda fk: fk[0] not in matmul_stages):
            fn(k)
    cnt_ref[...] = cnt_scr[...]


def _mid(x, attn, yconv, mk_b, mv_b, cnt0, w, alpha):
    b, s, _ = x.shape
    tt = _pick_tile(s, 512)
    row = lambda width: pl.BlockSpec((1, tt, width), lambda bi, ji: (bi, ji, 0))
    full = lambda shape: pl.BlockSpec(shape, lambda bi, ji: (0,) * len(shape))
    mem = pl.BlockSpec((1, N_MEM, D_MODEL), lambda bi, ji: (bi, 0, 0))
    vec = full((1, D_MODEL))
    n_part = MID_PARTS if tt % (MID_PARTS * SUBLANES * 2) == 0 else 1
    pr = tt // n_part
    lower = jnp.tril(jnp.ones((pr, pr), F32), -1).astype(BF16)
    return pl.pallas_call(
        functools.partial(_mid_body, tt=tt, alpha=alpha),
        grid=(b, s // tt),
        in_specs=[row(D_MODEL), row(N_HEADS * V_HEAD), row(CONV_WIDTH), mem, mem, full((1, LANES)), full((pr, pr)),
                  full((D_MODEL, D_MODEL)), vec, vec, full((D_MODEL, D_MODEL)), full((D_MODEL, D_MODEL)),
                  vec, vec, full((D_MODEL, LANES)), full((1, LANES))],
        out_specs=(row(D_MODEL), row(LANES), full((1, LANES)),
                   pl.BlockSpec((1, n_part, SUBLANES, pr), lambda bi, ji: (bi, ji, 0, 0))),
        out_shape=(jax.ShapeDtypeStruct((b, s, D_MODEL), F32), jax.ShapeDtypeStruct((b, s, LANES), F32),
                   jax.ShapeDtypeStruct((1, LANES), F32),
                   jax.ShapeDtypeStruct((b, s // pr, SUBLANES, pr), F32)),
        scratch_shapes=[
            pltpu.VMEM((1, LANES), F32),
            pltpu.VMEM((tt, D_MODEL), F32),
            pltpu.VMEM((tt, D_MODEL), F32),
            pltpu.VMEM((tt, D_MODEL), BF16),
            pltpu.VMEM((tt, D_MODEL), BF16),
            pltpu.VMEM((n_part, 2, pr, N_MEM), F32),
            pltpu.VMEM((n_part, 2, pr, N_MEM), BF16),
            pltpu.VMEM((tt, D_MODEL), BF16),
            pltpu.VMEM((tt, LANES), F32),
        ],
        compiler_params=_cp(("arbitrary", "arbitrary")),
        name="mid",
    )(x, attn, yconv, mk_b, mv_b, cnt0, lower, w['w_out'], w['ln1_g'], w['ln1_b'], w['w_xq'], w['w_xo'],
      w['ln2_g'], w['ln2_b'], w['w_router'], w['b_router'])


def _row_slice(ref, row):
    return ref.at[pl.ds(pl.multiple_of(row * SUBLANES, SUBLANES), SUBLANES), :]


def _slot_index(t, kk, g, n):
    if g >= n:
        return kk * g + t
    shift = g.bit_length() - 1
    return ((t >> shift) << (shift + 1)) + kk * g + (t & (g - 1))


def _rows_slice(ref, row, n_rows):
    return ref.at[pl.ds(pl.multiple_of(row * SUBLANES, SUBLANES), n_rows * SUBLANES), :]


def _dispatch_body(pad_ref, dest_ref, dest_s_ref, x_ref, x_s_ref, xs_ref, buf0, buf1, zbuf, sem0, sem1, zsem,
                   *, tt, ts, nt, nblk, rows, g_p, g_s):
    i = pl.program_id(0)
    bufs = (buf0, buf1)
    sems = (sem0, sem1)
    len_bits = rows.bit_length() - 1

    def scatter_rows(buf, sem, src_ref, d_ref, n, g):
        for c in range(ROW_CHUNKS):
            buf[pl.ds(c, n, stride=SUBLANES), :] = src_ref[:, c * LANES:(c + 1) * LANES]
        unroll = min(n, GATHER_UNROLL)

        def issue(j, carry):
            for u in range(unroll):
                t = j * unroll + u
                src = _row_slice(buf, t)
                for kk in range(2):
                    d = d_ref[0, 0, _slot_index(t, kk, g, n)]
                    pltpu.make_async_copy(src, _row_slice(xs_ref, d), sem).start()
            return carry

        lax.fori_loop(0, n // unroll, issue, 0)

    def wait_rows(buf, sem, n):
        for _ in range(2):
            pltpu.make_async_copy(_rows_slice(buf, 0, n), _rows_slice(xs_ref, 0, n), sem).wait()

    def zero_fill(wait):
        def fire(copy):
            if wait:
                copy.wait()
            else:
                copy.start()

        def per_expert(e, carry):
            first = pad_ref[e]
            n_pad = pad_ref[N_EXPERTS + e]
            for bit in range(len_bits):
                size = 1 << bit
                off = (n_pad >> (bit + 1)) << (bit + 1)

                @pl.when(((n_pad >> bit) & 1) == 1)
                def _():
                    fire(pltpu.make_async_copy(_rows_slice(zbuf, 0, size), _rows_slice(xs_ref, first + off, size),
                                               zsem))
            return carry

        lax.fori_loop(0, N_EXPERTS, per_expert, 0)

        def per_block(j, carry):
            fire(pltpu.make_async_copy(zbuf, _rows_slice(xs_ref, j * rows, rows), zsem))
            return carry

        lax.fori_loop(pad_ref[2 * N_EXPERTS], nblk, per_block, 0)

    def run(slot):
        @pl.when(i < nt)
        def _():
            scatter_rows(bufs[slot], sems[slot], x_ref, dest_ref, tt, g_p)

        @pl.when(i == nt)
        def _():
            zbuf[...] = jnp.zeros(zbuf.shape, F32)
            scatter_rows(bufs[slot], sems[slot], x_s_ref, dest_s_ref, ts, g_s)
            zero_fill(False)

        @pl.when(i > 0)
        def _():
            wait_rows(bufs[1 - slot], sems[1 - slot], tt)

        @pl.when(i == nt)
        def _():
            wait_rows(bufs[slot], sems[slot], ts)
            zero_fill(True)

    @pl.when(i % 2 == 0)
    def _():
        run(0)

    @pl.when(i % 2 == 1)
    def _():
        run(1)


def _dispatch(x_p, dest_p, g_p, x_s, dest_s, g_s, pad_info, nblk, rows):
    n_p, n_s = x_p.shape[0], x_s.shape[0]
    tt = _pick_tile(n_p, 256)
    nt = n_p // tt
    assert n_s <= tt
    last = lambda i, pad: jnp.minimum(i, nt - 1)
    grid_spec = pltpu.PrefetchScalarGridSpec(
        num_scalar_prefetch=1,
        grid=(nt + 1,),
        in_specs=[
            pl.BlockSpec((1, 1, 2 * tt), lambda i, pad: (last(i, pad), 0, 0), memory_space=pltpu.SMEM),
            pl.BlockSpec((1, 1, 2 * n_s), lambda i, pad: (0, 0, 0), memory_space=pltpu.SMEM),
            pl.BlockSpec((tt, D_MODEL), lambda i, pad: (last(i, pad), 0)),
            pl.BlockSpec((n_s, D_MODEL), lambda i, pad: (0, 0)),
        ],
        out_specs=pl.BlockSpec(memory_space=pl.ANY),
        scratch_shapes=[pltpu.VMEM((tt * SUBLANES, LANES), F32), pltpu.VMEM((tt * SUBLANES, LANES), F32),
                        pltpu.VMEM((rows * SUBLANES, LANES), F32),
                        pltpu.SemaphoreType.DMA, pltpu.SemaphoreType.DMA, pltpu.SemaphoreType.DMA],
    )
    return pl.pallas_call(
        functools.partial(_dispatch_body, tt=tt, ts=n_s, nt=nt, nblk=nblk, rows=rows, g_p=g_p, g_s=g_s),
        grid_spec=grid_spec,
        out_shape=jax.ShapeDtypeStruct((nblk * rows * SUBLANES, LANES), F32),
        compiler_params=_cp(("arbitrary",)),
        name="moe_dispatch",
    )(pad_info, dest_p.reshape(nt, 1, 2 * tt), dest_s.reshape(1, 1, 2 * n_s), x_p, x_s)


def _experts_body(be_ref, nu_ref, x_ref, wg_ref, wu_ref, wd_ref, y_ref, xb_scr, g_scr, u_scr, h_scr,
                  wgb_scr, wub_scr, wdb_scr, *, rows):
    i = pl.program_id(0)
    cur = jnp.minimum(i, nu_ref[0] - 1)
    new_expert = (i == 0) | (be_ref[cur] != be_ref[jnp.maximum(cur - 1, 0)])

    @pl.when((i < nu_ref[0]) & new_expert)
    def _():
        for r in range(0, D_MODEL, WCAST_ROWS):
            wgb_scr[r:r + WCAST_ROWS, :] = wg_ref[0, r:r + WCAST_ROWS, :].astype(BF16)
            wub_scr[r:r + WCAST_ROWS, :] = wu_ref[0, r:r + WCAST_ROWS, :].astype(BF16)
        for r in range(0, D_EXPERT, WCAST_ROWS // 2):
            wdb_scr[r:r + WCAST_ROWS // 2, :] = wd_ref[0, r:r + WCAST_ROWS // 2, :].astype(BF16)
    n_part = MOE_PARTS
    pr = rows // n_part
    act_rs = min(pr, ACT_ROW_SLAB)

    def load(k):
        base = k * pr * SUBLANES
        xb_scr[k] = jnp.concatenate([x_ref[pl.ds(base + c, pr, stride=SUBLANES), :] for c in range(ROW_CHUNKS)],
                                    axis=-1).astype(BF16)

    def gate_up(k):
        g_scr[k] = jnp.dot(xb_scr[k], wgb_scr[...], preferred_element_type=F32)
        u_scr[k] = jnp.dot(xb_scr[k], wub_scr[...], preferred_element_type=F32)

    def act(k):
        for r in range(pr // act_rs):
            rows_r = slice(r * act_rs, (r + 1) * act_rs)
            g = g_scr[k, rows_r, :]
            h_scr[k, rows_r, :] = ((g * jax.nn.sigmoid(g)) * u_scr[k, rows_r, :]).astype(BF16)

    def down(k):
        y = jnp.dot(h_scr[k], wdb_scr[...], preferred_element_type=F32)
        base = k * pr * SUBLANES
        for c in range(ROW_CHUNKS):
            y_ref[pl.ds(base + c, pr, stride=SUBLANES), :] = y[:, c * LANES:(c + 1) * LANES]

    @pl.when(i < nu_ref[0])
    def _():
        stages = (load, gate_up, act, down)
        matmul_stages = (gate_up, down)
        for t in range(len(stages) + n_part - 1):
            todo = [(stages[t - k], k) for k in range(n_part) if 0 <= t - k < len(stages)]
            for fn, k in sorted(todo, key=lambda fk: fk[0] not in matmul_stages):
                fn(k)

    @pl.when(i >= nu_ref[0])
    def _():
        y_ref[...] = jnp.zeros(y_ref.shape, F32)


def _experts(xs, block_e, n_used, wg, wu, wd, rows):
    nblk = xs.shape[0] // (rows * SUBLANES)
    clamp = lambda i, nu: jnp.minimum(i, nu[0] - 1)
    grid_spec = pltpu.PrefetchScalarGridSpec(
        num_scalar_prefetch=2,
        grid=(nblk,),
        in_specs=[
            pl.BlockSpec((rows * SUBLANES, LANES), lambda i, be, nu: (clamp(i, nu), 0)),
            pl.BlockSpec((1, D_MODEL, D_EXPERT), lambda i, be, nu: (be[clamp(i, nu)], 0, 0)),
            pl.BlockSpec((1, D_MODEL, D_EXPERT), lambda i, be, nu: (be[clamp(i, nu)], 0, 0)),
            pl.BlockSpec((1, D_EXPERT, D_MODEL), lambda i, be, nu: (be[clamp(i, nu)], 0, 0)),
        ],
        out_specs=pl.BlockSpec((rows * SUBLANES, LANES), lambda i, be, nu: (i, 0)),
        scratch_shapes=[
            pltpu.VMEM((MOE_PARTS, rows // MOE_PARTS, D_MODEL), BF16),
            pltpu.VMEM((MOE_PARTS, rows // MOE_PARTS, D_EXPERT), F32),
            pltpu.VMEM((MOE_PARTS, rows // MOE_PARTS, D_EXPERT), F32),
            pltpu.VMEM((MOE_PARTS, rows // MOE_PARTS, D_EXPERT), BF16),
            pltpu.VMEM((D_MODEL, D_EXPERT), BF16),
            pltpu.VMEM((D_MODEL, D_EXPERT), BF16),
            pltpu.VMEM((D_EXPERT, D_MODEL), BF16),
        ],
    )
    return pl.pallas_call(
        functools.partial(_experts_body, rows=rows),
        grid_spec=grid_spec,
        out_shape=jax.ShapeDtypeStruct(xs.shape, F32),
        compiler_params=_cp(("arbitrary",)),
        name="moe_experts",
    )(block_e, n_used, xs, wg, wu, wd)


def _combine_body(dest_ref, destn_ref, x_ref, ri_ref, g3_ref, b3_ref, ys_ref, o_ref,
                  b00, b01, b10, b11, sem0, sem1, *, tt, nt, g, alpha):
    i = pl.program_id(0)
    bufs = ((b00, b01), (b10, b11))
    sems = (sem0, sem1)
    rs = min(tt, LN_ROW_SLAB)
    unroll = min(tt, GATHER_UNROLL)

    def gather_rows(dref, slot):
        def copy(t, kk):
            d = dref[0, 0, _slot_index(t, kk, g, tt)]
            return pltpu.make_async_copy(_row_slice(ys_ref, d), _row_slice(bufs[slot][kk], t), sems[slot])

        def body(j, carry):
            for u in range(unroll):
                for kk in range(2):
                    copy(j * unroll + u, kk).start()
            return carry

        lax.fori_loop(0, tt // unroll, body, 0)

    def wait_rows(slot):
        for kk in range(2):
            pltpu.make_async_copy(ys_ref.at[pl.ds(0, tt * SUBLANES), :], bufs[slot][kk], sems[slot]).wait()

    def run(slot):
        if slot == 0:
            @pl.when(i == 0)
            def _():
                gather_rows(dest_ref, 0)

        @pl.when(i + 1 < nt)
        def _():
            gather_rows(destn_ref, 1 - slot)

        wait_rows(slot)
        for r in range(tt // rs):
            rows = slice(r * rs, (r + 1) * rs)
            base = r * rs * SUBLANES
            y0 = jnp.concatenate([bufs[slot][0][pl.ds(base + c, rs, stride=SUBLANES), :]
                                  for c in range(ROW_CHUNKS)], axis=-1)
            y1 = jnp.concatenate([bufs[slot][1][pl.ds(base + c, rs, stride=SUBLANES), :]
                                  for c in range(ROW_CHUNKS)], axis=-1)
            ri = ri_ref[rows, :]
            moe = y0 * ri[:, 4:5] + y1 * ri[:, 5:6]
            o_ref[rows, :] = _layer_norm(alpha * x_ref[rows, :] + moe, g3_ref[...], b3_ref[...])

    @pl.when(i % 2 == 0)
    def _():
        run(0)

    @pl.when(i % 2 == 1)
    def _():
        run(1)


def _combine(x2d, rinfo, dest, g, ys, g3, b3, alpha):
    n = x2d.shape[0]
    tt = _pick_tile(n, 256)
    nt = n // tt
    dest3 = dest.reshape(nt, 1, 2 * tt)
    vec = pl.BlockSpec((1, D_MODEL), lambda i: (0, 0))
    stage = pltpu.VMEM((tt * SUBLANES, LANES), F32)
    return pl.pallas_call(
        functools.partial(_combine_body, tt=tt, nt=nt, g=g, alpha=alpha),
        grid=(nt,),
        in_specs=[
            pl.BlockSpec((1, 1, 2 * tt), lambda i: (i, 0, 0), memory_space=pltpu.SMEM),
            pl.BlockSpec((1, 1, 2 * tt), lambda i: (jnp.minimum(i + 1, nt - 1), 0, 0), memory_space=pltpu.SMEM),
            pl.BlockSpec((tt, D_MODEL), lambda i: (i, 0)),
            pl.BlockSpec((tt, LANES), lambda i: (i, 0)),
            vec, vec,
            pl.BlockSpec(memory_space=pl.ANY),
        ],
        out_specs=pl.BlockSpec((tt, D_MODEL), lambda i: (i, 0)),
        out_shape=jax.ShapeDtypeStruct((n, D_MODEL), F32),
        scratch_shapes=[stage, stage, stage, stage, pltpu.SemaphoreType.DMA, pltpu.SemaphoreType.DMA],
        compiler_params=_cp(("arbitrary",)),
        name="moe_combine",
    )(dest3, dest3, x2d, rinfo, g3, b3, ys)


def _rope_tables(pos):
    half = QK_ROPE // 2
    inv = ROPE_THETA ** (-jnp.arange(half, dtype=F32) / half)
    ang = pos.astype(F32)[:, None] * inv[None, :]
    cos, sin = jnp.cos(ang), jnp.sin(ang)
    n = pos.shape[0]
    pad_r = LANES - ROPE_LANE0 - QK_ROPE
    cos_t = jnp.concatenate([jnp.ones((n, ROPE_LANE0), F32), cos, cos, jnp.ones((n, pad_r), F32)], -1)
    sin_t = jnp.concatenate([jnp.zeros((n, ROPE_LANE0), F32), sin, sin, jnp.zeros((n, pad_r), F32)], -1)
    return cos_t, sin_t


def _swap_neg(wr):
    half = QK_ROPE // 2
    return jnp.concatenate([-wr[:, half:], wr[:, :half]], axis=1)


def _prep_weights(l, w_in, q_norm_g, kv_norm_g, w_uq, w_ukv, conv_w, w_out, ln1_g, ln1_b, w_xq, w_xk, w_xv, w_xo,
                  ln2_g, ln2_b, w_router_group, b_router_group, w_router_expert, b_router_expert,
                  w_exp_gate, w_exp_up, w_exp_down, ln3_g, ln3_b):
    wi = w_in[l]
    c0 = Q_LORA + KV_LORA
    w_kr = wi[:, c0:c0 + QK_ROPE]
    zl = jnp.zeros((D_MODEL, ROPE_LANE0), F32)
    zr = jnp.zeros((D_MODEL, LANES - ROPE_LANE0 - QK_ROPE), F32)
    w_in_p = jnp.concatenate([wi[:, :c0], wi[:, c0 + QK_ROPE:], zl, w_kr, zr, zl, _swap_neg(w_kr), zr], axis=1)
    wq = w_uq[l].reshape(Q_LORA, N_HEADS, QK_NOPE + QK_ROPE)
    zq = jnp.zeros((Q_LORA, N_HEADS, HEAD_PAD - QK_NOPE - QK_ROPE), F32)
    wq_a = jnp.concatenate([wq, zq], axis=2).reshape(Q_LORA, N_HEADS * HEAD_PAD)
    wq_rot = jnp.concatenate([-wq[:, :, QK_NOPE + QK_ROPE // 2:], wq[:, :, QK_NOPE:QK_NOPE + QK_ROPE // 2]], axis=2)
    wq_b = jnp.concatenate([jnp.zeros((Q_LORA, N_HEADS, QK_NOPE), F32), wq_rot, zq], axis=2)
    wq_b = wq_b.reshape(Q_LORA, N_HEADS * HEAD_PAD)
    wkv = w_ukv[l].reshape(KV_LORA, N_HEADS, QK_NOPE + V_HEAD)
    wk_p = jnp.concatenate([wkv[:, :, :QK_NOPE], jnp.zeros((KV_LORA, N_HEADS, HEAD_PAD - QK_NOPE), F32)], axis=2)
    wk_p = wk_p.reshape(KV_LORA, N_HEADS * HEAD_PAD)
    wv_p = jnp.concatenate([wkv[:, :, QK_NOPE:], jnp.zeros((KV_LORA, N_HEADS, HEAD_PAD - V_HEAD), F32)], axis=2)
    wv_p = wv_p.reshape(KV_LORA, N_HEADS * HEAD_PAD)
    w_router = jnp.concatenate([w_router_group[l], w_router_expert[l],
                                jnp.zeros((D_MODEL, LANES - N_GROUPS - N_EXPERTS), F32)], axis=1)
    b_router = jnp.concatenate([b_router_group[l], b_router_expert[l].reshape(-1),
                                jnp.zeros((LANES - N_GROUPS - N_EXPERTS,), F32)]).reshape(1, LANES)
    return dict(
        w_in=w_in_p.astype(BF16),
        q_norm_g=q_norm_g[l].reshape(1, Q_LORA), kv_norm_g=kv_norm_g[l].reshape(1, KV_LORA),
        w_uq=jnp.concatenate([wq_a, wq_b], axis=1).astype(BF16),
        w_ukv=jnp.concatenate([wk_p, wv_p], axis=1).astype(BF16),
        conv_w=conv_w[l],
        w_out=w_out[l].astype(BF16), ln1_g=ln1_g[l].reshape(1, -1), ln1_b=ln1_b[l].reshape(1, -1),
        w_xq=w_xq[l].astype(BF16), w_xk=w_xk[l].astype(BF16), w_xv=w_xv[l].astype(BF16),
        w_xo=w_xo[l].astype(BF16), ln2_g=ln2_g[l].reshape(1, -1), ln2_b=ln2_b[l].reshape(1, -1),
        w_router=w_router.astype(BF16), b_router=b_router,
        w_exp_gate=w_exp_gate[l], w_exp_up=w_exp_up[l], w_exp_down=w_exp_down[l],
        ln3_g=ln3_g[l].reshape(1, -1), ln3_b=ln3_b[l].reshape(1, -1),
    )


def _slots(rt, pstarts):
    g = rt.shape[-1]
    rt = rt.reshape(-1, SUBLANES, g)
    dest = pstarts[rt[:, 0:2, :].astype(I32)] + rt[:, 2:4, :].astype(I32)
    return dest.reshape(-1), g


def _layer(l, depth, xp, xs, lat_past, kr_past, conv_past, mk_s, mv_s, mem_prompt, w):
    alpha = (2 * depth) ** 0.25
    b, s, _ = xp.shape
    bs, ss, _ = xs.shape
    past = lat_past.shape[1]

    cos_p, sin_p = _rope_tables(jnp.arange(s))
    q_p, k_p, v_p, yc_p, lat_p, kr_p, cst_p = _inproj(
        xp, jnp.zeros((b, CONV_K - 1, CONV_WIDTH), F32), cos_p, sin_p, w)
    attn_p = _attention(q_p, k_p, v_p, 0, None)
    mk, mv, mk_b, mv_b = _memkv(mem_prompt.reshape(b * N_MEM, D_MODEL), w['w_xk'], w['w_xv'])
    cnt0 = jnp.zeros((1, LANES), F32)
    x2_p, ri_p, cnt_p, rt_p = _mid(xp, attn_p, yc_p, mk_b.reshape(b, N_MEM, D_MODEL), mv_b.reshape(b, N_MEM, D_MODEL),
                             cnt0, w, alpha)

    cos_s, sin_s = _rope_tables(past + jnp.arange(ss))
    q_s, _, _, yc_s, lat_s, kr_s, cst_s = _inproj(xs, conv_past, cos_s, sin_s, w)
    n_keys = past + ss
    sk = -(-n_keys // 512) * 512
    lat_all = jnp.concatenate([lat_past, lat_s, jnp.zeros((bs, sk - n_keys, KV_LORA), F32)], axis=1)
    kr_all = jnp.concatenate([kr_past, kr_s, jnp.zeros((bs, sk - n_keys, QK_ROPE), F32)], axis=1)
    kr_all = jnp.pad(kr_all, ((0, 0), (0, 0), (ROPE_LANE0, LANES - ROPE_LANE0 - QK_ROPE)))
    k_s, v_s = _kvup(lat_all, kr_all, w['w_ukv'])
    attn_s = _attention(q_s, k_s, v_s, past, n_keys)
    mk_sb = mk_s.reshape(bs, N_MEM, D_MODEL).astype(BF16)
    mv_sb = mv_s.reshape(bs, N_MEM, D_MODEL).astype(BF16)
    x2_s, ri_s, cnt, rt_s = _mid(xs, attn_s, yc_s, mk_sb, mv_sb, cnt_p, w, alpha)

    n_p, n_s = b * s, bs * ss
    counts = cnt[0, ROUTER_LANE0:ROUTER_LANE0 + N_EXPERTS].astype(I32)
    padded = (counts + MOE_ROWS - 1) // MOE_ROWS * MOE_ROWS
    pends = jnp.cumsum(padded)
    pstarts = pends - padded
    nblk = -(-2 * (n_p + n_s) // MOE_ROWS) + N_EXPERTS
    blk_start = jnp.arange(nblk, dtype=I32) * MOE_ROWS
    block_e = jnp.minimum(jnp.sum((pends[None, :] <= blk_start[:, None]).astype(I32), axis=1), N_EXPERTS - 1)
    n_used = (pends[-1] // MOE_ROWS).astype(I32).reshape(1)
    ri_p2, ri_s2 = ri_p.reshape(n_p, LANES), ri_s.reshape(n_s, LANES)
    dest_p, g_p = _slots(rt_p, pstarts)
    dest_s, g_s = _slots(rt_s, pstarts)
    x2_p2, x2_s2 = x2_p.reshape(n_p, D_MODEL), x2_s.reshape(n_s, D_MODEL)
    pad_info = jnp.concatenate([pstarts + counts, padded - counts, n_used]).astype(I32)
    slots = _dispatch(x2_p2, dest_p, g_p, x2_s2, dest_s, g_s, pad_info, nblk, MOE_ROWS)
    ys = _experts(slots, block_e, n_used, w['w_exp_gate'], w['w_exp_up'], w['w_exp_down'], MOE_ROWS)
    y_p = _combine(x2_p2, ri_p2, dest_p, g_p, ys, w['ln3_g'], w['ln3_b'], alpha).reshape(b, s, D_MODEL)
    y_s = _combine(x2_s2, ri_s2, dest_s, g_s, ys, w['ln3_g'], w['ln3_b'], alpha).reshape(bs, ss, D_MODEL)
    return (y_p, y_s, lat_p, kr_p, cst_p, mk.reshape(b, N_MEM, X_HEADS, X_HEAD_DIM),
            mv.reshape(b, N_MEM, X_HEADS, X_HEAD_DIM), lat_s, kr_s, cst_s)


def kernel(x_prompt, x_sample, cache_kv_latent, cache_k_rope, cache_conv, cache_mem_k, cache_mem_v, mem_prompt,
           w_in, q_norm_g, kv_norm_g, w_uq, w_ukv, conv_w, w_out, ln1_g, ln1_b, w_xq, w_xk, w_xv, w_xo, ln2_g,
           ln2_b, w_router_group, b_router_group, w_router_expert, b_router_expert, w_exp_gate, w_exp_up,
           w_exp_down, ln3_g, ln3_b):
    depth = w_in.shape[0]
    xp, xs = x_prompt, x_sample
    outs = [[] for _ in range(8)]
    for l in range(depth):
        w = _prep_weights(l, w_in, q_norm_g, kv_norm_g, w_uq, w_ukv, conv_w, w_out, ln1_g, ln1_b, w_xq, w_xk, w_xv,
                          w_xo, ln2_g, ln2_b, w_router_group, b_router_group, w_router_expert, b_router_expert,
                          w_exp_gate, w_exp_up, w_exp_down, ln3_g, ln3_b)
        res = _layer(l, depth, xp, xs, cache_kv_latent[l], cache_k_rope[l], cache_conv[l], cache_mem_k[l],
                     cache_mem_v[l], mem_prompt, w)
        xp, xs = res[0], res[1]
        for acc, r in zip(outs, res[2:]):
            acc.append(r)
    return (xp, xs) + tuple(jnp.stack(o) for o in outs)
```

```python
import functools
import math

import numpy as np
import jax
import jax.numpy as jnp
from jax import lax
from jax.experimental import pallas as pl
from jax.experimental.pallas import tpu as pltpu

F32 = jnp.float32
BF16 = jnp.bfloat16
I32 = jnp.int32

D_MODEL = 1024
CHUNK = 64
N_HEADS = 8
QK_NOPE = 64
QK_ROPE = 32
V_HEAD = 64
Q_LORA = 256
KV_LORA = 128
ROPE_THETA = 10000.0
MLA_SCALE = (QK_NOPE + QK_ROPE) ** -0.5
Q_PRESCALE = MLA_SCALE * math.log2(math.e)
CONV_WIDTH = 512
CONV_K = 3
N_MEM = 256
X_HEADS = 4
X_HEAD_DIM = D_MODEL // X_HEADS
X_SCALE = X_HEAD_DIM ** -0.5
N_GROUPS = 4
EXPERTS_PER_GROUP = 8
N_EXPERTS = N_GROUPS * EXPERTS_PER_GROUP
D_EXPERT = 512
LN_EPS = 1e-5
RMS_EPS = 1e-6

LANES = 128
SUBLANES = 8
ROW_CHUNKS = D_MODEL // LANES
HEAD_PAD = LANES
ROPE_LANE0 = QK_NOPE
ROUTER_LANE0 = N_GROUPS
VMEM_LIMIT = 56 * 1024 * 1024
MOE_ROWS = 512
MOE_PARTS = 2
ACT_ROW_SLAB = 32
GATHER_UNROLL = 8
KVUP_ROWS = 1536
WCAST_ROWS = 64
ATTN_TQ = 1024
ATTN_TK = 512
ATTN_SUB = 2
ATTN_ROW_SLAB = 64
LN_ROW_SLAB = 16
XATTN_ROW_SLAB = 64
ROUTE_GROUP = 256
MID_SKEW = 1
MID_ROWS = 512
MID_PARTS = 2


def _cp(sem, vmem=VMEM_LIMIT):
    return pltpu.CompilerParams(dimension_semantics=sem, vmem_limit_bytes=vmem)


def _pick_tile(n, pref):
    t = min(n, pref)
    while n % t:
        t //= 2
    return t


def _with_ones_lane(v):
    lane = lax.broadcasted_iota(I32, v.shape, 1)
    return jnp.where((lane & (HEAD_PAD - 1)) == V_HEAD, 1.0, v)


def _inproj_body(x_ref, cinit_ref, cos_ref, sin_ref, win_ref, qg_ref, kvg_ref, wuq_ref, wukv_ref, cw_ref,
                 q_ref, k_ref, v_ref, yc_ref, lat_ref, kr_ref, cst_ref, u_scr, *, tt, nj):
    j = pl.program_id(1)

    @pl.when(j == 0)
    def _():
        u_scr[6:8, :] = cinit_ref[0]

    x = x_ref[0].astype(BF16)
    proj = jnp.dot(x, win_ref[...], preferred_element_type=F32)
    cq = proj[:, 0:256]
    ckv = proj[:, 256:384]
    gb = proj[:, 384:896]
    gc = proj[:, 896:1408]
    gv = proj[:, 1408:1920]
    kr_blk = proj[:, 1920:2048]
    cos_t = cos_ref[...]
    sin_t = sin_ref[...]

    def rotate(blk):
        return blk * cos_t + pltpu.roll(blk, LANES - QK_ROPE, 1) * sin_t

    cqn = cq * lax.rsqrt(jnp.mean(cq * cq, -1, keepdims=True) + RMS_EPS) * qg_ref[...]
    ckvn = ckv * lax.rsqrt(jnp.mean(ckv * ckv, -1, keepdims=True) + RMS_EPS) * kvg_ref[...]
    lat_ref[0] = ckvn
    kr_p = rotate(kr_blk)
    kr_ref[0] = kr_p[:, ROPE_LANE0:ROPE_LANE0 + QK_ROPE]
    qq = jnp.dot(cqn.astype(BF16), wuq_ref[...], preferred_element_type=F32)
    kv = jnp.dot(ckvn.astype(BF16), wukv_ref[...], preferred_element_type=F32)
    hw = N_HEADS * HEAD_PAD
    for h in range(N_HEADS):
        sl = slice(h * HEAD_PAD, (h + 1) * HEAD_PAD)
        q_ref[0, :, sl] = (rotate(qq[:, sl]) * Q_PRESCALE).astype(BF16)
        k_ref[0, :, sl] = (kv[:, sl] + kr_p).astype(BF16)
    v_ref[0] = _with_ones_lane(kv[:, hw:2 * hw]).astype(BF16)
    u = gc * gv
    u_scr[8:8 + tt, :] = u
    conv = cw_ref[0:1, :] * u_scr[6:6 + tt, :] + cw_ref[1:2, :] * u_scr[7:7 + tt, :] + cw_ref[2:3, :] * u
    yc_ref[0] = (gb * conv).astype(BF16)
    last2 = u_scr[tt + 6:tt + 8, :]
    u_scr[6:8, :] = last2

    @pl.when(j == nj - 1)
    def _():
        cst_ref[0] = last2


def _inproj(x, conv_init, cos_t, sin_t, w):
    b, s, _ = x.shape
    tt = _pick_tile(s, 512)
    nj = s // tt
    wn = w['w_in'].shape[1]
    full = lambda shape: pl.BlockSpec(shape, lambda bi, ji: (0,) * len(shape))
    out_shapes = (
        jax.ShapeDtypeStruct((b, s, N_HEADS * HEAD_PAD), BF16),
        jax.ShapeDtypeStruct((b, s, N_HEADS * HEAD_PAD), BF16),
        jax.ShapeDtypeStruct((b, s, N_HEADS * HEAD_PAD), BF16),
        jax.ShapeDtypeStruct((b, s, CONV_WIDTH), BF16),
        jax.ShapeDtypeStruct((b, s, KV_LORA), F32),
        jax.ShapeDtypeStruct((b, s, QK_ROPE), F32),
        jax.ShapeDtypeStruct((b, CONV_K - 1, CONV_WIDTH), F32),
    )
    row = lambda width: pl.BlockSpec((1, tt, width), lambda bi, ji: (bi, ji, 0))
    return pl.pallas_call(
        functools.partial(_inproj_body, tt=tt, nj=nj),
        grid=(b, nj),
        in_specs=[
            row(D_MODEL),
            pl.BlockSpec((1, CONV_K - 1, CONV_WIDTH), lambda bi, ji: (bi, 0, 0)),
            pl.BlockSpec((tt, LANES), lambda bi, ji: (ji, 0)),
            pl.BlockSpec((tt, LANES), lambda bi, ji: (ji, 0)),
            full((D_MODEL, wn)),
            full((1, Q_LORA)),
            full((1, KV_LORA)),
            full(w['w_uq'].shape),
            full(w['w_ukv'].shape),
            full((CONV_K, CONV_WIDTH)),
        ],
        out_specs=(
            row(N_HEADS * HEAD_PAD), row(N_HEADS * HEAD_PAD), row(N_HEADS * HEAD_PAD), row(CONV_WIDTH),
            row(KV_LORA), row(QK_ROPE),
            pl.BlockSpec((1, CONV_K - 1, CONV_WIDTH), lambda bi, ji: (bi, 0, 0)),
        ),
        out_shape=out_shapes,
        scratch_shapes=[pltpu.VMEM((tt + 8, CONV_WIDTH), F32)],
        compiler_params=_cp(("parallel", "arbitrary")),
        name="inproj",
    )(x, conv_init, cos_t, sin_t, w['w_in'], w['q_norm_g'], w['kv_norm_g'], w['w_uq'], w['w_ukv'], w['conv_w'])


def _kvup_body(lat_ref, krp_ref, wukv_ref, k_ref, v_ref):
    kv = jnp.dot(lat_ref[0].astype(BF16), wukv_ref[...], preferred_element_type=F32)
    kr_p = krp_ref[0]
    hw = N_HEADS * HEAD_PAD
    for h in range(N_HEADS):
        sl = slice(h * HEAD_PAD, (h + 1) * HEAD_PAD)
        k_ref[0, :, sl] = (kv[:, sl] + kr_p).astype(BF16)
    v_ref[0] = _with_ones_lane(kv[:, hw:2 * hw]).astype(BF16)


def _kvup(lat, kr_padded, w_ukv):
    b, s, _ = lat.shape
    tt = _pick_tile(s, KVUP_ROWS)
    row = lambda width: pl.BlockSpec((1, tt, width), lambda bi, ji: (bi, ji, 0))
    return pl.pallas_call(
        _kvup_body,
        grid=(b, s // tt),
        in_specs=[row(KV_LORA), row(LANES), pl.BlockSpec(w_ukv.shape, lambda bi, ji: (0, 0))],
        out_specs=(row(N_HEADS * HEAD_PAD), row(N_HEADS * HEAD_PAD)),
        out_shape=(jax.ShapeDtypeStruct((b, s, N_HEADS * HEAD_PAD), BF16),
                   jax.ShapeDtypeStruct((b, s, N_HEADS * HEAD_PAD), BF16)),
        compiler_params=_cp(("parallel", "parallel")),
        name="kvup",
    )(lat, kr_padded, w_ukv)


def _attn_body(qi_ref, ki_ref, fl_ref, q_ref, k_ref, v_ref, o_ref, m_scr, acc_scr, s_scr, p_scr, a_scr,
               *, tq, tk, n_sub, rs, q_pos0, n_valid, combos):
    step = pl.program_id(1)
    qi = qi_ref[step]
    ki = ki_ref[step]
    flags = fl_ref[step]
    reps = tk // LANES
    th = tq // n_sub

    @pl.when((flags & 1) != 0)
    def _():
        m_scr[...] = jnp.full(m_scr.shape, -jnp.inf, F32)
        acc_scr[...] = jnp.zeros(acc_scr.shape, F32)

    def scores(item, buf):
        sub, h, _ = item
        hs = slice(h * HEAD_PAD, (h + 1) * HEAD_PAD)
        s_scr[buf] = lax.dot_general(q_ref[0, sub * th:(sub + 1) * th, hs], k_ref[0, :, hs],
                                     (((1,), (1,)), ((), ())), preferred_element_type=F32)

    def softmax_pv(item, buf):
        sub, h, masked = item
        hs = slice(h * HEAD_PAD, (h + 1) * HEAD_PAD)
        for r in range(th // rs):
            rows = slice(r * rs, (r + 1) * rs)
            arows = slice(sub * th + r * rs, sub * th + (r + 1) * rs)
            s_r = s_scr[buf, rows, :]
            if masked:
                qpos = q_pos0 + qi * tq + sub * th + r * rs + lax.broadcasted_iota(I32, (rs, tk), 0)
                kpos = ki * tk + lax.broadcasted_iota(I32, (rs, tk), 1)
                mask = (kpos >> 6) <= (qpos >> 6)
                if n_valid is not None:
                    mask = mask & (kpos < n_valid)
                s_r = jnp.where(mask, s_r, -jnp.inf)
            m_old = m_scr[h, arows, :]
            m_new = jnp.maximum(m_old, jnp.max(s_r, axis=-1, keepdims=True))
            a_scr[buf, rows, :] = jnp.exp2(m_old - m_new)
            m_rep = jnp.concatenate([m_new] * reps, axis=1)
            p_scr[buf, rows, :] = jnp.exp2(s_r - m_rep).astype(BF16)
            m_scr[h, arows, :] = m_new
        pv = jnp.dot(p_scr[buf], v_ref[0, :, hs], preferred_element_type=F32)
        srows = slice(sub * th, (sub + 1) * th)
        acc_scr[h, srows, :] = a_scr[buf] * acc_scr[h, srows, :] + pv

    def run(modes):
        items = [(sub, h, mode == 2) for sub, mode in enumerate(modes) if mode != 0 for h in range(N_HEADS)]
        scores(items[0], 0)
        for n, item in enumerate(items):
            if n + 1 < len(items):
                scores(items[n + 1], (n + 1) % 2)
            softmax_pv(item, n % 2)

    for code, modes in combos:
        @pl.when((flags >> 2) == code)
        def _(modes=modes):
            run(modes)

    @pl.when((flags & 2) != 0)
    def _():
        for h in range(N_HEADS):
            acc = acc_scr[h]
            o_ref[0, :, h * V_HEAD:(h + 1) * V_HEAD] = (acc[:, 0:V_HEAD] / acc[:, V_HEAD:V_HEAD + 1]).astype(BF16)


def _attn_tables(nq, nk, tq, tk, n_sub, q_pos0, n_valid):
    th = tq // n_sub
    qi_l, ki_l, fl_l, combos = [], [], [], {}
    for qi in range(nq):
        sub_lo = [q_pos0 + qi * tq + j * th for j in range(n_sub)]
        sub_last = []
        for lo in sub_lo:
            last_pos = ((lo + th - 1) // CHUNK) * CHUNK + CHUNK - 1
            if n_valid is not None:
                last_pos = min(last_pos, n_valid - 1)
            sub_last.append(min(nk - 1, last_pos // tk))
        k_last = max(sub_last)
        for ki in range(k_last + 1):
            k_hi = ki * tk + tk - 1
            modes = []
            for lo, last in zip(sub_lo, sub_last):
                if ki > last:
                    modes.append(0)
                elif (k_hi // CHUNK) > (lo // CHUNK) or (n_valid is not None and k_hi >= n_valid):
                    modes.append(2)
                else:
                    modes.append(1)
            code = sum(m * 3 ** j for j, m in enumerate(modes))
            combos[code] = tuple(modes)
            qi_l.append(qi); ki_l.append(ki)
            fl_l.append((1 if ki == 0 else 0) | (2 if ki == k_last else 0) | (code << 2))
    to_arr = lambda vals: jnp.asarray(np.array(vals, np.int32))
    return to_arr(qi_l), to_arr(ki_l), to_arr(fl_l), tuple(sorted(combos.items()))


def _attention(q, k, v, q_pos0, n_valid):
    b, sq, _ = q.shape
    sk = k.shape[1]
    tq = _pick_tile(sq, ATTN_TQ)
    tk = _pick_tile(sk, ATTN_TK)
    n_sub = ATTN_SUB if tq % (ATTN_SUB * 2 * SUBLANES) == 0 else 1
    th = tq // n_sub
    qi_t, ki_t, fl_t, combos = _attn_tables(sq // tq, sk // tk, tq, tk, n_sub, q_pos0, n_valid)
    n_steps = int(qi_t.shape[0])
    grid_spec = pltpu.PrefetchScalarGridSpec(
        num_scalar_prefetch=3,
        grid=(b, n_steps),
        in_specs=[
            pl.BlockSpec((1, tq, N_HEADS * HEAD_PAD), lambda bi, si, qt, kt, ft: (bi, qt[si], 0)),
            pl.BlockSpec((1, tk, N_HEADS * HEAD_PAD), lambda bi, si, qt, kt, ft: (bi, kt[si], 0)),
            pl.BlockSpec((1, tk, N_HEADS * HEAD_PAD), lambda bi, si, qt, kt, ft: (bi, kt[si], 0)),
        ],
        out_specs=pl.BlockSpec((1, tq, N_HEADS * V_HEAD), lambda bi, si, qt, kt, ft: (bi, qt[si], 0)),
        scratch_shapes=[
            pltpu.VMEM((N_HEADS, tq, LANES), F32),
            pltpu.VMEM((N_HEADS, tq, LANES), F32),
            pltpu.VMEM((2, th, tk), F32),
            pltpu.VMEM((2, th, tk), BF16),
            pltpu.VMEM((2, th, LANES), F32),
        ],
    )
    return pl.pallas_call(
        functools.partial(_attn_body, tq=tq, tk=tk, n_sub=n_sub, rs=min(th, ATTN_ROW_SLAB), q_pos0=q_pos0,
                          n_valid=n_valid, combos=combos),
        grid_spec=grid_spec,
        out_shape=jax.ShapeDtypeStruct((b, sq, N_HEADS * V_HEAD), BF16),
        compiler_params=_cp(("parallel", "arbitrary")),
        name="mla_attn",
    )(qi_t, ki_t, fl_t, q, k, v)


def _memkv_body(mem_ref, wk_ref, wv_ref, mk_ref, mv_ref, mkb_ref, mvb_ref):
    m = mem_ref[...].astype(BF16)
    mk = jnp.dot(m, wk_ref[...], preferred_element_type=F32)
    mv = jnp.dot(m, wv_ref[...], preferred_element_type=F32)
    mk_ref[...] = mk
    mv_ref[...] = mv
    mkb_ref[...] = mk.astype(BF16)
    mvb_ref[...] = mv.astype(BF16)


def _memkv(mem2d, w_xk, w_xv):
    n = mem2d.shape[0]
    tt = _pick_tile(n, 256)
    row = pl.BlockSpec((tt, D_MODEL), lambda i: (i, 0))
    wspec = pl.BlockSpec((D_MODEL, D_MODEL), lambda i: (0, 0))
    return pl.pallas_call(
        _memkv_body,
        grid=(n // tt,),
        in_specs=[row, wspec, wspec],
        out_specs=(row, row, row, row),
        out_shape=(jax.ShapeDtypeStruct((n, D_MODEL), F32), jax.ShapeDtypeStruct((n, D_MODEL), F32),
                   jax.ShapeDtypeStruct((n, D_MODEL), BF16), jax.ShapeDtypeStruct((n, D_MODEL), BF16)),
        compiler_params=_cp(("parallel",)),
        name="memkv",
    )(mem2d, w_xk, w_xv)


def _layer_norm(x, g, b):
    mu = jnp.mean(x, -1, keepdims=True)
    xc = x - mu
    var = jnp.mean(xc * xc, -1, keepdims=True)
    return xc * lax.rsqrt(var + LN_EPS) * g + b


def _mid_body(x_ref, at_ref, yc_ref, mk_ref, mv_ref, cnt0_ref, low_ref, wo_ref, g1_ref, b1_ref, wq_ref, wxo_ref,
              g2_ref, b2_ref, wr_ref, br_ref, x2_ref, ri_ref, cnt_ref, rt_ref,
              cnt_scr, a_scr, xb_scr, q_scr, sc_scr, p_scr, o_scr, lg_scr, *, tt, alpha):
    step = pl.program_id(0)

    @pl.when(step == 0)
    def _():
        cnt_scr[...] = cnt0_ref[...]
        lg_scr[...] = jnp.zeros(lg_scr.shape, F32)

    cur_slot = step % 2
    prev_slot = 1 - cur_slot
    routed = (step > 0).astype(F32)

    n_part = MID_PARTS if tt % (MID_PARTS * SUBLANES * 2) == 0 else 1
    pr = tt // n_part
    ln_rs = min(pr, LN_ROW_SLAB)
    sm_rs = min(pr, XATTN_ROW_SLAB)
    c_exp = X_SCALE * math.log2(math.e)
    lane = lax.broadcasted_iota(I32, (pr, LANES), 1)

    def part_rows(k):
        return slice(k * pr, (k + 1) * pr)

    def out_proj(k):
        rp = part_rows(k)
        mix = jnp.concatenate([at_ref[0, rp, :], yc_ref[0, rp, :]], axis=-1)
        a_scr[rp, :] = jnp.dot(mix, wo_ref[...], preferred_element_type=F32)

    def norm1(k):
        for r in range(pr // ln_rs):
            rows = slice(k * pr + r * ln_rs, k * pr + (r + 1) * ln_rs)
            x1 = _layer_norm(alpha * x_ref[0, rows, :] + a_scr[rows, :], g1_ref[...], b1_ref[...])
            x2_ref[0, rows, :] = x1
            xb_scr[rows, :] = x1.astype(BF16)

    def q_proj(k):
        rp = part_rows(k)
        q_scr[rp, :] = jnp.dot(xb_scr[rp, :], wq_ref[...], preferred_element_type=F32).astype(BF16)

    def cross_attn(k):
        rp = part_rows(k)
        for h in range(X_HEADS):
            sl = slice(h * X_HEAD_DIM, (h + 1) * X_HEAD_DIM)
            b2 = h % 2
            sc_scr[k, b2] = lax.dot_general(q_scr[rp, sl], mk_ref[0, :, sl], (((1,), (1,)), ((), ())),
                                            preferred_element_type=F32)
            for r in range(pr // sm_rs):
                rows = slice(r * sm_rs, (r + 1) * sm_rs)
                s_r = sc_scr[k, b2, rows, :]
                e = jnp.exp2((s_r - jnp.max(s_r, -1, keepdims=True)) * c_exp)
                p_scr[k, b2, rows, :] = (e / jnp.sum(e, -1, keepdims=True)).astype(BF16)
            o_scr[rp, sl] = jnp.dot(p_scr[k, b2], mv_ref[0, :, sl], preferred_element_type=F32).astype(BF16)

    def x_out_proj(k):
        rp = part_rows(k)
        a_scr[rp, :] = jnp.dot(o_scr[rp, :], wxo_ref[...], preferred_element_type=F32)

    def norm2(k):
        for r in range(pr // ln_rs):
            rows = slice(k * pr + r * ln_rs, k * pr + (r + 1) * ln_rs)
            x2 = _layer_norm(alpha * x2_ref[0, rows, :] + a_scr[rows, :], g2_ref[...], b2_ref[...])
            x2_ref[0, rows, :] = x2
            xb_scr[rows, :] = x2.astype(BF16)

    def router_logits(k):
        rp = part_rows(k)
        lg_scr[cur_slot, rp, :] = jnp.dot(xb_scr[rp, :], wr_ref[...], preferred_element_type=F32) + br_ref[...]

    def route(k):
        rp = part_rows(k)
        logits = lg_scr[prev_slot, rp, :]
        neg = -jnp.inf
        is_g = lane < N_GROUPS
        lg = jnp.where(is_g, logits, neg)
        mg = jnp.max(lg, -1, keepdims=True)
        g_idx = jnp.min(jnp.where(lg == mg, lane, LANES), -1, keepdims=True)
        pg = 1.0 / jnp.sum(jnp.where(is_g, jnp.exp(logits - mg), 0.0), -1, keepdims=True)
        in_grp = ((lane >= ROUTER_LANE0) & (lane < ROUTER_LANE0 + N_EXPERTS)
                  & (((lane - ROUTER_LANE0) >> 3) == g_idx))
        le = jnp.where(in_grp, logits, neg)
        v1 = jnp.max(le, -1, keepdims=True)
        i1 = jnp.min(jnp.where(le == v1, lane, LANES), -1, keepdims=True)
        le2 = jnp.where(lane == i1, neg, le)
        v2 = jnp.max(le2, -1, keepdims=True)
        i2 = jnp.min(jnp.where(le2 == v2, lane, LANES), -1, keepdims=True)
        e2 = jnp.exp(v2 - v1)
        den = 1.0 + e2
        gate1 = (1.0 / den) * pg
        gate2 = (e2 / den) * pg
        oh1 = (lane == i1).astype(F32)
        oh2 = (lane == i2).astype(F32)
        oh = oh1 + oh2
        base = cnt_scr[...] + jnp.dot(low_ref[...], oh.astype(BF16), preferred_element_type=F32)
        rank1 = jnp.sum(oh1 * base, -1, keepdims=True)
        rank2 = jnp.sum(oh2 * base, -1, keepdims=True)
        cnt_scr[...] = cnt_scr[...] + jnp.sum(oh, 0, keepdims=True) * routed
        e1f = (i1 - ROUTER_LANE0).astype(F32)
        e2f = (i2 - ROUTER_LANE0).astype(F32)
        ri = jnp.where(lane == 0, e1f, jnp.where(lane == 1, e2f, jnp.where(
            lane == 2, rank1, jnp.where(lane == 3, rank2, jnp.where(lane == 4, gate1, jnp.where(
                lane == 5, gate2, 0.0))))))
        ri_ref[0, rp, :] = ri
        ri_t = jnp.transpose(ri)[0:SUBLANES, :]
        grp = min(pr, ROUTE_GROUP)
        for j in range(pr // grp):
            rt_ref[0, k * (pr // grp) + j] = ri_t[:, j * grp:(j + 1) * grp]

    stages = (out_proj, norm1, q_proj, cross_attn, x_out_proj, norm2, router_logits)
    matmul_stages = (out_proj, q_proj, x_out_proj, router_logits)
    for t in range(len(stages) + MID_SKEW * (n_part - 1)):
        todo = [(stages[t - MID_SKEW * k], k) for k in range(n_part) if 0 <= t - MID_SKEW * k < len(stages)]
        for fn, k in sorted(todo, key=lambda fk: fk[0] not in matmul_stages):
            fn(k)
        if t < n_part:
            route(t)
    cnt_ref[...] = cnt_scr[...]


def _mid(x, attn, yconv, mk_b, mv_b, cnt0, w, alpha):
    b, s, _ = x.shape
    tt = _pick_tile(s, MID_ROWS)
    nj = s // tt
    n_tiles = b * nj
    cur = lambda i: jnp.minimum(i, n_tiles - 1)
    prev = lambda i: jnp.maximum(i - 1, 0)
    row = lambda width: pl.BlockSpec((1, tt, width), lambda i: (cur(i) // nj, cur(i) % nj, 0))
    row_prev = lambda width: pl.BlockSpec((1, tt, width), lambda i: (prev(i) // nj, prev(i) % nj, 0))
    full = lambda shape: pl.BlockSpec(shape, lambda i: (0,) * len(shape))
    mem = pl.BlockSpec((1, N_MEM, D_MODEL), lambda i: (cur(i) // nj, 0, 0))
    vec = full((1, D_MODEL))
    n_part = MID_PARTS if tt % (MID_PARTS * SUBLANES * 2) == 0 else 1
    pr = tt // n_part
    grp = min(pr, ROUTE_GROUP)
    lower = jnp.tril(jnp.ones((pr, pr), F32), -1).astype(BF16)
    return pl.pallas_call(
        functools.partial(_mid_body, tt=tt, alpha=alpha),
        grid=(n_tiles + 1,),
        in_specs=[row(D_MODEL), row(N_HEADS * V_HEAD), row(CONV_WIDTH), mem, mem, full((1, LANES)), full((pr, pr)),
                  full((D_MODEL, D_MODEL)), vec, vec, full((D_MODEL, D_MODEL)), full((D_MODEL, D_MODEL)),
                  vec, vec, full((D_MODEL, LANES)), full((1, LANES))],
        out_specs=(row(D_MODEL), row_prev(LANES), full((1, LANES)),
                   pl.BlockSpec((1, tt // grp, SUBLANES, grp), lambda i: (prev(i) // nj, prev(i) % nj, 0, 0))),
        out_shape=(jax.ShapeDtypeStruct((b, s, D_MODEL), F32), jax.ShapeDtypeStruct((b, s, LANES), F32),
                   jax.ShapeDtypeStruct((1, LANES), F32),
                   jax.ShapeDtypeStruct((b, s // grp, SUBLANES, grp), F32)),
        scratch_shapes=[
            pltpu.VMEM((1, LANES), F32),
            pltpu.VMEM((tt, D_MODEL), F32),
            pltpu.VMEM((tt, D_MODEL), BF16),
            pltpu.VMEM((tt, D_MODEL), BF16),
            pltpu.VMEM((n_part, 2, pr, N_MEM), F32),
            pltpu.VMEM((n_part, 2, pr, N_MEM), BF16),
            pltpu.VMEM((tt, D_MODEL), BF16),
            pltpu.VMEM((2, tt, LANES), F32),
        ],
        compiler_params=_cp(("arbitrary",)),
        name="mid",
    )(x, attn, yconv, mk_b, mv_b, cnt0, lower, w['w_out'], w['ln1_g'], w['ln1_b'], w['w_xq'], w['w_xo'],
      w['ln2_g'], w['ln2_b'], w['w_router'], w['b_router'])


def _row_slice(ref, row):
    return ref.at[pl.ds(pl.multiple_of(row * SUBLANES, SUBLANES), SUBLANES), :]


def _slot_index(t, kk, g, n):
    if g >= n:
        return kk * g + t
    shift = g.bit_length() - 1
    return ((t >> shift) << (shift + 1)) + kk * g + (t & (g - 1))


def _rows_slice(ref, row, n_rows):
    return ref.at[pl.ds(pl.multiple_of(row * SUBLANES, SUBLANES), n_rows * SUBLANES), :]


def _dispatch_body(pad_ref, dest_ref, dest_s_ref, x_ref, x_s_ref, xs_ref, buf0, buf1, zbuf, sem0, sem1, zsem,
                   *, tt, ts, nt, nblk, rows, g_p, g_s):
    i = pl.program_id(0)
    bufs = (buf0, buf1)
    sems = (sem0, sem1)
    len_bits = rows.bit_length() - 1

    def scatter_rows(buf, sem, src_ref, d_ref, n, g):
        for c in range(ROW_CHUNKS):
            buf[pl.ds(c, n, stride=SUBLANES), :] = src_ref[:, c * LANES:(c + 1) * LANES]
        unroll = min(n, GATHER_UNROLL)

        def issue(j, carry):
            for u in range(unroll):
                t = j * unroll + u
                src = _row_slice(buf, t)
                for kk in range(2):
                    d = d_ref[0, 0, _slot_index(t, kk, g, n)]
                    pltpu.make_async_copy(src, _row_slice(xs_ref, d), sem).start()
            return carry

        lax.fori_loop(0, n // unroll, issue, 0)

    def wait_rows(buf, sem, n):
        for _ in range(2):
            pltpu.make_async_copy(_rows_slice(buf, 0, n), _rows_slice(xs_ref, 0, n), sem).wait()

    def zero_fill(wait):
        def fire(copy):
            if wait:
                copy.wait()
            else:
                copy.start()

        def per_expert(e, carry):
            first = pad_ref[e]
            n_pad = pad_ref[N_EXPERTS + e]
            for bit in range(len_bits):
                size = 1 << bit
                off = (n_pad >> (bit + 1)) << (bit + 1)

                @pl.when(((n_pad >> bit) & 1) == 1)
                def _():
                    fire(pltpu.make_async_copy(_rows_slice(zbuf, 0, size), _rows_slice(xs_ref, first + off, size),
                                               zsem))
            return carry

        lax.fori_loop(0, N_EXPERTS, per_expert, 0)

        def per_block(j, carry):
            fire(pltpu.make_async_copy(zbuf, _rows_slice(xs_ref, j * rows, rows), zsem))
            return carry

        lax.fori_loop(pad_ref[2 * N_EXPERTS], nblk, per_block, 0)

    def run(slot):
        @pl.when(i < nt)
        def _():
            scatter_rows(bufs[slot], sems[slot], x_ref, dest_ref, tt, g_p)

        @pl.when(i == nt)
        def _():
            zbuf[...] = jnp.zeros(zbuf.shape, F32)
            scatter_rows(bufs[slot], sems[slot], x_s_ref, dest_s_ref, ts, g_s)
            zero_fill(False)

        @pl.when(i > 0)
        def _():
            wait_rows(bufs[1 - slot], sems[1 - slot], tt)

        @pl.when(i == nt)
        def _():
            wait_rows(bufs[slot], sems[slot], ts)
            zero_fill(True)

    @pl.when(i % 2 == 0)
    def _():
        run(0)

    @pl.when(i % 2 == 1)
    def _():
        run(1)


def _dispatch(x_p, dest_p, g_p, x_s, dest_s, g_s, pad_info, nblk, rows):
    n_p, n_s = x_p.shape[0], x_s.shape[0]
    tt = _pick_tile(n_p, 256)
    nt = n_p // tt
    assert n_s <= tt
    last = lambda i, pad: jnp.minimum(i, nt - 1)
    grid_spec = pltpu.PrefetchScalarGridSpec(
        num_scalar_prefetch=1,
        grid=(nt + 1,),
        in_specs=[
            pl.BlockSpec((1, 1, 2 * tt), lambda i, pad: (last(i, pad), 0, 0), memory_space=pltpu.SMEM),
            pl.BlockSpec((1, 1, 2 * n_s), lambda i, pad: (0, 0, 0), memory_space=pltpu.SMEM),
            pl.BlockSpec((tt, D_MODEL), lambda i, pad: (last(i, pad), 0)),
            pl.BlockSpec((n_s, D_MODEL), lambda i, pad: (0, 0)),
        ],
        out_specs=pl.BlockSpec(memory_space=pl.ANY),
        scratch_shapes=[pltpu.VMEM((tt * SUBLANES, LANES), F32), pltpu.VMEM((tt * SUBLANES, LANES), F32),
                        pltpu.VMEM((rows * SUBLANES, LANES), F32),
                        pltpu.SemaphoreType.DMA, pltpu.SemaphoreType.DMA, pltpu.SemaphoreType.DMA],
    )
    return pl.pallas_call(
        functools.partial(_dispatch_body, tt=tt, ts=n_s, nt=nt, nblk=nblk, rows=rows, g_p=g_p, g_s=g_s),
        grid_spec=grid_spec,
        out_shape=jax.ShapeDtypeStruct((nblk * rows * SUBLANES, LANES), F32),
        compiler_params=_cp(("arbitrary",)),
        name="moe_dispatch",
    )(pad_info, dest_p.reshape(nt, 1, 2 * tt), dest_s.reshape(1, 1, 2 * n_s), x_p, x_s)


def _experts_body(be_ref, nu_ref, x_ref, wg_ref, wu_ref, wd_ref, y_ref, xb_scr, g_scr, u_scr, h_scr,
                  wgb_scr, wub_scr, wdb_scr, *, rows):
    i = pl.program_id(0)
    cur = jnp.minimum(i, nu_ref[0] - 1)
    new_expert = (i == 0) | (be_ref[cur] != be_ref[jnp.maximum(cur - 1, 0)])

    @pl.when((i < nu_ref[0]) & new_expert)
    def _():
        for r in range(0, D_MODEL, WCAST_ROWS):
            wgb_scr[r:r + WCAST_ROWS, :] = wg_ref[0, r:r + WCAST_ROWS, :].astype(BF16)
            wub_scr[r:r + WCAST_ROWS, :] = wu_ref[0, r:r + WCAST_ROWS, :].astype(BF16)
        for r in range(0, D_EXPERT, WCAST_ROWS // 2):
            wdb_scr[r:r + WCAST_ROWS // 2, :] = wd_ref[0, r:r + WCAST_ROWS // 2, :].astype(BF16)
    n_part = MOE_PARTS
    pr = rows // n_part
    act_rs = min(pr, ACT_ROW_SLAB)

    def load(k):
        base = k * pr * SUBLANES
        xb_scr[k] = jnp.concatenate([x_ref[pl.ds(base + c, pr, stride=SUBLANES), :] for c in range(ROW_CHUNKS)],
                                    axis=-1).astype(BF16)

    def gate_up(k):
        g_scr[k] = jnp.dot(xb_scr[k], wgb_scr[...], preferred_element_type=F32)
        u_scr[k] = jnp.dot(xb_scr[k], wub_scr[...], preferred_element_type=F32)

    def act(k):
        for r in range(pr // act_rs):
            rows_r = slice(r * act_rs, (r + 1) * act_rs)
            g = g_scr[k, rows_r, :]
            h_scr[k, rows_r, :] = ((g * jax.nn.sigmoid(g)) * u_scr[k, rows_r, :]).astype(BF16)

    def down(k):
        y = jnp.dot(h_scr[k], wdb_scr[...], preferred_element_type=F32)
        base = k * pr * SUBLANES
        for c in range(ROW_CHUNKS):
            y_ref[pl.ds(base + c, pr, stride=SUBLANES), :] = y[:, c * LANES:(c + 1) * LANES]

    @pl.when(i < nu_ref[0])
    def _():
        stages = (load, gate_up, act, down)
        matmul_stages = (gate_up, down)
        for t in range(len(stages) + n_part - 1):
            todo = [(stages[t - k], k) for k in range(n_part) if 0 <= t - k < len(stages)]
            for fn, k in sorted(todo, key=lambda fk: fk[0] not in matmul_stages):
                fn(k)

    @pl.when(i >= nu_ref[0])
    def _():
        y_ref[...] = jnp.zeros(y_ref.shape, F32)


def _experts(xs, block_e, n_used, wg, wu, wd, rows):
    nblk = xs.shape[0] // (rows * SUBLANES)
    clamp = lambda i, nu: jnp.minimum(i, nu[0] - 1)
    grid_spec = pltpu.PrefetchScalarGridSpec(
        num_scalar_prefetch=2,
        grid=(nblk,),
        in_specs=[
            pl.BlockSpec((rows * SUBLANES, LANES), lambda i, be, nu: (clamp(i, nu), 0)),
            pl.BlockSpec((1, D_MODEL, D_EXPERT), lambda i, be, nu: (be[clamp(i, nu)], 0, 0)),
            pl.BlockSpec((1, D_MODEL, D_EXPERT), lambda i, be, nu: (be[clamp(i, nu)], 0, 0)),
            pl.BlockSpec((1, D_EXPERT, D_MODEL), lambda i, be, nu: (be[clamp(i, nu)], 0, 0)),
        ],
        out_specs=pl.BlockSpec((rows * SUBLANES, LANES), lambda i, be, nu: (i, 0)),
        scratch_shapes=[
            pltpu.VMEM((MOE_PARTS, rows // MOE_PARTS, D_MODEL), BF16),
            pltpu.VMEM((MOE_PARTS, rows // MOE_PARTS, D_EXPERT), F32),
            pltpu.VMEM((MOE_PARTS, rows // MOE_PARTS, D_EXPERT), F32),
            pltpu.VMEM((MOE_PARTS, rows // MOE_PARTS, D_EXPERT), BF16),
            pltpu.VMEM((D_MODEL, D_EXPERT), BF16),
            pltpu.VMEM((D_MODEL, D_EXPERT), BF16),
            pltpu.VMEM((D_EXPERT, D_MODEL), BF16),
        ],
    )
    return pl.pallas_call(
        functools.partial(_experts_body, rows=rows),
        grid_spec=grid_spec,
        out_shape=jax.ShapeDtypeStruct(xs.shape, F32),
        compiler_params=_cp(("arbitrary",)),
        name="moe_experts",
    )(block_e, n_used, xs, wg, wu, wd)


def _combine_body(dest_ref, destn_ref, x_ref, ri_ref, g3_ref, b3_ref, ys_ref, o_ref,
                  b00, b01, b10, b11, sem0, sem1, *, tt, nt, g, alpha):
    i = pl.program_id(0)
    bufs = ((b00, b01), (b10, b11))
    sems = (sem0, sem1)
    rs = min(tt, LN_ROW_SLAB)
    unroll = min(tt, GATHER_UNROLL)

    def gather_rows(dref, slot):
        def copy(t, kk):
            d = dref[0, 0, _slot_index(t, kk, g, tt)]
            return pltpu.make_async_copy(_row_slice(ys_ref, d), _row_slice(bufs[slot][kk], t), sems[slot])

        def body(j, carry):
            for u in range(unroll):
                for kk in range(2):
                    copy(j * unroll + u, kk).start()
            return carry

        lax.fori_loop(0, tt // unroll, body, 0)

    def wait_rows(slot):
        for kk in range(2):
            pltpu.make_async_copy(ys_ref.at[pl.ds(0, tt * SUBLANES), :], bufs[slot][kk], sems[slot]).wait()

    def run(slot):
        if slot == 0:
            @pl.when(i == 0)
            def _():
                gather_rows(dest_ref, 0)

        @pl.when(i + 1 < nt)
        def _():
            gather_rows(destn_ref, 1 - slot)

        wait_rows(slot)
        for r in range(tt // rs):
            rows = slice(r * rs, (r + 1) * rs)
            base = r * rs * SUBLANES
            y0 = jnp.concatenate([bufs[slot][0][pl.ds(base + c, rs, stride=SUBLANES), :]
                                  for c in range(ROW_CHUNKS)], axis=-1)
            y1 = jnp.concatenate([bufs[slot][1][pl.ds(base + c, rs, stride=SUBLANES), :]
                                  for c in range(ROW_CHUNKS)], axis=-1)
            ri = ri_ref[rows, :]
            moe = y0 * ri[:, 4:5] + y1 * ri[:, 5:6]
            o_ref[rows, :] = _layer_norm(alpha * x_ref[rows, :] + moe, g3_ref[...], b3_ref[...])

    @pl.when(i % 2 == 0)
    def _():
        run(0)

    @pl.when(i % 2 == 1)
    def _():
        run(1)


def _combine(x2d, rinfo, dest, g, ys, g3, b3, alpha):
    n = x2d.shape[0]
    tt = _pick_tile(n, 256)
    nt = n // tt
    dest3 = dest.reshape(nt, 1, 2 * tt)
    vec = pl.BlockSpec((1, D_MODEL), lambda i: (0, 0))
    stage = pltpu.VMEM((tt * SUBLANES, LANES), F32)
    return pl.pallas_call(
        functools.partial(_combine_body, tt=tt, nt=nt, g=g, alpha=alpha),
        grid=(nt,),
        in_specs=[
            pl.BlockSpec((1, 1, 2 * tt), lambda i: (i, 0, 0), memory_space=pltpu.SMEM),
            pl.BlockSpec((1, 1, 2 * tt), lambda i: (jnp.minimum(i + 1, nt - 1), 0, 0), memory_space=pltpu.SMEM),
            pl.BlockSpec((tt, D_MODEL), lambda i: (i, 0)),
            pl.BlockSpec((tt, LANES), lambda i: (i, 0)),
            vec, vec,
            pl.BlockSpec(memory_space=pl.ANY),
        ],
        out_specs=pl.BlockSpec((tt, D_MODEL), lambda i: (i, 0)),
        out_shape=jax.ShapeDtypeStruct((n, D_MODEL), F32),
        scratch_shapes=[stage, stage, stage, stage, pltpu.SemaphoreType.DMA, pltpu.SemaphoreType.DMA],
        compiler_params=_cp(("arbitrary",)),
        name="moe_combine",
    )(dest3, dest3, x2d, rinfo, g3, b3, ys)


def _rope_tables(pos):
    half = QK_ROPE // 2
    inv = ROPE_THETA ** (-jnp.arange(half, dtype=F32) / half)
    ang = pos.astype(F32)[:, None] * inv[None, :]
    cos, sin = jnp.cos(ang), jnp.sin(ang)
    n = pos.shape[0]
    pad_r = LANES - ROPE_LANE0 - QK_ROPE
    cos_t = jnp.concatenate([jnp.ones((n, ROPE_LANE0), F32), cos, cos, jnp.zeros((n, pad_r), F32)], -1)
    sin_t = jnp.concatenate([jnp.zeros((n, ROPE_LANE0), F32), sin, sin, jnp.zeros((n, pad_r), F32)], -1)
    return cos_t, sin_t


def _swap_neg(wr):
    half = QK_ROPE // 2
    return jnp.concatenate([-wr[:, half:], wr[:, :half]], axis=1)


def _prep_weights(l, w_in, q_norm_g, kv_norm_g, w_uq, w_ukv, conv_w, w_out, ln1_g, ln1_b, w_xq, w_xk, w_xv, w_xo,
                  ln2_g, ln2_b, w_router_group, b_router_group, w_router_expert, b_router_expert,
                  w_exp_gate, w_exp_up, w_exp_down, ln3_g, ln3_b):
    wi = w_in[l]
    c0 = Q_LORA + KV_LORA
    w_kr = wi[:, c0:c0 + QK_ROPE]
    zl = jnp.zeros((D_MODEL, ROPE_LANE0), F32)
    w_in_p = jnp.concatenate([wi[:, :c0], wi[:, c0 + QK_ROPE:], zl, w_kr, _swap_neg(w_kr)], axis=1)
    wq = w_uq[l].reshape(Q_LORA, N_HEADS, QK_NOPE + QK_ROPE)
    wq_rot = jnp.concatenate([-wq[:, :, QK_NOPE + QK_ROPE // 2:], wq[:, :, QK_NOPE:QK_NOPE + QK_ROPE // 2]], axis=2)
    wq_a = jnp.concatenate([wq, wq_rot], axis=2).reshape(Q_LORA, N_HEADS * HEAD_PAD)
    wkv = w_ukv[l].reshape(KV_LORA, N_HEADS, QK_NOPE + V_HEAD)
    wk_p = jnp.concatenate([wkv[:, :, :QK_NOPE], jnp.zeros((KV_LORA, N_HEADS, HEAD_PAD - QK_NOPE), F32)], axis=2)
    wk_p = wk_p.reshape(KV_LORA, N_HEADS * HEAD_PAD)
    wv_p = jnp.concatenate([wkv[:, :, QK_NOPE:], jnp.zeros((KV_LORA, N_HEADS, HEAD_PAD - V_HEAD), F32)], axis=2)
    wv_p = wv_p.reshape(KV_LORA, N_HEADS * HEAD_PAD)
    w_router = jnp.concatenate([w_router_group[l], w_router_expert[l],
                                jnp.zeros((D_MODEL, LANES - N_GROUPS - N_EXPERTS), F32)], axis=1)
    b_router = jnp.concatenate([b_router_group[l], b_router_expert[l].reshape(-1),
                                jnp.zeros((LANES - N_GROUPS - N_EXPERTS,), F32)]).reshape(1, LANES)
    return dict(
        w_in=w_in_p.astype(BF16),
        q_norm_g=q_norm_g[l].reshape(1, Q_LORA), kv_norm_g=kv_norm_g[l].reshape(1, KV_LORA),
        w_uq=wq_a.astype(BF16),
        w_ukv=jnp.concatenate([wk_p, wv_p], axis=1).astype(BF16),
        conv_w=conv_w[l],
        w_out=w_out[l].astype(BF16), ln1_g=ln1_g[l].reshape(1, -1), ln1_b=ln1_b[l].reshape(1, -1),
        w_xq=w_xq[l].astype(BF16), w_xk=w_xk[l].astype(BF16), w_xv=w_xv[l].astype(BF16),
        w_xo=w_xo[l].astype(BF16), ln2_g=ln2_g[l].reshape(1, -1), ln2_b=ln2_b[l].reshape(1, -1),
        w_router=w_router.astype(BF16), b_router=b_router,
        w_exp_gate=w_exp_gate[l], w_exp_up=w_exp_up[l], w_exp_down=w_exp_down[l],
        ln3_g=ln3_g[l].reshape(1, -1), ln3_b=ln3_b[l].reshape(1, -1),
    )


def _slots(rt, pstarts):
    g = rt.shape[-1]
    rt = rt.reshape(-1, SUBLANES, g)
    dest = pstarts[rt[:, 0:2, :].astype(I32)] + rt[:, 2:4, :].astype(I32)
    return dest.reshape(-1), g


def _layer(l, depth, xp, xs, lat_past, kr_past, conv_past, mk_s, mv_s, mem_prompt, w):
    alpha = (2 * depth) ** 0.25
    b, s, _ = xp.shape
    bs, ss, _ = xs.shape
    past = lat_past.shape[1]

    cos_p, sin_p = _rope_tables(jnp.arange(s))
    q_p, k_p, v_p, yc_p, lat_p, kr_p, cst_p = _inproj(
        xp, jnp.zeros((b, CONV_K - 1, CONV_WIDTH), F32), cos_p, sin_p, w)
    attn_p = _attention(q_p, k_p, v_p, 0, None)
    mk, mv, mk_b, mv_b = _memkv(mem_prompt.reshape(b * N_MEM, D_MODEL), w['w_xk'], w['w_xv'])
    cnt0 = jnp.zeros((1, LANES), F32)
    x2_p, ri_p, cnt_p, rt_p = _mid(xp, attn_p, yc_p, mk_b.reshape(b, N_MEM, D_MODEL), mv_b.reshape(b, N_MEM, D_MODEL),
                             cnt0, w, alpha)

    cos_s, sin_s = _rope_tables(past + jnp.arange(ss))
    q_s, _, _, yc_s, lat_s, kr_s, cst_s = _inproj(xs, conv_past, cos_s, sin_s, w)
    n_keys = past + ss
    sk = -(-n_keys // 512) * 512
    lat_all = jnp.concatenate([lat_past, lat_s, jnp.zeros((bs, sk - n_keys, KV_LORA), F32)], axis=1)
    kr_all = jnp.concatenate([kr_past, kr_s, jnp.zeros((bs, sk - n_keys, QK_ROPE), F32)], axis=1)
    kr_all = jnp.pad(kr_all, ((0, 0), (0, 0), (ROPE_LANE0, LANES - ROPE_LANE0 - QK_ROPE)))
    k_s, v_s = _kvup(lat_all, kr_all, w['w_ukv'])
    attn_s = _attention(q_s, k_s, v_s, past, n_keys)
    mk_sb = mk_s.reshape(bs, N_MEM, D_MODEL).astype(BF16)
    mv_sb = mv_s.reshape(bs, N_MEM, D_MODEL).astype(BF16)
    x2_s, ri_s, cnt, rt_s = _mid(xs, attn_s, yc_s, mk_sb, mv_sb, cnt_p, w, alpha)

    n_p, n_s = b * s, bs * ss
    counts = cnt[0, ROUTER_LANE0:ROUTER_LANE0 + N_EXPERTS].astype(I32)
    padded = (counts + MOE_ROWS - 1) // MOE_ROWS * MOE_ROWS
    pends = jnp.cumsum(padded)
    pstarts = pends - padded
    nblk = -(-2 * (n_p + n_s) // MOE_ROWS) + N_EXPERTS
    blk_start = jnp.arange(nblk, dtype=I32) * MOE_ROWS
    block_e = jnp.minimum(jnp.sum((pends[None, :] <= blk_start[:, None]).astype(I32), axis=1), N_EXPERTS - 1)
    n_used = (pends[-1] // MOE_ROWS).astype(I32).reshape(1)
    ri_p2, ri_s2 = ri_p.reshape(n_p, LANES), ri_s.reshape(n_s, LANES)
    dest_p, g_p = _slots(rt_p, pstarts)
    dest_s, g_s = _slots(rt_s, pstarts)
    x2_p2, x2_s2 = x2_p.reshape(n_p, D_MODEL), x2_s.reshape(n_s, D_MODEL)
    pad_info = jnp.concatenate([pstarts + counts, padded - counts, n_used]).astype(I32)
    slots = _dispatch(x2_p2, dest_p, g_p, x2_s2, dest_s, g_s, pad_info, nblk, MOE_ROWS)
    ys = _experts(slots, block_e, n_used, w['w_exp_gate'], w['w_exp_up'], w['w_exp_down'], MOE_ROWS)
    y_p = _combine(x2_p2, ri_p2, dest_p, g_p, ys, w['ln3_g'], w['ln3_b'], alpha).reshape(b, s, D_MODEL)
    y_s = _combine(x2_s2, ri_s2, dest_s, g_s, ys, w['ln3_g'], w['ln3_b'], alpha).reshape(bs, ss, D_MODEL)
    return (y_p, y_s, lat_p, kr_p, cst_p, mk.reshape(b, N_MEM, X_HEADS, X_HEAD_DIM),
            mv.reshape(b, N_MEM, X_HEADS, X_HEAD_DIM), lat_s, kr_s, cst_s)


def kernel(x_prompt, x_sample, cache_kv_latent, cache_k_rope, cache_conv, cache_mem_k, cache_mem_v, mem_prompt,
           w_in, q_norm_g, kv_norm_g, w_uq, w_ukv, conv_w, w_out, ln1_g, ln1_b, w_xq, w_xk, w_xv, w_xo, ln2_g,
           ln2_b, w_router_group, b_router_group, w_router_expert, b_router_expert, w_exp_gate, w_exp_up,
           w_exp_down, ln3_g, ln3_b):
    depth = w_in.shape[0]
    xp, xs = x_prompt, x_sample
    outs = [[] for _ in range(8)]
    for l in range(depth):
        w = _prep_weights(l, w_in, q_norm_g, kv_norm_g, w_uq, w_ukv, conv_w, w_out, ln1_g, ln1_b, w_xq, w_xk, w_xv,
                          w_xo, ln2_g, ln2_b, w_router_group, b_router_group, w_router_expert, b_router_expert,
                          w_exp_gate, w_exp_up, w_exp_down, ln3_g, ln3_b)
        res = _layer(l, depth, xp, xs, cache_kv_latent[l], cache_k_rope[l], cache_conv[l], cache_mem_k[l],
                     cache_mem_v[l], mem_prompt, w)
        xp, xs = res[0], res[1]
        for acc, r in zip(outs, res[2:]):
            acc.append(r)
    return (xp, xs) + tuple(jnp.stack(o) for o in outs)
```

```python
import functools
import math

import numpy as np
import jax
import jax.numpy as jnp
from jax import lax
from jax.experimental import pallas as pl
from jax.experimental.pallas import tpu as pltpu

F32 = jnp.float32
BF16 = jnp.bfloat16
I32 = jnp.int32

D_MODEL = 1024
CHUNK = 64
N_HEADS = 8
QK_NOPE = 64
QK_ROPE = 32
V_HEAD = 64
Q_LORA = 256
KV_LORA = 128
ROPE_THETA = 10000.0
MLA_SCALE = (QK_NOPE + QK_ROPE) ** -0.5
Q_PRESCALE = MLA_SCALE * math.log2(math.e)
CONV_WIDTH = 512
CONV_K = 3
N_MEM = 256
X_HEADS = 4
X_HEAD_DIM = D_MODEL // X_HEADS
X_SCALE = X_HEAD_DIM ** -0.5
N_GROUPS = 4
EXPERTS_PER_GROUP = 8
N_EXPERTS = N_GROUPS * EXPERTS_PER_GROUP
D_EXPERT = 512
LN_EPS = 1e-5
RMS_EPS = 1e-6

LANES = 128
SUBLANES = 8
ROW_CHUNKS = D_MODEL // LANES
HEAD_PAD = LANES
ROPE_LANE0 = QK_NOPE
ROUTER_LANE0 = N_GROUPS
VMEM_LIMIT = 56 * 1024 * 1024
MOE_ROWS = 512
MOE_PARTS = 2
ACT_ROW_SLAB = 32
GATHER_UNROLL = 8
KVUP_ROWS = 1536
WCAST_ROWS = 64
ATTN_TQ = 1024
ATTN_TK = 512
ATTN_SUB = 2
ATTN_ROW_SLAB = 64
LN_ROW_SLAB = 16
XATTN_ROW_SLAB = 64
INPROJ_ROWS = 512
ROUTE_GROUP = 256
MID_SKEW = 1
MID_ROWS = 512
MID_PARTS = 2


def _cp(sem, vmem=VMEM_LIMIT):
    return pltpu.CompilerParams(dimension_semantics=sem, vmem_limit_bytes=vmem)


def _pick_tile(n, pref):
    t = min(n, pref)
    while n % t:
        t //= 2
    return t


def _with_ones_lane(v):
    lane = lax.broadcasted_iota(I32, v.shape, 1)
    return jnp.where((lane & (HEAD_PAD - 1)) == V_HEAD, 1.0, v)


def _inproj_body(x_ref, cinit_ref, cos_ref, sin_ref, win_ref, qg_ref, kvg_ref, wuq_ref, wukv_ref, cw_ref,
                 q_ref, k_ref, v_ref, yc_ref, lat_ref, kr_ref, cst_ref, u_scr, *, tt, nj):
    j = pl.program_id(1)

    @pl.when(j == 0)
    def _():
        u_scr[6:8, :] = cinit_ref[0]

    x = x_ref[0].astype(BF16)
    proj = jnp.dot(x, win_ref[...], preferred_element_type=F32)
    cq = proj[:, 0:256]
    ckv = proj[:, 256:384]
    gb = proj[:, 384:896]
    gc = proj[:, 896:1408]
    gv = proj[:, 1408:1920]
    kr_blk = proj[:, 1920:2048]
    cos_t = cos_ref[...]
    sin_t = sin_ref[...]

    def rotate(blk):
        return blk * cos_t + pltpu.roll(blk, LANES - QK_ROPE, 1) * sin_t

    cqn = cq * lax.rsqrt(jnp.mean(cq * cq, -1, keepdims=True) + RMS_EPS) * qg_ref[...]
    ckvn = ckv * lax.rsqrt(jnp.mean(ckv * ckv, -1, keepdims=True) + RMS_EPS) * kvg_ref[...]
    lat_ref[0] = ckvn
    kr_p = rotate(kr_blk)
    kr_ref[0] = kr_p[:, ROPE_LANE0:ROPE_LANE0 + QK_ROPE]
    qq = jnp.dot(cqn.astype(BF16), wuq_ref[...], preferred_element_type=F32)
    kv = jnp.dot(ckvn.astype(BF16), wukv_ref[...], preferred_element_type=F32)
    hw = N_HEADS * HEAD_PAD
    for h in range(N_HEADS):
        sl = slice(h * HEAD_PAD, (h + 1) * HEAD_PAD)
        q_ref[0, :, sl] = (rotate(qq[:, sl]) * Q_PRESCALE).astype(BF16)
        k_ref[0, :, sl] = (kv[:, sl] + kr_p).astype(BF16)
    v_ref[0] = _with_ones_lane(kv[:, hw:2 * hw]).astype(BF16)
    u = gc * gv
    u_scr[8:8 + tt, :] = u
    conv = cw_ref[0:1, :] * u_scr[6:6 + tt, :] + cw_ref[1:2, :] * u_scr[7:7 + tt, :] + cw_ref[2:3, :] * u
    yc_ref[0] = (gb * conv).astype(BF16)
    last2 = u_scr[tt + 6:tt + 8, :]
    u_scr[6:8, :] = last2

    @pl.when(j == nj - 1)
    def _():
        cst_ref[0] = last2


def _inproj(x, conv_init, cos_t, sin_t, w):
    b, s, _ = x.shape
    tt = _pick_tile(s, INPROJ_ROWS)
    nj = s // tt
    wn = w['w_in'].shape[1]
    full = lambda shape: pl.BlockSpec(shape, lambda bi, ji: (0,) * len(shape))
    out_shapes = (
        jax.ShapeDtypeStruct((b, s, N_HEADS * HEAD_PAD), BF16),
        jax.ShapeDtypeStruct((b, s, N_HEADS * HEAD_PAD), BF16),
        jax.ShapeDtypeStruct((b, s, N_HEADS * HEAD_PAD), BF16),
        jax.ShapeDtypeStruct((b, s, CONV_WIDTH), BF16),
        jax.ShapeDtypeStruct((b, s, KV_LORA), F32),
        jax.ShapeDtypeStruct((b, s, QK_ROPE), F32),
        jax.ShapeDtypeStruct((b, CONV_K - 1, CONV_WIDTH), F32),
    )
    row = lambda width: pl.BlockSpec((1, tt, width), lambda bi, ji: (bi, ji, 0))
    return pl.pallas_call(
        functools.partial(_inproj_body, tt=tt, nj=nj),
        grid=(b, nj),
        in_specs=[
            row(D_MODEL),
            pl.BlockSpec((1, CONV_K - 1, CONV_WIDTH), lambda bi, ji: (bi, 0, 0)),
            pl.BlockSpec((tt, LANES), lambda bi, ji: (ji, 0)),
            pl.BlockSpec((tt, LANES), lambda bi, ji: (ji, 0)),
            full((D_MODEL, wn)),
            full((1, Q_LORA)),
            full((1, KV_LORA)),
            full(w['w_uq'].shape),
            full(w['w_ukv'].shape),
            full((CONV_K, CONV_WIDTH)),
        ],
        out_specs=(
            row(N_HEADS * HEAD_PAD), row(N_HEADS * HEAD_PAD), row(N_HEADS * HEAD_PAD), row(CONV_WIDTH),
            row(KV_LORA), row(QK_ROPE),
            pl.BlockSpec((1, CONV_K - 1, CONV_WIDTH), lambda bi, ji: (bi, 0, 0)),
        ),
        out_shape=out_shapes,
        scratch_shapes=[pltpu.VMEM((tt + 8, CONV_WIDTH), F32)],
        compiler_params=_cp(("parallel", "arbitrary")),
        name="inproj",
    )(x, conv_init, cos_t, sin_t, w['w_in'], w['q_norm_g'], w['kv_norm_g'], w['w_uq'], w['w_ukv'], w['conv_w'])


def _kvup_body(lat_ref, krp_ref, wukv_ref, k_ref, v_ref):
    kv = jnp.dot(lat_ref[0].astype(BF16), wukv_ref[...], preferred_element_type=F32)
    kr_p = krp_ref[0]
    hw = N_HEADS * HEAD_PAD
    for h in range(N_HEADS):
        sl = slice(h * HEAD_PAD, (h + 1) * HEAD_PAD)
        k_ref[0, :, sl] = (kv[:, sl] + kr_p).astype(BF16)
    v_ref[0] = _with_ones_lane(kv[:, hw:2 * hw]).astype(BF16)


def _kvup(lat, kr_padded, w_ukv):
    b, s, _ = lat.shape
    tt = _pick_tile(s, KVUP_ROWS)
    row = lambda width: pl.BlockSpec((1, tt, width), lambda bi, ji: (bi, ji, 0))
    return pl.pallas_call(
        _kvup_body,
        grid=(b, s // tt),
        in_specs=[row(KV_LORA), row(LANES), pl.BlockSpec(w_ukv.shape, lambda bi, ji: (0, 0))],
        out_specs=(row(N_HEADS * HEAD_PAD), row(N_HEADS * HEAD_PAD)),
        out_shape=(jax.ShapeDtypeStruct((b, s, N_HEADS * HEAD_PAD), BF16),
                   jax.ShapeDtypeStruct((b, s, N_HEADS * HEAD_PAD), BF16)),
        compiler_params=_cp(("parallel", "parallel")),
        name="kvup",
    )(lat, kr_padded, w_ukv)


def _attn_body(qi_ref, ki_ref, fl_ref, q_ref, k_ref, v_ref, o_ref, m_scr, acc_scr, s_scr, p_scr, a_scr,
               *, tq, tk, n_sub, rs, q_pos0, n_valid, combos):
    step = pl.program_id(1)
    qi = qi_ref[step]
    ki = ki_ref[step]
    flags = fl_ref[step]
    reps = tk // LANES
    th = tq // n_sub

    @pl.when((flags & 1) != 0)
    def _():
        m_scr[...] = jnp.full(m_scr.shape, -jnp.inf, F32)
        acc_scr[...] = jnp.zeros(acc_scr.shape, F32)

    def scores(item, buf):
        sub, h, _ = item
        hs = slice(h * HEAD_PAD, (h + 1) * HEAD_PAD)
        s_scr[buf] = lax.dot_general(q_ref[0, sub * th:(sub + 1) * th, hs], k_ref[0, :, hs],
                                     (((1,), (1,)), ((), ())), preferred_element_type=F32)

    def softmax_pv(item, buf):
        sub, h, masked = item
        hs = slice(h * HEAD_PAD, (h + 1) * HEAD_PAD)
        for r in range(th // rs):
            rows = slice(r * rs, (r + 1) * rs)
            arows = slice(sub * th + r * rs, sub * th + (r + 1) * rs)
            s_r = s_scr[buf, rows, :]
            if masked:
                qpos = q_pos0 + qi * tq + sub * th + r * rs + lax.broadcasted_iota(I32, (rs, tk), 0)
                kpos = ki * tk + lax.broadcasted_iota(I32, (rs, tk), 1)
                mask = (kpos >> 6) <= (qpos >> 6)
                if n_valid is not None:
                    mask = mask & (kpos < n_valid)
                s_r = jnp.where(mask, s_r, -jnp.inf)
            m_old = m_scr[h, arows, :]
            m_new = jnp.maximum(m_old, jnp.max(s_r, axis=-1, keepdims=True))
            a_scr[buf, rows, :] = jnp.exp2(m_old - m_new)
            m_rep = jnp.concatenate([m_new] * reps, axis=1)
            p_scr[buf, rows, :] = jnp.exp2(s_r - m_rep).astype(BF16)
            m_scr[h, arows, :] = m_new
        pv = jnp.dot(p_scr[buf], v_ref[0, :, hs], preferred_element_type=F32)
        srows = slice(sub * th, (sub + 1) * th)
        acc_scr[h, srows, :] = a_scr[buf] * acc_scr[h, srows, :] + pv

    def run(modes):
        items = [(sub, h, mode == 2) for sub, mode in enumerate(modes) if mode != 0 for h in range(N_HEADS)]
        scores(items[0], 0)
        for n, item in enumerate(items):
            if n + 1 < len(items):
                scores(items[n + 1], (n + 1) % 2)
            softmax_pv(item, n % 2)

    for code, modes in combos:
        @pl.when((flags >> 2) == code)
        def _(modes=modes):
            run(modes)

    @pl.when((flags & 2) != 0)
    def _():
        for h in range(N_HEADS):
            acc = acc_scr[h]
            o_ref[0, :, h * V_HEAD:(h + 1) * V_HEAD] = (acc[:, 0:V_HEAD] / acc[:, V_HEAD:V_HEAD + 1]).astype(BF16)


def _attn_tables(nq, nk, tq, tk, n_sub, q_pos0, n_valid):
    th = tq // n_sub
    qi_l, ki_l, fl_l, combos = [], [], [], {}
    for qi in range(nq):
        sub_lo = [q_pos0 + qi * tq + j * th for j in range(n_sub)]
        sub_last = []
        for lo in sub_lo:
            last_pos = ((lo + th - 1) // CHUNK) * CHUNK + CHUNK - 1
            if n_valid is not None:
                last_pos = min(last_pos, n_valid - 1)
            sub_last.append(min(nk - 1, last_pos // tk))
        k_last = max(sub_last)
        for ki in range(k_last + 1):
            k_hi = ki * tk + tk - 1
            modes = []
            for lo, last in zip(sub_lo, sub_last):
                if ki > last:
                    modes.append(0)
                elif (k_hi // CHUNK) > (lo // CHUNK) or (n_valid is not None and k_hi >= n_valid):
                    modes.append(2)
                else:
                    modes.append(1)
            code = sum(m * 3 ** j for j, m in enumerate(modes))
            combos[code] = tuple(modes)
            qi_l.append(qi); ki_l.append(ki)
            fl_l.append((1 if ki == 0 else 0) | (2 if ki == k_last else 0) | (code << 2))
    to_arr = lambda vals: jnp.asarray(np.array(vals, np.int32))
    return to_arr(qi_l), to_arr(ki_l), to_arr(fl_l), tuple(sorted(combos.items()))


def _attention(q, k, v, q_pos0, n_valid):
    b, sq, _ = q.shape
    sk = k.shape[1]
    tq = _pick_tile(sq, ATTN_TQ)
    tk = _pick_tile(sk, ATTN_TK)
    n_sub = ATTN_SUB if tq % (ATTN_SUB * 2 * SUBLANES) == 0 else 1
    th = tq // n_sub
    qi_t, ki_t, fl_t, combos = _attn_tables(sq // tq, sk // tk, tq, tk, n_sub, q_pos0, n_valid)
    n_steps = int(qi_t.shape[0])
    grid_spec = pltpu.PrefetchScalarGridSpec(
        num_scalar_prefetch=3,
        grid=(b, n_steps),
        in_specs=[
            pl.BlockSpec((1, tq, N_HEADS * HEAD_PAD), lambda bi, si, qt, kt, ft: (bi, qt[si], 0)),
            pl.BlockSpec((1, tk, N_HEADS * HEAD_PAD), lambda bi, si, qt, kt, ft: (bi, kt[si], 0)),
            pl.BlockSpec((1, tk, N_HEADS * HEAD_PAD), lambda bi, si, qt, kt, ft: (bi, kt[si], 0)),
        ],
        out_specs=pl.BlockSpec((1, tq, N_HEADS * V_HEAD), lambda bi, si, qt, kt, ft: (bi, qt[si], 0)),
        scratch_shapes=[
            pltpu.VMEM((N_HEADS, tq, LANES), F32),
            pltpu.VMEM((N_HEADS, tq, LANES), F32),
            pltpu.VMEM((2, th, tk), F32),
            pltpu.VMEM((2, th, tk), BF16),
            pltpu.VMEM((2, th, LANES), F32),
        ],
    )
    return pl.pallas_call(
        functools.partial(_attn_body, tq=tq, tk=tk, n_sub=n_sub, rs=min(th, ATTN_ROW_SLAB), q_pos0=q_pos0,
                          n_valid=n_valid, combos=combos),
        grid_spec=grid_spec,
        out_shape=jax.ShapeDtypeStruct((b, sq, N_HEADS * V_HEAD), BF16),
        compiler_params=_cp(("parallel", "arbitrary")),
        name="mla_attn",
    )(qi_t, ki_t, fl_t, q, k, v)


def _memkv_body(mem_ref, wk_ref, wv_ref, mk_ref, mv_ref, mkb_ref, mvb_ref):
    m = mem_ref[...].astype(BF16)
    mk = jnp.dot(m, wk_ref[...], preferred_element_type=F32)
    mv = jnp.dot(m, wv_ref[...], preferred_element_type=F32)
    mk_ref[...] = mk
    mv_ref[...] = mv
    mkb_ref[...] = mk.astype(BF16)
    mvb_ref[...] = mv.astype(BF16)


def _memkv(mem2d, w_xk, w_xv):
    n = mem2d.shape[0]
    tt = _pick_tile(n, 256)
    row = pl.BlockSpec((tt, D_MODEL), lambda i: (i, 0))
    wspec = pl.BlockSpec((D_MODEL, D_MODEL), lambda i: (0, 0))
    return pl.pallas_call(
        _memkv_body,
        grid=(n // tt,),
        in_specs=[row, wspec, wspec],
        out_specs=(row, row, row, row),
        out_shape=(jax.ShapeDtypeStruct((n, D_MODEL), F32), jax.ShapeDtypeStruct((n, D_MODEL), F32),
                   jax.ShapeDtypeStruct((n, D_MODEL), BF16), jax.ShapeDtypeStruct((n, D_MODEL), BF16)),
        compiler_params=_cp(("parallel",)),
        name="memkv",
    )(mem2d, w_xk, w_xv)


def _layer_norm(x, g, b):
    mu = jnp.mean(x, -1, keepdims=True)
    xc = x - mu
    var = jnp.mean(xc * xc, -1, keepdims=True)
    return xc * lax.rsqrt(var + LN_EPS) * g + b


def _mid_body(x_ref, at_ref, yc_ref, mk_ref, mv_ref, cnt0_ref, low_ref, wo_ref, g1_ref, b1_ref, wq_ref, wxo_ref,
              g2_ref, b2_ref, wr_ref, br_ref, x2_ref, ri_ref, cnt_ref, rt_ref,
              cnt_scr, a_scr, xb_scr, q_scr, sc_scr, p_scr, o_scr, lg_scr, *, tt, alpha):
    step = pl.program_id(0)

    @pl.when(step == 0)
    def _():
        cnt_scr[...] = cnt0_ref[...]
        lg_scr[...] = jnp.zeros(lg_scr.shape, F32)

    cur_slot = step % 2
    prev_slot = 1 - cur_slot
    routed = (step > 0).astype(F32)

    n_part = MID_PARTS if tt % (MID_PARTS * SUBLANES * 2) == 0 else 1
    pr = tt // n_part
    ln_rs = min(pr, LN_ROW_SLAB)
    sm_rs = min(pr, XATTN_ROW_SLAB)
    c_exp = X_SCALE * math.log2(math.e)
    lane = lax.broadcasted_iota(I32, (pr, LANES), 1)

    def part_rows(k):
        return slice(k * pr, (k + 1) * pr)

    def out_proj(k):
        rp = part_rows(k)
        mix = jnp.concatenate([at_ref[0, rp, :], yc_ref[0, rp, :]], axis=-1)
        a_scr[rp, :] = jnp.dot(mix, wo_ref[...], preferred_element_type=F32)

    def norm1(k):
        for r in range(pr // ln_rs):
            rows = slice(k * pr + r * ln_rs, k * pr + (r + 1) * ln_rs)
            x1 = _layer_norm(alpha * x_ref[0, rows, :] + a_scr[rows, :], g1_ref[...], b1_ref[...])
            x2_ref[0, rows, :] = x1
            xb_scr[rows, :] = x1.astype(BF16)

    def q_proj(k):
        rp = part_rows(k)
        q_scr[rp, :] = jnp.dot(xb_scr[rp, :], wq_ref[...], preferred_element_type=F32).astype(BF16)

    def cross_attn(k):
        rp = part_rows(k)
        for h in range(X_HEADS):
            sl = slice(h * X_HEAD_DIM, (h + 1) * X_HEAD_DIM)
            b2 = h % 2
            sc_scr[k, b2] = lax.dot_general(q_scr[rp, sl], mk_ref[0, :, sl], (((1,), (1,)), ((), ())),
                                            preferred_element_type=F32)
            for r in range(pr // sm_rs):
                rows = slice(r * sm_rs, (r + 1) * sm_rs)
                s_r = sc_scr[k, b2, rows, :]
                e = jnp.exp2((s_r - jnp.max(s_r, -1, keepdims=True)) * c_exp)
                p_scr[k, b2, rows, :] = (e / jnp.sum(e, -1, keepdims=True)).astype(BF16)
            o_scr[rp, sl] = jnp.dot(p_scr[k, b2], mv_ref[0, :, sl], preferred_element_type=F32).astype(BF16)

    def x_out_proj(k):
        rp = part_rows(k)
        a_scr[rp, :] = jnp.dot(o_scr[rp, :], wxo_ref[...], preferred_element_type=F32)

    def norm2(k):
        for r in range(pr // ln_rs):
            rows = slice(k * pr + r * ln_rs, k * pr + (r + 1) * ln_rs)
            x2 = _layer_norm(alpha * x2_ref[0, rows, :] + a_scr[rows, :], g2_ref[...], b2_ref[...])
            x2_ref[0, rows, :] = x2
            xb_scr[rows, :] = x2.astype(BF16)

    def router_logits(k):
        rp = part_rows(k)
        lg_scr[cur_slot, rp, :] = jnp.dot(xb_scr[rp, :], wr_ref[...], preferred_element_type=F32) + br_ref[...]

    def route(k):
        rp = part_rows(k)
        logits = lg_scr[prev_slot, rp, :]
        neg = -jnp.inf
        is_g = lane < N_GROUPS
        lg = jnp.where(is_g, logits, neg)
        mg = jnp.max(lg, -1, keepdims=True)
        g_idx = jnp.min(jnp.where(lg == mg, lane, LANES), -1, keepdims=True)
        pg = 1.0 / jnp.sum(jnp.where(is_g, jnp.exp(logits - mg), 0.0), -1, keepdims=True)
        in_grp = ((lane >= ROUTER_LANE0) & (lane < ROUTER_LANE0 + N_EXPERTS)
                  & (((lane - ROUTER_LANE0) >> 3) == g_idx))
        le = jnp.where(in_grp, logits, neg)
        v1 = jnp.max(le, -1, keepdims=True)
        i1 = jnp.min(jnp.where(le == v1, lane, LANES), -1, keepdims=True)
        le2 = jnp.where(lane == i1, neg, le)
        v2 = jnp.max(le2, -1, keepdims=True)
        i2 = jnp.min(jnp.where(le2 == v2, lane, LANES), -1, keepdims=True)
        e2 = jnp.exp(v2 - v1)
        den = 1.0 + e2
        gate1 = (1.0 / den) * pg
        gate2 = (e2 / den) * pg
        oh1 = (lane == i1).astype(F32)
        oh2 = (lane == i2).astype(F32)
        oh = oh1 + oh2
        base = cnt_scr[...] + jnp.dot(low_ref[...], oh.astype(BF16), preferred_element_type=F32)
        rank1 = jnp.sum(oh1 * base, -1, keepdims=True)
        rank2 = jnp.sum(oh2 * base, -1, keepdims=True)
        cnt_scr[...] = cnt_scr[...] + jnp.sum(oh, 0, keepdims=True) * routed
        e1f = (i1 - ROUTER_LANE0).astype(F32)
        e2f = (i2 - ROUTER_LANE0).astype(F32)
        ri = jnp.where(lane == 0, e1f, jnp.where(lane == 1, e2f, jnp.where(
            lane == 2, rank1, jnp.where(lane == 3, rank2, jnp.where(lane == 4, gate1, jnp.where(
                lane == 5, gate2, 0.0))))))
        ri_ref[0, rp, :] = ri
        ri_t = jnp.transpose(ri)[0:SUBLANES, :]
        grp = min(pr, ROUTE_GROUP)
        for j in range(pr // grp):
            rt_ref[0, k * (pr // grp) + j] = ri_t[:, j * grp:(j + 1) * grp]

    stages = (out_proj, norm1, q_proj, cross_attn, x_out_proj, norm2, router_logits)
    matmul_stages = (out_proj, q_proj, x_out_proj, router_logits)
    for t in range(len(stages) + MID_SKEW * (n_part - 1)):
        todo = [(stages[t - MID_SKEW * k], k) for k in range(n_part) if 0 <= t - MID_SKEW * k < len(stages)]
        for fn, k in sorted(todo, key=lambda fk: fk[0] not in matmul_stages):
            fn(k)
        if t < n_part:
            route(t)
    cnt_ref[...] = cnt_scr[...]


def _mid(x, attn, yconv, mk_b, mv_b, cnt0, w, alpha):
    b, s, _ = x.shape
    tt = _pick_tile(s, MID_ROWS)
    nj = s // tt
    n_tiles = b * nj
    cur = lambda i: jnp.minimum(i, n_tiles - 1)
    prev = lambda i: jnp.maximum(i - 1, 0)
    row = lambda width: pl.BlockSpec((1, tt, width), lambda i: (cur(i) // nj, cur(i) % nj, 0))
    row_prev = lambda width: pl.BlockSpec((1, tt, width), lambda i: (prev(i) // nj, prev(i) % nj, 0))
    full = lambda shape: pl.BlockSpec(shape, lambda i: (0,) * len(shape))
    mem = pl.BlockSpec((1, N_MEM, D_MODEL), lambda i: (cur(i) // nj, 0, 0))
    vec = full((1, D_MODEL))
    n_part = MID_PARTS if tt % (MID_PARTS * SUBLANES * 2) == 0 else 1
    pr = tt // n_part
    grp = min(pr, ROUTE_GROUP)
    lower = jnp.tril(jnp.ones((pr, pr), F32), -1).astype(BF16)
    return pl.pallas_call(
        functools.partial(_mid_body, tt=tt, alpha=alpha),
        grid=(n_tiles + 1,),
        in_specs=[row(D_MODEL), row(N_HEADS * V_HEAD), row(CONV_WIDTH), mem, mem, full((1, LANES)), full((pr, pr)),
                  full((D_MODEL, D_MODEL)), vec, vec, full((D_MODEL, D_MODEL)), full((D_MODEL, D_MODEL)),
                  vec, vec, full((D_MODEL, LANES)), full((1, LANES))],
        out_specs=(row(D_MODEL), row_prev(LANES), full((1, LANES)),
                   pl.BlockSpec((1, tt // grp, SUBLANES, grp), lambda i: (prev(i) // nj, prev(i) % nj, 0, 0))),
        out_shape=(jax.ShapeDtypeStruct((b, s, D_MODEL), F32), jax.ShapeDtypeStruct((b, s, LANES), F32),
                   jax.ShapeDtypeStruct((1, LANES), F32),
                   jax.ShapeDtypeStruct((b, s // grp, SUBLANES, grp), F32)),
        scratch_shapes=[
            pltpu.VMEM((1, LANES), F32),
            pltpu.VMEM((tt, D_MODEL), F32),
            pltpu.VMEM((tt, D_MODEL), BF16),
            pltpu.VMEM((tt, D_MODEL), BF16),
            pltpu.VMEM((n_part, 2, pr, N_MEM), F32),
            pltpu.VMEM((n_part, 2, pr, N_MEM), BF16),
            pltpu.VMEM((tt, D_MODEL), BF16),
            pltpu.VMEM((2, tt, LANES), F32),
        ],
        compiler_params=_cp(("arbitrary",)),
        name="mid",
    )(x, attn, yconv, mk_b, mv_b, cnt0, lower, w['w_out'], w['ln1_g'], w['ln1_b'], w['w_xq'], w['w_xo'],
      w['ln2_g'], w['ln2_b'], w['w_router'], w['b_router'])


def _row_slice(ref, row):
    return ref.at[pl.ds(pl.multiple_of(row * SUBLANES, SUBLANES), SUBLANES), :]


def _slot_index(t, kk, g, n):
    if g >= n:
        return kk * g + t
    shift = g.bit_length() - 1
    return ((t >> shift) << (shift + 1)) + kk * g + (t & (g - 1))


def _rows_slice(ref, row, n_rows):
    return ref.at[pl.ds(pl.multiple_of(row * SUBLANES, SUBLANES), n_rows * SUBLANES), :]


def _dispatch_body(pad_ref, dest_ref, dest_s_ref, x_ref, x_s_ref, xs_ref, buf0, buf1, zbuf, sem0, sem1, zsem,
                   *, tt, ts, nt, nblk, rows, g_p, g_s):
    i = pl.program_id(0)
    bufs = (buf0, buf1)
    sems = (sem0, sem1)
    len_bits = rows.bit_length() - 1

    def scatter_rows(buf, sem, src_ref, d_ref, n, g):
        for c in range(ROW_CHUNKS):
            buf[pl.ds(c, n, stride=SUBLANES), :] = src_ref[:, c * LANES:(c + 1) * LANES]
        unroll = min(n, GATHER_UNROLL)

        def issue(j, carry):
            for u in range(unroll):
                t = j * unroll + u
                src = _row_slice(buf, t)
                for kk in range(2):
                    d = d_ref[0, 0, _slot_index(t, kk, g, n)]
                    pltpu.make_async_copy(src, _row_slice(xs_ref, d), sem).start(priority=kk)
            return carry

        lax.fori_loop(0, n // unroll, issue, 0)

    def wait_rows(buf, sem, n):
        for _ in range(2):
            pltpu.make_async_copy(_rows_slice(buf, 0, n), _rows_slice(xs_ref, 0, n), sem).wait()

    def zero_fill(wait):
        def fire(copy):
            if wait:
                copy.wait()
            else:
                copy.start()

        def per_expert(e, carry):
            first = pad_ref[e]
            n_pad = pad_ref[N_EXPERTS + e]
            for bit in range(len_bits):
                size = 1 << bit
                off = (n_pad >> (bit + 1)) << (bit + 1)

                @pl.when(((n_pad >> bit) & 1) == 1)
                def _():
                    fire(pltpu.make_async_copy(_rows_slice(zbuf, 0, size), _rows_slice(xs_ref, first + off, size),
                                               zsem))
            return carry

        lax.fori_loop(0, N_EXPERTS, per_expert, 0)

        def per_block(j, carry):
            fire(pltpu.make_async_copy(zbuf, _rows_slice(xs_ref, j * rows, rows), zsem))
            return carry

        lax.fori_loop(pad_ref[2 * N_EXPERTS], nblk, per_block, 0)

    def run(slot):
        @pl.when(i < nt)
        def _():
            scatter_rows(bufs[slot], sems[slot], x_ref, dest_ref, tt, g_p)

        @pl.when(i == nt)
        def _():
            zbuf[...] = jnp.zeros(zbuf.shape, F32)
            scatter_rows(bufs[slot], sems[slot], x_s_ref, dest_s_ref, ts, g_s)
            zero_fill(False)

        @pl.when(i > 0)
        def _():
            wait_rows(bufs[1 - slot], sems[1 - slot], tt)

        @pl.when(i == nt)
        def _():
            wait_rows(bufs[slot], sems[slot], ts)
            zero_fill(True)

    @pl.when(i % 2 == 0)
    def _():
        run(0)

    @pl.when(i % 2 == 1)
    def _():
        run(1)


def _dispatch(x_p, dest_p, g_p, x_s, dest_s, g_s, pad_info, nblk, rows):
    n_p, n_s = x_p.shape[0], x_s.shape[0]
    tt = _pick_tile(n_p, 256)
    nt = n_p // tt
    assert n_s <= tt
    last = lambda i, pad: jnp.minimum(i, nt - 1)
    grid_spec = pltpu.PrefetchScalarGridSpec(
        num_scalar_prefetch=1,
        grid=(nt + 1,),
        in_specs=[
            pl.BlockSpec((1, 1, 2 * tt), lambda i, pad: (last(i, pad), 0, 0), memory_space=pltpu.SMEM),
            pl.BlockSpec((1, 1, 2 * n_s), lambda i, pad: (0, 0, 0), memory_space=pltpu.SMEM),
            pl.BlockSpec((tt, D_MODEL), lambda i, pad: (last(i, pad), 0)),
            pl.BlockSpec((n_s, D_MODEL), lambda i, pad: (0, 0)),
        ],
        out_specs=pl.BlockSpec(memory_space=pl.ANY),
        scratch_shapes=[pltpu.VMEM((tt * SUBLANES, LANES), F32), pltpu.VMEM((tt * SUBLANES, LANES), F32),
                        pltpu.VMEM((rows * SUBLANES, LANES), F32),
                        pltpu.SemaphoreType.DMA, pltpu.SemaphoreType.DMA, pltpu.SemaphoreType.DMA],
    )
    return pl.pallas_call(
        functools.partial(_dispatch_body, tt=tt, ts=n_s, nt=nt, nblk=nblk, rows=rows, g_p=g_p, g_s=g_s),
        grid_spec=grid_spec,
        out_shape=jax.ShapeDtypeStruct((nblk * rows * SUBLANES, LANES), F32),
        compiler_params=_cp(("arbitrary",)),
        name="moe_dispatch",
    )(pad_info, dest_p.reshape(nt, 1, 2 * tt), dest_s.reshape(1, 1, 2 * n_s), x_p, x_s)


def _experts_body(be_ref, nu_ref, x_ref, wg_ref, wu_ref, wd_ref, y_ref, xb_scr, g_scr, u_scr, h_scr,
                  wgb_scr, wub_scr, wdb_scr, *, rows):
    i = pl.program_id(0)
    cur = jnp.minimum(i, nu_ref[0] - 1)
    new_expert = (i == 0) | (be_ref[cur] != be_ref[jnp.maximum(cur - 1, 0)])

    @pl.when((i < nu_ref[0]) & new_expert)
    def _():
        for r in range(0, D_MODEL, WCAST_ROWS):
            wgb_scr[r:r + WCAST_ROWS, :] = wg_ref[0, r:r + WCAST_ROWS, :].astype(BF16)
            wub_scr[r:r + WCAST_ROWS, :] = wu_ref[0, r:r + WCAST_ROWS, :].astype(BF16)
        for r in range(0, D_EXPERT, WCAST_ROWS // 2):
            wdb_scr[r:r + WCAST_ROWS // 2, :] = wd_ref[0, r:r + WCAST_ROWS // 2, :].astype(BF16)
    n_part = MOE_PARTS
    pr = rows // n_part
    act_rs = min(pr, ACT_ROW_SLAB)

    def load(k):
        base = k * pr * SUBLANES
        xb_scr[k] = jnp.concatenate([x_ref[pl.ds(base + c, pr, stride=SUBLANES), :] for c in range(ROW_CHUNKS)],
                                    axis=-1).astype(BF16)

    def gate_up(k):
        g_scr[k] = jnp.dot(xb_scr[k], wgb_scr[...], preferred_element_type=F32)
        u_scr[k] = jnp.dot(xb_scr[k], wub_scr[...], preferred_element_type=F32)

    def act(k):
        for r in range(pr // act_rs):
            rows_r = slice(r * act_rs, (r + 1) * act_rs)
            g = g_scr[k, rows_r, :]
            h_scr[k, rows_r, :] = ((g * jax.nn.sigmoid(g)) * u_scr[k, rows_r, :]).astype(BF16)

    def down(k):
        y = jnp.dot(h_scr[k], wdb_scr[...], preferred_element_type=F32)
        base = k * pr * SUBLANES
        for c in range(ROW_CHUNKS):
            y_ref[pl.ds(base + c, pr, stride=SUBLANES), :] = y[:, c * LANES:(c + 1) * LANES]

    @pl.when(i < nu_ref[0])
    def _():
        stages = (load, gate_up, act, down)
        matmul_stages = (gate_up, down)
        for t in range(len(stages) + n_part - 1):
            todo = [(stages[t - k], k) for k in range(n_part) if 0 <= t - k < len(stages)]
            for fn, k in sorted(todo, key=lambda fk: fk[0] not in matmul_stages):
                fn(k)

    @pl.when(i >= nu_ref[0])
    def _():
        y_ref[...] = jnp.zeros(y_ref.shape, F32)


def _experts(xs, block_e, n_used, wg, wu, wd, rows):
    nblk = xs.shape[0] // (rows * SUBLANES)
    clamp = lambda i, nu: jnp.minimum(i, nu[0] - 1)
    grid_spec = pltpu.PrefetchScalarGridSpec(
        num_scalar_prefetch=2,
        grid=(nblk,),
        in_specs=[
            pl.BlockSpec((rows * SUBLANES, LANES), lambda i, be, nu: (clamp(i, nu), 0)),
            pl.BlockSpec((1, D_MODEL, D_EXPERT), lambda i, be, nu: (be[clamp(i, nu)], 0, 0)),
            pl.BlockSpec((1, D_MODEL, D_EXPERT), lambda i, be, nu: (be[clamp(i, nu)], 0, 0)),
            pl.BlockSpec((1, D_EXPERT, D_MODEL), lambda i, be, nu: (be[clamp(i, nu)], 0, 0)),
        ],
        out_specs=pl.BlockSpec((rows * SUBLANES, LANES), lambda i, be, nu: (i, 0)),
        scratch_shapes=[
            pltpu.VMEM((MOE_PARTS, rows // MOE_PARTS, D_MODEL), BF16),
            pltpu.VMEM((MOE_PARTS, rows // MOE_PARTS, D_EXPERT), F32),
            pltpu.VMEM((MOE_PARTS, rows // MOE_PARTS, D_EXPERT), F32),
            pltpu.VMEM((MOE_PARTS, rows // MOE_PARTS, D_EXPERT), BF16),
            pltpu.VMEM((D_MODEL, D_EXPERT), BF16),
            pltpu.VMEM((D_MODEL, D_EXPERT), BF16),
            pltpu.VMEM((D_EXPERT, D_MODEL), BF16),
        ],
    )
    return pl.pallas_call(
        functools.partial(_experts_body, rows=rows),
        grid_spec=grid_spec,
        out_shape=jax.ShapeDtypeStruct(xs.shape, F32),
        compiler_params=_cp(("arbitrary",)),
        name="moe_experts",
    )(block_e, n_used, xs, wg, wu, wd)


def _combine_body(dest_ref, destn_ref, x_ref, ri_ref, g3_ref, b3_ref, ys_ref, o_ref,
                  b00, b01, b10, b11, sem0, sem1, *, tt, nt, g, alpha):
    i = pl.program_id(0)
    bufs = ((b00, b01), (b10, b11))
    sems = (sem0, sem1)
    rs = min(tt, LN_ROW_SLAB)
    unroll = min(tt, GATHER_UNROLL)

    def gather_rows(dref, slot):
        def copy(t, kk):
            d = dref[0, 0, _slot_index(t, kk, g, tt)]
            return pltpu.make_async_copy(_row_slice(ys_ref, d), _row_slice(bufs[slot][kk], t), sems[slot])

        def body(j, carry):
            for u in range(unroll):
                for kk in range(2):
                    copy(j * unroll + u, kk).start(priority=kk)
            return carry

        lax.fori_loop(0, tt // unroll, body, 0)

    def wait_rows(slot):
        for kk in range(2):
            pltpu.make_async_copy(ys_ref.at[pl.ds(0, tt * SUBLANES), :], bufs[slot][kk], sems[slot]).wait()

    def run(slot):
        if slot == 0:
            @pl.when(i == 0)
            def _():
                gather_rows(dest_ref, 0)

        @pl.when(i + 1 < nt)
        def _():
            gather_rows(destn_ref, 1 - slot)

        wait_rows(slot)
        for r in range(tt // rs):
            rows = slice(r * rs, (r + 1) * rs)
            base = r * rs * SUBLANES
            y0 = jnp.concatenate([bufs[slot][0][pl.ds(base + c, rs, stride=SUBLANES), :]
                                  for c in range(ROW_CHUNKS)], axis=-1)
            y1 = jnp.concatenate([bufs[slot][1][pl.ds(base + c, rs, stride=SUBLANES), :]
                                  for c in range(ROW_CHUNKS)], axis=-1)
            ri = ri_ref[rows, :]
            moe = y0 * ri[:, 4:5] + y1 * ri[:, 5:6]
            o_ref[rows, :] = _layer_norm(alpha * x_ref[rows, :] + moe, g3_ref[...], b3_ref[...])

    @pl.when(i % 2 == 0)
    def _():
        run(0)

    @pl.when(i % 2 == 1)
    def _():
        run(1)


def _combine(x2d, rinfo, dest, g, ys, g3, b3, alpha):
    n = x2d.shape[0]
    tt = _pick_tile(n, 256)
    nt = n // tt
    dest3 = dest.reshape(nt, 1, 2 * tt)
    vec = pl.BlockSpec((1, D_MODEL), lambda i: (0, 0))
    stage = pltpu.VMEM((tt * SUBLANES, LANES), F32)
    return pl.pallas_call(
        functools.partial(_combine_body, tt=tt, nt=nt, g=g, alpha=alpha),
        grid=(nt,),
        in_specs=[
            pl.BlockSpec((1, 1, 2 * tt), lambda i: (i, 0, 0), memory_space=pltpu.SMEM),
            pl.BlockSpec((1, 1, 2 * tt), lambda i: (jnp.minimum(i + 1, nt - 1), 0, 0), memory_space=pltpu.SMEM),
            pl.BlockSpec((tt, D_MODEL), lambda i: (i, 0)),
            pl.BlockSpec((tt, LANES), lambda i: (i, 0)),
            vec, vec,
            pl.BlockSpec(memory_space=pl.ANY),
        ],
        out_specs=pl.BlockSpec((tt, D_MODEL), lambda i: (i, 0)),
        out_shape=jax.ShapeDtypeStruct((n, D_MODEL), F32),
        scratch_shapes=[stage, stage, stage, stage, pltpu.SemaphoreType.DMA, pltpu.SemaphoreType.DMA],
        compiler_params=_cp(("arbitrary",)),
        name="moe_combine",
    )(dest3, dest3, x2d, rinfo, g3, b3, ys)


def _rope_tables(pos):
    half = QK_ROPE // 2
    inv = ROPE_THETA ** (-jnp.arange(half, dtype=F32) / half)
    ang = pos.astype(F32)[:, None] * inv[None, :]
    cos, sin = jnp.cos(ang), jnp.sin(ang)
    n = pos.shape[0]
    pad_r = LANES - ROPE_LANE0 - QK_ROPE
    cos_t = jnp.concatenate([jnp.ones((n, ROPE_LANE0), F32), cos, cos, jnp.zeros((n, pad_r), F32)], -1)
    sin_t = jnp.concatenate([jnp.zeros((n, ROPE_LANE0), F32), sin, sin, jnp.zeros((n, pad_r), F32)], -1)
    return cos_t, sin_t


def _swap_neg(wr):
    half = QK_ROPE // 2
    return jnp.concatenate([-wr[:, half:], wr[:, :half]], axis=1)


def _prep_weights(l, w_in, q_norm_g, kv_norm_g, w_uq, w_ukv, conv_w, w_out, ln1_g, ln1_b, w_xq, w_xk, w_xv, w_xo,
                  ln2_g, ln2_b, w_router_group, b_router_group, w_router_expert, b_router_expert,
                  w_exp_gate, w_exp_up, w_exp_down, ln3_g, ln3_b):
    wi = w_in[l]
    c0 = Q_LORA + KV_LORA
    w_kr = wi[:, c0:c0 + QK_ROPE]
    zl = jnp.zeros((D_MODEL, ROPE_LANE0), F32)
    w_in_p = jnp.concatenate([wi[:, :c0], wi[:, c0 + QK_ROPE:], zl, w_kr, _swap_neg(w_kr)], axis=1)
    wq = w_uq[l].reshape(Q_LORA, N_HEADS, QK_NOPE + QK_ROPE)
    wq_rot = jnp.concatenate([-wq[:, :, QK_NOPE + QK_ROPE // 2:], wq[:, :, QK_NOPE:QK_NOPE + QK_ROPE // 2]], axis=2)
    wq_a = jnp.concatenate([wq, wq_rot], axis=2).reshape(Q_LORA, N_HEADS * HEAD_PAD)
    wkv = w_ukv[l].reshape(KV_LORA, N_HEADS, QK_NOPE + V_HEAD)
    wk_p = jnp.concatenate([wkv[:, :, :QK_NOPE], jnp.zeros((KV_LORA, N_HEADS, HEAD_PAD - QK_NOPE), F32)], axis=2)
    wk_p = wk_p.reshape(KV_LORA, N_HEADS * HEAD_PAD)
    wv_p = jnp.concatenate([wkv[:, :, QK_NOPE:], jnp.zeros((KV_LORA, N_HEADS, HEAD_PAD - V_HEAD), F32)], axis=2)
    wv_p = wv_p.reshape(KV_LORA, N_HEADS * HEAD_PAD)
    w_router = jnp.concatenate([w_router_group[l], w_router_expert[l],
                                jnp.zeros((D_MODEL, LANES - N_GROUPS - N_EXPERTS), F32)], axis=1)
    b_router = jnp.concatenate([b_router_group[l], b_router_expert[l].reshape(-1),
                                jnp.zeros((LANES - N_GROUPS - N_EXPERTS,), F32)]).reshape(1, LANES)
    return dict(
        w_in=w_in_p.astype(BF16),
        q_norm_g=q_norm_g[l].reshape(1, Q_LORA), kv_norm_g=kv_norm_g[l].reshape(1, KV_LORA),
        w_uq=wq_a.astype(BF16),
        w_ukv=jnp.concatenate([wk_p, wv_p], axis=1).astype(BF16),
        conv_w=conv_w[l],
        w_out=w_out[l].astype(BF16), ln1_g=ln1_g[l].reshape(1, -1), ln1_b=ln1_b[l].reshape(1, -1),
        w_xq=w_xq[l].astype(BF16), w_xk=w_xk[l].astype(BF16), w_xv=w_xv[l].astype(BF16),
        w_xo=w_xo[l].astype(BF16), ln2_g=ln2_g[l].reshape(1, -1), ln2_b=ln2_b[l].reshape(1, -1),
        w_router=w_router.astype(BF16), b_router=b_router,
        w_exp_gate=w_exp_gate[l], w_exp_up=w_exp_up[l], w_exp_down=w_exp_down[l],
        ln3_g=ln3_g[l].reshape(1, -1), ln3_b=ln3_b[l].reshape(1, -1),
    )


def _slots(rt, pstarts):
    g = rt.shape[-1]
    rt = rt.reshape(-1, SUBLANES, g)
    dest = pstarts[rt[:, 0:2, :].astype(I32)] + rt[:, 2:4, :].astype(I32)
    return dest.reshape(-1), g


def _layer(l, depth, xp, xs, lat_past, kr_past, conv_past, mk_s, mv_s, mem_prompt, w):
    alpha = (2 * depth) ** 0.25
    b, s, _ = xp.shape
    bs, ss, _ = xs.shape
    past = lat_past.shape[1]

    cos_p, sin_p = _rope_tables(jnp.arange(s))
    q_p, k_p, v_p, yc_p, lat_p, kr_p, cst_p = _inproj(
        xp, jnp.zeros((b, CONV_K - 1, CONV_WIDTH), F32), cos_p, sin_p, w)
    attn_p = _attention(q_p, k_p, v_p, 0, None)
    mk, mv, mk_b, mv_b = _memkv(mem_prompt.reshape(b * N_MEM, D_MODEL), w['w_xk'], w['w_xv'])
    cnt0 = jnp.zeros((1, LANES), F32)
    x2_p, ri_p, cnt_p, rt_p = _mid(xp, attn_p, yc_p, mk_b.reshape(b, N_MEM, D_MODEL), mv_b.reshape(b, N_MEM, D_MODEL),
                             cnt0, w, alpha)

    cos_s, sin_s = _rope_tables(past + jnp.arange(ss))
    q_s, _, _, yc_s, lat_s, kr_s, cst_s = _inproj(xs, conv_past, cos_s, sin_s, w)
    n_keys = past + ss
    sk = -(-n_keys // 512) * 512
    lat_all = jnp.concatenate([lat_past, lat_s, jnp.zeros((bs, sk - n_keys, KV_LORA), F32)], axis=1)
    kr_all = jnp.concatenate([kr_past, kr_s, jnp.zeros((bs, sk - n_keys, QK_ROPE), F32)], axis=1)
    kr_all = jnp.pad(kr_all, ((0, 0), (0, 0), (ROPE_LANE0, LANES - ROPE_LANE0 - QK_ROPE)))
    k_s, v_s = _kvup(lat_all, kr_all, w['w_ukv'])
    attn_s = _attention(q_s, k_s, v_s, past, n_keys)
    mk_sb = mk_s.reshape(bs, N_MEM, D_MODEL).astype(BF16)
    mv_sb = mv_s.reshape(bs, N_MEM, D_MODEL).astype(BF16)
    x2_s, ri_s, cnt, rt_s = _mid(xs, attn_s, yc_s, mk_sb, mv_sb, cnt_p, w, alpha)

    n_p, n_s = b * s, bs * ss
    counts = cnt[0, ROUTER_LANE0:ROUTER_LANE0 + N_EXPERTS].astype(I32)
    padded = (counts + MOE_ROWS - 1) // MOE_ROWS * MOE_ROWS
    pends = jnp.cumsum(padded)
    pstarts = pends - padded
    nblk = -(-2 * (n_p + n_s) // MOE_ROWS) + N_EXPERTS
    blk_start = jnp.arange(nblk, dtype=I32) * MOE_ROWS
    block_e = jnp.minimum(jnp.sum((pends[None, :] <= blk_start[:, None]).astype(I32), axis=1), N_EXPERTS - 1)
    n_used = (pends[-1] // MOE_ROWS).astype(I32).reshape(1)
    ri_p2, ri_s2 = ri_p.reshape(n_p, LANES), ri_s.reshape(n_s, LANES)
    dest_p, g_p = _slots(rt_p, pstarts)
    dest_s, g_s = _slots(rt_s, pstarts)
    x2_p2, x2_s2 = x2_p.reshape(n_p, D_MODEL), x2_s.reshape(n_s, D_MODEL)
    pad_info = jnp.concatenate([pstarts + counts, padded - counts, n_used]).astype(I32)
    slots = _dispatch(x2_p2, dest_p, g_p, x2_s2, dest_s, g_s, pad_info, nblk, MOE_ROWS)
    ys = _experts(slots, block_e, n_used, w['w_exp_gate'], w['w_exp_up'], w['w_exp_down'], MOE_ROWS)
    y_p = _combine(x2_p2, ri_p2, dest_p, g_p, ys, w['ln3_g'], w['ln3_b'], alpha).reshape(b, s, D_MODEL)
    y_s = _combine(x2_s2, ri_s2, dest_s, g_s, ys, w['ln3_g'], w['ln3_b'], alpha).reshape(bs, ss, D_MODEL)
    return (y_p, y_s, lat_p, kr_p, cst_p, mk.reshape(b, N_MEM, X_HEADS, X_HEAD_DIM),
            mv.reshape(b, N_MEM, X_HEADS, X_HEAD_DIM), lat_s, kr_s, cst_s)


def kernel(x_prompt, x_sample, cache_kv_latent, cache_k_rope, cache_conv, cache_mem_k, cache_mem_v, mem_prompt,
           w_in, q_norm_g, kv_norm_g, w_uq, w_ukv, conv_w, w_out, ln1_g, ln1_b, w_xq, w_xk, w_xv, w_xo, ln2_g,
           ln2_b, w_router_group, b_router_group, w_router_expert, b_router_expert, w_exp_gate, w_exp_up,
           w_exp_down, ln3_g, ln3_b):
    depth = w_in.shape[0]
    xp, xs = x_prompt, x_sample
    outs = [[] for _ in range(8)]
    for l in range(depth):
        w = _prep_weights(l, w_in, q_norm_g, kv_norm_g, w_uq, w_ukv, conv_w, w_out, ln1_g, ln1_b, w_xq, w_xk, w_xv,
                          w_xo, ln2_g, ln2_b, w_router_group, b_router_group, w_router_expert, b_router_expert,
                          w_exp_gate, w_exp_up, w_exp_down, ln3_g, ln3_b)
        res = _layer(l, depth, xp, xs, cache_kv_latent[l], cache_k_rope[l], cache_conv[l], cache_mem_k[l],
                     cache_mem_v[l], mem_prompt, w)
        xp, xs = res[0], res[1]
        for acc, r in zip(outs, res[2:]):
            acc.append(r)
    return (xp, xs) + tuple(jnp.stack(o) for o in outs)
```

```python
import functools
import math

import numpy as np
import jax
import jax.numpy as jnp
from jax import lax
from jax.experimental import pallas as pl
from jax.experimental.pallas import tpu as pltpu

F32 = jnp.float32
BF16 = jnp.bfloat16
I32 = jnp.int32

D_MODEL = 1024
CHUNK = 64
N_HEADS = 8
QK_NOPE = 64
QK_ROPE = 32
V_HEAD = 64
Q_LORA = 256
KV_LORA = 128
ROPE_THETA = 10000.0
MLA_SCALE = (QK_NOPE + QK_ROPE) ** -0.5
Q_PRESCALE = MLA_SCALE * math.log2(math.e)
CONV_WIDTH = 512
CONV_K = 3
N_MEM = 256
X_HEADS = 4
X_HEAD_DIM = D_MODEL // X_HEADS
X_SCALE = X_HEAD_DIM ** -0.5
N_GROUPS = 4
EXPERTS_PER_GROUP = 8
N_EXPERTS = N_GROUPS * EXPERTS_PER_GROUP
D_EXPERT = 512
LN_EPS = 1e-5
RMS_EPS = 1e-6

LANES = 128
SUBLANES = 8
ROW_CHUNKS = D_MODEL // LANES
HEAD_PAD = LANES
ROPE_LANE0 = QK_NOPE
ROUTER_LANE0 = N_GROUPS
VMEM_LIMIT = 56 * 1024 * 1024
MOE_ROWS = 512
MOE_PARTS = 2
ACT_ROW_SLAB = 32
GATHER_UNROLL = 8
WCAST_ROWS = 64
ATTN_TQ = 1024
ATTN_TK = 512
ATTN_SUB = 2
ATTN_ROW_SLAB = 64
LN_ROW_SLAB = 16
XATTN_ROW_SLAB = 64
INPROJ_ROWS = 512
ROUTE_GROUP = 256
MID_SKEW = 1
MID_ROWS = 512
MID_PARTS = 2


def _cp(sem, vmem=VMEM_LIMIT):
    return pltpu.CompilerParams(dimension_semantics=sem, vmem_limit_bytes=vmem)


def _pick_tile(n, pref):
    t = min(n, pref)
    while n % t:
        t //= 2
    return t


def _with_ones_lane(v):
    lane = lax.broadcasted_iota(I32, v.shape, 1)
    return jnp.where((lane & (HEAD_PAD - 1)) == V_HEAD, 1.0, v)


def _inproj_body(x_ref, cinit_ref, cos_ref, sin_ref, win_ref, qg_ref, kvg_ref, wuq_ref, wukv_ref, cw_ref,
                 q_ref, k_ref, v_ref, yc_ref, lat_ref, kr_ref, cst_ref, u_scr, *, tt, nj):
    j = pl.program_id(1)

    @pl.when(j == 0)
    def _():
        u_scr[6:8, :] = cinit_ref[0]

    x = x_ref[0].astype(BF16)
    proj = jnp.dot(x, win_ref[...], preferred_element_type=F32)
    cq = proj[:, 0:256]
    ckv = proj[:, 256:384]
    gb = proj[:, 384:896]
    gc = proj[:, 896:1408]
    gv = proj[:, 1408:1920]
    kr_blk = proj[:, 1920:2048]
    cos_t = cos_ref[...]
    sin_t = sin_ref[...]

    def rotate(blk):
        return blk * cos_t + pltpu.roll(blk, LANES - QK_ROPE, 1) * sin_t

    cqn = cq * lax.rsqrt(jnp.mean(cq * cq, -1, keepdims=True) + RMS_EPS) * qg_ref[...]
    ckvn = ckv * lax.rsqrt(jnp.mean(ckv * ckv, -1, keepdims=True) + RMS_EPS) * kvg_ref[...]
    lat_ref[0] = ckvn
    kr_p = rotate(kr_blk)
    kr_ref[0] = kr_p[:, ROPE_LANE0:ROPE_LANE0 + QK_ROPE]
    qq = jnp.dot(cqn.astype(BF16), wuq_ref[...], preferred_element_type=F32)
    kv = jnp.dot(ckvn.astype(BF16), wukv_ref[...], preferred_element_type=F32)
    hw = N_HEADS * HEAD_PAD
    for h in range(N_HEADS):
        sl = slice(h * HEAD_PAD, (h + 1) * HEAD_PAD)
        q_ref[0, :, sl] = (rotate(qq[:, sl]) * Q_PRESCALE).astype(BF16)
        k_ref[0, :, sl] = (kv[:, sl] + kr_p).astype(BF16)
    v_ref[0] = _with_ones_lane(kv[:, hw:2 * hw]).astype(BF16)
    u = gc * gv
    u_scr[8:8 + tt, :] = u
    conv = cw_ref[0:1, :] * u_scr[6:6 + tt, :] + cw_ref[1:2, :] * u_scr[7:7 + tt, :] + cw_ref[2:3, :] * u
    yc_ref[0] = (gb * conv).astype(BF16)
    last2 = u_scr[tt + 6:tt + 8, :]
    u_scr[6:8, :] = last2

    @pl.when(j == nj - 1)
    def _():
        cst_ref[0] = last2


def _inproj(x, conv_init, cos_t, sin_t, w):
    b, s, _ = x.shape
    tt = _pick_tile(s, INPROJ_ROWS)
    nj = s // tt
    wn = w['w_in'].shape[1]
    full = lambda shape: pl.BlockSpec(shape, lambda bi, ji: (0,) * len(shape))
    out_shapes = (
        jax.ShapeDtypeStruct((b, s, N_HEADS * HEAD_PAD), BF16),
        jax.ShapeDtypeStruct((b, s, N_HEADS * HEAD_PAD), BF16),
        jax.ShapeDtypeStruct((b, s, N_HEADS * HEAD_PAD), BF16),
        jax.ShapeDtypeStruct((b, s, CONV_WIDTH), BF16),
        jax.ShapeDtypeStruct((b, s, KV_LORA), F32),
        jax.ShapeDtypeStruct((b, s, QK_ROPE), F32),
        jax.ShapeDtypeStruct((b, CONV_K - 1, CONV_WIDTH), F32),
    )
    row = lambda width: pl.BlockSpec((1, tt, width), lambda bi, ji: (bi, ji, 0))
    return pl.pallas_call(
        functools.partial(_inproj_body, tt=tt, nj=nj),
        grid=(b, nj),
        in_specs=[
            row(D_MODEL),
            pl.BlockSpec((1, CONV_K - 1, CONV_WIDTH), lambda bi, ji: (bi, 0, 0)),
            pl.BlockSpec((tt, LANES), lambda bi, ji: (ji, 0)),
            pl.BlockSpec((tt, LANES), lambda bi, ji: (ji, 0)),
            full((D_MODEL, wn)),
            full((1, Q_LORA)),
            full((1, KV_LORA)),
            full(w['w_uq'].shape),
            full(w['w_ukv'].shape),
            full((CONV_K, CONV_WIDTH)),
        ],
        out_specs=(
            row(N_HEADS * HEAD_PAD), row(N_HEADS * HEAD_PAD), row(N_HEADS * HEAD_PAD), row(CONV_WIDTH),
            row(KV_LORA), row(QK_ROPE),
            pl.BlockSpec((1, CONV_K - 1, CONV_WIDTH), lambda bi, ji: (bi, 0, 0)),
        ),
        out_shape=out_shapes,
        scratch_shapes=[pltpu.VMEM((tt + 8, CONV_WIDTH), F32)],
        compiler_params=_cp(("parallel", "arbitrary")),
        name="inproj",
    )(x, conv_init, cos_t, sin_t, w['w_in'], w['q_norm_g'], w['kv_norm_g'], w['w_uq'], w['w_ukv'], w['conv_w'])


def _attn_body(qi_ref, ki_ref, fl_ref, q_ref, k_ref, v_ref, o_ref, m_scr, acc_scr, s_scr, p_scr, a_scr,
               *, tq, tk, n_sub, rs, q_pos0, n_valid, combos):
    step = pl.program_id(1)
    qi = qi_ref[step]
    ki = ki_ref[step]
    flags = fl_ref[step]
    reps = tk // LANES
    th = tq // n_sub

    @pl.when((flags & 1) != 0)
    def _():
        m_scr[...] = jnp.full(m_scr.shape, -jnp.inf, F32)
        acc_scr[...] = jnp.zeros(acc_scr.shape, F32)

    def scores(item, buf):
        sub, h, _ = item
        hs = slice(h * HEAD_PAD, (h + 1) * HEAD_PAD)
        s_scr[buf] = lax.dot_general(q_ref[0, sub * th:(sub + 1) * th, hs], k_ref[0, :, hs],
                                     (((1,), (1,)), ((), ())), preferred_element_type=F32)

    def softmax_pv(item, buf):
        sub, h, masked = item
        hs = slice(h * HEAD_PAD, (h + 1) * HEAD_PAD)
        for r in range(th // rs):
            rows = slice(r * rs, (r + 1) * rs)
            arows = slice(sub * th + r * rs, sub * th + (r + 1) * rs)
            s_r = s_scr[buf, rows, :]
            if masked:
                qpos = q_pos0 + qi * tq + sub * th + r * rs + lax.broadcasted_iota(I32, (rs, tk), 0)
                kpos = ki * tk + lax.broadcasted_iota(I32, (rs, tk), 1)
                mask = (kpos >> 6) <= (qpos >> 6)
                if n_valid is not None:
                    mask = mask & (kpos < n_valid)
                s_r = jnp.where(mask, s_r, -jnp.inf)
            m_old = m_scr[h, arows, :]
            m_new = jnp.maximum(m_old, jnp.max(s_r, axis=-1, keepdims=True))
            a_scr[buf, rows, :] = jnp.exp2(m_old - m_new)
            m_rep = jnp.concatenate([m_new] * reps, axis=1)
            p_scr[buf, rows, :] = jnp.exp2(s_r - m_rep).astype(BF16)
            m_scr[h, arows, :] = m_new
        pv = jnp.dot(p_scr[buf], v_ref[0, :, hs], preferred_element_type=F32)
        srows = slice(sub * th, (sub + 1) * th)
        acc_scr[h, srows, :] = a_scr[buf] * acc_scr[h, srows, :] + pv

    def run(modes):
        items = [(sub, h, mode == 2) for sub, mode in enumerate(modes) if mode != 0 for h in range(N_HEADS)]
        scores(items[0], 0)
        for n, item in enumerate(items):
            if n + 1 < len(items):
                scores(items[n + 1], (n + 1) % 2)
            softmax_pv(item, n % 2)

    for code, modes in combos:
        @pl.when((flags >> 2) == code)
        def _(modes=modes):
            run(modes)

    @pl.when((flags & 2) != 0)
    def _():
        for h in range(N_HEADS):
            acc = acc_scr[h]
            o_ref[0, :, h * V_HEAD:(h + 1) * V_HEAD] = (acc[:, 0:V_HEAD] / acc[:, V_HEAD:V_HEAD + 1]).astype(BF16)


def _attn_tables(nq, nk, tq, tk, n_sub, q_pos0, n_valid):
    th = tq // n_sub
    qi_l, ki_l, fl_l, combos = [], [], [], {}
    for qi in range(nq):
        sub_lo = [q_pos0 + qi * tq + j * th for j in range(n_sub)]
        sub_last = []
        for lo in sub_lo:
            last_pos = ((lo + th - 1) // CHUNK) * CHUNK + CHUNK - 1
            if n_valid is not None:
                last_pos = min(last_pos, n_valid - 1)
            sub_last.append(min(nk - 1, last_pos // tk))
        k_last = max(sub_last)
        for ki in range(k_last + 1):
            k_hi = ki * tk + tk - 1
            modes = []
            for lo, last in zip(sub_lo, sub_last):
                if ki > last:
                    modes.append(0)
                elif (k_hi // CHUNK) > (lo // CHUNK) or (n_valid is not None and k_hi >= n_valid):
                    modes.append(2)
                else:
                    modes.append(1)
            code = sum(m * 3 ** j for j, m in enumerate(modes))
            combos[code] = tuple(modes)
            qi_l.append(qi); ki_l.append(ki)
            fl_l.append((1 if ki == 0 else 0) | (2 if ki == k_last else 0) | (code << 2))
    to_arr = lambda vals: jnp.asarray(np.array(vals, np.int32))
    return to_arr(qi_l), to_arr(ki_l), to_arr(fl_l), tuple(sorted(combos.items()))


def _attention(q, k, v, q_pos0, n_valid):
    b, sq, _ = q.shape
    sk = k.shape[1]
    tq = _pick_tile(sq, ATTN_TQ)
    tk = _pick_tile(sk, ATTN_TK)
    n_sub = ATTN_SUB if tq % (ATTN_SUB * 2 * SUBLANES) == 0 else 1
    th = tq // n_sub
    qi_t, ki_t, fl_t, combos = _attn_tables(sq // tq, sk // tk, tq, tk, n_sub, q_pos0, n_valid)
    n_steps = int(qi_t.shape[0])
    grid_spec = pltpu.PrefetchScalarGridSpec(
        num_scalar_prefetch=3,
        grid=(b, n_steps),
        in_specs=[
            pl.BlockSpec((1, tq, N_HEADS * HEAD_PAD), lambda bi, si, qt, kt, ft: (bi, qt[si], 0)),
            pl.BlockSpec((1, tk, N_HEADS * HEAD_PAD), lambda bi, si, qt, kt, ft: (bi, kt[si], 0)),
            pl.BlockSpec((1, tk, N_HEADS * HEAD_PAD), lambda bi, si, qt, kt, ft: (bi, kt[si], 0)),
        ],
        out_specs=pl.BlockSpec((1, tq, N_HEADS * V_HEAD), lambda bi, si, qt, kt, ft: (bi, qt[si], 0)),
        scratch_shapes=[
            pltpu.VMEM((N_HEADS, tq, LANES), F32),
            pltpu.VMEM((N_HEADS, tq, LANES), F32),
            pltpu.VMEM((2, th, tk), F32),
            pltpu.VMEM((2, th, tk), BF16),
            pltpu.VMEM((2, th, LANES), F32),
        ],
    )
    return pl.pallas_call(
        functools.partial(_attn_body, tq=tq, tk=tk, n_sub=n_sub, rs=min(th, ATTN_ROW_SLAB), q_pos0=q_pos0,
                          n_valid=n_valid, combos=combos),
        grid_spec=grid_spec,
        out_shape=jax.ShapeDtypeStruct((b, sq, N_HEADS * V_HEAD), BF16),
        compiler_params=_cp(("parallel", "arbitrary")),
        name="mla_attn",
    )(qi_t, ki_t, fl_t, q, k, v)


def _decode_attn_body(q_ref, lat_ref, krp_ref, wukv_ref, o_ref, qa_scr, qr_scr, *, ss, n_valid):
    hw = N_HEADS * HEAD_PAD
    latb = lat_ref[0].astype(BF16)
    krb = krp_ref[0].astype(BF16)
    contract_last = (((1,), (1,)), ((), ()))
    for h in range(N_HEADS):
        hs = slice(h * HEAD_PAD, (h + 1) * HEAD_PAD)
        rows = slice(h * ss, (h + 1) * ss)
        q_h = q_ref[0, :, hs]
        qa_scr[rows, :] = lax.dot_general(q_h, wukv_ref[:, hs], contract_last,
                                          preferred_element_type=F32).astype(BF16)
        qr_scr[rows, :] = q_h
    s = (lax.dot_general(qa_scr[...], latb, contract_last, preferred_element_type=F32)
         + lax.dot_general(qr_scr[...], krb, contract_last, preferred_element_type=F32))
    kpos = lax.broadcasted_iota(I32, s.shape, 1)
    s = jnp.where(kpos < n_valid, s, -jnp.inf)
    p = jnp.exp2(s - jnp.max(s, -1, keepdims=True))
    ol = jnp.dot(p.astype(BF16), latb, preferred_element_type=F32) / jnp.sum(p, -1, keepdims=True)
    for h in range(N_HEADS):
        rows = slice(h * ss, (h + 1) * ss)
        o_h = jnp.dot(ol[rows, :].astype(BF16), wukv_ref[:, hw + h * HEAD_PAD:hw + (h + 1) * HEAD_PAD],
                      preferred_element_type=F32)
        o_ref[0, :, h * V_HEAD:(h + 1) * V_HEAD] = o_h[:, 0:V_HEAD].astype(BF16)


def _decode_attention(q, lat_all, kr_padded, w_ukv, n_valid):
    b, ss, _ = q.shape
    sk = lat_all.shape[1]
    return pl.pallas_call(
        functools.partial(_decode_attn_body, ss=ss, n_valid=n_valid),
        grid=(b,),
        in_specs=[pl.BlockSpec((1, ss, N_HEADS * HEAD_PAD), lambda bi: (bi, 0, 0)),
                  pl.BlockSpec((1, sk, KV_LORA), lambda bi: (bi, 0, 0)),
                  pl.BlockSpec((1, sk, LANES), lambda bi: (bi, 0, 0)),
                  pl.BlockSpec(w_ukv.shape, lambda bi: (0, 0))],
        out_specs=pl.BlockSpec((1, ss, N_HEADS * V_HEAD), lambda bi: (bi, 0, 0)),
        out_shape=jax.ShapeDtypeStruct((b, ss, N_HEADS * V_HEAD), BF16),
        scratch_shapes=[pltpu.VMEM((N_HEADS * ss, KV_LORA), BF16), pltpu.VMEM((N_HEADS * ss, HEAD_PAD), BF16)],
        compiler_params=_cp(("parallel",)),
        name="decode_attn",
    )(q, lat_all, kr_padded, w_ukv)


def _memkv_body(mem_ref, wk_ref, wv_ref, mk_ref, mv_ref, mkb_ref, mvb_ref):
    m = mem_ref[...].astype(BF16)
    mk = jnp.dot(m, wk_ref[...], preferred_element_type=F32)
    mv = jnp.dot(m, wv_ref[...], preferred_element_type=F32)
    mk_ref[...] = mk
    mv_ref[...] = mv
    mkb_ref[...] = mk.astype(BF16)
    mvb_ref[...] = mv.astype(BF16)


def _memkv(mem2d, w_xk, w_xv):
    n = mem2d.shape[0]
    tt = _pick_tile(n, 256)
    row = pl.BlockSpec((tt, D_MODEL), lambda i: (i, 0))
    wspec = pl.BlockSpec((D_MODEL, D_MODEL), lambda i: (0, 0))
    return pl.pallas_call(
        _memkv_body,
        grid=(n // tt,),
        in_specs=[row, wspec, wspec],
        out_specs=(row, row, row, row),
        out_shape=(jax.ShapeDtypeStruct((n, D_MODEL), F32), jax.ShapeDtypeStruct((n, D_MODEL), F32),
                   jax.ShapeDtypeStruct((n, D_MODEL), BF16), jax.ShapeDtypeStruct((n, D_MODEL), BF16)),
        compiler_params=_cp(("parallel",)),
        name="memkv",
    )(mem2d, w_xk, w_xv)


def _layer_norm(x, g, b):
    mu = jnp.mean(x, -1, keepdims=True)
    xc = x - mu
    var = jnp.mean(xc * xc, -1, keepdims=True)
    return xc * lax.rsqrt(var + LN_EPS) * g + b


def _mid_body(x_ref, at_ref, yc_ref, mk_ref, mv_ref, cnt0_ref, low_ref, wo_ref, g1_ref, b1_ref, wq_ref, wxo_ref,
              g2_ref, b2_ref, wr_ref, br_ref, x2_ref, ri_ref, cnt_ref, rt_ref,
              cnt_scr, a_scr, xb_scr, q_scr, sc_scr, p_scr, o_scr, lg_scr, *, tt, alpha):
    step = pl.program_id(0)

    @pl.when(step == 0)
    def _():
        cnt_scr[...] = cnt0_ref[...]
        lg_scr[...] = jnp.zeros(lg_scr.shape, F32)

    cur_slot = step % 2
    prev_slot = 1 - cur_slot
    routed = (step > 0).astype(F32)

    n_part = MID_PARTS if tt % (MID_PARTS * SUBLANES * 2) == 0 else 1
    pr = tt // n_part
    ln_rs = min(pr, LN_ROW_SLAB)
    sm_rs = min(pr, XATTN_ROW_SLAB)
    c_exp = X_SCALE * math.log2(math.e)
    lane = lax.broadcasted_iota(I32, (pr, LANES), 1)

    def part_rows(k):
        return slice(k * pr, (k + 1) * pr)

    def out_proj(k):
        rp = part_rows(k)
        mix = jnp.concatenate([at_ref[0, rp, :], yc_ref[0, rp, :]], axis=-1)
        a_scr[rp, :] = jnp.dot(mix, wo_ref[...], preferred_element_type=F32)

    def norm1(k):
        for r in range(pr // ln_rs):
            rows = slice(k * pr + r * ln_rs, k * pr + (r + 1) * ln_rs)
            x1 = _layer_norm(alpha * x_ref[0, rows, :] + a_scr[rows, :], g1_ref[...], b1_ref[...])
            x2_ref[0, rows, :] = x1
            xb_scr[rows, :] = x1.astype(BF16)

    def q_proj(k):
        rp = part_rows(k)
        q_scr[rp, :] = jnp.dot(xb_scr[rp, :], wq_ref[...], preferred_element_type=F32).astype(BF16)

    def cross_attn(k):
        rp = part_rows(k)
        for h in range(X_HEADS):
            sl = slice(h * X_HEAD_DIM, (h + 1) * X_HEAD_DIM)
            b2 = h % 2
            sc_scr[k, b2] = lax.dot_general(q_scr[rp, sl], mk_ref[0, :, sl], (((1,), (1,)), ((), ())),
                                            preferred_element_type=F32)
            for r in range(pr // sm_rs):
                rows = slice(r * sm_rs, (r + 1) * sm_rs)
                s_r = sc_scr[k, b2, rows, :]
                e = jnp.exp2((s_r - jnp.max(s_r, -1, keepdims=True)) * c_exp)
                p_scr[k, b2, rows, :] = (e / jnp.sum(e, -1, keepdims=True)).astype(BF16)
            o_scr[rp, sl] = jnp.dot(p_scr[k, b2], mv_ref[0, :, sl], preferred_element_type=F32).astype(BF16)

    def x_out_proj(k):
        rp = part_rows(k)
        a_scr[rp, :] = jnp.dot(o_scr[rp, :], wxo_ref[...], preferred_element_type=F32)

    def norm2(k):
        for r in range(pr // ln_rs):
            rows = slice(k * pr + r * ln_rs, k * pr + (r + 1) * ln_rs)
            x2 = _layer_norm(alpha * x2_ref[0, rows, :] + a_scr[rows, :], g2_ref[...], b2_ref[...])
            x2_ref[0, rows, :] = x2
            xb_scr[rows, :] = x2.astype(BF16)

    def router_logits(k):
        rp = part_rows(k)
        lg_scr[cur_slot, rp, :] = jnp.dot(xb_scr[rp, :], wr_ref[...], preferred_element_type=F32) + br_ref[...]

    def route(k):
        rp = part_rows(k)
        logits = lg_scr[prev_slot, rp, :]
        neg = -jnp.inf
        is_g = lane < N_GROUPS
        lg = jnp.where(is_g, logits, neg)
        mg = jnp.max(lg, -1, keepdims=True)
        g_idx = jnp.min(jnp.where(lg == mg, lane, LANES), -1, keepdims=True)
        pg = 1.0 / jnp.sum(jnp.where(is_g, jnp.exp(logits - mg), 0.0), -1, keepdims=True)
        in_grp = ((lane >= ROUTER_LANE0) & (lane < ROUTER_LANE0 + N_EXPERTS)
                  & (((lane - ROUTER_LANE0) >> 3) == g_idx))
        le = jnp.where(in_grp, logits, neg)
        v1 = jnp.max(le, -1, keepdims=True)
        i1 = jnp.min(jnp.where(le == v1, lane, LANES), -1, keepdims=True)
        le2 = jnp.where(lane == i1, neg, le)
        v2 = jnp.max(le2, -1, keepdims=True)
        i2 = jnp.min(jnp.where(le2 == v2, lane, LANES), -1, keepdims=True)
        e2 = jnp.exp(v2 - v1)
        den = 1.0 + e2
        gate1 = (1.0 / den) * pg
        gate2 = (e2 / den) * pg
        oh1 = (lane == i1).astype(F32)
        oh2 = (lane == i2).astype(F32)
        oh = oh1 + oh2
        base = cnt_scr[...] + jnp.dot(low_ref[...], oh.astype(BF16), preferred_element_type=F32)
        rank1 = jnp.sum(oh1 * base, -1, keepdims=True)
        rank2 = jnp.sum(oh2 * base, -1, keepdims=True)
        cnt_scr[...] = cnt_scr[...] + jnp.sum(oh, 0, keepdims=True) * routed
        e1f = (i1 - ROUTER_LANE0).astype(F32)
        e2f = (i2 - ROUTER_LANE0).astype(F32)
        ri = jnp.where(lane == 0, e1f, jnp.where(lane == 1, e2f, jnp.where(
            lane == 2, rank1, jnp.where(lane == 3, rank2, jnp.where(lane == 4, gate1, jnp.where(
                lane == 5, gate2, 0.0))))))
        ri_ref[0, rp, :] = ri
        ri_t = jnp.transpose(ri)[0:SUBLANES, :]
        grp = min(pr, ROUTE_GROUP)
        for j in range(pr // grp):
            rt_ref[0, k * (pr // grp) + j] = ri_t[:, j * grp:(j + 1) * grp]

    stages = (out_proj, norm1, q_proj, cross_attn, x_out_proj, norm2, router_logits)
    matmul_stages = (out_proj, q_proj, x_out_proj, router_logits)
    for t in range(len(stages) + MID_SKEW * (n_part - 1)):
        todo = [(stages[t - MID_SKEW * k], k) for k in range(n_part) if 0 <= t - MID_SKEW * k < len(stages)]
        for fn, k in sorted(todo, key=lambda fk: fk[0] not in matmul_stages):
            fn(k)
        if t < n_part:
            route(t)
    cnt_ref[...] = cnt_scr[...]


def _mid(x, attn, yconv, mk_b, mv_b, cnt0, w, alpha):
    b, s, _ = x.shape
    tt = _pick_tile(s, MID_ROWS)
    nj = s // tt
    n_tiles = b * nj
    cur = lambda i: jnp.minimum(i, n_tiles - 1)
    prev = lambda i: jnp.maximum(i - 1, 0)
    row = lambda width: pl.BlockSpec((1, tt, width), lambda i: (cur(i) // nj, cur(i) % nj, 0))
    row_prev = lambda width: pl.BlockSpec((1, tt, width), lambda i: (prev(i) // nj, prev(i) % nj, 0))
    full = lambda shape: pl.BlockSpec(shape, lambda i: (0,) * len(shape))
    mem = pl.BlockSpec((1, N_MEM, D_MODEL), lambda i: (cur(i) // nj, 0, 0))
    vec = full((1, D_MODEL))
    n_part = MID_PARTS if tt % (MID_PARTS * SUBLANES * 2) == 0 else 1
    pr = tt // n_part
    grp = min(pr, ROUTE_GROUP)
    lower = jnp.tril(jnp.ones((pr, pr), F32), -1).astype(BF16)
    return pl.pallas_call(
        functools.partial(_mid_body, tt=tt, alpha=alpha),
        grid=(n_tiles + 1,),
        in_specs=[row(D_MODEL), row(N_HEADS * V_HEAD), row(CONV_WIDTH), mem, mem, full((1, LANES)), full((pr, pr)),
                  full((D_MODEL, D_MODEL)), vec, vec, full((D_MODEL, D_MODEL)), full((D_MODEL, D_MODEL)),
                  vec, vec, full((D_MODEL, LANES)), full((1, LANES))],
        out_specs=(row(D_MODEL), row_prev(LANES), full((1, LANES)),
                   pl.BlockSpec((1, tt // grp, SUBLANES, grp), lambda i: (prev(i) // nj, prev(i) % nj, 0, 0))),
        out_shape=(jax.ShapeDtypeStruct((b, s, D_MODEL), F32), jax.ShapeDtypeStruct((b, s, LANES), F32),
                   jax.ShapeDtypeStruct((1, LANES), F32),
                   jax.ShapeDtypeStruct((b, s // grp, SUBLANES, grp), F32)),
        scratch_shapes=[
            pltpu.VMEM((1, LANES), F32),
            pltpu.VMEM((tt, D_MODEL), F32),
            pltpu.VMEM((tt, D_MODEL), BF16),
            pltpu.VMEM((tt, D_MODEL), BF16),
            pltpu.VMEM((n_part, 2, pr, N_MEM), F32),
            pltpu.VMEM((n_part, 2, pr, N_MEM), BF16),
            pltpu.VMEM((tt, D_MODEL), BF16),
            pltpu.VMEM((2, tt, LANES), F32),
        ],
        compiler_params=_cp(("arbitrary",)),
        name="mid",
    )(x, attn, yconv, mk_b, mv_b, cnt0, lower, w['w_out'], w['ln1_g'], w['ln1_b'], w['w_xq'], w['w_xo'],
      w['ln2_g'], w['ln2_b'], w['w_router'], w['b_router'])


def _row_slice(ref, row):
    return ref.at[pl.ds(pl.multiple_of(row * SUBLANES, SUBLANES), SUBLANES), :]


def _slot_index(t, kk, g, n):
    if g >= n:
        return kk * g + t
    shift = g.bit_length() - 1
    return ((t >> shift) << (shift + 1)) + kk * g + (t & (g - 1))


def _rows_slice(ref, row, n_rows):
    return ref.at[pl.ds(pl.multiple_of(row * SUBLANES, SUBLANES), n_rows * SUBLANES), :]


def _dispatch_body(pad_ref, dest_ref, dest_s_ref, x_ref, x_s_ref, xs_ref, buf0, buf1, zbuf, sem0, sem1, zsem,
                   *, tt, ts, nt, nblk, rows, g_p, g_s):
    i = pl.program_id(0)
    bufs = (buf0, buf1)
    sems = (sem0, sem1)
    len_bits = rows.bit_length() - 1

    def scatter_rows(buf, sem, src_ref, d_ref, n, g):
        for c in range(ROW_CHUNKS):
            buf[pl.ds(c, n, stride=SUBLANES), :] = src_ref[:, c * LANES:(c + 1) * LANES]
        unroll = min(n, GATHER_UNROLL)

        def issue(j, carry):
            for u in range(unroll):
                t = j * unroll + u
                src = _row_slice(buf, t)
                for kk in range(2):
                    d = d_ref[0, 0, _slot_index(t, kk, g, n)]
                    pltpu.make_async_copy(src, _row_slice(xs_ref, d), sem).start(priority=kk)
            return carry

        lax.fori_loop(0, n // unroll, issue, 0)

    def wait_rows(buf, sem, n):
        for _ in range(2):
            pltpu.make_async_copy(_rows_slice(buf, 0, n), _rows_slice(xs_ref, 0, n), sem).wait()

    def zero_fill(wait):
        def fire(copy):
            if wait:
                copy.wait()
            else:
                copy.start()

        def per_expert(e, carry):
            first = pad_ref[e]
            n_pad = pad_ref[N_EXPERTS + e]
            for bit in range(len_bits):
                size = 1 << bit
                off = (n_pad >> (bit + 1)) << (bit + 1)

                @pl.when(((n_pad >> bit) & 1) == 1)
                def _():
                    fire(pltpu.make_async_copy(_rows_slice(zbuf, 0, size), _rows_slice(xs_ref, first + off, size),
                                               zsem))
            return carry

        lax.fori_loop(0, N_EXPERTS, per_expert, 0)

        def per_block(j, carry):
            fire(pltpu.make_async_copy(zbuf, _rows_slice(xs_ref, j * rows, rows), zsem))
            return carry

        lax.fori_loop(pad_ref[2 * N_EXPERTS], nblk, per_block, 0)

    def run(slot):
        @pl.when(i < nt)
        def _():
            scatter_rows(bufs[slot], sems[slot], x_ref, dest_ref, tt, g_p)

        @pl.when(i == nt)
        def _():
            zbuf[...] = jnp.zeros(zbuf.shape, F32)
            scatter_rows(bufs[slot], sems[slot], x_s_ref, dest_s_ref, ts, g_s)
            zero_fill(False)

        @pl.when(i > 0)
        def _():
            wait_rows(bufs[1 - slot], sems[1 - slot], tt)

        @pl.when(i == nt)
        def _():
            wait_rows(bufs[slot], sems[slot], ts)
            zero_fill(True)

    @pl.when(i % 2 == 0)
    def _():
        run(0)

    @pl.when(i % 2 == 1)
    def _():
        run(1)


def _dispatch(x_p, dest_p, g_p, x_s, dest_s, g_s, pad_info, nblk, rows):
    n_p, n_s = x_p.shape[0], x_s.shape[0]
    tt = _pick_tile(n_p, 256)
    nt = n_p // tt
    assert n_s <= tt
    last = lambda i, pad: jnp.minimum(i, nt - 1)
    grid_spec = pltpu.PrefetchScalarGridSpec(
        num_scalar_prefetch=1,
        grid=(nt + 1,),
        in_specs=[
            pl.BlockSpec((1, 1, 2 * tt), lambda i, pad: (last(i, pad), 0, 0), memory_space=pltpu.SMEM),
            pl.BlockSpec((1, 1, 2 * n_s), lambda i, pad: (0, 0, 0), memory_space=pltpu.SMEM),
            pl.BlockSpec((tt, D_MODEL), lambda i, pad: (last(i, pad), 0)),
            pl.BlockSpec((n_s, D_MODEL), lambda i, pad: (0, 0)),
        ],
        out_specs=pl.BlockSpec(memory_space=pl.ANY),
        scratch_shapes=[pltpu.VMEM((tt * SUBLANES, LANES), F32), pltpu.VMEM((tt * SUBLANES, LANES), F32),
                        pltpu.VMEM((rows * SUBLANES, LANES), F32),
                        pltpu.SemaphoreType.DMA, pltpu.SemaphoreType.DMA, pltpu.SemaphoreType.DMA],
    )
    return pl.pallas_call(
        functools.partial(_dispatch_body, tt=tt, ts=n_s, nt=nt, nblk=nblk, rows=rows, g_p=g_p, g_s=g_s),
        grid_spec=grid_spec,
        out_shape=jax.ShapeDtypeStruct((nblk * rows * SUBLANES, LANES), F32),
        compiler_params=_cp(("arbitrary",)),
        name="moe_dispatch",
    )(pad_info, dest_p.reshape(nt, 1, 2 * tt), dest_s.reshape(1, 1, 2 * n_s), x_p, x_s)


def _experts_body(be_ref, nu_ref, x_ref, wg_ref, wu_ref, wd_ref, y_ref, xb_scr, g_scr, u_scr, h_scr,
                  wgb_scr, wub_scr, wdb_scr, *, rows):
    i = pl.program_id(0)
    cur = jnp.minimum(i, nu_ref[0] - 1)
    new_expert = (i == 0) | (be_ref[cur] != be_ref[jnp.maximum(cur - 1, 0)])

    @pl.when((i < nu_ref[0]) & new_expert)
    def _():
        for r in range(0, D_MODEL, WCAST_ROWS):
            wgb_scr[r:r + WCAST_ROWS, :] = wg_ref[0, r:r + WCAST_ROWS, :].astype(BF16)
            wub_scr[r:r + WCAST_ROWS, :] = wu_ref[0, r:r + WCAST_ROWS, :].astype(BF16)
        for r in range(0, D_EXPERT, WCAST_ROWS // 2):
            wdb_scr[r:r + WCAST_ROWS // 2, :] = wd_ref[0, r:r + WCAST_ROWS // 2, :].astype(BF16)
    n_part = MOE_PARTS
    pr = rows // n_part
    act_rs = min(pr, ACT_ROW_SLAB)

    def load(k):
        base = k * pr * SUBLANES
        xb_scr[k] = jnp.concatenate([x_ref[pl.ds(base + c, pr, stride=SUBLANES), :] for c in range(ROW_CHUNKS)],
                                    axis=-1).astype(BF16)

    def gate_up(k):
        g_scr[k] = jnp.dot(xb_scr[k], wgb_scr[...], preferred_element_type=F32)
        u_scr[k] = jnp.dot(xb_scr[k], wub_scr[...], preferred_element_type=F32)

    def act(k):
        for r in range(pr // act_rs):
            rows_r = slice(r * act_rs, (r + 1) * act_rs)
            g = g_scr[k, rows_r, :]
            h_scr[k, rows_r, :] = ((g * jax.nn.sigmoid(g)) * u_scr[k, rows_r, :]).astype(BF16)

    def down(k):
        y = jnp.dot(h_scr[k], wdb_scr[...], preferred_element_type=F32)
        base = k * pr * SUBLANES
        for c in range(ROW_CHUNKS):
            y_ref[pl.ds(base + c, pr, stride=SUBLANES), :] = y[:, c * LANES:(c + 1) * LANES]

    @pl.when(i < nu_ref[0])
    def _():
        stages = (load, gate_up, act, down)
        matmul_stages = (gate_up, down)
        for t in range(len(stages) + n_part - 1):
            todo = [(stages[t - k], k) for k in range(n_part) if 0 <= t - k < len(stages)]
            for fn, k in sorted(todo, key=lambda fk: fk[0] not in matmul_stages):
                fn(k)

    @pl.when(i >= nu_ref[0])
    def _():
        y_ref[...] = jnp.zeros(y_ref.shape, F32)


def _experts(xs, block_e, n_used, wg, wu, wd, rows):
    nblk = xs.shape[0] // (rows * SUBLANES)
    clamp = lambda i, nu: jnp.minimum(i, nu[0] - 1)
    grid_spec = pltpu.PrefetchScalarGridSpec(
        num_scalar_prefetch=2,
        grid=(nblk,),
        in_specs=[
            pl.BlockSpec((rows * SUBLANES, LANES), lambda i, be, nu: (clamp(i, nu), 0)),
            pl.BlockSpec((1, D_MODEL, D_EXPERT), lambda i, be, nu: (be[clamp(i, nu)], 0, 0)),
            pl.BlockSpec((1, D_MODEL, D_EXPERT), lambda i, be, nu: (be[clamp(i, nu)], 0, 0)),
            pl.BlockSpec((1, D_EXPERT, D_MODEL), lambda i, be, nu: (be[clamp(i, nu)], 0, 0)),
        ],
        out_specs=pl.BlockSpec((rows * SUBLANES, LANES), lambda i, be, nu: (i, 0)),
        scratch_shapes=[
            pltpu.VMEM((MOE_PARTS, rows // MOE_PARTS, D_MODEL), BF16),
            pltpu.VMEM((MOE_PARTS, rows // MOE_PARTS, D_EXPERT), F32),
            pltpu.VMEM((MOE_PARTS, rows // MOE_PARTS, D_EXPERT), F32),
            pltpu.VMEM((MOE_PARTS, rows // MOE_PARTS, D_EXPERT), BF16),
            pltpu.VMEM((D_MODEL, D_EXPERT), BF16),
            pltpu.VMEM((D_MODEL, D_EXPERT), BF16),
            pltpu.VMEM((D_EXPERT, D_MODEL), BF16),
        ],
    )
    return pl.pallas_call(
        functools.partial(_experts_body, rows=rows),
        grid_spec=grid_spec,
        out_shape=jax.ShapeDtypeStruct(xs.shape, F32),
        compiler_params=_cp(("arbitrary",)),
        name="moe_experts",
    )(block_e, n_used, xs, wg, wu, wd)


def _combine_body(dest_ref, destn_ref, x_ref, ri_ref, g3_ref, b3_ref, ys_ref, o_ref,
                  b00, b01, b10, b11, sem0, sem1, *, tt, nt, g, alpha):
    i = pl.program_id(0)
    bufs = ((b00, b01), (b10, b11))
    sems = (sem0, sem1)
    rs = min(tt, LN_ROW_SLAB)
    unroll = min(tt, GATHER_UNROLL)

    def gather_rows(dref, slot):
        def copy(t, kk):
            d = dref[0, 0, _slot_index(t, kk, g, tt)]
            return pltpu.make_async_copy(_row_slice(ys_ref, d), _row_slice(bufs[slot][kk], t), sems[slot])

        def body(j, carry):
            for u in range(unroll):
                for kk in range(2):
                    copy(j * unroll + u, kk).start(priority=kk)
            return carry

        lax.fori_loop(0, tt // unroll, body, 0)

    def wait_rows(slot):
        for kk in range(2):
            pltpu.make_async_copy(ys_ref.at[pl.ds(0, tt * SUBLANES), :], bufs[slot][kk], sems[slot]).wait()

    def run(slot):
        if slot == 0:
            @pl.when(i == 0)
            def _():
                gather_rows(dest_ref, 0)

        @pl.when(i + 1 < nt)
        def _():
            gather_rows(destn_ref, 1 - slot)

        wait_rows(slot)
        for r in range(tt // rs):
            rows = slice(r * rs, (r + 1) * rs)
            base = r * rs * SUBLANES
            y0 = jnp.concatenate([bufs[slot][0][pl.ds(base + c, rs, stride=SUBLANES), :]
                                  for c in range(ROW_CHUNKS)], axis=-1)
            y1 = jnp.concatenate([bufs[slot][1][pl.ds(base + c, rs, stride=SUBLANES), :]
                                  for c in range(ROW_CHUNKS)], axis=-1)
            ri = ri_ref[rows, :]
            moe = y0 * ri[:, 4:5] + y1 * ri[:, 5:6]
            o_ref[rows, :] = _layer_norm(alpha * x_ref[rows, :] + moe, g3_ref[...], b3_ref[...])

    @pl.when(i % 2 == 0)
    def _():
        run(0)

    @pl.when(i % 2 == 1)
    def _():
        run(1)


def _combine(x2d, rinfo, dest, g, ys, g3, b3, alpha):
    n = x2d.shape[0]
    tt = _pick_tile(n, 256)
    nt = n // tt
    dest3 = dest.reshape(nt, 1, 2 * tt)
    vec = pl.BlockSpec((1, D_MODEL), lambda i: (0, 0))
    stage = pltpu.VMEM((tt * SUBLANES, LANES), F32)
    return pl.pallas_call(
        functools.partial(_combine_body, tt=tt, nt=nt, g=g, alpha=alpha),
        grid=(nt,),
        in_specs=[
            pl.BlockSpec((1, 1, 2 * tt), lambda i: (i, 0, 0), memory_space=pltpu.SMEM),
            pl.BlockSpec((1, 1, 2 * tt), lambda i: (jnp.minimum(i + 1, nt - 1), 0, 0), memory_space=pltpu.SMEM),
            pl.BlockSpec((tt, D_MODEL), lambda i: (i, 0)),
            pl.BlockSpec((tt, LANES), lambda i: (i, 0)),
            vec, vec,
            pl.BlockSpec(memory_space=pl.ANY),
        ],
        out_specs=pl.BlockSpec((tt, D_MODEL), lambda i: (i, 0)),
        out_shape=jax.ShapeDtypeStruct((n, D_MODEL), F32),
        scratch_shapes=[stage, stage, stage, stage, pltpu.SemaphoreType.DMA, pltpu.SemaphoreType.DMA],
        compiler_params=_cp(("arbitrary",)),
        name="moe_combine",
    )(dest3, dest3, x2d, rinfo, g3, b3, ys)


def _rope_tables(pos):
    half = QK_ROPE // 2
    inv = ROPE_THETA ** (-jnp.arange(half, dtype=F32) / half)
    ang = pos.astype(F32)[:, None] * inv[None, :]
    cos, sin = jnp.cos(ang), jnp.sin(ang)
    n = pos.shape[0]
    pad_r = LANES - ROPE_LANE0 - QK_ROPE
    cos_t = jnp.concatenate([jnp.ones((n, ROPE_LANE0), F32), cos, cos, jnp.zeros((n, pad_r), F32)], -1)
    sin_t = jnp.concatenate([jnp.zeros((n, ROPE_LANE0), F32), sin, sin, jnp.zeros((n, pad_r), F32)], -1)
    return cos_t, sin_t


def _swap_neg(wr):
    half = QK_ROPE // 2
    return jnp.concatenate([-wr[:, half:], wr[:, :half]], axis=1)


def _prep_weights(l, w_in, q_norm_g, kv_norm_g, w_uq, w_ukv, conv_w, w_out, ln1_g, ln1_b, w_xq, w_xk, w_xv, w_xo,
                  ln2_g, ln2_b, w_router_group, b_router_group, w_router_expert, b_router_expert,
                  w_exp_gate, w_exp_up, w_exp_down, ln3_g, ln3_b):
    wi = w_in[l]
    c0 = Q_LORA + KV_LORA
    w_kr = wi[:, c0:c0 + QK_ROPE]
    zl = jnp.zeros((D_MODEL, ROPE_LANE0), F32)
    w_in_p = jnp.concatenate([wi[:, :c0], wi[:, c0 + QK_ROPE:], zl, w_kr, _swap_neg(w_kr)], axis=1)
    wq = w_uq[l].reshape(Q_LORA, N_HEADS, QK_NOPE + QK_ROPE)
    wq_rot = jnp.concatenate([-wq[:, :, QK_NOPE + QK_ROPE // 2:], wq[:, :, QK_NOPE:QK_NOPE + QK_ROPE // 2]], axis=2)
    wq_a = jnp.concatenate([wq, wq_rot], axis=2).reshape(Q_LORA, N_HEADS * HEAD_PAD)
    wkv = w_ukv[l].reshape(KV_LORA, N_HEADS, QK_NOPE + V_HEAD)
    wk_p = jnp.concatenate([wkv[:, :, :QK_NOPE], jnp.zeros((KV_LORA, N_HEADS, HEAD_PAD - QK_NOPE), F32)], axis=2)
    wk_p = wk_p.reshape(KV_LORA, N_HEADS * HEAD_PAD)
    wv_p = jnp.concatenate([wkv[:, :, QK_NOPE:], jnp.zeros((KV_LORA, N_HEADS, HEAD_PAD - V_HEAD), F32)], axis=2)
    wv_p = wv_p.reshape(KV_LORA, N_HEADS * HEAD_PAD)
    w_router = jnp.concatenate([w_router_group[l], w_router_expert[l],
                                jnp.zeros((D_MODEL, LANES - N_GROUPS - N_EXPERTS), F32)], axis=1)
    b_router = jnp.concatenate([b_router_group[l], b_router_expert[l].reshape(-1),
                                jnp.zeros((LANES - N_GROUPS - N_EXPERTS,), F32)]).reshape(1, LANES)
    return dict(
        w_in=w_in_p.astype(BF16),
        q_norm_g=q_norm_g[l].reshape(1, Q_LORA), kv_norm_g=kv_norm_g[l].reshape(1, KV_LORA),
        w_uq=wq_a.astype(BF16),
        w_ukv=jnp.concatenate([wk_p, wv_p], axis=1).astype(BF16),
        conv_w=conv_w[l],
        w_out=w_out[l].astype(BF16), ln1_g=ln1_g[l].reshape(1, -1), ln1_b=ln1_b[l].reshape(1, -1),
        w_xq=w_xq[l].astype(BF16), w_xk=w_xk[l].astype(BF16), w_xv=w_xv[l].astype(BF16),
        w_xo=w_xo[l].astype(BF16), ln2_g=ln2_g[l].reshape(1, -1), ln2_b=ln2_b[l].reshape(1, -1),
        w_router=w_router.astype(BF16), b_router=b_router,
        w_exp_gate=w_exp_gate[l], w_exp_up=w_exp_up[l], w_exp_down=w_exp_down[l],
        ln3_g=ln3_g[l].reshape(1, -1), ln3_b=ln3_b[l].reshape(1, -1),
    )


def _slots(rt, pstarts):
    g = rt.shape[-1]
    rt = rt.reshape(-1, SUBLANES, g)
    dest = pstarts[rt[:, 0:2, :].astype(I32)] + rt[:, 2:4, :].astype(I32)
    return dest.reshape(-1), g


def _layer(l, depth, xp, xs, lat_past, kr_past, conv_past, mk_s, mv_s, mem_prompt, w):
    alpha = (2 * depth) ** 0.25
    b, s, _ = xp.shape
    bs, ss, _ = xs.shape
    past = lat_past.shape[1]

    cos_p, sin_p = _rope_tables(jnp.arange(s))
    q_p, k_p, v_p, yc_p, lat_p, kr_p, cst_p = _inproj(
        xp, jnp.zeros((b, CONV_K - 1, CONV_WIDTH), F32), cos_p, sin_p, w)
    attn_p = _attention(q_p, k_p, v_p, 0, None)
    mk, mv, mk_b, mv_b = _memkv(mem_prompt.reshape(b * N_MEM, D_MODEL), w['w_xk'], w['w_xv'])
    cnt0 = jnp.zeros((1, LANES), F32)
    x2_p, ri_p, cnt_p, rt_p = _mid(xp, attn_p, yc_p, mk_b.reshape(b, N_MEM, D_MODEL), mv_b.reshape(b, N_MEM, D_MODEL),
                             cnt0, w, alpha)

    cos_s, sin_s = _rope_tables(past + jnp.arange(ss))
    q_s, _, _, yc_s, lat_s, kr_s, cst_s = _inproj(xs, conv_past, cos_s, sin_s, w)
    n_keys = past + ss
    sk = -(-n_keys // LANES) * LANES
    lat_all = jnp.concatenate([lat_past, lat_s, jnp.zeros((bs, sk - n_keys, KV_LORA), F32)], axis=1)
    kr_all = jnp.concatenate([kr_past, kr_s, jnp.zeros((bs, sk - n_keys, QK_ROPE), F32)], axis=1)
    kr_all = jnp.pad(kr_all, ((0, 0), (0, 0), (ROPE_LANE0, LANES - ROPE_LANE0 - QK_ROPE)))
    attn_s = _decode_attention(q_s, lat_all, kr_all, w['w_ukv'], n_keys)
    mk_sb = mk_s.reshape(bs, N_MEM, D_MODEL).astype(BF16)
    mv_sb = mv_s.reshape(bs, N_MEM, D_MODEL).astype(BF16)
    x2_s, ri_s, cnt, rt_s = _mid(xs, attn_s, yc_s, mk_sb, mv_sb, cnt_p, w, alpha)

    n_p, n_s = b * s, bs * ss
    counts = cnt[0, ROUTER_LANE0:ROUTER_LANE0 + N_EXPERTS].astype(I32)
    padded = (counts + MOE_ROWS - 1) // MOE_ROWS * MOE_ROWS
    pends = jnp.cumsum(padded)
    pstarts = pends - padded
    nblk = -(-2 * (n_p + n_s) // MOE_ROWS) + N_EXPERTS
    blk_start = jnp.arange(nblk, dtype=I32) * MOE_ROWS
    block_e = jnp.minimum(jnp.sum((pends[None, :] <= blk_start[:, None]).astype(I32), axis=1), N_EXPERTS - 1)
    n_used = (pends[-1] // MOE_ROWS).astype(I32).reshape(1)
    ri_p2, ri_s2 = ri_p.reshape(n_p, LANES), ri_s.reshape(n_s, LANES)
    dest_p, g_p = _slots(rt_p, pstarts)
    dest_s, g_s = _slots(rt_s, pstarts)
    x2_p2, x2_s2 = x2_p.reshape(n_p, D_MODEL), x2_s.reshape(n_s, D_MODEL)
    pad_info = jnp.concatenate([pstarts + counts, padded - counts, n_used]).astype(I32)
    slots = _dispatch(x2_p2, dest_p, g_p, x2_s2, dest_s, g_s, pad_info, nblk, MOE_ROWS)
    ys = _experts(slots, block_e, n_used, w['w_exp_gate'], w['w_exp_up'], w['w_exp_down'], MOE_ROWS)
    y_p = _combine(x2_p2, ri_p2, dest_p, g_p, ys, w['ln3_g'], w['ln3_b'], alpha).reshape(b, s, D_MODEL)
    y_s = _combine(x2_s2, ri_s2, dest_s, g_s, ys, w['ln3_g'], w['ln3_b'], alpha).reshape(bs, ss, D_MODEL)
    return (y_p, y_s, lat_p, kr_p, cst_p, mk.reshape(b, N_MEM, X_HEADS, X_HEAD_DIM),
            mv.reshape(b, N_MEM, X_HEADS, X_HEAD_DIM), lat_s, kr_s, cst_s)


def kernel(x_prompt, x_sample, cache_kv_latent, cache_k_rope, cache_conv, cache_mem_k, cache_mem_v, mem_prompt,
           w_in, q_norm_g, kv_norm_g, w_uq, w_ukv, conv_w, w_out, ln1_g, ln1_b, w_xq, w_xk, w_xv, w_xo, ln2_g,
           ln2_b, w_router_group, b_router_group, w_router_expert, b_router_expert, w_exp_gate, w_exp_up,
           w_exp_down, ln3_g, ln3_b):
    depth = w_in.shape[0]
    xp, xs = x_prompt, x_sample
    outs = [[] for _ in range(8)]
    for l in range(depth):
        w = _prep_weights(l, w_in, q_norm_g, kv_norm_g, w_uq, w_ukv, conv_w, w_out, ln1_g, ln1_b, w_xq, w_xk, w_xv,
                          w_xo, ln2_g, ln2_b, w_router_group, b_router_group, w_router_expert, b_router_expert,
                          w_exp_gate, w_exp_up, w_exp_down, ln3_g, ln3_b)
        res = _layer(l, depth, xp, xs, cache_kv_latent[l], cache_k_rope[l], cache_conv[l], cache_mem_k[l],
                     cache_mem_v[l], mem_prompt, w)
        xp, xs = res[0], res[1]
        for acc, r in zip(outs, res[2:]):
            acc.append(r)
    return (xp, xs) + tuple(jnp.stack(o) for o in outs)
```

```python
import functools
import math

import numpy as np
import jax
import jax.numpy as jnp
from jax import lax
from jax.experimental import pallas as pl
from jax.experimental.pallas import tpu as pltpu

F32 = jnp.float32
BF16 = jnp.bfloat16
I32 = jnp.int32

D_MODEL = 1024
CHUNK = 64
N_HEADS = 8
QK_NOPE = 64
QK_ROPE = 32
V_HEAD = 64
Q_LORA = 256
KV_LORA = 128
ROPE_THETA = 10000.0
MLA_SCALE = (QK_NOPE + QK_ROPE) ** -0.5
Q_PRESCALE = MLA_SCALE * math.log2(math.e)
CONV_WIDTH = 512
CONV_K = 3
N_MEM = 256
X_HEADS = 4
X_HEAD_DIM = D_MODEL // X_HEADS
X_SCALE = X_HEAD_DIM ** -0.5
N_GROUPS = 4
EXPERTS_PER_GROUP = 8
N_EXPERTS = N_GROUPS * EXPERTS_PER_GROUP
D_EXPERT = 512
LN_EPS = 1e-5
RMS_EPS = 1e-6

LANES = 128
SUBLANES = 8
ROW_CHUNKS = D_MODEL // LANES
HEAD_PAD = LANES
ROPE_LANE0 = QK_NOPE
ROUTER_LANE0 = N_GROUPS
VMEM_LIMIT = 56 * 1024 * 1024
MOE_ROWS = 512
MOE_PARTS = 2
ACT_ROW_SLAB = 32
GATHER_UNROLL = 8
WCAST_ROWS = 64
ATTN_TQ = 1024
ATTN_TK = 512
ATTN_SUB = 2
ATTN_ROW_SLAB = 64
LN_ROW_SLAB = 16
XATTN_ROW_SLAB = 64
INPROJ_ROWS = 512
ROUTE_GROUP = 256
MID_SKEW = 1
MID_ROWS = 512
MID_PARTS = 2


def _cp(sem, vmem=VMEM_LIMIT):
    return pltpu.CompilerParams(dimension_semantics=sem, vmem_limit_bytes=vmem)


def _pick_tile(n, pref):
    t = min(n, pref)
    while n % t:
        t //= 2
    return t


def _with_ones_lane(v):
    lane = lax.broadcasted_iota(I32, v.shape, 1)
    return jnp.where((lane & (HEAD_PAD - 1)) == V_HEAD, 1.0, v)


def _inproj_body(x_ref, cinit_ref, cos_ref, sin_ref, win_ref, qg_ref, kvg_ref, wuq_ref, wukv_ref, cw_ref,
                 q_ref, k_ref, v_ref, yc_ref, lat_ref, kr_ref, cst_ref, u_scr, *, tt, nj):
    j = pl.program_id(1)

    @pl.when(j == 0)
    def _():
        u_scr[6:8, :] = cinit_ref[0]

    x = x_ref[0].astype(BF16)
    proj = jnp.dot(x, win_ref[...], preferred_element_type=F32)
    cq = proj[:, 0:256]
    ckv = proj[:, 256:384]
    gb = proj[:, 384:896]
    gc = proj[:, 896:1408]
    gv = proj[:, 1408:1920]
    kr_blk = proj[:, 1920:2048]
    cos_t = cos_ref[...]
    sin_t = sin_ref[...]

    def rotate(blk):
        return blk * cos_t + pltpu.roll(blk, LANES - QK_ROPE, 1) * sin_t

    cqn = cq * lax.rsqrt(jnp.mean(cq * cq, -1, keepdims=True) + RMS_EPS) * qg_ref[...]
    ckvn = ckv * lax.rsqrt(jnp.mean(ckv * ckv, -1, keepdims=True) + RMS_EPS) * kvg_ref[...]
    lat_ref[0] = ckvn
    kr_p = rotate(kr_blk)
    kr_ref[0] = kr_p[:, ROPE_LANE0:ROPE_LANE0 + QK_ROPE]
    qq = jnp.dot(cqn.astype(BF16), wuq_ref[...], preferred_element_type=F32)
    kv = jnp.dot(ckvn.astype(BF16), wukv_ref[...], preferred_element_type=F32)
    hw = N_HEADS * HEAD_PAD
    for h in range(N_HEADS):
        sl = slice(h * HEAD_PAD, (h + 1) * HEAD_PAD)
        q_ref[0, :, sl] = (rotate(qq[:, sl]) * Q_PRESCALE).astype(BF16)
        k_ref[0, :, sl] = (kv[:, sl] + kr_p).astype(BF16)
    v_ref[0] = _with_ones_lane(kv[:, hw:2 * hw]).astype(BF16)
    u = gc * gv
    u_scr[8:8 + tt, :] = u
    conv = cw_ref[0:1, :] * u_scr[6:6 + tt, :] + cw_ref[1:2, :] * u_scr[7:7 + tt, :] + cw_ref[2:3, :] * u
    yc_ref[0] = (gb * conv).astype(BF16)
    last2 = u_scr[tt + 6:tt + 8, :]
    u_scr[6:8, :] = last2

    @pl.when(j == nj - 1)
    def _():
        cst_ref[0] = last2


def _inproj(x, conv_init, cos_t, sin_t, w):
    b, s, _ = x.shape
    tt = _pick_tile(s, INPROJ_ROWS)
    nj = s // tt
    wn = w['w_in'].shape[1]
    full = lambda shape: pl.BlockSpec(shape, lambda bi, ji: (0,) * len(shape))
    out_shapes = (
        jax.ShapeDtypeStruct((b, s, N_HEADS * HEAD_PAD), BF16),
        jax.ShapeDtypeStruct((b, s, N_HEADS * HEAD_PAD), BF16),
        jax.ShapeDtypeStruct((b, s, N_HEADS * HEAD_PAD), BF16),
        jax.ShapeDtypeStruct((b, s, CONV_WIDTH), BF16),
        jax.ShapeDtypeStruct((b, s, KV_LORA), F32),
        jax.ShapeDtypeStruct((b, s, QK_ROPE), F32),
        jax.ShapeDtypeStruct((b, CONV_K - 1, CONV_WIDTH), F32),
    )
    row = lambda width: pl.BlockSpec((1, tt, width), lambda bi, ji: (bi, ji, 0))
    return pl.pallas_call(
        functools.partial(_inproj_body, tt=tt, nj=nj),
        grid=(b, nj),
        in_specs=[
            row(D_MODEL),
            pl.BlockSpec((1, CONV_K - 1, CONV_WIDTH), lambda bi, ji: (bi, 0, 0)),
            pl.BlockSpec((tt, LANES), lambda bi, ji: (ji, 0)),
            pl.BlockSpec((tt, LANES), lambda bi, ji: (ji, 0)),
            full((D_MODEL, wn)),
            full((1, Q_LORA)),
            full((1, KV_LORA)),
            full(w['w_uq'].shape),
            full(w['w_ukv'].shape),
            full((CONV_K, CONV_WIDTH)),
        ],
        out_specs=(
            row(N_HEADS * HEAD_PAD), row(N_HEADS * HEAD_PAD), row(N_HEADS * HEAD_PAD), row(CONV_WIDTH),
            row(KV_LORA), row(QK_ROPE),
            pl.BlockSpec((1, CONV_K - 1, CONV_WIDTH), lambda bi, ji: (bi, 0, 0)),
        ),
        out_shape=out_shapes,
        scratch_shapes=[pltpu.VMEM((tt + 8, CONV_WIDTH), F32)],
        compiler_params=_cp(("parallel", "arbitrary")),
        name="inproj",
    )(x, conv_init, cos_t, sin_t, w['w_in'], w['q_norm_g'], w['kv_norm_g'], w['w_uq'], w['w_ukv'], w['conv_w'])


def _attn_body(qi_ref, ki_ref, fl_ref, q_ref, k_ref, v_ref, o_ref, m_scr, acc_scr, s_scr, p_scr, a_scr,
               *, tq, tk, n_sub, rs, q_pos0, n_valid, combos):
    step = pl.program_id(1)
    qi = qi_ref[step]
    ki = ki_ref[step]
    flags = fl_ref[step]
    reps = tk // LANES
    th = tq // n_sub

    @pl.when((flags & 1) != 0)
    def _():
        m_scr[...] = jnp.full(m_scr.shape, -jnp.inf, F32)
        acc_scr[...] = jnp.zeros(acc_scr.shape, F32)

    def scores(item, buf):
        sub, h, _ = item
        hs = slice(h * HEAD_PAD, (h + 1) * HEAD_PAD)
        s_scr[buf] = lax.dot_general(q_ref[0, sub * th:(sub + 1) * th, hs], k_ref[0, :, hs],
                                     (((1,), (1,)), ((), ())), preferred_element_type=F32)

    def softmax_pv(item, buf):
        sub, h, masked = item
        hs = slice(h * HEAD_PAD, (h + 1) * HEAD_PAD)
        for r in range(th // rs):
            rows = slice(r * rs, (r + 1) * rs)
            arows = slice(sub * th + r * rs, sub * th + (r + 1) * rs)
            s_r = s_scr[buf, rows, :]
            if masked:
                qpos = q_pos0 + qi * tq + sub * th + r * rs + lax.broadcasted_iota(I32, (rs, tk), 0)
                kpos = ki * tk + lax.broadcasted_iota(I32, (rs, tk), 1)
                mask = (kpos >> 6) <= (qpos >> 6)
                if n_valid is not None:
                    mask = mask & (kpos < n_valid)
                s_r = jnp.where(mask, s_r, -jnp.inf)
            m_old = m_scr[h, arows, :]
            m_new = jnp.maximum(m_old, jnp.max(s_r, axis=-1, keepdims=True))
            a_scr[buf, rows, :] = jnp.exp2(m_old - m_new)
            m_rep = jnp.concatenate([m_new] * reps, axis=1)
            p_scr[buf, rows, :] = jnp.exp2(s_r - m_rep).astype(BF16)
            m_scr[h, arows, :] = m_new
        pv = jnp.dot(p_scr[buf], v_ref[0, :, hs], preferred_element_type=F32)
        srows = slice(sub * th, (sub + 1) * th)
        acc_scr[h, srows, :] = a_scr[buf] * acc_scr[h, srows, :] + pv

    def run(modes):
        items = [(sub, h, mode == 2) for sub, mode in enumerate(modes) if mode != 0 for h in range(N_HEADS)]
        scores(items[0], 0)
        for n, item in enumerate(items):
            if n + 1 < len(items):
                scores(items[n + 1], (n + 1) % 2)
            softmax_pv(item, n % 2)

    for code, modes in combos:
        @pl.when((flags >> 2) == code)
        def _(modes=modes):
            run(modes)

    @pl.when((flags & 2) != 0)
    def _():
        for h in range(N_HEADS):
            acc = acc_scr[h]
            o_ref[0, :, h * V_HEAD:(h + 1) * V_HEAD] = (acc[:, 0:V_HEAD] / acc[:, V_HEAD:V_HEAD + 1]).astype(BF16)


def _attn_tables(nq, nk, tq, tk, n_sub, q_pos0, n_valid):
    th = tq // n_sub
    qi_l, ki_l, fl_l, combos = [], [], [], {}
    for qi in range(nq):
        sub_lo = [q_pos0 + qi * tq + j * th for j in range(n_sub)]
        sub_last = []
        for lo in sub_lo:
            last_pos = ((lo + th - 1) // CHUNK) * CHUNK + CHUNK - 1
            if n_valid is not None:
                last_pos = min(last_pos, n_valid - 1)
            sub_last.append(min(nk - 1, last_pos // tk))
        k_last = max(sub_last)
        for ki in range(k_last + 1):
            k_hi = ki * tk + tk - 1
            modes = []
            for lo, last in zip(sub_lo, sub_last):
                if ki > last:
                    modes.append(0)
                elif (k_hi // CHUNK) > (lo // CHUNK) or (n_valid is not None and k_hi >= n_valid):
                    modes.append(2)
                else:
                    modes.append(1)
            code = sum(m * 3 ** j for j, m in enumerate(modes))
            combos[code] = tuple(modes)
            qi_l.append(qi); ki_l.append(ki)
            fl_l.append((1 if ki == 0 else 0) | (2 if ki == k_last else 0) | (code << 2))
    to_arr = lambda vals: jnp.asarray(np.array(vals, np.int32))
    return to_arr(qi_l), to_arr(ki_l), to_arr(fl_l), tuple(sorted(combos.items()))


def _attention(q, k, v, q_pos0, n_valid):
    b, sq, _ = q.shape
    sk = k.shape[1]
    tq = _pick_tile(sq, ATTN_TQ)
    tk = _pick_tile(sk, ATTN_TK)
    n_sub = ATTN_SUB if tq % (ATTN_SUB * 2 * SUBLANES) == 0 else 1
    th = tq // n_sub
    qi_t, ki_t, fl_t, combos = _attn_tables(sq // tq, sk // tk, tq, tk, n_sub, q_pos0, n_valid)
    n_steps = int(qi_t.shape[0])
    grid_spec = pltpu.PrefetchScalarGridSpec(
        num_scalar_prefetch=3,
        grid=(b, n_steps),
        in_specs=[
            pl.BlockSpec((1, tq, N_HEADS * HEAD_PAD), lambda bi, si, qt, kt, ft: (bi, qt[si], 0)),
            pl.BlockSpec((1, tk, N_HEADS * HEAD_PAD), lambda bi, si, qt, kt, ft: (bi, kt[si], 0)),
            pl.BlockSpec((1, tk, N_HEADS * HEAD_PAD), lambda bi, si, qt, kt, ft: (bi, kt[si], 0)),
        ],
        out_specs=pl.BlockSpec((1, tq, N_HEADS * V_HEAD), lambda bi, si, qt, kt, ft: (bi, qt[si], 0)),
        scratch_shapes=[
            pltpu.VMEM((N_HEADS, tq, LANES), F32),
            pltpu.VMEM((N_HEADS, tq, LANES), F32),
            pltpu.VMEM((2, th, tk), F32),
            pltpu.VMEM((2, th, tk), BF16),
            pltpu.VMEM((2, th, LANES), F32),
        ],
    )
    return pl.pallas_call(
        functools.partial(_attn_body, tq=tq, tk=tk, n_sub=n_sub, rs=min(th, ATTN_ROW_SLAB), q_pos0=q_pos0,
                          n_valid=n_valid, combos=combos),
        grid_spec=grid_spec,
        out_shape=jax.ShapeDtypeStruct((b, sq, N_HEADS * V_HEAD), BF16),
        compiler_params=_cp(("parallel", "arbitrary")),
        name="mla_attn",
    )(qi_t, ki_t, fl_t, q, k, v)


def _decode_attn_body(q_ref, lat_ref, krp_ref, wukv_ref, o_ref, qa_scr, qr_scr, *, ss, n_valid):
    hw = N_HEADS * HEAD_PAD
    latb = lat_ref[0].astype(BF16)
    krb = krp_ref[0].astype(BF16)
    contract_last = (((1,), (1,)), ((), ()))
    for h in range(N_HEADS):
        hs = slice(h * HEAD_PAD, (h + 1) * HEAD_PAD)
        rows = slice(h * ss, (h + 1) * ss)
        q_h = q_ref[0, :, hs]
        qa_scr[rows, :] = lax.dot_general(q_h, wukv_ref[:, hs], contract_last,
                                          preferred_element_type=F32).astype(BF16)
        qr_scr[rows, :] = q_h
    s = (lax.dot_general(qa_scr[...], latb, contract_last, preferred_element_type=F32)
         + lax.dot_general(qr_scr[...], krb, contract_last, preferred_element_type=F32))
    kpos = lax.broadcasted_iota(I32, s.shape, 1)
    s = jnp.where(kpos < n_valid, s, -jnp.inf)
    p = jnp.exp2(s - jnp.max(s, -1, keepdims=True))
    ol = jnp.dot(p.astype(BF16), latb, preferred_element_type=F32) / jnp.sum(p, -1, keepdims=True)
    for h in range(N_HEADS):
        rows = slice(h * ss, (h + 1) * ss)
        o_h = jnp.dot(ol[rows, :].astype(BF16), wukv_ref[:, hw + h * HEAD_PAD:hw + (h + 1) * HEAD_PAD],
                      preferred_element_type=F32)
        o_ref[0, :, h * V_HEAD:(h + 1) * V_HEAD] = o_h[:, 0:V_HEAD].astype(BF16)


def _decode_attention(q, lat_all, kr_padded, w_ukv, n_valid):
    b, ss, _ = q.shape
    sk = lat_all.shape[1]
    return pl.pallas_call(
        functools.partial(_decode_attn_body, ss=ss, n_valid=n_valid),
        grid=(b,),
        in_specs=[pl.BlockSpec((1, ss, N_HEADS * HEAD_PAD), lambda bi: (bi, 0, 0)),
                  pl.BlockSpec((1, sk, KV_LORA), lambda bi: (bi, 0, 0)),
                  pl.BlockSpec((1, sk, LANES), lambda bi: (bi, 0, 0)),
                  pl.BlockSpec(w_ukv.shape, lambda bi: (0, 0))],
        out_specs=pl.BlockSpec((1, ss, N_HEADS * V_HEAD), lambda bi: (bi, 0, 0)),
        out_shape=jax.ShapeDtypeStruct((b, ss, N_HEADS * V_HEAD), BF16),
        scratch_shapes=[pltpu.VMEM((N_HEADS * ss, KV_LORA), BF16), pltpu.VMEM((N_HEADS * ss, HEAD_PAD), BF16)],
        compiler_params=_cp(("parallel",)),
        name="decode_attn",
    )(q, lat_all, kr_padded, w_ukv)


def _memkv_body(mem_ref, wk_ref, wv_ref, mk_ref, mv_ref, mkb_ref, mvb_ref):
    m = mem_ref[...].astype(BF16)
    mk = jnp.dot(m, wk_ref[...], preferred_element_type=F32)
    mv = jnp.dot(m, wv_ref[...], preferred_element_type=F32)
    mk_ref[...] = mk
    mv_ref[...] = mv
    mkb_ref[...] = mk.astype(BF16)
    mvb_ref[...] = mv.astype(BF16)


def _memkv(mem2d, w_xk, w_xv):
    n = mem2d.shape[0]
    tt = _pick_tile(n, 256)
    row = pl.BlockSpec((tt, D_MODEL), lambda i: (i, 0))
    wspec = pl.BlockSpec((D_MODEL, D_MODEL), lambda i: (0, 0))
    return pl.pallas_call(
        _memkv_body,
        grid=(n // tt,),
        in_specs=[row, wspec, wspec],
        out_specs=(row, row, row, row),
        out_shape=(jax.ShapeDtypeStruct((n, D_MODEL), F32), jax.ShapeDtypeStruct((n, D_MODEL), F32),
                   jax.ShapeDtypeStruct((n, D_MODEL), BF16), jax.ShapeDtypeStruct((n, D_MODEL), BF16)),
        compiler_params=_cp(("parallel",)),
        name="memkv",
    )(mem2d, w_xk, w_xv)


def _layer_norm(x, g, b):
    mu = jnp.mean(x, -1, keepdims=True)
    xc = x - mu
    var = jnp.mean(xc * xc, -1, keepdims=True)
    return xc * lax.rsqrt(var + LN_EPS) * g + b


def _mid_body(x_ref, at_ref, yc_ref, mk_ref, mv_ref, cnt0_ref, low_ref, wo_ref, g1_ref, b1_ref, wq_ref, wxo_ref,
              g2_ref, b2_ref, wr_ref, br_ref, x2_ref, ri_ref, cnt_ref, rt_ref,
              cnt_scr, a_scr, xb_scr, q_scr, sc_scr, p_scr, o_scr, lg_scr, *, tt, alpha):
    step = pl.program_id(0)

    @pl.when(step == 0)
    def _():
        cnt_scr[...] = cnt0_ref[...]
        lg_scr[...] = jnp.zeros(lg_scr.shape, F32)

    cur_slot = step % 2
    prev_slot = 1 - cur_slot
    routed = (step > 0).astype(F32)

    n_part = MID_PARTS if tt % (MID_PARTS * SUBLANES * 2) == 0 else 1
    pr = tt // n_part
    ln_rs = min(pr, LN_ROW_SLAB)
    sm_rs = min(pr, XATTN_ROW_SLAB)
    c_exp = X_SCALE * math.log2(math.e)
    lane = lax.broadcasted_iota(I32, (pr, LANES), 1)

    def part_rows(k):
        return slice(k * pr, (k + 1) * pr)

    def out_proj(k):
        rp = part_rows(k)
        mix = jnp.concatenate([at_ref[0, rp, :], yc_ref[0, rp, :]], axis=-1)
        a_scr[rp, :] = jnp.dot(mix, wo_ref[...], preferred_element_type=F32)

    def norm1(k):
        for r in range(pr // ln_rs):
            rows = slice(k * pr + r * ln_rs, k * pr + (r + 1) * ln_rs)
            x1 = _layer_norm(alpha * x_ref[0, rows, :] + a_scr[rows, :], g1_ref[...], b1_ref[...])
            x2_ref[0, rows, :] = x1
            xb_scr[rows, :] = x1.astype(BF16)

    def q_proj(k):
        rp = part_rows(k)
        q_scr[rp, :] = jnp.dot(xb_scr[rp, :], wq_ref[...], preferred_element_type=F32).astype(BF16)

    def cross_attn(k):
        rp = part_rows(k)
        for h in range(X_HEADS):
            sl = slice(h * X_HEAD_DIM, (h + 1) * X_HEAD_DIM)
            b2 = h % 2
            sc_scr[k, b2] = lax.dot_general(q_scr[rp, sl], mk_ref[0, :, sl], (((1,), (1,)), ((), ())),
                                            preferred_element_type=F32)
            for r in range(pr // sm_rs):
                rows = slice(r * sm_rs, (r + 1) * sm_rs)
                s_r = sc_scr[k, b2, rows, :]
                e = jnp.exp2((s_r - jnp.max(s_r, -1, keepdims=True)) * c_exp)
                p_scr[k, b2, rows, :] = (e / jnp.sum(e, -1, keepdims=True)).astype(BF16)
            o_scr[rp, sl] = jnp.dot(p_scr[k, b2], mv_ref[0, :, sl], preferred_element_type=F32).astype(BF16)

    def x_out_proj(k):
        rp = part_rows(k)
        a_scr[rp, :] = jnp.dot(o_scr[rp, :], wxo_ref[...], preferred_element_type=F32)

    def norm2(k):
        for r in range(pr // ln_rs):
            rows = slice(k * pr + r * ln_rs, k * pr + (r + 1) * ln_rs)
            x2 = _layer_norm(alpha * x2_ref[0, rows, :] + a_scr[rows, :], g2_ref[...], b2_ref[...])
            x2_ref[0, rows, :] = x2
            xb_scr[rows, :] = x2.astype(BF16)

    def router_logits(k):
        rp = part_rows(k)
        lg_scr[cur_slot, rp, :] = jnp.dot(xb_scr[rp, :], wr_ref[...], preferred_element_type=F32) + br_ref[...]

    def route(k):
        rp = part_rows(k)
        logits = lg_scr[prev_slot, rp, :]
        neg = -jnp.inf
        is_g = lane < N_GROUPS
        lg = jnp.where(is_g, logits, neg)
        mg = jnp.max(lg, -1, keepdims=True)
        g_idx = jnp.min(jnp.where(lg == mg, lane, LANES), -1, keepdims=True)
        pg = 1.0 / jnp.sum(jnp.where(is_g, jnp.exp(logits - mg), 0.0), -1, keepdims=True)
        in_grp = ((lane >= ROUTER_LANE0) & (lane < ROUTER_LANE0 + N_EXPERTS)
                  & (((lane - ROUTER_LANE0) >> 3) == g_idx))
        le = jnp.where(in_grp, logits, neg)
        v1 = jnp.max(le, -1, keepdims=True)
        i1 = jnp.min(jnp.where(le == v1, lane, LANES), -1, keepdims=True)
        le2 = jnp.where(lane == i1, neg, le)
        v2 = jnp.max(le2, -1, keepdims=True)
        i2 = jnp.min(jnp.where(le2 == v2, lane, LANES), -1, keepdims=True)
        e2 = jnp.exp(v2 - v1)
        den = 1.0 + e2
        gate1 = (1.0 / den) * pg
        gate2 = (e2 / den) * pg
        oh1 = (lane == i1).astype(F32)
        oh2 = (lane == i2).astype(F32)
        oh = oh1 + oh2
        base = cnt_scr[...] + jnp.dot(low_ref[...], oh.astype(BF16), preferred_element_type=F32)
        rank1 = jnp.sum(oh1 * base, -1, keepdims=True)
        rank2 = jnp.sum(oh2 * base, -1, keepdims=True)
        cnt_scr[...] = cnt_scr[...] + jnp.sum(oh, 0, keepdims=True) * routed
        e1f = (i1 - ROUTER_LANE0).astype(F32)
        e2f = (i2 - ROUTER_LANE0).astype(F32)
        ri = jnp.where(lane == 0, e1f, jnp.where(lane == 1, e2f, jnp.where(
            lane == 2, rank1, jnp.where(lane == 3, rank2, jnp.where(lane == 4, gate1, jnp.where(
                lane == 5, gate2, 0.0))))))
        ri_ref[0, rp, :] = ri
        ri_t = jnp.transpose(ri)[0:SUBLANES, :]
        grp = min(pr, ROUTE_GROUP)
        for j in range(pr // grp):
            rt_ref[0, k * (pr // grp) + j] = ri_t[:, j * grp:(j + 1) * grp]

    stages = (out_proj, norm1, q_proj, cross_attn, x_out_proj, norm2, router_logits)
    matmul_stages = (out_proj, q_proj, x_out_proj, router_logits)
    for t in range(len(stages) + MID_SKEW * (n_part - 1)):
        todo = [(stages[t - MID_SKEW * k], k) for k in range(n_part) if 0 <= t - MID_SKEW * k < len(stages)]
        for fn, k in sorted(todo, key=lambda fk: fk[0] not in matmul_stages):
            fn(k)
        if t < n_part:
            route(t)
    cnt_ref[...] = cnt_scr[...]


def _mid(x, attn, yconv, mk_b, mv_b, cnt0, w, alpha):
    b, s, _ = x.shape
    tt = _pick_tile(s, MID_ROWS)
    nj = s // tt
    n_tiles = b * nj
    cur = lambda i: jnp.minimum(i, n_tiles - 1)
    prev = lambda i: jnp.maximum(i - 1, 0)
    row = lambda width: pl.BlockSpec((1, tt, width), lambda i: (cur(i) // nj, cur(i) % nj, 0))
    row_prev = lambda width: pl.BlockSpec((1, tt, width), lambda i: (prev(i) // nj, prev(i) % nj, 0))
    full = lambda shape: pl.BlockSpec(shape, lambda i: (0,) * len(shape))
    mem = pl.BlockSpec((1, N_MEM, D_MODEL), lambda i: (cur(i) // nj, 0, 0))
    vec = full((1, D_MODEL))
    n_part = MID_PARTS if tt % (MID_PARTS * SUBLANES * 2) == 0 else 1
    pr = tt // n_part
    grp = min(pr, ROUTE_GROUP)
    lower = jnp.tril(jnp.ones((pr, pr), F32), -1).astype(BF16)
    return pl.pallas_call(
        functools.partial(_mid_body, tt=tt, alpha=alpha),
        grid=(n_tiles + 1,),
        in_specs=[row(D_MODEL), row(N_HEADS * V_HEAD), row(CONV_WIDTH), mem, mem, full((1, LANES)), full((pr, pr)),
                  full((D_MODEL, D_MODEL)), vec, vec, full((D_MODEL, D_MODEL)), full((D_MODEL, D_MODEL)),
                  vec, vec, full((D_MODEL, LANES)), full((1, LANES))],
        out_specs=(row(D_MODEL), row_prev(LANES), full((1, LANES)),
                   pl.BlockSpec((1, tt // grp, SUBLANES, grp), lambda i: (prev(i) // nj, prev(i) % nj, 0, 0))),
        out_shape=(jax.ShapeDtypeStruct((b, s, D_MODEL), F32), jax.ShapeDtypeStruct((b, s, LANES), F32),
                   jax.ShapeDtypeStruct((1, LANES), F32),
                   jax.ShapeDtypeStruct((b, s // grp, SUBLANES, grp), F32)),
        scratch_shapes=[
            pltpu.VMEM((1, LANES), F32),
            pltpu.VMEM((tt, D_MODEL), F32),
            pltpu.VMEM((tt, D_MODEL), BF16),
            pltpu.VMEM((tt, D_MODEL), BF16),
            pltpu.VMEM((n_part, 2, pr, N_MEM), F32),
            pltpu.VMEM((n_part, 2, pr, N_MEM), BF16),
            pltpu.VMEM((tt, D_MODEL), BF16),
            pltpu.VMEM((2, tt, LANES), F32),
        ],
        compiler_params=_cp(("arbitrary",)),
        name="mid",
    )(x, attn, yconv, mk_b, mv_b, cnt0, lower, w['w_out'], w['ln1_g'], w['ln1_b'], w['w_xq'], w['w_xo'],
      w['ln2_g'], w['ln2_b'], w['w_router'], w['b_router'])


def _row_slice(ref, row):
    return ref.at[pl.ds(pl.multiple_of(row * SUBLANES, SUBLANES), SUBLANES), :]


def _slot_index(t, kk, g, n):
    if g >= n:
        return kk * g + t
    shift = g.bit_length() - 1
    return ((t >> shift) << (shift + 1)) + kk * g + (t & (g - 1))


def _rows_slice(ref, row, n_rows):
    return ref.at[pl.ds(pl.multiple_of(row * SUBLANES, SUBLANES), n_rows * SUBLANES), :]


def _dispatch_body(pad_ref, dest_ref, dest_s_ref, x_ref, x_s_ref, xs_ref, buf0, buf1, zbuf, sem0, sem1, zsem,
                   *, tt, ts, nt, nblk, rows, g_p, g_s):
    i = pl.program_id(0)
    bufs = (buf0, buf1)
    sems = (sem0, sem1)
    len_bits = rows.bit_length() - 1

    def scatter_rows(buf, sem, src_ref, d_ref, n, g):
        for c in range(ROW_CHUNKS):
            buf[pl.ds(c, n, stride=SUBLANES), :] = src_ref[:, c * LANES:(c + 1) * LANES]
        unroll = min(n, GATHER_UNROLL)

        def issue(j, carry):
            for u in range(unroll):
                t = j * unroll + u
                src = _row_slice(buf, t)
                for kk in range(2):
                    d = d_ref[0, 0, _slot_index(t, kk, g, n)]
                    pltpu.make_async_copy(src, _row_slice(xs_ref, d), sem).start(priority=kk)
            return carry

        lax.fori_loop(0, n // unroll, issue, 0)

    def wait_rows(buf, sem, n):
        for _ in range(2):
            pltpu.make_async_copy(_rows_slice(buf, 0, n), _rows_slice(xs_ref, 0, n), sem).wait()

    def zero_fill(wait):
        def fire(copy):
            if wait:
                copy.wait()
            else:
                copy.start()

        def per_expert(e, carry):
            first = pad_ref[e]
            n_pad = pad_ref[N_EXPERTS + e]
            for bit in range(len_bits):
                size = 1 << bit
                off = (n_pad >> (bit + 1)) << (bit + 1)

                @pl.when(((n_pad >> bit) & 1) == 1)
                def _():
                    fire(pltpu.make_async_copy(_rows_slice(zbuf, 0, size), _rows_slice(xs_ref, first + off, size),
                                               zsem))
            return carry

        lax.fori_loop(0, N_EXPERTS, per_expert, 0)

        def per_block(j, carry):
            fire(pltpu.make_async_copy(zbuf, _rows_slice(xs_ref, j * rows, rows), zsem))
            return carry

        lax.fori_loop(pad_ref[2 * N_EXPERTS], nblk, per_block, 0)

    def run(slot):
        @pl.when(i < nt)
        def _():
            scatter_rows(bufs[slot], sems[slot], x_ref, dest_ref, tt, g_p)

        @pl.when(i == nt)
        def _():
            zbuf[...] = jnp.zeros(zbuf.shape, F32)
            scatter_rows(bufs[slot], sems[slot], x_s_ref, dest_s_ref, ts, g_s)
            zero_fill(False)

        @pl.when(i > 0)
        def _():
            wait_rows(bufs[1 - slot], sems[1 - slot], tt)

        @pl.when(i == nt)
        def _():
            wait_rows(bufs[slot], sems[slot], ts)
            zero_fill(True)

    @pl.when(i % 2 == 0)
    def _():
        run(0)

    @pl.when(i % 2 == 1)
    def _():
        run(1)


def _dispatch(x_p, dest_p, g_p, x_s, dest_s, g_s, pad_info, nblk, rows):
    n_p, n_s = x_p.shape[0], x_s.shape[0]
    tt = _pick_tile(n_p, 256)
    nt = n_p // tt
    assert n_s <= tt
    last = lambda i, pad: jnp.minimum(i, nt - 1)
    grid_spec = pltpu.PrefetchScalarGridSpec(
        num_scalar_prefetch=1,
        grid=(nt + 1,),
        in_specs=[
            pl.BlockSpec((1, 1, 2 * tt), lambda i, pad: (last(i, pad), 0, 0), memory_space=pltpu.SMEM),
            pl.BlockSpec((1, 1, 2 * n_s), lambda i, pad: (0, 0, 0), memory_space=pltpu.SMEM),
            pl.BlockSpec((tt, D_MODEL), lambda i, pad: (last(i, pad), 0)),
            pl.BlockSpec((n_s, D_MODEL), lambda i, pad: (0, 0)),
        ],
        out_specs=pl.BlockSpec(memory_space=pl.ANY),
        scratch_shapes=[pltpu.VMEM((tt * SUBLANES, LANES), F32), pltpu.VMEM((tt * SUBLANES, LANES), F32),
                        pltpu.VMEM((rows * SUBLANES, LANES), F32),
                        pltpu.SemaphoreType.DMA, pltpu.SemaphoreType.DMA, pltpu.SemaphoreType.DMA],
    )
    return pl.pallas_call(
        functools.partial(_dispatch_body, tt=tt, ts=n_s, nt=nt, nblk=nblk, rows=rows, g_p=g_p, g_s=g_s),
        grid_spec=grid_spec,
        out_shape=jax.ShapeDtypeStruct((nblk * rows * SUBLANES, LANES), F32),
        compiler_params=_cp(("arbitrary",)),
        name="moe_dispatch",
    )(pad_info, dest_p.reshape(nt, 1, 2 * tt), dest_s.reshape(1, 1, 2 * n_s), x_p, x_s)


def _experts_body(be_ref, nu_ref, x_ref, wg_ref, wu_ref, wd_ref, y_ref, xb_scr, g_scr, u_scr, h_scr,
                  wgb_scr, wub_scr, wdb_scr, *, rows):
    i = pl.program_id(0)
    cur = jnp.minimum(i, nu_ref[0] - 1)
    new_expert = (i == 0) | (be_ref[cur] != be_ref[jnp.maximum(cur - 1, 0)])

    @pl.when((i < nu_ref[0]) & new_expert)
    def _():
        for r in range(0, D_MODEL, WCAST_ROWS):
            wgb_scr[r:r + WCAST_ROWS, :] = wg_ref[0, r:r + WCAST_ROWS, :].astype(BF16)
            wub_scr[r:r + WCAST_ROWS, :] = wu_ref[0, r:r + WCAST_ROWS, :].astype(BF16)
        for r in range(0, D_EXPERT, WCAST_ROWS // 2):
            wdb_scr[r:r + WCAST_ROWS // 2, :] = wd_ref[0, r:r + WCAST_ROWS // 2, :].astype(BF16)
    n_part = MOE_PARTS
    pr = rows // n_part
    act_rs = min(pr, ACT_ROW_SLAB)

    def load(k):
        base = k * pr * SUBLANES
        xb_scr[k] = jnp.concatenate([x_ref[pl.ds(base + c, pr, stride=SUBLANES), :] for c in range(ROW_CHUNKS)],
                                    axis=-1).astype(BF16)

    def gate_up(k):
        g_scr[k] = jnp.dot(xb_scr[k], wgb_scr[...], preferred_element_type=F32)
        u_scr[k] = jnp.dot(xb_scr[k], wub_scr[...], preferred_element_type=F32)

    def act(k):
        for r in range(pr // act_rs):
            rows_r = slice(r * act_rs, (r + 1) * act_rs)
            g = g_scr[k, rows_r, :]
            h_scr[k, rows_r, :] = ((g * jax.nn.sigmoid(g)) * u_scr[k, rows_r, :]).astype(BF16)

    def down(k):
        y = jnp.dot(h_scr[k], wdb_scr[...], preferred_element_type=F32)
        base = k * pr * SUBLANES
        for c in range(ROW_CHUNKS):
            y_ref[pl.ds(base + c, pr, stride=SUBLANES), :] = y[:, c * LANES:(c + 1) * LANES]

    @pl.when(i < nu_ref[0])
    def _():
        stages = (load, gate_up, act, down)
        matmul_stages = (gate_up, down)
        for t in range(len(stages) + n_part - 1):
            todo = [(stages[t - k], k) for k in range(n_part) if 0 <= t - k < len(stages)]
            for fn, k in sorted(todo, key=lambda fk: fk[0] not in matmul_stages):
                fn(k)

    @pl.when(i >= nu_ref[0])
    def _():
        y_ref[...] = jnp.zeros(y_ref.shape, F32)


def _experts(xs, block_e, n_used, wg, wu, wd, rows):
    nblk = xs.shape[0] // (rows * SUBLANES)
    clamp = lambda i, nu: jnp.minimum(i, nu[0] - 1)
    grid_spec = pltpu.PrefetchScalarGridSpec(
        num_scalar_prefetch=2,
        grid=(nblk,),
        in_specs=[
            pl.BlockSpec((rows * SUBLANES, LANES), lambda i, be, nu: (clamp(i, nu), 0)),
            pl.BlockSpec((1, D_MODEL, D_EXPERT), lambda i, be, nu: (be[clamp(i, nu)], 0, 0)),
            pl.BlockSpec((1, D_MODEL, D_EXPERT), lambda i, be, nu: (be[clamp(i, nu)], 0, 0)),
            pl.BlockSpec((1, D_EXPERT, D_MODEL), lambda i, be, nu: (be[clamp(i, nu)], 0, 0)),
        ],
        out_specs=pl.BlockSpec((rows * SUBLANES, LANES), lambda i, be, nu: (i, 0)),
        scratch_shapes=[
            pltpu.VMEM((MOE_PARTS, rows // MOE_PARTS, D_MODEL), BF16),
            pltpu.VMEM((MOE_PARTS, rows // MOE_PARTS, D_EXPERT), F32),
            pltpu.VMEM((MOE_PARTS, rows // MOE_PARTS, D_EXPERT), F32),
            pltpu.VMEM((MOE_PARTS, rows // MOE_PARTS, D_EXPERT), BF16),
            pltpu.VMEM((D_MODEL, D_EXPERT), BF16),
            pltpu.VMEM((D_MODEL, D_EXPERT), BF16),
            pltpu.VMEM((D_EXPERT, D_MODEL), BF16),
        ],
    )
    return pl.pallas_call(
        functools.partial(_experts_body, rows=rows),
        grid_spec=grid_spec,
        out_shape=jax.ShapeDtypeStruct(xs.shape, F32),
        compiler_params=_cp(("arbitrary",)),
        name="moe_experts",
    )(block_e, n_used, xs, wg, wu, wd)


def _combine_body(dest_ref, destn_ref, x_ref, ri_ref, g3_ref, b3_ref, ys_ref, o_ref,
                  b00, b01, b10, b11, sem0, sem1, *, tt, nt, g, alpha):
    i = pl.program_id(0)
    bufs = ((b00, b01), (b10, b11))
    sems = (sem0, sem1)
    rs = min(tt, LN_ROW_SLAB)
    unroll = min(tt, GATHER_UNROLL)

    def gather_rows(dref, slot):
        def copy(t, kk):
            d = dref[0, 0, _slot_index(t, kk, g, tt)]
            return pltpu.make_async_copy(_row_slice(ys_ref, d), _row_slice(bufs[slot][kk], t), sems[slot])

        def body(j, carry):
            for u in range(unroll):
                for kk in range(2):
                    copy(j * unroll + u, kk).start(priority=1)
            return carry

        lax.fori_loop(0, tt // unroll, body, 0)

    def wait_rows(slot):
        for kk in range(2):
            pltpu.make_async_copy(ys_ref.at[pl.ds(0, tt * SUBLANES), :], bufs[slot][kk], sems[slot]).wait()

    def run(slot):
        if slot == 0:
            @pl.when(i == 0)
            def _():
                gather_rows(dest_ref, 0)

        @pl.when(i + 1 < nt)
        def _():
            gather_rows(destn_ref, 1 - slot)

        wait_rows(slot)
        for r in range(tt // rs):
            rows = slice(r * rs, (r + 1) * rs)
            base = r * rs * SUBLANES
            y0 = jnp.concatenate([bufs[slot][0][pl.ds(base + c, rs, stride=SUBLANES), :]
                                  for c in range(ROW_CHUNKS)], axis=-1)
            y1 = jnp.concatenate([bufs[slot][1][pl.ds(base + c, rs, stride=SUBLANES), :]
                                  for c in range(ROW_CHUNKS)], axis=-1)
            ri = ri_ref[rows, :]
            moe = y0 * ri[:, 4:5] + y1 * ri[:, 5:6]
            o_ref[rows, :] = _layer_norm(alpha * x_ref[rows, :] + moe, g3_ref[...], b3_ref[...])

    @pl.when(i % 2 == 0)
    def _():
        run(0)

    @pl.when(i % 2 == 1)
    def _():
        run(1)


def _combine(x2d, rinfo, dest, g, ys, g3, b3, alpha):
    n = x2d.shape[0]
    tt = _pick_tile(n, 256)
    nt = n // tt
    dest3 = dest.reshape(nt, 1, 2 * tt)
    vec = pl.BlockSpec((1, D_MODEL), lambda i: (0, 0))
    stage = pltpu.VMEM((tt * SUBLANES, LANES), F32)
    return pl.pallas_call(
        functools.partial(_combine_body, tt=tt, nt=nt, g=g, alpha=alpha),
        grid=(nt,),
        in_specs=[
            pl.BlockSpec((1, 1, 2 * tt), lambda i: (i, 0, 0), memory_space=pltpu.SMEM),
            pl.BlockSpec((1, 1, 2 * tt), lambda i: (jnp.minimum(i + 1, nt - 1), 0, 0), memory_space=pltpu.SMEM),
            pl.BlockSpec((tt, D_MODEL), lambda i: (i, 0)),
            pl.BlockSpec((tt, LANES), lambda i: (i, 0)),
            vec, vec,
            pl.BlockSpec(memory_space=pl.ANY),
        ],
        out_specs=pl.BlockSpec((tt, D_MODEL), lambda i: (i, 0)),
        out_shape=jax.ShapeDtypeStruct((n, D_MODEL), F32),
        scratch_shapes=[stage, stage, stage, stage, pltpu.SemaphoreType.DMA, pltpu.SemaphoreType.DMA],
        compiler_params=_cp(("arbitrary",)),
        name="moe_combine",
    )(dest3, dest3, x2d, rinfo, g3, b3, ys)


def _rope_tables(pos):
    half = QK_ROPE // 2
    inv = ROPE_THETA ** (-jnp.arange(half, dtype=F32) / half)
    ang = pos.astype(F32)[:, None] * inv[None, :]
    cos, sin = jnp.cos(ang), jnp.sin(ang)
    n = pos.shape[0]
    pad_r = LANES - ROPE_LANE0 - QK_ROPE
    cos_t = jnp.concatenate([jnp.ones((n, ROPE_LANE0), F32), cos, cos, jnp.zeros((n, pad_r), F32)], -1)
    sin_t = jnp.concatenate([jnp.zeros((n, ROPE_LANE0), F32), sin, sin, jnp.zeros((n, pad_r), F32)], -1)
    return cos_t, sin_t


def _swap_neg(wr):
    half = QK_ROPE // 2
    return jnp.concatenate([-wr[:, half:], wr[:, :half]], axis=1)


def _prep_weights(l, w_in, q_norm_g, kv_norm_g, w_uq, w_ukv, conv_w, w_out, ln1_g, ln1_b, w_xq, w_xk, w_xv, w_xo,
                  ln2_g, ln2_b, w_router_group, b_router_group, w_router_expert, b_router_expert,
                  w_exp_gate, w_exp_up, w_exp_down, ln3_g, ln3_b):
    wi = w_in[l]
    c0 = Q_LORA + KV_LORA
    w_kr = wi[:, c0:c0 + QK_ROPE]
    zl = jnp.zeros((D_MODEL, ROPE_LANE0), F32)
    w_in_p = jnp.concatenate([wi[:, :c0], wi[:, c0 + QK_ROPE:], zl, w_kr, _swap_neg(w_kr)], axis=1)
    wq = w_uq[l].reshape(Q_LORA, N_HEADS, QK_NOPE + QK_ROPE)
    wq_rot = jnp.concatenate([-wq[:, :, QK_NOPE + QK_ROPE // 2:], wq[:, :, QK_NOPE:QK_NOPE + QK_ROPE // 2]], axis=2)
    wq_a = jnp.concatenate([wq, wq_rot], axis=2).reshape(Q_LORA, N_HEADS * HEAD_PAD)
    wkv = w_ukv[l].reshape(KV_LORA, N_HEADS, QK_NOPE + V_HEAD)
    wk_p = jnp.concatenate([wkv[:, :, :QK_NOPE], jnp.zeros((KV_LORA, N_HEADS, HEAD_PAD - QK_NOPE), F32)], axis=2)
    wk_p = wk_p.reshape(KV_LORA, N_HEADS * HEAD_PAD)
    wv_p = jnp.concatenate([wkv[:, :, QK_NOPE:], jnp.zeros((KV_LORA, N_HEADS, HEAD_PAD - V_HEAD), F32)], axis=2)
    wv_p = wv_p.reshape(KV_LORA, N_HEADS * HEAD_PAD)
    w_router = jnp.concatenate([w_router_group[l], w_router_expert[l],
                                jnp.zeros((D_MODEL, LANES - N_GROUPS - N_EXPERTS), F32)], axis=1)
    b_router = jnp.concatenate([b_router_group[l], b_router_expert[l].reshape(-1),
                                jnp.zeros((LANES - N_GROUPS - N_EXPERTS,), F32)]).reshape(1, LANES)
    return dict(
        w_in=w_in_p.astype(BF16),
        q_norm_g=q_norm_g[l].reshape(1, Q_LORA), kv_norm_g=kv_norm_g[l].reshape(1, KV_LORA),
        w_uq=wq_a.astype(BF16),
        w_ukv=jnp.concatenate([wk_p, wv_p], axis=1).astype(BF16),
        conv_w=conv_w[l],
        w_out=w_out[l].astype(BF16), ln1_g=ln1_g[l].reshape(1, -1), ln1_b=ln1_b[l].reshape(1, -1),
        w_xq=w_xq[l].astype(BF16), w_xk=w_xk[l].astype(BF16), w_xv=w_xv[l].astype(BF16),
        w_xo=w_xo[l].astype(BF16), ln2_g=ln2_g[l].reshape(1, -1), ln2_b=ln2_b[l].reshape(1, -1),
        w_router=w_router.astype(BF16), b_router=b_router,
        w_exp_gate=w_exp_gate[l], w_exp_up=w_exp_up[l], w_exp_down=w_exp_down[l],
        ln3_g=ln3_g[l].reshape(1, -1), ln3_b=ln3_b[l].reshape(1, -1),
    )


def _slots(rt, pstarts):
    g = rt.shape[-1]
    rt = rt.reshape(-1, SUBLANES, g)
    dest = pstarts[rt[:, 0:2, :].astype(I32)] + rt[:, 2:4, :].astype(I32)
    return dest.reshape(-1), g


def _layer(l, depth, xp, xs, lat_past, kr_past, conv_past, mk_s, mv_s, mem_prompt, w):
    alpha = (2 * depth) ** 0.25
    b, s, _ = xp.shape
    bs, ss, _ = xs.shape
    past = lat_past.shape[1]

    cos_p, sin_p = _rope_tables(jnp.arange(s))
    q_p, k_p, v_p, yc_p, lat_p, kr_p, cst_p = _inproj(
        xp, jnp.zeros((b, CONV_K - 1, CONV_WIDTH), F32), cos_p, sin_p, w)
    attn_p = _attention(q_p, k_p, v_p, 0, None)
    mk, mv, mk_b, mv_b = _memkv(mem_prompt.reshape(b * N_MEM, D_MODEL), w['w_xk'], w['w_xv'])
    cnt0 = jnp.zeros((1, LANES), F32)
    x2_p, ri_p, cnt_p, rt_p = _mid(xp, attn_p, yc_p, mk_b.reshape(b, N_MEM, D_MODEL), mv_b.reshape(b, N_MEM, D_MODEL),
                             cnt0, w, alpha)

    cos_s, sin_s = _rope_tables(past + jnp.arange(ss))
    q_s, _, _, yc_s, lat_s, kr_s, cst_s = _inproj(xs, conv_past, cos_s, sin_s, w)
    n_keys = past + ss
    sk = -(-n_keys // LANES) * LANES
    lat_all = jnp.concatenate([lat_past, lat_s, jnp.zeros((bs, sk - n_keys, KV_LORA), F32)], axis=1)
    kr_all = jnp.concatenate([kr_past, kr_s, jnp.zeros((bs, sk - n_keys, QK_ROPE), F32)], axis=1)
    kr_all = jnp.pad(kr_all, ((0, 0), (0, 0), (ROPE_LANE0, LANES - ROPE_LANE0 - QK_ROPE)))
    attn_s = _decode_attention(q_s, lat_all, kr_all, w['w_ukv'], n_keys)
    mk_sb = mk_s.reshape(bs, N_MEM, D_MODEL).astype(BF16)
    mv_sb = mv_s.reshape(bs, N_MEM, D_MODEL).astype(BF16)
    x2_s, ri_s, cnt, rt_s = _mid(xs, attn_s, yc_s, mk_sb, mv_sb, cnt_p, w, alpha)

    n_p, n_s = b * s, bs * ss
    counts = cnt[0, ROUTER_LANE0:ROUTER_LANE0 + N_EXPERTS].astype(I32)
    padded = (counts + MOE_ROWS - 1) // MOE_ROWS * MOE_ROWS
    pends = jnp.cumsum(padded)
    pstarts = pends - padded
    nblk = -(-2 * (n_p + n_s) // MOE_ROWS) + N_EXPERTS
    blk_start = jnp.arange(nblk, dtype=I32) * MOE_ROWS
    block_e = jnp.minimum(jnp.sum((pends[None, :] <= blk_start[:, None]).astype(I32), axis=1), N_EXPERTS - 1)
    n_used = (pends[-1] // MOE_ROWS).astype(I32).reshape(1)
    ri_p2, ri_s2 = ri_p.reshape(n_p, LANES), ri_s.reshape(n_s, LANES)
    dest_p, g_p = _slots(rt_p, pstarts)
    dest_s, g_s = _slots(rt_s, pstarts)
    x2_p2, x2_s2 = x2_p.reshape(n_p, D_MODEL), x2_s.reshape(n_s, D_MODEL)
    pad_info = jnp.concatenate([pstarts + counts, padded - counts, n_used]).astype(I32)
    slots = _dispatch(x2_p2, dest_p, g_p, x2_s2, dest_s, g_s, pad_info, nblk, MOE_ROWS)
    ys = _experts(slots, block_e, n_used, w['w_exp_gate'], w['w_exp_up'], w['w_exp_down'], MOE_ROWS)
    y_p = _combine(x2_p2, ri_p2, dest_p, g_p, ys, w['ln3_g'], w['ln3_b'], alpha).reshape(b, s, D_MODEL)
    y_s = _combine(x2_s2, ri_s2, dest_s, g_s, ys, w['ln3_g'], w['ln3_b'], alpha).reshape(bs, ss, D_MODEL)
    return (y_p, y_s, lat_p, kr_p, cst_p, mk.reshape(b, N_MEM, X_HEADS, X_HEAD_DIM),
            mv.reshape(b, N_MEM, X_HEADS, X_HEAD_DIM), lat_s, kr_s, cst_s)


def kernel(x_prompt, x_sample, cache_kv_latent, cache_k_rope, cache_conv, cache_mem_k, cache_mem_v, mem_prompt,
           w_in, q_norm_g, kv_norm_g, w_uq, w_ukv, conv_w, w_out, ln1_g, ln1_b, w_xq, w_xk, w_xv, w_xo, ln2_g,
           ln2_b, w_router_group, b_router_group, w_router_expert, b_router_expert, w_exp_gate, w_exp_up,
           w_exp_down, ln3_g, ln3_b):
    depth = w_in.shape[0]
    xp, xs = x_prompt, x_sample
    outs = [[] for _ in range(8)]
    for l in range(depth):
        w = _prep_weights(l, w_in, q_norm_g, kv_norm_g, w_uq, w_ukv, conv_w, w_out, ln1_g, ln1_b, w_xq, w_xk, w_xv,
                          w_xo, ln2_g, ln2_b, w_router_group, b_router_group, w_router_expert, b_router_expert,
                          w_exp_gate, w_exp_up, w_exp_down, ln3_g, ln3_b)
        res = _layer(l, depth, xp, xs, cache_kv_latent[l], cache_k_rope[l], cache_conv[l], cache_mem_k[l],
                     cache_mem_v[l], mem_prompt, w)
        xp, xs = res[0], res[1]
        for acc, r in zip(outs, res[2:]):
            acc.append(r)
    return (xp, xs) + tuple(jnp.stack(o) for o in outs)
```

```python
import functools
import math

import numpy as np
import jax
import jax.numpy as jnp
from jax import lax
from jax.experimental import pallas as pl
from jax.experimental.pallas import tpu as pltpu

F32 = jnp.float32
BF16 = jnp.bfloat16
I32 = jnp.int32

D_MODEL = 1024
CHUNK = 64
N_HEADS = 8
QK_NOPE = 64
QK_ROPE = 32
V_HEAD = 64
Q_LORA = 256
KV_LORA = 128
ROPE_THETA = 10000.0
MLA_SCALE = (QK_NOPE + QK_ROPE) ** -0.5
Q_PRESCALE = MLA_SCALE * math.log2(math.e)
CONV_WIDTH = 512
CONV_K = 3
N_MEM = 256
X_HEADS = 4
X_HEAD_DIM = D_MODEL // X_HEADS
X_SCALE = X_HEAD_DIM ** -0.5
N_GROUPS = 4
EXPERTS_PER_GROUP = 8
N_EXPERTS = N_GROUPS * EXPERTS_PER_GROUP
D_EXPERT = 512
LN_EPS = 1e-5
RMS_EPS = 1e-6

LANES = 128
SUBLANES = 8
ROW_CHUNKS = D_MODEL // LANES
PACKED_CHUNKS = ROW_CHUNKS // 2
HEAD_PAD = LANES
ROPE_LANE0 = QK_NOPE
ROUTER_LANE0 = N_GROUPS
VMEM_LIMIT = 56 * 1024 * 1024
MOE_ROWS = 512
MOE_PARTS = 2
ACT_ROW_SLAB = 32
GATHER_UNROLL = 8
WCAST_ROWS = 64
ATTN_TQ = 1024
ATTN_TK = 512
ATTN_SUB = 2
ATTN_ROW_SLAB = 64
LN_ROW_SLAB = 16
XATTN_ROW_SLAB = 64
INPROJ_ROWS = 512
ROUTE_GROUP = 256
MID_SKEW = 1
MID_ROWS = 512
MID_PARTS = 2


def _cp(sem, vmem=VMEM_LIMIT):
    return pltpu.CompilerParams(dimension_semantics=sem, vmem_limit_bytes=vmem)


def _pick_tile(n, pref):
    t = min(n, pref)
    while n % t:
        t //= 2
    return t


def _with_ones_lane(v):
    lane = lax.broadcasted_iota(I32, v.shape, 1)
    return jnp.where((lane & (HEAD_PAD - 1)) == V_HEAD, 1.0, v)


def _inproj_body(x_ref, cinit_ref, cos_ref, sin_ref, win_ref, qg_ref, kvg_ref, wuq_ref, wukv_ref, cw_ref,
                 q_ref, k_ref, v_ref, yc_ref, lat_ref, kr_ref, cst_ref, u_scr, *, tt, nj):
    j = pl.program_id(1)

    @pl.when(j == 0)
    def _():
        u_scr[6:8, :] = cinit_ref[0]

    x = x_ref[0].astype(BF16)
    proj = jnp.dot(x, win_ref[...], preferred_element_type=F32)
    cq = proj[:, 0:256]
    ckv = proj[:, 256:384]
    gb = proj[:, 384:896]
    gc = proj[:, 896:1408]
    gv = proj[:, 1408:1920]
    kr_blk = proj[:, 1920:2048]
    cos_t = cos_ref[...]
    sin_t = sin_ref[...]

    def rotate(blk):
        return blk * cos_t + pltpu.roll(blk, LANES - QK_ROPE, 1) * sin_t

    cqn = cq * lax.rsqrt(jnp.mean(cq * cq, -1, keepdims=True) + RMS_EPS) * qg_ref[...]
    ckvn = ckv * lax.rsqrt(jnp.mean(ckv * ckv, -1, keepdims=True) + RMS_EPS) * kvg_ref[...]
    lat_ref[0] = ckvn
    kr_p = rotate(kr_blk)
    kr_ref[0] = kr_p[:, ROPE_LANE0:ROPE_LANE0 + QK_ROPE]
    qq = jnp.dot(cqn.astype(BF16), wuq_ref[...], preferred_element_type=F32)
    kv = jnp.dot(ckvn.astype(BF16), wukv_ref[...], preferred_element_type=F32)
    hw = N_HEADS * HEAD_PAD
    for h in range(N_HEADS):
        sl = slice(h * HEAD_PAD, (h + 1) * HEAD_PAD)
        q_ref[0, :, sl] = (rotate(qq[:, sl]) * Q_PRESCALE).astype(BF16)
        k_ref[0, :, sl] = (kv[:, sl] + kr_p).astype(BF16)
    v_ref[0] = _with_ones_lane(kv[:, hw:2 * hw]).astype(BF16)
    u = gc * gv
    u_scr[8:8 + tt, :] = u
    conv = cw_ref[0:1, :] * u_scr[6:6 + tt, :] + cw_ref[1:2, :] * u_scr[7:7 + tt, :] + cw_ref[2:3, :] * u
    yc_ref[0] = (gb * conv).astype(BF16)
    last2 = u_scr[tt + 6:tt + 8, :]
    u_scr[6:8, :] = last2

    @pl.when(j == nj - 1)
    def _():
        cst_ref[0] = last2


def _inproj(x, conv_init, cos_t, sin_t, w):
    b, s, _ = x.shape
    tt = _pick_tile(s, INPROJ_ROWS)
    nj = s // tt
    wn = w['w_in'].shape[1]
    full = lambda shape: pl.BlockSpec(shape, lambda bi, ji: (0,) * len(shape))
    out_shapes = (
        jax.ShapeDtypeStruct((b, s, N_HEADS * HEAD_PAD), BF16),
        jax.ShapeDtypeStruct((b, s, N_HEADS * HEAD_PAD), BF16),
        jax.ShapeDtypeStruct((b, s, N_HEADS * HEAD_PAD), BF16),
        jax.ShapeDtypeStruct((b, s, CONV_WIDTH), BF16),
        jax.ShapeDtypeStruct((b, s, KV_LORA), F32),
        jax.ShapeDtypeStruct((b, s, QK_ROPE), F32),
        jax.ShapeDtypeStruct((b, CONV_K - 1, CONV_WIDTH), F32),
    )
    row = lambda width: pl.BlockSpec((1, tt, width), lambda bi, ji: (bi, ji, 0))
    return pl.pallas_call(
        functools.partial(_inproj_body, tt=tt, nj=nj),
        grid=(b, nj),
        in_specs=[
            row(D_MODEL),
            pl.BlockSpec((1, CONV_K - 1, CONV_WIDTH), lambda bi, ji: (bi, 0, 0)),
            pl.BlockSpec((tt, LANES), lambda bi, ji: (ji, 0)),
            pl.BlockSpec((tt, LANES), lambda bi, ji: (ji, 0)),
            full((D_MODEL, wn)),
            full((1, Q_LORA)),
            full((1, KV_LORA)),
            full(w['w_uq'].shape),
            full(w['w_ukv'].shape),
            full((CONV_K, CONV_WIDTH)),
        ],
        out_specs=(
            row(N_HEADS * HEAD_PAD), row(N_HEADS * HEAD_PAD), row(N_HEADS * HEAD_PAD), row(CONV_WIDTH),
            row(KV_LORA), row(QK_ROPE),
            pl.BlockSpec((1, CONV_K - 1, CONV_WIDTH), lambda bi, ji: (bi, 0, 0)),
        ),
        out_shape=out_shapes,
        scratch_shapes=[pltpu.VMEM((tt + 8, CONV_WIDTH), F32)],
        compiler_params=_cp(("parallel", "arbitrary")),
        name="inproj",
    )(x, conv_init, cos_t, sin_t, w['w_in'], w['q_norm_g'], w['kv_norm_g'], w['w_uq'], w['w_ukv'], w['conv_w'])


def _attn_body(qi_ref, ki_ref, fl_ref, q_ref, k_ref, v_ref, o_ref, m_scr, acc_scr, s_scr, p_scr, a_scr,
               *, tq, tk, n_sub, rs, q_pos0, n_valid, combos):
    step = pl.program_id(1)
    qi = qi_ref[step]
    ki = ki_ref[step]
    flags = fl_ref[step]
    reps = tk // LANES
    th = tq // n_sub

    @pl.when((flags & 1) != 0)
    def _():
        m_scr[...] = jnp.full(m_scr.shape, -jnp.inf, F32)
        acc_scr[...] = jnp.zeros(acc_scr.shape, F32)

    def scores(item, buf):
        sub, h, _ = item
        hs = slice(h * HEAD_PAD, (h + 1) * HEAD_PAD)
        s_scr[buf] = lax.dot_general(q_ref[0, sub * th:(sub + 1) * th, hs], k_ref[0, :, hs],
                                     (((1,), (1,)), ((), ())), preferred_element_type=F32)

    def softmax_pv(item, buf):
        sub, h, masked = item
        hs = slice(h * HEAD_PAD, (h + 1) * HEAD_PAD)
        for r in range(th // rs):
            rows = slice(r * rs, (r + 1) * rs)
            arows = slice(sub * th + r * rs, sub * th + (r + 1) * rs)
            s_r = s_scr[buf, rows, :]
            if masked:
                qpos = q_pos0 + qi * tq + sub * th + r * rs + lax.broadcasted_iota(I32, (rs, tk), 0)
                kpos = ki * tk + lax.broadcasted_iota(I32, (rs, tk), 1)
                mask = (kpos >> 6) <= (qpos >> 6)
                if n_valid is not None:
                    mask = mask & (kpos < n_valid)
                s_r = jnp.where(mask, s_r, -jnp.inf)
            m_old = m_scr[h, arows, :]
            m_new = jnp.maximum(m_old, jnp.max(s_r, axis=-1, keepdims=True))
            a_scr[buf, rows, :] = jnp.exp2(m_old - m_new)
            m_rep = jnp.concatenate([m_new] * reps, axis=1)
            p_scr[buf, rows, :] = jnp.exp2(s_r - m_rep).astype(BF16)
            m_scr[h, arows, :] = m_new
        pv = jnp.dot(p_scr[buf], v_ref[0, :, hs], preferred_element_type=F32)
        srows = slice(sub * th, (sub + 1) * th)
        acc_scr[h, srows, :] = a_scr[buf] * acc_scr[h, srows, :] + pv

    def run(modes):
        items = [(sub, h, mode == 2) for sub, mode in enumerate(modes) if mode != 0 for h in range(N_HEADS)]
        scores(items[0], 0)
        for n, item in enumerate(items):
            if n + 1 < len(items):
                scores(items[n + 1], (n + 1) % 2)
            softmax_pv(item, n % 2)

    for code, modes in combos:
        @pl.when((flags >> 2) == code)
        def _(modes=modes):
            run(modes)

    @pl.when((flags & 2) != 0)
    def _():
        for h in range(N_HEADS):
            acc = acc_scr[h]
            o_ref[0, :, h * V_HEAD:(h + 1) * V_HEAD] = (acc[:, 0:V_HEAD] / acc[:, V_HEAD:V_HEAD + 1]).astype(BF16)


def _attn_tables(nq, nk, tq, tk, n_sub, q_pos0, n_valid):
    th = tq // n_sub
    qi_l, ki_l, fl_l, combos = [], [], [], {}
    for qi in range(nq):
        sub_lo = [q_pos0 + qi * tq + j * th for j in range(n_sub)]
        sub_last = []
        for lo in sub_lo:
            last_pos = ((lo + th - 1) // CHUNK) * CHUNK + CHUNK - 1
            if n_valid is not None:
                last_pos = min(last_pos, n_valid - 1)
            sub_last.append(min(nk - 1, last_pos // tk))
        k_last = max(sub_last)
        for ki in range(k_last + 1):
            k_hi = ki * tk + tk - 1
            modes = []
            for lo, last in zip(sub_lo, sub_last):
                if ki > last:
                    modes.append(0)
                elif (k_hi // CHUNK) > (lo // CHUNK) or (n_valid is not None and k_hi >= n_valid):
                    modes.append(2)
                else:
                    modes.append(1)
            code = sum(m * 3 ** j for j, m in enumerate(modes))
            combos[code] = tuple(modes)
            qi_l.append(qi); ki_l.append(ki)
            fl_l.append((1 if ki == 0 else 0) | (2 if ki == k_last else 0) | (code << 2))
    to_arr = lambda vals: jnp.asarray(np.array(vals, np.int32))
    return to_arr(qi_l), to_arr(ki_l), to_arr(fl_l), tuple(sorted(combos.items()))


def _attention(q, k, v, q_pos0, n_valid):
    b, sq, _ = q.shape
    sk = k.shape[1]
    tq = _pick_tile(sq, ATTN_TQ)
    tk = _pick_tile(sk, ATTN_TK)
    n_sub = ATTN_SUB if tq % (ATTN_SUB * 2 * SUBLANES) == 0 else 1
    th = tq // n_sub
    qi_t, ki_t, fl_t, combos = _attn_tables(sq // tq, sk // tk, tq, tk, n_sub, q_pos0, n_valid)
    n_steps = int(qi_t.shape[0])
    grid_spec = pltpu.PrefetchScalarGridSpec(
        num_scalar_prefetch=3,
        grid=(b, n_steps),
        in_specs=[
            pl.BlockSpec((1, tq, N_HEADS * HEAD_PAD), lambda bi, si, qt, kt, ft: (bi, qt[si], 0)),
            pl.BlockSpec((1, tk, N_HEADS * HEAD_PAD), lambda bi, si, qt, kt, ft: (bi, kt[si], 0)),
            pl.BlockSpec((1, tk, N_HEADS * HEAD_PAD), lambda bi, si, qt, kt, ft: (bi, kt[si], 0)),
        ],
        out_specs=pl.BlockSpec((1, tq, N_HEADS * V_HEAD), lambda bi, si, qt, kt, ft: (bi, qt[si], 0)),
        scratch_shapes=[
            pltpu.VMEM((N_HEADS, tq, LANES), F32),
            pltpu.VMEM((N_HEADS, tq, LANES), F32),
            pltpu.VMEM((2, th, tk), F32),
            pltpu.VMEM((2, th, tk), BF16),
            pltpu.VMEM((2, th, LANES), F32),
        ],
    )
    return pl.pallas_call(
        functools.partial(_attn_body, tq=tq, tk=tk, n_sub=n_sub, rs=min(th, ATTN_ROW_SLAB), q_pos0=q_pos0,
                          n_valid=n_valid, combos=combos),
        grid_spec=grid_spec,
        out_shape=jax.ShapeDtypeStruct((b, sq, N_HEADS * V_HEAD), BF16),
        compiler_params=_cp(("parallel", "arbitrary")),
        name="mla_attn",
    )(qi_t, ki_t, fl_t, q, k, v)


def _decode_attn_body(q_ref, lat_ref, krp_ref, wukv_ref, o_ref, qa_scr, qr_scr, *, ss, n_valid):
    hw = N_HEADS * HEAD_PAD
    latb = lat_ref[0].astype(BF16)
    krb = krp_ref[0].astype(BF16)
    contract_last = (((1,), (1,)), ((), ()))
    for h in range(N_HEADS):
        hs = slice(h * HEAD_PAD, (h + 1) * HEAD_PAD)
        rows = slice(h * ss, (h + 1) * ss)
        q_h = q_ref[0, :, hs]
        qa_scr[rows, :] = lax.dot_general(q_h, wukv_ref[:, hs], contract_last,
                                          preferred_element_type=F32).astype(BF16)
        qr_scr[rows, :] = q_h
    s = (lax.dot_general(qa_scr[...], latb, contract_last, preferred_element_type=F32)
         + lax.dot_general(qr_scr[...], krb, contract_last, preferred_element_type=F32))
    kpos = lax.broadcasted_iota(I32, s.shape, 1)
    s = jnp.where(kpos < n_valid, s, -jnp.inf)
    p = jnp.exp2(s - jnp.max(s, -1, keepdims=True))
    ol = jnp.dot(p.astype(BF16), latb, preferred_element_type=F32) / jnp.sum(p, -1, keepdims=True)
    for h in range(N_HEADS):
        rows = slice(h * ss, (h + 1) * ss)
        o_h = jnp.dot(ol[rows, :].astype(BF16), wukv_ref[:, hw + h * HEAD_PAD:hw + (h + 1) * HEAD_PAD],
                      preferred_element_type=F32)
        o_ref[0, :, h * V_HEAD:(h + 1) * V_HEAD] = o_h[:, 0:V_HEAD].astype(BF16)


def _decode_attention(q, lat_all, kr_padded, w_ukv, n_valid):
    b, ss, _ = q.shape
    sk = lat_all.shape[1]
    return pl.pallas_call(
        functools.partial(_decode_attn_body, ss=ss, n_valid=n_valid),
        grid=(b,),
        in_specs=[pl.BlockSpec((1, ss, N_HEADS * HEAD_PAD), lambda bi: (bi, 0, 0)),
                  pl.BlockSpec((1, sk, KV_LORA), lambda bi: (bi, 0, 0)),
                  pl.BlockSpec((1, sk, LANES), lambda bi: (bi, 0, 0)),
                  pl.BlockSpec(w_ukv.shape, lambda bi: (0, 0))],
        out_specs=pl.BlockSpec((1, ss, N_HEADS * V_HEAD), lambda bi: (bi, 0, 0)),
        out_shape=jax.ShapeDtypeStruct((b, ss, N_HEADS * V_HEAD), BF16),
        scratch_shapes=[pltpu.VMEM((N_HEADS * ss, KV_LORA), BF16), pltpu.VMEM((N_HEADS * ss, HEAD_PAD), BF16)],
        compiler_params=_cp(("parallel",)),
        name="decode_attn",
    )(q, lat_all, kr_padded, w_ukv)


def _memkv_body(mem_ref, wk_ref, wv_ref, mk_ref, mv_ref, mkb_ref, mvb_ref):
    m = mem_ref[...].astype(BF16)
    mk = jnp.dot(m, wk_ref[...], preferred_element_type=F32)
    mv = jnp.dot(m, wv_ref[...], preferred_element_type=F32)
    mk_ref[...] = mk
    mv_ref[...] = mv
    mkb_ref[...] = mk.astype(BF16)
    mvb_ref[...] = mv.astype(BF16)


def _memkv(mem2d, w_xk, w_xv):
    n = mem2d.shape[0]
    tt = _pick_tile(n, 256)
    row = pl.BlockSpec((tt, D_MODEL), lambda i: (i, 0))
    wspec = pl.BlockSpec((D_MODEL, D_MODEL), lambda i: (0, 0))
    return pl.pallas_call(
        _memkv_body,
        grid=(n // tt,),
        in_specs=[row, wspec, wspec],
        out_specs=(row, row, row, row),
        out_shape=(jax.ShapeDtypeStruct((n, D_MODEL), F32), jax.ShapeDtypeStruct((n, D_MODEL), F32),
                   jax.ShapeDtypeStruct((n, D_MODEL), BF16), jax.ShapeDtypeStruct((n, D_MODEL), BF16)),
        compiler_params=_cp(("parallel",)),
        name="memkv",
    )(mem2d, w_xk, w_xv)


def _layer_norm(x, g, b):
    mu = jnp.mean(x, -1, keepdims=True)
    xc = x - mu
    var = jnp.mean(xc * xc, -1, keepdims=True)
    return xc * lax.rsqrt(var + LN_EPS) * g + b


def _mid_body(x_ref, at_ref, yc_ref, mk_ref, mv_ref, cnt0_ref, low_ref, wo_ref, g1_ref, b1_ref, wq_ref, wxo_ref,
              g2_ref, b2_ref, wr_ref, br_ref, x2_ref, ri_ref, cnt_ref, rt_ref,
              cnt_scr, a_scr, xb_scr, q_scr, sc_scr, p_scr, o_scr, lg_scr, *, tt, alpha):
    step = pl.program_id(0)

    @pl.when(step == 0)
    def _():
        cnt_scr[...] = cnt0_ref[...]
        lg_scr[...] = jnp.zeros(lg_scr.shape, F32)

    cur_slot = step % 2
    prev_slot = 1 - cur_slot
    routed = (step > 0).astype(F32)

    n_part = MID_PARTS if tt % (MID_PARTS * SUBLANES * 2) == 0 else 1
    pr = tt // n_part
    ln_rs = min(pr, LN_ROW_SLAB)
    sm_rs = min(pr, XATTN_ROW_SLAB)
    c_exp = X_SCALE * math.log2(math.e)
    lane = lax.broadcasted_iota(I32, (pr, LANES), 1)

    def part_rows(k):
        return slice(k * pr, (k + 1) * pr)

    def out_proj(k):
        rp = part_rows(k)
        mix = jnp.concatenate([at_ref[0, rp, :], yc_ref[0, rp, :]], axis=-1)
        a_scr[rp, :] = jnp.dot(mix, wo_ref[...], preferred_element_type=F32)

    def norm1(k):
        for r in range(pr // ln_rs):
            rows = slice(k * pr + r * ln_rs, k * pr + (r + 1) * ln_rs)
            x1 = _layer_norm(alpha * x_ref[0, rows, :] + a_scr[rows, :], g1_ref[...], b1_ref[...])
            x2_ref[0, rows, :] = x1
            xb_scr[rows, :] = x1.astype(BF16)

    def q_proj(k):
        rp = part_rows(k)
        q_scr[rp, :] = jnp.dot(xb_scr[rp, :], wq_ref[...], preferred_element_type=F32).astype(BF16)

    def cross_attn(k):
        rp = part_rows(k)
        for h in range(X_HEADS):
            sl = slice(h * X_HEAD_DIM, (h + 1) * X_HEAD_DIM)
            b2 = h % 2
            sc_scr[k, b2] = lax.dot_general(q_scr[rp, sl], mk_ref[0, :, sl], (((1,), (1,)), ((), ())),
                                            preferred_element_type=F32)
            for r in range(pr // sm_rs):
                rows = slice(r * sm_rs, (r + 1) * sm_rs)
                s_r = sc_scr[k, b2, rows, :]
                e = jnp.exp2((s_r - jnp.max(s_r, -1, keepdims=True)) * c_exp)
                p_scr[k, b2, rows, :] = (e / jnp.sum(e, -1, keepdims=True)).astype(BF16)
            o_scr[rp, sl] = jnp.dot(p_scr[k, b2], mv_ref[0, :, sl], preferred_element_type=F32).astype(BF16)

    def x_out_proj(k):
        rp = part_rows(k)
        a_scr[rp, :] = jnp.dot(o_scr[rp, :], wxo_ref[...], preferred_element_type=F32)

    def norm2(k):
        for r in range(pr // ln_rs):
            rows = slice(k * pr + r * ln_rs, k * pr + (r + 1) * ln_rs)
            x2 = _layer_norm(alpha * x2_ref[0, rows, :] + a_scr[rows, :], g2_ref[...], b2_ref[...])
            x2_ref[0, rows, :] = x2
            xb_scr[rows, :] = x2.astype(BF16)

    def router_logits(k):
        rp = part_rows(k)
        lg_scr[cur_slot, rp, :] = jnp.dot(xb_scr[rp, :], wr_ref[...], preferred_element_type=F32) + br_ref[...]

    def route(k):
        rp = part_rows(k)
        logits = lg_scr[prev_slot, rp, :]
        neg = -jnp.inf
        is_g = lane < N_GROUPS
        lg = jnp.where(is_g, logits, neg)
        mg = jnp.max(lg, -1, keepdims=True)
        g_idx = jnp.min(jnp.where(lg == mg, lane, LANES), -1, keepdims=True)
        pg = 1.0 / jnp.sum(jnp.where(is_g, jnp.exp(logits - mg), 0.0), -1, keepdims=True)
        in_grp = ((lane >= ROUTER_LANE0) & (lane < ROUTER_LANE0 + N_EXPERTS)
                  & (((lane - ROUTER_LANE0) >> 3) == g_idx))
        le = jnp.where(in_grp, logits, neg)
        v1 = jnp.max(le, -1, keepdims=True)
        i1 = jnp.min(jnp.where(le == v1, lane, LANES), -1, keepdims=True)
        le2 = jnp.where(lane == i1, neg, le)
        v2 = jnp.max(le2, -1, keepdims=True)
        i2 = jnp.min(jnp.where(le2 == v2, lane, LANES), -1, keepdims=True)
        e2 = jnp.exp(v2 - v1)
        den = 1.0 + e2
        gate1 = (1.0 / den) * pg
        gate2 = (e2 / den) * pg
        oh1 = (lane == i1).astype(F32)
        oh2 = (lane == i2).astype(F32)
        oh = oh1 + oh2
        base = cnt_scr[...] + jnp.dot(low_ref[...], oh.astype(BF16), preferred_element_type=F32)
        rank1 = jnp.sum(oh1 * base, -1, keepdims=True)
        rank2 = jnp.sum(oh2 * base, -1, keepdims=True)
        cnt_scr[...] = cnt_scr[...] + jnp.sum(oh, 0, keepdims=True) * routed
        e1f = (i1 - ROUTER_LANE0).astype(F32)
        e2f = (i2 - ROUTER_LANE0).astype(F32)
        ri = jnp.where(lane == 0, e1f, jnp.where(lane == 1, e2f, jnp.where(
            lane == 2, rank1, jnp.where(lane == 3, rank2, jnp.where(lane == 4, gate1, jnp.where(
                lane == 5, gate2, 0.0))))))
        ri_ref[0, rp, :] = ri
        ri_t = jnp.transpose(ri)[0:SUBLANES, :]
        grp = min(pr, ROUTE_GROUP)
        for j in range(pr // grp):
            rt_ref[0, k * (pr // grp) + j] = ri_t[:, j * grp:(j + 1) * grp]

    stages = (out_proj, norm1, q_proj, cross_attn, x_out_proj, norm2, router_logits)
    matmul_stages = (out_proj, q_proj, x_out_proj, router_logits)
    for t in range(len(stages) + MID_SKEW * (n_part - 1)):
        todo = [(stages[t - MID_SKEW * k], k) for k in range(n_part) if 0 <= t - MID_SKEW * k < len(stages)]
        for fn, k in sorted(todo, key=lambda fk: fk[0] not in matmul_stages):
            fn(k)
        if t < n_part:
            route(t)
    cnt_ref[...] = cnt_scr[...]


def _mid(x, attn, yconv, mk_b, mv_b, cnt0, w, alpha):
    b, s, _ = x.shape
    tt = _pick_tile(s, MID_ROWS)
    nj = s // tt
    n_tiles = b * nj
    cur = lambda i: jnp.minimum(i, n_tiles - 1)
    prev = lambda i: jnp.maximum(i - 1, 0)
    row = lambda width: pl.BlockSpec((1, tt, width), lambda i: (cur(i) // nj, cur(i) % nj, 0))
    row_prev = lambda width: pl.BlockSpec((1, tt, width), lambda i: (prev(i) // nj, prev(i) % nj, 0))
    full = lambda shape: pl.BlockSpec(shape, lambda i: (0,) * len(shape))
    mem = pl.BlockSpec((1, N_MEM, D_MODEL), lambda i: (cur(i) // nj, 0, 0))
    vec = full((1, D_MODEL))
    n_part = MID_PARTS if tt % (MID_PARTS * SUBLANES * 2) == 0 else 1
    pr = tt // n_part
    grp = min(pr, ROUTE_GROUP)
    lower = jnp.tril(jnp.ones((pr, pr), F32), -1).astype(BF16)
    return pl.pallas_call(
        functools.partial(_mid_body, tt=tt, alpha=alpha),
        grid=(n_tiles + 1,),
        in_specs=[row(D_MODEL), row(N_HEADS * V_HEAD), row(CONV_WIDTH), mem, mem, full((1, LANES)), full((pr, pr)),
                  full((D_MODEL, D_MODEL)), vec, vec, full((D_MODEL, D_MODEL)), full((D_MODEL, D_MODEL)),
                  vec, vec, full((D_MODEL, LANES)), full((1, LANES))],
        out_specs=(row(D_MODEL), row_prev(LANES), full((1, LANES)),
                   pl.BlockSpec((1, tt // grp, SUBLANES, grp), lambda i: (prev(i) // nj, prev(i) % nj, 0, 0))),
        out_shape=(jax.ShapeDtypeStruct((b, s, D_MODEL), F32), jax.ShapeDtypeStruct((b, s, LANES), F32),
                   jax.ShapeDtypeStruct((1, LANES), F32),
                   jax.ShapeDtypeStruct((b, s // grp, SUBLANES, grp), F32)),
        scratch_shapes=[
            pltpu.VMEM((1, LANES), F32),
            pltpu.VMEM((tt, D_MODEL), F32),
            pltpu.VMEM((tt, D_MODEL), BF16),
            pltpu.VMEM((tt, D_MODEL), BF16),
            pltpu.VMEM((n_part, 2, pr, N_MEM), F32),
            pltpu.VMEM((n_part, 2, pr, N_MEM), BF16),
            pltpu.VMEM((tt, D_MODEL), BF16),
            pltpu.VMEM((2, tt, LANES), F32),
        ],
        compiler_params=_cp(("arbitrary",)),
        name="mid",
    )(x, attn, yconv, mk_b, mv_b, cnt0, lower, w['w_out'], w['ln1_g'], w['ln1_b'], w['w_xq'], w['w_xo'],
      w['ln2_g'], w['ln2_b'], w['w_router'], w['b_router'])


def _row_slice(ref, row):
    return ref.at[pl.ds(pl.multiple_of(row * SUBLANES, SUBLANES), SUBLANES), :]


def _slot_index(t, kk, g, n):
    if g >= n:
        return kk * g + t
    shift = g.bit_length() - 1
    return ((t >> shift) << (shift + 1)) + kk * g + (t & (g - 1))


def _rows_slice(ref, row, n_rows):
    return ref.at[pl.ds(pl.multiple_of(row * SUBLANES, SUBLANES), n_rows * SUBLANES), :]


def _dispatch_body(pad_ref, dest_ref, dest_s_ref, x_ref, x_s_ref, xs_ref, buf0, buf1, zbuf, sem0, sem1, zsem,
                   *, tt, ts, nt, nblk, rows, g_p, g_s):
    i = pl.program_id(0)
    bufs = (buf0, buf1)
    sems = (sem0, sem1)
    len_bits = rows.bit_length() - 1

    def scatter_rows(buf, sem, src_ref, d_ref, n, g):
        for c in range(ROW_CHUNKS):
            buf[pl.ds(c, n, stride=SUBLANES), :] = src_ref[:, c * LANES:(c + 1) * LANES]
        unroll = min(n, GATHER_UNROLL)

        def issue(j, carry):
            for u in range(unroll):
                t = j * unroll + u
                src = _row_slice(buf, t)
                for kk in range(2):
                    d = d_ref[0, 0, _slot_index(t, kk, g, n)]
                    pltpu.make_async_copy(src, _row_slice(xs_ref, d), sem).start(priority=kk)
            return carry

        lax.fori_loop(0, n // unroll, issue, 0)

    def wait_rows(buf, sem, n):
        for _ in range(2):
            pltpu.make_async_copy(_rows_slice(buf, 0, n), _rows_slice(xs_ref, 0, n), sem).wait()

    def zero_fill(wait):
        def fire(copy):
            if wait:
                copy.wait()
            else:
                copy.start()

        def per_expert(e, carry):
            first = pad_ref[e]
            n_pad = pad_ref[N_EXPERTS + e]
            for bit in range(len_bits):
                size = 1 << bit
                off = (n_pad >> (bit + 1)) << (bit + 1)

                @pl.when(((n_pad >> bit) & 1) == 1)
                def _():
                    fire(pltpu.make_async_copy(_rows_slice(zbuf, 0, size), _rows_slice(xs_ref, first + off, size),
                                               zsem))
            return carry

        lax.fori_loop(0, N_EXPERTS, per_expert, 0)

        def per_block(j, carry):
            fire(pltpu.make_async_copy(zbuf, _rows_slice(xs_ref, j * rows, rows), zsem))
            return carry

        lax.fori_loop(pad_ref[2 * N_EXPERTS], nblk, per_block, 0)

    def run(slot):
        @pl.when(i < nt)
        def _():
            scatter_rows(bufs[slot], sems[slot], x_ref, dest_ref, tt, g_p)

        @pl.when(i == nt)
        def _():
            zbuf[...] = jnp.zeros(zbuf.shape, F32)
            scatter_rows(bufs[slot], sems[slot], x_s_ref, dest_s_ref, ts, g_s)
            zero_fill(False)

        @pl.when(i > 0)
        def _():
            wait_rows(bufs[1 - slot], sems[1 - slot], tt)

        @pl.when(i == nt)
        def _():
            wait_rows(bufs[slot], sems[slot], ts)
            zero_fill(True)

    @pl.when(i % 2 == 0)
    def _():
        run(0)

    @pl.when(i % 2 == 1)
    def _():
        run(1)


def _dispatch(x_p, dest_p, g_p, x_s, dest_s, g_s, pad_info, nblk, rows):
    n_p, n_s = x_p.shape[0], x_s.shape[0]
    tt = _pick_tile(n_p, 256)
    nt = n_p // tt
    assert n_s <= tt
    last = lambda i, pad: jnp.minimum(i, nt - 1)
    grid_spec = pltpu.PrefetchScalarGridSpec(
        num_scalar_prefetch=1,
        grid=(nt + 1,),
        in_specs=[
            pl.BlockSpec((1, 1, 2 * tt), lambda i, pad: (last(i, pad), 0, 0), memory_space=pltpu.SMEM),
            pl.BlockSpec((1, 1, 2 * n_s), lambda i, pad: (0, 0, 0), memory_space=pltpu.SMEM),
            pl.BlockSpec((tt, D_MODEL), lambda i, pad: (last(i, pad), 0)),
            pl.BlockSpec((n_s, D_MODEL), lambda i, pad: (0, 0)),
        ],
        out_specs=pl.BlockSpec(memory_space=pl.ANY),
        scratch_shapes=[pltpu.VMEM((tt * SUBLANES, LANES), F32), pltpu.VMEM((tt * SUBLANES, LANES), F32),
                        pltpu.VMEM((rows * SUBLANES, LANES), F32),
                        pltpu.SemaphoreType.DMA, pltpu.SemaphoreType.DMA, pltpu.SemaphoreType.DMA],
    )
    return pl.pallas_call(
        functools.partial(_dispatch_body, tt=tt, ts=n_s, nt=nt, nblk=nblk, rows=rows, g_p=g_p, g_s=g_s),
        grid_spec=grid_spec,
        out_shape=jax.ShapeDtypeStruct((nblk * rows * SUBLANES, LANES), F32),
        compiler_params=_cp(("arbitrary",)),
        name="moe_dispatch",
    )(pad_info, dest_p.reshape(nt, 1, 2 * tt), dest_s.reshape(1, 1, 2 * n_s), x_p, x_s)


def _bf16_bits(x):
    return lax.bitcast_convert_type(x.astype(BF16).astype(F32), jnp.uint32)


def _experts_body(be_ref, nu_ref, x_ref, wg_ref, wu_ref, wd_ref, y_ref, xb_scr, g_scr, u_scr, h_scr,
                  wgb_scr, wub_scr, wdb_scr, *, rows):
    i = pl.program_id(0)
    cur = jnp.minimum(i, nu_ref[0] - 1)
    new_expert = (i == 0) | (be_ref[cur] != be_ref[jnp.maximum(cur - 1, 0)])

    @pl.when((i < nu_ref[0]) & new_expert)
    def _():
        for r in range(0, D_MODEL, WCAST_ROWS):
            wgb_scr[r:r + WCAST_ROWS, :] = wg_ref[0, r:r + WCAST_ROWS, :].astype(BF16)
            wub_scr[r:r + WCAST_ROWS, :] = wu_ref[0, r:r + WCAST_ROWS, :].astype(BF16)
        for r in range(0, D_EXPERT, WCAST_ROWS // 2):
            wdb_scr[r:r + WCAST_ROWS // 2, :] = wd_ref[0, r:r + WCAST_ROWS // 2, :].astype(BF16)
    n_part = MOE_PARTS
    pr = rows // n_part
    act_rs = min(pr, ACT_ROW_SLAB)

    def load(k):
        base = k * pr * SUBLANES
        xb_scr[k] = jnp.concatenate([x_ref[pl.ds(base + c, pr, stride=SUBLANES), :] for c in range(ROW_CHUNKS)],
                                    axis=-1).astype(BF16)

    def gate_up(k):
        g_scr[k] = jnp.dot(xb_scr[k], wgb_scr[...], preferred_element_type=F32)
        u_scr[k] = jnp.dot(xb_scr[k], wub_scr[...], preferred_element_type=F32)

    def act(k):
        for r in range(pr // act_rs):
            rows_r = slice(r * act_rs, (r + 1) * act_rs)
            g = g_scr[k, rows_r, :]
            h_scr[k, rows_r, :] = ((g * jax.nn.sigmoid(g)) * u_scr[k, rows_r, :]).astype(BF16)

    def down(k):
        y = jnp.dot(h_scr[k], wdb_scr[...], preferred_element_type=F32)
        for c in range(PACKED_CHUNKS):
            lo = y[:, c * LANES:(c + 1) * LANES]
            hi = y[:, (PACKED_CHUNKS + c) * LANES:(PACKED_CHUNKS + c + 1) * LANES]
            y_ref[pl.ds(k * pr, pr), c, :] = (_bf16_bits(lo) >> 16) | (_bf16_bits(hi) & jnp.uint32(0xFFFF0000))

    @pl.when(i < nu_ref[0])
    def _():
        stages = (load, gate_up, act, down)
        matmul_stages = (gate_up, down)
        for t in range(len(stages) + n_part - 1):
            todo = [(stages[t - k], k) for k in range(n_part) if 0 <= t - k < len(stages)]
            for fn, k in sorted(todo, key=lambda fk: fk[0] not in matmul_stages):
                fn(k)

    @pl.when(i >= nu_ref[0])
    def _():
        y_ref[...] = jnp.zeros(y_ref.shape, jnp.uint32)


def _experts(xs, block_e, n_used, wg, wu, wd, rows):
    nblk = xs.shape[0] // (rows * SUBLANES)
    clamp = lambda i, nu: jnp.minimum(i, nu[0] - 1)
    grid_spec = pltpu.PrefetchScalarGridSpec(
        num_scalar_prefetch=2,
        grid=(nblk,),
        in_specs=[
            pl.BlockSpec((rows * SUBLANES, LANES), lambda i, be, nu: (clamp(i, nu), 0)),
            pl.BlockSpec((1, D_MODEL, D_EXPERT), lambda i, be, nu: (be[clamp(i, nu)], 0, 0)),
            pl.BlockSpec((1, D_MODEL, D_EXPERT), lambda i, be, nu: (be[clamp(i, nu)], 0, 0)),
            pl.BlockSpec((1, D_EXPERT, D_MODEL), lambda i, be, nu: (be[clamp(i, nu)], 0, 0)),
        ],
        out_specs=pl.BlockSpec((rows, PACKED_CHUNKS, LANES), lambda i, be, nu: (i, 0, 0)),
        scratch_shapes=[
            pltpu.VMEM((MOE_PARTS, rows // MOE_PARTS, D_MODEL), BF16),
            pltpu.VMEM((MOE_PARTS, rows // MOE_PARTS, D_EXPERT), F32),
            pltpu.VMEM((MOE_PARTS, rows // MOE_PARTS, D_EXPERT), F32),
            pltpu.VMEM((MOE_PARTS, rows // MOE_PARTS, D_EXPERT), BF16),
            pltpu.VMEM((D_MODEL, D_EXPERT), BF16),
            pltpu.VMEM((D_MODEL, D_EXPERT), BF16),
            pltpu.VMEM((D_EXPERT, D_MODEL), BF16),
        ],
    )
    return pl.pallas_call(
        functools.partial(_experts_body, rows=rows),
        grid_spec=grid_spec,
        out_shape=jax.ShapeDtypeStruct((nblk * rows, PACKED_CHUNKS, LANES), jnp.uint32),
        compiler_params=_cp(("arbitrary",)),
        name="moe_experts",
    )(block_e, n_used, xs, wg, wu, wd)


def _unpack_rows(buf, rows):
    words = [buf[rows, c, :] for c in range(PACKED_CHUNKS)]
    low = [lax.bitcast_convert_type(w << 16, F32) for w in words]
    high = [lax.bitcast_convert_type(w & jnp.uint32(0xFFFF0000), F32) for w in words]
    return jnp.concatenate(low + high, axis=-1)


def _combine_body(dest_ref, destn_ref, x_ref, ri_ref, g3_ref, b3_ref, ys_ref, o_ref,
                  b00, b01, b10, b11, sem0, sem1, *, tt, nt, g, alpha):
    i = pl.program_id(0)
    bufs = ((b00, b01), (b10, b11))
    sems = (sem0, sem1)
    rs = min(tt, LN_ROW_SLAB)
    unroll = min(tt, GATHER_UNROLL)

    def gather_rows(dref, slot):
        def copy(t, kk):
            d = dref[0, 0, _slot_index(t, kk, g, tt)]
            return pltpu.make_async_copy(ys_ref.at[d], bufs[slot][kk].at[t], sems[slot])

        def body(j, carry):
            for u in range(unroll):
                for kk in range(2):
                    copy(j * unroll + u, kk).start(priority=kk)
            return carry

        lax.fori_loop(0, tt // unroll, body, 0)

    def wait_rows(slot):
        for kk in range(2):
            pltpu.make_async_copy(ys_ref.at[pl.ds(0, tt)], bufs[slot][kk], sems[slot]).wait()

    def run(slot):
        if slot == 0:
            @pl.when(i == 0)
            def _():
                gather_rows(dest_ref, 0)

        @pl.when(i + 1 < nt)
        def _():
            gather_rows(destn_ref, 1 - slot)

        wait_rows(slot)
        for r in range(tt // rs):
            rows = slice(r * rs, (r + 1) * rs)
            y0 = _unpack_rows(bufs[slot][0], rows)
            y1 = _unpack_rows(bufs[slot][1], rows)
            ri = ri_ref[rows, :]
            moe = y0 * ri[:, 4:5] + y1 * ri[:, 5:6]
            o_ref[rows, :] = _layer_norm(alpha * x_ref[rows, :] + moe, g3_ref[...], b3_ref[...])

    @pl.when(i % 2 == 0)
    def _():
        run(0)

    @pl.when(i % 2 == 1)
    def _():
        run(1)


def _combine(x2d, rinfo, dest, g, ys, g3, b3, alpha):
    n = x2d.shape[0]
    tt = _pick_tile(n, 256)
    nt = n // tt
    dest3 = dest.reshape(nt, 1, 2 * tt)
    vec = pl.BlockSpec((1, D_MODEL), lambda i: (0, 0))
    stage = pltpu.VMEM((tt, PACKED_CHUNKS, LANES), jnp.uint32)
    return pl.pallas_call(
        functools.partial(_combine_body, tt=tt, nt=nt, g=g, alpha=alpha),
        grid=(nt,),
        in_specs=[
            pl.BlockSpec((1, 1, 2 * tt), lambda i: (i, 0, 0), memory_space=pltpu.SMEM),
            pl.BlockSpec((1, 1, 2 * tt), lambda i: (jnp.minimum(i + 1, nt - 1), 0, 0), memory_space=pltpu.SMEM),
            pl.BlockSpec((tt, D_MODEL), lambda i: (i, 0)),
            pl.BlockSpec((tt, LANES), lambda i: (i, 0)),
            vec, vec,
            pl.BlockSpec(memory_space=pl.ANY),
        ],
        out_specs=pl.BlockSpec((tt, D_MODEL), lambda i: (i, 0)),
        out_shape=jax.ShapeDtypeStruct((n, D_MODEL), F32),
        scratch_shapes=[stage, stage, stage, stage, pltpu.SemaphoreType.DMA, pltpu.SemaphoreType.DMA],
        compiler_params=_cp(("arbitrary",)),
        name="moe_combine",
    )(dest3, dest3, x2d, rinfo, g3, b3, ys)


def _rope_tables(pos):
    half = QK_ROPE // 2
    inv = ROPE_THETA ** (-jnp.arange(half, dtype=F32) / half)
    ang = pos.astype(F32)[:, None] * inv[None, :]
    cos, sin = jnp.cos(ang), jnp.sin(ang)
    n = pos.shape[0]
    pad_r = LANES - ROPE_LANE0 - QK_ROPE
    cos_t = jnp.concatenate([jnp.ones((n, ROPE_LANE0), F32), cos, cos, jnp.zeros((n, pad_r), F32)], -1)
    sin_t = jnp.concatenate([jnp.zeros((n, ROPE_LANE0), F32), sin, sin, jnp.zeros((n, pad_r), F32)], -1)
    return cos_t, sin_t


def _swap_neg(wr):
    half = QK_ROPE // 2
    return jnp.concatenate([-wr[:, half:], wr[:, :half]], axis=1)


def _prep_weights(l, w_in, q_norm_g, kv_norm_g, w_uq, w_ukv, conv_w, w_out, ln1_g, ln1_b, w_xq, w_xk, w_xv, w_xo,
                  ln2_g, ln2_b, w_router_group, b_router_group, w_router_expert, b_router_expert,
                  w_exp_gate, w_exp_up, w_exp_down, ln3_g, ln3_b):
    wi = w_in[l]
    c0 = Q_LORA + KV_LORA
    w_kr = wi[:, c0:c0 + QK_ROPE]
    zl = jnp.zeros((D_MODEL, ROPE_LANE0), F32)
    w_in_p = jnp.concatenate([wi[:, :c0], wi[:, c0 + QK_ROPE:], zl, w_kr, _swap_neg(w_kr)], axis=1)
    wq = w_uq[l].reshape(Q_LORA, N_HEADS, QK_NOPE + QK_ROPE)
    wq_rot = jnp.concatenate([-wq[:, :, QK_NOPE + QK_ROPE // 2:], wq[:, :, QK_NOPE:QK_NOPE + QK_ROPE // 2]], axis=2)
    wq_a = jnp.concatenate([wq, wq_rot], axis=2).reshape(Q_LORA, N_HEADS * HEAD_PAD)
    wkv = w_ukv[l].reshape(KV_LORA, N_HEADS, QK_NOPE + V_HEAD)
    wk_p = jnp.concatenate([wkv[:, :, :QK_NOPE], jnp.zeros((KV_LORA, N_HEADS, HEAD_PAD - QK_NOPE), F32)], axis=2)
    wk_p = wk_p.reshape(KV_LORA, N_HEADS * HEAD_PAD)
    wv_p = jnp.concatenate([wkv[:, :, QK_NOPE:], jnp.zeros((KV_LORA, N_HEADS, HEAD_PAD - V_HEAD), F32)], axis=2)
    wv_p = wv_p.reshape(KV_LORA, N_HEADS * HEAD_PAD)
    w_router = jnp.concatenate([w_router_group[l], w_router_expert[l],
                                jnp.zeros((D_MODEL, LANES - N_GROUPS - N_EXPERTS), F32)], axis=1)
    b_router = jnp.concatenate([b_router_group[l], b_router_expert[l].reshape(-1),
                                jnp.zeros((LANES - N_GROUPS - N_EXPERTS,), F32)]).reshape(1, LANES)
    return dict(
        w_in=w_in_p.astype(BF16),
        q_norm_g=q_norm_g[l].reshape(1, Q_LORA), kv_norm_g=kv_norm_g[l].reshape(1, KV_LORA),
        w_uq=wq_a.astype(BF16),
        w_ukv=jnp.concatenate([wk_p, wv_p], axis=1).astype(BF16),
        conv_w=conv_w[l],
        w_out=w_out[l].astype(BF16), ln1_g=ln1_g[l].reshape(1, -1), ln1_b=ln1_b[l].reshape(1, -1),
        w_xq=w_xq[l].astype(BF16), w_xk=w_xk[l].astype(BF16), w_xv=w_xv[l].astype(BF16),
        w_xo=w_xo[l].astype(BF16), ln2_g=ln2_g[l].reshape(1, -1), ln2_b=ln2_b[l].reshape(1, -1),
        w_router=w_router.astype(BF16), b_router=b_router,
        w_exp_gate=w_exp_gate[l], w_exp_up=w_exp_up[l], w_exp_down=w_exp_down[l],
        ln3_g=ln3_g[l].reshape(1, -1), ln3_b=ln3_b[l].reshape(1, -1),
    )


def _slots(rt, pstarts):
    g = rt.shape[-1]
    rt = rt.reshape(-1, SUBLANES, g)
    dest = pstarts[rt[:, 0:2, :].astype(I32)] + rt[:, 2:4, :].astype(I32)
    return dest.reshape(-1), g


def _layer(l, depth, xp, xs, lat_past, kr_past, conv_past, mk_s, mv_s, mem_prompt, w):
    alpha = (2 * depth) ** 0.25
    b, s, _ = xp.shape
    bs, ss, _ = xs.shape
    past = lat_past.shape[1]

    cos_p, sin_p = _rope_tables(jnp.arange(s))
    q_p, k_p, v_p, yc_p, lat_p, kr_p, cst_p = _inproj(
        xp, jnp.zeros((b, CONV_K - 1, CONV_WIDTH), F32), cos_p, sin_p, w)
    attn_p = _attention(q_p, k_p, v_p, 0, None)
    mk, mv, mk_b, mv_b = _memkv(mem_prompt.reshape(b * N_MEM, D_MODEL), w['w_xk'], w['w_xv'])
    cnt0 = jnp.zeros((1, LANES), F32)
    x2_p, ri_p, cnt_p, rt_p = _mid(xp, attn_p, yc_p, mk_b.reshape(b, N_MEM, D_MODEL), mv_b.reshape(b, N_MEM, D_MODEL),
                             cnt0, w, alpha)

    cos_s, sin_s = _rope_tables(past + jnp.arange(ss))
    q_s, _, _, yc_s, lat_s, kr_s, cst_s = _inproj(xs, conv_past, cos_s, sin_s, w)
    n_keys = past + ss
    sk = -(-n_keys // LANES) * LANES
    lat_all = jnp.concatenate([lat_past, lat_s, jnp.zeros((bs, sk - n_keys, KV_LORA), F32)], axis=1)
    kr_all = jnp.concatenate([kr_past, kr_s, jnp.zeros((bs, sk - n_keys, QK_ROPE), F32)], axis=1)
    kr_all = jnp.pad(kr_all, ((0, 0), (0, 0), (ROPE_LANE0, LANES - ROPE_LANE0 - QK_ROPE)))
    attn_s = _decode_attention(q_s, lat_all, kr_all, w['w_ukv'], n_keys)
    mk_sb = mk_s.reshape(bs, N_MEM, D_MODEL).astype(BF16)
    mv_sb = mv_s.reshape(bs, N_MEM, D_MODEL).astype(BF16)
    x2_s, ri_s, cnt, rt_s = _mid(xs, attn_s, yc_s, mk_sb, mv_sb, cnt_p, w, alpha)

    n_p, n_s = b * s, bs * ss
    counts = cnt[0, ROUTER_LANE0:ROUTER_LANE0 + N_EXPERTS].astype(I32)
    padded = (counts + MOE_ROWS - 1) // MOE_ROWS * MOE_ROWS
    pends = jnp.cumsum(padded)
    pstarts = pends - padded
    nblk = -(-2 * (n_p + n_s) // MOE_ROWS) + N_EXPERTS
    blk_start = jnp.arange(nblk, dtype=I32) * MOE_ROWS
    block_e = jnp.minimum(jnp.sum((pends[None, :] <= blk_start[:, None]).astype(I32), axis=1), N_EXPERTS - 1)
    n_used = (pends[-1] // MOE_ROWS).astype(I32).reshape(1)
    ri_p2, ri_s2 = ri_p.reshape(n_p, LANES), ri_s.reshape(n_s, LANES)
    dest_p, g_p = _slots(rt_p, pstarts)
    dest_s, g_s = _slots(rt_s, pstarts)
    x2_p2, x2_s2 = x2_p.reshape(n_p, D_MODEL), x2_s.reshape(n_s, D_MODEL)
    pad_info = jnp.concatenate([pstarts + counts, padded - counts, n_used]).astype(I32)
    slots = _dispatch(x2_p2, dest_p, g_p, x2_s2, dest_s, g_s, pad_info, nblk, MOE_ROWS)
    ys = _experts(slots, block_e, n_used, w['w_exp_gate'], w['w_exp_up'], w['w_exp_down'], MOE_ROWS)
    y_p = _combine(x2_p2, ri_p2, dest_p, g_p, ys, w['ln3_g'], w['ln3_b'], alpha).reshape(b, s, D_MODEL)
    y_s = _combine(x2_s2, ri_s2, dest_s, g_s, ys, w['ln3_g'], w['ln3_b'], alpha).reshape(bs, ss, D_MODEL)
    return (y_p, y_s, lat_p, kr_p, cst_p, mk.reshape(b, N_MEM, X_HEADS, X_HEAD_DIM),
            mv.reshape(b, N_MEM, X_HEADS, X_HEAD_DIM), lat_s, kr_s, cst_s)


def kernel(x_prompt, x_sample, cache_kv_latent, cache_k_rope, cache_conv, cache_mem_k, cache_mem_v, mem_prompt,
           w_in, q_norm_g, kv_norm_g, w_uq, w_ukv, conv_w, w_out, ln1_g, ln1_b, w_xq, w_xk, w_xv, w_xo, ln2_g,
           ln2_b, w_router_group, b_router_group, w_router_expert, b_router_expert, w_exp_gate, w_exp_up,
           w_exp_down, ln3_g, ln3_b):
    depth = w_in.shape[0]
    xp, xs = x_prompt, x_sample
    outs = [[] for _ in range(8)]
    for l in range(depth):
        w = _prep_weights(l, w_in, q_norm_g, kv_norm_g, w_uq, w_ukv, conv_w, w_out, ln1_g, ln1_b, w_xq, w_xk, w_xv,
                          w_xo, ln2_g, ln2_b, w_router_group, b_router_group, w_router_expert, b_router_expert,
                          w_exp_gate, w_exp_up, w_exp_down, ln3_g, ln3_b)
        res = _layer(l, depth, xp, xs, cache_kv_latent[l], cache_k_rope[l], cache_conv[l], cache_mem_k[l],
                     cache_mem_v[l], mem_prompt, w)
        xp, xs = res[0], res[1]
        for acc, r in zip(outs, res[2:]):
            acc.append(r)
    return (xp, xs) + tuple(jnp.stack(o) for o in outs)
```

```python
import functools
import math

import numpy as np
import jax
import jax.numpy as jnp
from jax import lax
from jax.experimental import pallas as pl
from jax.experimental.pallas import tpu as pltpu

F32 = jnp.float32
BF16 = jnp.bfloat16
I32 = jnp.int32

D_MODEL = 1024
CHUNK = 64
N_HEADS = 8
QK_NOPE = 64
QK_ROPE = 32
V_HEAD = 64
Q_LORA = 256
KV_LORA = 128
ROPE_THETA = 10000.0
MLA_SCALE = (QK_NOPE + QK_ROPE) ** -0.5
Q_PRESCALE = MLA_SCALE * math.log2(math.e)
CONV_WIDTH = 512
CONV_K = 3
N_MEM = 256
X_HEADS = 4
X_HEAD_DIM = D_MODEL // X_HEADS
X_SCALE = X_HEAD_DIM ** -0.5
N_GROUPS = 4
EXPERTS_PER_GROUP = 8
N_EXPERTS = N_GROUPS * EXPERTS_PER_GROUP
D_EXPERT = 512
LN_EPS = 1e-5
RMS_EPS = 1e-6

LANES = 128
SUBLANES = 8
ROW_CHUNKS = D_MODEL // LANES
HEAD_PAD = LANES
ROPE_LANE0 = QK_NOPE
ROUTER_LANE0 = N_GROUPS
VMEM_LIMIT = 56 * 1024 * 1024
MOE_ROWS = 512
MOE_PARTS = 2
ACT_ROW_SLAB = 32
GATHER_UNROLL = 8
WCAST_ROWS = 64
ATTN_TQ = 2048
ATTN_TK = 512
ATTN_SUB = 4
ATTN_ROW_SLAB = 64
LN_ROW_SLAB = 16
XATTN_ROW_SLAB = 64
INPROJ_ROWS = 512
ROUTE_GROUP = 256
MID_SKEW = 1
MID_ROWS = 512
MID_PARTS = 2


def _cp(sem, vmem=VMEM_LIMIT):
    return pltpu.CompilerParams(dimension_semantics=sem, vmem_limit_bytes=vmem)


def _pick_tile(n, pref):
    t = min(n, pref)
    while n % t:
        t //= 2
    return t


def _with_ones_lane(v):
    lane = lax.broadcasted_iota(I32, v.shape, 1)
    return jnp.where((lane & (HEAD_PAD - 1)) == V_HEAD, 1.0, v)


def _inproj_body(x_ref, cinit_ref, cos_ref, sin_ref, win_ref, qg_ref, kvg_ref, wuq_ref, wukv_ref, cw_ref,
                 q_ref, k_ref, v_ref, yc_ref, lat_ref, kr_ref, cst_ref, u_scr, *, tt, nj):
    j = pl.program_id(1)

    @pl.when(j == 0)
    def _():
        u_scr[6:8, :] = cinit_ref[0]

    x = x_ref[0].astype(BF16)
    proj = jnp.dot(x, win_ref[...], preferred_element_type=F32)
    cq = proj[:, 0:256]
    ckv = proj[:, 256:384]
    gb = proj[:, 384:896]
    gc = proj[:, 896:1408]
    gv = proj[:, 1408:1920]
    kr_blk = proj[:, 1920:2048]
    cos_t = cos_ref[...]
    sin_t = sin_ref[...]

    def rotate(blk):
        return blk * cos_t + pltpu.roll(blk, LANES - QK_ROPE, 1) * sin_t

    cqn = cq * lax.rsqrt(jnp.mean(cq * cq, -1, keepdims=True) + RMS_EPS) * qg_ref[...]
    ckvn = ckv * lax.rsqrt(jnp.mean(ckv * ckv, -1, keepdims=True) + RMS_EPS) * kvg_ref[...]
    lat_ref[0] = ckvn
    kr_p = rotate(kr_blk)
    kr_ref[0] = kr_p[:, ROPE_LANE0:ROPE_LANE0 + QK_ROPE]
    qq = jnp.dot(cqn.astype(BF16), wuq_ref[...], preferred_element_type=F32)
    kv = jnp.dot(ckvn.astype(BF16), wukv_ref[...], preferred_element_type=F32)
    hw = N_HEADS * HEAD_PAD
    for h in range(N_HEADS):
        sl = slice(h * HEAD_PAD, (h + 1) * HEAD_PAD)
        q_ref[0, :, sl] = (rotate(qq[:, sl]) * Q_PRESCALE).astype(BF16)
        k_ref[0, :, sl] = (kv[:, sl] + kr_p).astype(BF16)
    v_ref[0] = _with_ones_lane(kv[:, hw:2 * hw]).astype(BF16)
    u = gc * gv
    u_scr[8:8 + tt, :] = u
    conv = cw_ref[0:1, :] * u_scr[6:6 + tt, :] + cw_ref[1:2, :] * u_scr[7:7 + tt, :] + cw_ref[2:3, :] * u
    yc_ref[0] = (gb * conv).astype(BF16)
    last2 = u_scr[tt + 6:tt + 8, :]
    u_scr[6:8, :] = last2

    @pl.when(j == nj - 1)
    def _():
        cst_ref[0] = last2


def _inproj(x, conv_init, cos_t, sin_t, w):
    b, s, _ = x.shape
    tt = _pick_tile(s, INPROJ_ROWS)
    nj = s // tt
    wn = w['w_in'].shape[1]
    full = lambda shape: pl.BlockSpec(shape, lambda bi, ji: (0,) * len(shape))
    out_shapes = (
        jax.ShapeDtypeStruct((b, s, N_HEADS * HEAD_PAD), BF16),
        jax.ShapeDtypeStruct((b, s, N_HEADS * HEAD_PAD), BF16),
        jax.ShapeDtypeStruct((b, s, N_HEADS * HEAD_PAD), BF16),
        jax.ShapeDtypeStruct((b, s, CONV_WIDTH), BF16),
        jax.ShapeDtypeStruct((b, s, KV_LORA), F32),
        jax.ShapeDtypeStruct((b, s, QK_ROPE), F32),
        jax.ShapeDtypeStruct((b, CONV_K - 1, CONV_WIDTH), F32),
    )
    row = lambda width: pl.BlockSpec((1, tt, width), lambda bi, ji: (bi, ji, 0))
    return pl.pallas_call(
        functools.partial(_inproj_body, tt=tt, nj=nj),
        grid=(b, nj),
        in_specs=[
            row(D_MODEL),
            pl.BlockSpec((1, CONV_K - 1, CONV_WIDTH), lambda bi, ji: (bi, 0, 0)),
            pl.BlockSpec((tt, LANES), lambda bi, ji: (ji, 0)),
            pl.BlockSpec((tt, LANES), lambda bi, ji: (ji, 0)),
            full((D_MODEL, wn)),
            full((1, Q_LORA)),
            full((1, KV_LORA)),
            full(w['w_uq'].shape),
            full(w['w_ukv'].shape),
            full((CONV_K, CONV_WIDTH)),
        ],
        out_specs=(
            row(N_HEADS * HEAD_PAD), row(N_HEADS * HEAD_PAD), row(N_HEADS * HEAD_PAD), row(CONV_WIDTH),
            row(KV_LORA), row(QK_ROPE),
            pl.BlockSpec((1, CONV_K - 1, CONV_WIDTH), lambda bi, ji: (bi, 0, 0)),
        ),
        out_shape=out_shapes,
        scratch_shapes=[pltpu.VMEM((tt + 8, CONV_WIDTH), F32)],
        compiler_params=_cp(("parallel", "arbitrary")),
        name="inproj",
    )(x, conv_init, cos_t, sin_t, w['w_in'], w['q_norm_g'], w['kv_norm_g'], w['w_uq'], w['w_ukv'], w['conv_w'])


def _attn_body(qi_ref, ki_ref, fl_ref, q_ref, k_ref, v_ref, o_ref, m_scr, acc_scr, s_scr, p_scr, a_scr,
               *, tq, tk, n_sub, rs, q_pos0, n_valid, combos):
    step = pl.program_id(1)
    qi = qi_ref[step]
    ki = ki_ref[step]
    flags = fl_ref[step]
    reps = tk // LANES
    th = tq // n_sub

    @pl.when((flags & 1) != 0)
    def _():
        m_scr[...] = jnp.full(m_scr.shape, -jnp.inf, F32)
        acc_scr[...] = jnp.zeros(acc_scr.shape, F32)

    def scores(item, buf):
        sub, h, _ = item
        hs = slice(h * HEAD_PAD, (h + 1) * HEAD_PAD)
        s_scr[buf] = lax.dot_general(q_ref[0, sub * th:(sub + 1) * th, hs], k_ref[0, :, hs],
                                     (((1,), (1,)), ((), ())), preferred_element_type=F32)

    def softmax_pv(item, buf):
        sub, h, masked = item
        hs = slice(h * HEAD_PAD, (h + 1) * HEAD_PAD)
        for r in range(th // rs):
            rows = slice(r * rs, (r + 1) * rs)
            arows = slice(sub * th + r * rs, sub * th + (r + 1) * rs)
            s_r = s_scr[buf, rows, :]
            if masked:
                qpos = q_pos0 + qi * tq + sub * th + r * rs + lax.broadcasted_iota(I32, (rs, tk), 0)
                kpos = ki * tk + lax.broadcasted_iota(I32, (rs, tk), 1)
                mask = (kpos >> 6) <= (qpos >> 6)
                if n_valid is not None:
                    mask = mask & (kpos < n_valid)
                s_r = jnp.where(mask, s_r, -jnp.inf)
            m_old = m_scr[h, arows, :]
            m_new = jnp.maximum(m_old, jnp.max(s_r, axis=-1, keepdims=True))
            a_scr[buf, rows, :] = jnp.exp2(m_old - m_new)
            m_rep = jnp.concatenate([m_new] * reps, axis=1)
            p_scr[buf, rows, :] = jnp.exp2(s_r - m_rep).astype(BF16)
            m_scr[h, arows, :] = m_new
        pv = jnp.dot(p_scr[buf], v_ref[0, :, hs], preferred_element_type=F32)
        srows = slice(sub * th, (sub + 1) * th)
        acc_scr[h, srows, :] = a_scr[buf] * acc_scr[h, srows, :] + pv

    def run(modes):
        items = [(sub, h, mode == 2) for sub, mode in enumerate(modes) if mode != 0 for h in range(N_HEADS)]
        scores(items[0], 0)
        for n, item in enumerate(items):
            if n + 1 < len(items):
                scores(items[n + 1], (n + 1) % 2)
            softmax_pv(item, n % 2)

    for code, modes in combos:
        @pl.when((flags >> 2) == code)
        def _(modes=modes):
            run(modes)

    @pl.when((flags & 2) != 0)
    def _():
        for h in range(N_HEADS):
            acc = acc_scr[h]
            o_ref[0, :, h * V_HEAD:(h + 1) * V_HEAD] = (acc[:, 0:V_HEAD] / acc[:, V_HEAD:V_HEAD + 1]).astype(BF16)


def _attn_tables(nq, nk, tq, tk, n_sub, q_pos0, n_valid):
    th = tq // n_sub
    qi_l, ki_l, fl_l, combos = [], [], [], {}
    for qi in range(nq):
        sub_lo = [q_pos0 + qi * tq + j * th for j in range(n_sub)]
        sub_last = []
        for lo in sub_lo:
            last_pos = ((lo + th - 1) // CHUNK) * CHUNK + CHUNK - 1
            if n_valid is not None:
                last_pos = min(last_pos, n_valid - 1)
            sub_last.append(min(nk - 1, last_pos // tk))
        k_last = max(sub_last)
        for ki in range(k_last + 1):
            k_hi = ki * tk + tk - 1
            modes = []
            for lo, last in zip(sub_lo, sub_last):
                if ki > last:
                    modes.append(0)
                elif (k_hi // CHUNK) > (lo // CHUNK) or (n_valid is not None and k_hi >= n_valid):
                    modes.append(2)
                else:
                    modes.append(1)
            code = sum(m * 3 ** j for j, m in enumerate(modes))
            combos[code] = tuple(modes)
            qi_l.append(qi); ki_l.append(ki)
            fl_l.append((1 if ki == 0 else 0) | (2 if ki == k_last else 0) | (code << 2))
    to_arr = lambda vals: jnp.asarray(np.array(vals, np.int32))
    return to_arr(qi_l), to_arr(ki_l), to_arr(fl_l), tuple(sorted(combos.items()))


def _attention(q, k, v, q_pos0, n_valid):
    b, sq, _ = q.shape
    sk = k.shape[1]
    tq = _pick_tile(sq, ATTN_TQ)
    tk = _pick_tile(sk, ATTN_TK)
    n_sub = ATTN_SUB if tq % (ATTN_SUB * 2 * SUBLANES) == 0 else 1
    th = tq // n_sub
    qi_t, ki_t, fl_t, combos = _attn_tables(sq // tq, sk // tk, tq, tk, n_sub, q_pos0, n_valid)
    n_steps = int(qi_t.shape[0])
    grid_spec = pltpu.PrefetchScalarGridSpec(
        num_scalar_prefetch=3,
        grid=(b, n_steps),
        in_specs=[
            pl.BlockSpec((1, tq, N_HEADS * HEAD_PAD), lambda bi, si, qt, kt, ft: (bi, qt[si], 0)),
            pl.BlockSpec((1, tk, N_HEADS * HEAD_PAD), lambda bi, si, qt, kt, ft: (bi, kt[si], 0)),
            pl.BlockSpec((1, tk, N_HEADS * HEAD_PAD), lambda bi, si, qt, kt, ft: (bi, kt[si], 0)),
        ],
        out_specs=pl.BlockSpec((1, tq, N_HEADS * V_HEAD), lambda bi, si, qt, kt, ft: (bi, qt[si], 0)),
        scratch_shapes=[
            pltpu.VMEM((N_HEADS, tq, LANES), F32),
            pltpu.VMEM((N_HEADS, tq, LANES), F32),
            pltpu.VMEM((2, th, tk), F32),
            pltpu.VMEM((2, th, tk), BF16),
            pltpu.VMEM((2, th, LANES), F32),
        ],
    )
    return pl.pallas_call(
        functools.partial(_attn_body, tq=tq, tk=tk, n_sub=n_sub, rs=min(th, ATTN_ROW_SLAB), q_pos0=q_pos0,
                          n_valid=n_valid, combos=combos),
        grid_spec=grid_spec,
        out_shape=jax.ShapeDtypeStruct((b, sq, N_HEADS * V_HEAD), BF16),
        compiler_params=_cp(("parallel", "arbitrary")),
        name="mla_attn",
    )(qi_t, ki_t, fl_t, q, k, v)


def _decode_attn_body(q_ref, lat_ref, krp_ref, wukv_ref, o_ref, qa_scr, qr_scr, *, ss, n_valid):
    hw = N_HEADS * HEAD_PAD
    latb = lat_ref[0].astype(BF16)
    krb = krp_ref[0].astype(BF16)
    contract_last = (((1,), (1,)), ((), ()))
    for h in range(N_HEADS):
        hs = slice(h * HEAD_PAD, (h + 1) * HEAD_PAD)
        rows = slice(h * ss, (h + 1) * ss)
        q_h = q_ref[0, :, hs]
        qa_scr[rows, :] = lax.dot_general(q_h, wukv_ref[:, hs], contract_last,
                                          preferred_element_type=F32).astype(BF16)
        qr_scr[rows, :] = q_h
    s = (lax.dot_general(qa_scr[...], latb, contract_last, preferred_element_type=F32)
         + lax.dot_general(qr_scr[...], krb, contract_last, preferred_element_type=F32))
    kpos = lax.broadcasted_iota(I32, s.shape, 1)
    s = jnp.where(kpos < n_valid, s, -jnp.inf)
    p = jnp.exp2(s - jnp.max(s, -1, keepdims=True))
    ol = jnp.dot(p.astype(BF16), latb, preferred_element_type=F32) / jnp.sum(p, -1, keepdims=True)
    for h in range(N_HEADS):
        rows = slice(h * ss, (h + 1) * ss)
        o_h = jnp.dot(ol[rows, :].astype(BF16), wukv_ref[:, hw + h * HEAD_PAD:hw + (h + 1) * HEAD_PAD],
                      preferred_element_type=F32)
        o_ref[0, :, h * V_HEAD:(h + 1) * V_HEAD] = o_h[:, 0:V_HEAD].astype(BF16)


def _decode_attention(q, lat_all, kr_padded, w_ukv, n_valid):
    b, ss, _ = q.shape
    sk = lat_all.shape[1]
    return pl.pallas_call(
        functools.partial(_decode_attn_body, ss=ss, n_valid=n_valid),
        grid=(b,),
        in_specs=[pl.BlockSpec((1, ss, N_HEADS * HEAD_PAD), lambda bi: (bi, 0, 0)),
                  pl.BlockSpec((1, sk, KV_LORA), lambda bi: (bi, 0, 0)),
                  pl.BlockSpec((1, sk, LANES), lambda bi: (bi, 0, 0)),
                  pl.BlockSpec(w_ukv.shape, lambda bi: (0, 0))],
        out_specs=pl.BlockSpec((1, ss, N_HEADS * V_HEAD), lambda bi: (bi, 0, 0)),
        out_shape=jax.ShapeDtypeStruct((b, ss, N_HEADS * V_HEAD), BF16),
        scratch_shapes=[pltpu.VMEM((N_HEADS * ss, KV_LORA), BF16), pltpu.VMEM((N_HEADS * ss, HEAD_PAD), BF16)],
        compiler_params=_cp(("parallel",)),
        name="decode_attn",
    )(q, lat_all, kr_padded, w_ukv)


def _memkv_body(mem_ref, wk_ref, wv_ref, mk_ref, mv_ref, mkb_ref, mvb_ref):
    m = mem_ref[...].astype(BF16)
    mk = jnp.dot(m, wk_ref[...], preferred_element_type=F32)
    mv = jnp.dot(m, wv_ref[...], preferred_element_type=F32)
    mk_ref[...] = mk
    mv_ref[...] = mv
    mkb_ref[...] = mk.astype(BF16)
    mvb_ref[...] = mv.astype(BF16)


def _memkv(mem2d, w_xk, w_xv):
    n = mem2d.shape[0]
    tt = _pick_tile(n, 256)
    row = pl.BlockSpec((tt, D_MODEL), lambda i: (i, 0))
    wspec = pl.BlockSpec((D_MODEL, D_MODEL), lambda i: (0, 0))
    return pl.pallas_call(
        _memkv_body,
        grid=(n // tt,),
        in_specs=[row, wspec, wspec],
        out_specs=(row, row, row, row),
        out_shape=(jax.ShapeDtypeStruct((n, D_MODEL), F32), jax.ShapeDtypeStruct((n, D_MODEL), F32),
                   jax.ShapeDtypeStruct((n, D_MODEL), BF16), jax.ShapeDtypeStruct((n, D_MODEL), BF16)),
        compiler_params=_cp(("parallel",)),
        name="memkv",
    )(mem2d, w_xk, w_xv)


def _layer_norm(x, g, b):
    mu = jnp.mean(x, -1, keepdims=True)
    xc = x - mu
    var = jnp.mean(xc * xc, -1, keepdims=True)
    return xc * lax.rsqrt(var + LN_EPS) * g + b


def _mid_body(x_ref, at_ref, yc_ref, mk_ref, mv_ref, cnt0_ref, low_ref, wo_ref, g1_ref, b1_ref, wq_ref, wxo_ref,
              g2_ref, b2_ref, wr_ref, br_ref, x2_ref, ri_ref, cnt_ref, rt_ref,
              cnt_scr, a_scr, xb_scr, q_scr, sc_scr, p_scr, o_scr, lg_scr, *, tt, alpha):
    step = pl.program_id(0)

    @pl.when(step == 0)
    def _():
        cnt_scr[...] = cnt0_ref[...]
        lg_scr[...] = jnp.zeros(lg_scr.shape, F32)

    cur_slot = step % 2
    prev_slot = 1 - cur_slot
    routed = (step > 0).astype(F32)

    n_part = MID_PARTS if tt % (MID_PARTS * SUBLANES * 2) == 0 else 1
    pr = tt // n_part
    ln_rs = min(pr, LN_ROW_SLAB)
    sm_rs = min(pr, XATTN_ROW_SLAB)
    c_exp = X_SCALE * math.log2(math.e)
    lane = lax.broadcasted_iota(I32, (pr, LANES), 1)

    def part_rows(k):
        return slice(k * pr, (k + 1) * pr)

    def out_proj(k):
        rp = part_rows(k)
        mix = jnp.concatenate([at_ref[0, rp, :], yc_ref[0, rp, :]], axis=-1)
        a_scr[rp, :] = jnp.dot(mix, wo_ref[...], preferred_element_type=F32)

    def norm1(k):
        for r in range(pr // ln_rs):
            rows = slice(k * pr + r * ln_rs, k * pr + (r + 1) * ln_rs)
            x1 = _layer_norm(alpha * x_ref[0, rows, :] + a_scr[rows, :], g1_ref[...], b1_ref[...])
            x2_ref[0, rows, :] = x1
            xb_scr[rows, :] = x1.astype(BF16)

    def q_proj(k):
        rp = part_rows(k)
        q_scr[rp, :] = jnp.dot(xb_scr[rp, :], wq_ref[...], preferred_element_type=F32).astype(BF16)

    def cross_attn(k):
        rp = part_rows(k)
        for h in range(X_HEADS):
            sl = slice(h * X_HEAD_DIM, (h + 1) * X_HEAD_DIM)
            b2 = h % 2
            sc_scr[k, b2] = lax.dot_general(q_scr[rp, sl], mk_ref[0, :, sl], (((1,), (1,)), ((), ())),
                                            preferred_element_type=F32)
            for r in range(pr // sm_rs):
                rows = slice(r * sm_rs, (r + 1) * sm_rs)
                s_r = sc_scr[k, b2, rows, :]
                e = jnp.exp2((s_r - jnp.max(s_r, -1, keepdims=True)) * c_exp)
                p_scr[k, b2, rows, :] = (e / jnp.sum(e, -1, keepdims=True)).astype(BF16)
            o_scr[rp, sl] = jnp.dot(p_scr[k, b2], mv_ref[0, :, sl], preferred_element_type=F32).astype(BF16)

    def x_out_proj(k):
        rp = part_rows(k)
        a_scr[rp, :] = jnp.dot(o_scr[rp, :], wxo_ref[...], preferred_element_type=F32)

    def norm2(k):
        for r in range(pr // ln_rs):
            rows = slice(k * pr + r * ln_rs, k * pr + (r + 1) * ln_rs)
            x2 = _layer_norm(alpha * x2_ref[0, rows, :] + a_scr[rows, :], g2_ref[...], b2_ref[...])
            x2_ref[0, rows, :] = x2
            xb_scr[rows, :] = x2.astype(BF16)

    def router_logits(k):
        rp = part_rows(k)
        lg_scr[cur_slot, rp, :] = jnp.dot(xb_scr[rp, :], wr_ref[...], preferred_element_type=F32) + br_ref[...]

    def route(k):
        rp = part_rows(k)
        logits = lg_scr[prev_slot, rp, :]
        neg = -jnp.inf
        is_g = lane < N_GROUPS
        lg = jnp.where(is_g, logits, neg)
        mg = jnp.max(lg, -1, keepdims=True)
        g_idx = jnp.min(jnp.where(lg == mg, lane, LANES), -1, keepdims=True)
        pg = 1.0 / jnp.sum(jnp.where(is_g, jnp.exp(logits - mg), 0.0), -1, keepdims=True)
        in_grp = ((lane >= ROUTER_LANE0) & (lane < ROUTER_LANE0 + N_EXPERTS)
                  & (((lane - ROUTER_LANE0) >> 3) == g_idx))
        le = jnp.where(in_grp, logits, neg)
        v1 = jnp.max(le, -1, keepdims=True)
        i1 = jnp.min(jnp.where(le == v1, lane, LANES), -1, keepdims=True)
        le2 = jnp.where(lane == i1, neg, le)
        v2 = jnp.max(le2, -1, keepdims=True)
        i2 = jnp.min(jnp.where(le2 == v2, lane, LANES), -1, keepdims=True)
        e2 = jnp.exp(v2 - v1)
        den = 1.0 + e2
        gate1 = (1.0 / den) * pg
        gate2 = (e2 / den) * pg
        oh1 = (lane == i1).astype(F32)
        oh2 = (lane == i2).astype(F32)
        oh = oh1 + oh2
        base = cnt_scr[...] + jnp.dot(low_ref[...], oh.astype(BF16), preferred_element_type=F32)
        rank1 = jnp.sum(oh1 * base, -1, keepdims=True)
        rank2 = jnp.sum(oh2 * base, -1, keepdims=True)
        cnt_scr[...] = cnt_scr[...] + jnp.sum(oh, 0, keepdims=True) * routed
        e1f = (i1 - ROUTER_LANE0).astype(F32)
        e2f = (i2 - ROUTER_LANE0).astype(F32)
        ri = jnp.where(lane == 0, e1f, jnp.where(lane == 1, e2f, jnp.where(
            lane == 2, rank1, jnp.where(lane == 3, rank2, jnp.where(lane == 4, gate1, jnp.where(
                lane == 5, gate2, 0.0))))))
        ri_ref[0, rp, :] = ri
        ri_t = jnp.transpose(ri)[0:SUBLANES, :]
        grp = min(pr, ROUTE_GROUP)
        for j in range(pr // grp):
            rt_ref[0, k * (pr // grp) + j] = ri_t[:, j * grp:(j + 1) * grp]

    stages = (out_proj, norm1, q_proj, cross_attn, x_out_proj, norm2, router_logits)
    matmul_stages = (out_proj, q_proj, x_out_proj, router_logits)
    for t in range(len(stages) + MID_SKEW * (n_part - 1)):
        todo = [(stages[t - MID_SKEW * k], k) for k in range(n_part) if 0 <= t - MID_SKEW * k < len(stages)]
        for fn, k in sorted(todo, key=lambda fk: fk[0] not in matmul_stages):
            fn(k)
        if t < n_part:
            route(t)
    cnt_ref[...] = cnt_scr[...]


def _mid(x, attn, yconv, mk_b, mv_b, cnt0, w, alpha):
    b, s, _ = x.shape
    tt = _pick_tile(s, MID_ROWS)
    nj = s // tt
    n_tiles = b * nj
    cur = lambda i: jnp.minimum(i, n_tiles - 1)
    prev = lambda i: jnp.maximum(i - 1, 0)
    row = lambda width: pl.BlockSpec((1, tt, width), lambda i: (cur(i) // nj, cur(i) % nj, 0))
    row_prev = lambda width: pl.BlockSpec((1, tt, width), lambda i: (prev(i) // nj, prev(i) % nj, 0))
    full = lambda shape: pl.BlockSpec(shape, lambda i: (0,) * len(shape))
    mem = pl.BlockSpec((1, N_MEM, D_MODEL), lambda i: (cur(i) // nj, 0, 0))
    vec = full((1, D_MODEL))
    n_part = MID_PARTS if tt % (MID_PARTS * SUBLANES * 2) == 0 else 1
    pr = tt // n_part
    grp = min(pr, ROUTE_GROUP)
    lower = jnp.tril(jnp.ones((pr, pr), F32), -1).astype(BF16)
    return pl.pallas_call(
        functools.partial(_mid_body, tt=tt, alpha=alpha),
        grid=(n_tiles + 1,),
        in_specs=[row(D_MODEL), row(N_HEADS * V_HEAD), row(CONV_WIDTH), mem, mem, full((1, LANES)), full((pr, pr)),
                  full((D_MODEL, D_MODEL)), vec, vec, full((D_MODEL, D_MODEL)), full((D_MODEL, D_MODEL)),
                  vec, vec, full((D_MODEL, LANES)), full((1, LANES))],
        out_specs=(row(D_MODEL), row_prev(LANES), full((1, LANES)),
                   pl.BlockSpec((1, tt // grp, SUBLANES, grp), lambda i: (prev(i) // nj, prev(i) % nj, 0, 0))),
        out_shape=(jax.ShapeDtypeStruct((b, s, D_MODEL), F32), jax.ShapeDtypeStruct((b, s, LANES), F32),
                   jax.ShapeDtypeStruct((1, LANES), F32),
                   jax.ShapeDtypeStruct((b, s // grp, SUBLANES, grp), F32)),
        scratch_shapes=[
            pltpu.VMEM((1, LANES), F32),
            pltpu.VMEM((tt, D_MODEL), F32),
            pltpu.VMEM((tt, D_MODEL), BF16),
            pltpu.VMEM((tt, D_MODEL), BF16),
            pltpu.VMEM((n_part, 2, pr, N_MEM), F32),
            pltpu.VMEM((n_part, 2, pr, N_MEM), BF16),
            pltpu.VMEM((tt, D_MODEL), BF16),
            pltpu.VMEM((2, tt, LANES), F32),
        ],
        compiler_params=_cp(("arbitrary",)),
        name="mid",
    )(x, attn, yconv, mk_b, mv_b, cnt0, lower, w['w_out'], w['ln1_g'], w['ln1_b'], w['w_xq'], w['w_xo'],
      w['ln2_g'], w['ln2_b'], w['w_router'], w['b_router'])


def _row_slice(ref, row):
    return ref.at[pl.ds(pl.multiple_of(row * SUBLANES, SUBLANES), SUBLANES), :]


def _slot_index(t, kk, g, n):
    if g >= n:
        return kk * g + t
    shift = g.bit_length() - 1
    return ((t >> shift) << (shift + 1)) + kk * g + (t & (g - 1))


def _rows_slice(ref, row, n_rows):
    return ref.at[pl.ds(pl.multiple_of(row * SUBLANES, SUBLANES), n_rows * SUBLANES), :]


def _dispatch_body(pad_ref, dest_ref, dest_s_ref, x_ref, x_s_ref, xs_ref, buf0, buf1, zbuf, sem0, sem1, zsem,
                   *, tt, ts, nt, nblk, rows, g_p, g_s):
    i = pl.program_id(0)
    bufs = (buf0, buf1)
    sems = (sem0, sem1)
    len_bits = rows.bit_length() - 1

    def scatter_rows(buf, sem, src_ref, d_ref, n, g):
        for c in range(ROW_CHUNKS):
            buf[pl.ds(c, n, stride=SUBLANES), :] = src_ref[:, c * LANES:(c + 1) * LANES]
        unroll = min(n, GATHER_UNROLL)

        def issue(j, carry):
            for u in range(unroll):
                t = j * unroll + u
                src = _row_slice(buf, t)
                for kk in range(2):
                    d = d_ref[0, 0, _slot_index(t, kk, g, n)]
                    pltpu.make_async_copy(src, _row_slice(xs_ref, d), sem).start(priority=kk)
            return carry

        lax.fori_loop(0, n // unroll, issue, 0)

    def wait_rows(buf, sem, n):
        for _ in range(2):
            pltpu.make_async_copy(_rows_slice(buf, 0, n), _rows_slice(xs_ref, 0, n), sem).wait()

    def zero_fill(wait):
        def fire(copy):
            if wait:
                copy.wait()
            else:
                copy.start()

        def per_expert(e, carry):
            first = pad_ref[e]
            n_pad = pad_ref[N_EXPERTS + e]
            for bit in range(len_bits):
                size = 1 << bit
                off = (n_pad >> (bit + 1)) << (bit + 1)

                @pl.when(((n_pad >> bit) & 1) == 1)
                def _():
                    fire(pltpu.make_async_copy(_rows_slice(zbuf, 0, size), _rows_slice(xs_ref, first + off, size),
                                               zsem))
            return carry

        lax.fori_loop(0, N_EXPERTS, per_expert, 0)

        def per_block(j, carry):
            fire(pltpu.make_async_copy(zbuf, _rows_slice(xs_ref, j * rows, rows), zsem))
            return carry

        lax.fori_loop(pad_ref[2 * N_EXPERTS], nblk, per_block, 0)

    def run(slot):
        @pl.when(i < nt)
        def _():
            scatter_rows(bufs[slot], sems[slot], x_ref, dest_ref, tt, g_p)

        @pl.when(i == nt)
        def _():
            zbuf[...] = jnp.zeros(zbuf.shape, F32)
            scatter_rows(bufs[slot], sems[slot], x_s_ref, dest_s_ref, ts, g_s)
            zero_fill(False)

        @pl.when(i > 0)
        def _():
            wait_rows(bufs[1 - slot], sems[1 - slot], tt)

        @pl.when(i == nt)
        def _():
            wait_rows(bufs[slot], sems[slot], ts)
            zero_fill(True)

    @pl.when(i % 2 == 0)
    def _():
        run(0)

    @pl.when(i % 2 == 1)
    def _():
        run(1)


def _dispatch(x_p, dest_p, g_p, x_s, dest_s, g_s, pad_info, nblk, rows):
    n_p, n_s = x_p.shape[0], x_s.shape[0]
    tt = _pick_tile(n_p, 256)
    nt = n_p // tt
    assert n_s <= tt
    last = lambda i, pad: jnp.minimum(i, nt - 1)
    grid_spec = pltpu.PrefetchScalarGridSpec(
        num_scalar_prefetch=1,
        grid=(nt + 1,),
        in_specs=[
            pl.BlockSpec((1, 1, 2 * tt), lambda i, pad: (last(i, pad), 0, 0), memory_space=pltpu.SMEM),
            pl.BlockSpec((1, 1, 2 * n_s), lambda i, pad: (0, 0, 0), memory_space=pltpu.SMEM),
            pl.BlockSpec((tt, D_MODEL), lambda i, pad: (last(i, pad), 0)),
            pl.BlockSpec((n_s, D_MODEL), lambda i, pad: (0, 0)),
        ],
        out_specs=pl.BlockSpec(memory_space=pl.ANY),
        scratch_shapes=[pltpu.VMEM((tt * SUBLANES, LANES), F32), pltpu.VMEM((tt * SUBLANES, LANES), F32),
                        pltpu.VMEM((rows * SUBLANES, LANES), F32),
                        pltpu.SemaphoreType.DMA, pltpu.SemaphoreType.DMA, pltpu.SemaphoreType.DMA],
    )
    return pl.pallas_call(
        functools.partial(_dispatch_body, tt=tt, ts=n_s, nt=nt, nblk=nblk, rows=rows, g_p=g_p, g_s=g_s),
        grid_spec=grid_spec,
        out_shape=jax.ShapeDtypeStruct((nblk * rows * SUBLANES, LANES), F32),
        compiler_params=_cp(("arbitrary",)),
        name="moe_dispatch",
    )(pad_info, dest_p.reshape(nt, 1, 2 * tt), dest_s.reshape(1, 1, 2 * n_s), x_p, x_s)


def _experts_body(be_ref, nu_ref, x_ref, wg_ref, wu_ref, wd_ref, y_ref, xb_scr, g_scr, u_scr, h_scr,
                  wgb_scr, wub_scr, wdb_scr, *, rows):
    i = pl.program_id(0)
    cur = jnp.minimum(i, nu_ref[0] - 1)
    new_expert = (i == 0) | (be_ref[cur] != be_ref[jnp.maximum(cur - 1, 0)])

    @pl.when((i < nu_ref[0]) & new_expert)
    def _():
        for r in range(0, D_MODEL, WCAST_ROWS):
            wgb_scr[r:r + WCAST_ROWS, :] = wg_ref[0, r:r + WCAST_ROWS, :].astype(BF16)
            wub_scr[r:r + WCAST_ROWS, :] = wu_ref[0, r:r + WCAST_ROWS, :].astype(BF16)
        for r in range(0, D_EXPERT, WCAST_ROWS // 2):
            wdb_scr[r:r + WCAST_ROWS // 2, :] = wd_ref[0, r:r + WCAST_ROWS // 2, :].astype(BF16)
    n_part = MOE_PARTS
    pr = rows // n_part
    act_rs = min(pr, ACT_ROW_SLAB)

    def load(k):
        base = k * pr * SUBLANES
        xb_scr[k] = jnp.concatenate([x_ref[pl.ds(base + c, pr, stride=SUBLANES), :] for c in range(ROW_CHUNKS)],
                                    axis=-1).astype(BF16)

    def gate_up(k):
        g_scr[k] = jnp.dot(xb_scr[k], wgb_scr[...], preferred_element_type=F32)
        u_scr[k] = jnp.dot(xb_scr[k], wub_scr[...], preferred_element_type=F32)

    def act(k):
        for r in range(pr // act_rs):
            rows_r = slice(r * act_rs, (r + 1) * act_rs)
            g = g_scr[k, rows_r, :]
            h_scr[k, rows_r, :] = ((g * jax.nn.sigmoid(g)) * u_scr[k, rows_r, :]).astype(BF16)

    def down(k):
        y = jnp.dot(h_scr[k], wdb_scr[...], preferred_element_type=F32)
        base = k * pr * SUBLANES
        for c in range(ROW_CHUNKS):
            y_ref[pl.ds(base + c, pr, stride=SUBLANES), :] = y[:, c * LANES:(c + 1) * LANES]

    @pl.when(i < nu_ref[0])
    def _():
        stages = (load, gate_up, act, down)
        matmul_stages = (gate_up, down)
        for t in range(len(stages) + n_part - 1):
            todo = [(stages[t - k], k) for k in range(n_part) if 0 <= t - k < len(stages)]
            for fn, k in sorted(todo, key=lambda fk: fk[0] not in matmul_stages):
                fn(k)

    @pl.when(i >= nu_ref[0])
    def _():
        y_ref[...] = jnp.zeros(y_ref.shape, F32)


def _experts(xs, block_e, n_used, wg, wu, wd, rows):
    nblk = xs.shape[0] // (rows * SUBLANES)
    clamp = lambda i, nu: jnp.minimum(i, nu[0] - 1)
    grid_spec = pltpu.PrefetchScalarGridSpec(
        num_scalar_prefetch=2,
        grid=(nblk,),
        in_specs=[
            pl.BlockSpec((rows * SUBLANES, LANES), lambda i, be, nu: (clamp(i, nu), 0)),
            pl.BlockSpec((1, D_MODEL, D_EXPERT), lambda i, be, nu: (be[clamp(i, nu)], 0, 0)),
            pl.BlockSpec((1, D_MODEL, D_EXPERT), lambda i, be, nu: (be[clamp(i, nu)], 0, 0)),
            pl.BlockSpec((1, D_EXPERT, D_MODEL), lambda i, be, nu: (be[clamp(i, nu)], 0, 0)),
        ],
        out_specs=pl.BlockSpec((rows * SUBLANES, LANES), lambda i, be, nu: (i, 0)),
        scratch_shapes=[
            pltpu.VMEM((MOE_PARTS, rows // MOE_PARTS, D_MODEL), BF16),
            pltpu.VMEM((MOE_PARTS, rows // MOE_PARTS, D_EXPERT), F32),
            pltpu.VMEM((MOE_PARTS, rows // MOE_PARTS, D_EXPERT), F32),
            pltpu.VMEM((MOE_PARTS, rows // MOE_PARTS, D_EXPERT), BF16),
            pltpu.VMEM((D_MODEL, D_EXPERT), BF16),
            pltpu.VMEM((D_MODEL, D_EXPERT), BF16),
            pltpu.VMEM((D_EXPERT, D_MODEL), BF16),
        ],
    )
    return pl.pallas_call(
        functools.partial(_experts_body, rows=rows),
        grid_spec=grid_spec,
        out_shape=jax.ShapeDtypeStruct(xs.shape, F32),
        compiler_params=_cp(("arbitrary",)),
        name="moe_experts",
    )(block_e, n_used, xs, wg, wu, wd)


def _combine_body(dest_ref, destn_ref, x_ref, ri_ref, g3_ref, b3_ref, ys_ref, o_ref,
                  b00, b01, b10, b11, sem0, sem1, *, tt, nt, g, alpha):
    i = pl.program_id(0)
    bufs = ((b00, b01), (b10, b11))
    sems = (sem0, sem1)
    rs = min(tt, LN_ROW_SLAB)
    unroll = min(tt, GATHER_UNROLL)

    def gather_rows(dref, slot):
        def copy(t, kk):
            d = dref[0, 0, _slot_index(t, kk, g, tt)]
            return pltpu.make_async_copy(_row_slice(ys_ref, d), _row_slice(bufs[slot][kk], t), sems[slot])

        def body(j, carry):
            for u in range(unroll):
                for kk in range(2):
                    copy(j * unroll + u, kk).start(priority=kk)
            return carry

        lax.fori_loop(0, tt // unroll, body, 0)

    def wait_rows(slot):
        for kk in range(2):
            pltpu.make_async_copy(ys_ref.at[pl.ds(0, tt * SUBLANES), :], bufs[slot][kk], sems[slot]).wait()

    def run(slot):
        if slot == 0:
            @pl.when(i == 0)
            def _():
                gather_rows(dest_ref, 0)

        @pl.when(i + 1 < nt)
        def _():
            gather_rows(destn_ref, 1 - slot)

        wait_rows(slot)
        for r in range(tt // rs):
            rows = slice(r * rs, (r + 1) * rs)
            base = r * rs * SUBLANES
            y0 = jnp.concatenate([bufs[slot][0][pl.ds(base + c, rs, stride=SUBLANES), :]
                                  for c in range(ROW_CHUNKS)], axis=-1)
            y1 = jnp.concatenate([bufs[slot][1][pl.ds(base + c, rs, stride=SUBLANES), :]
                                  for c in range(ROW_CHUNKS)], axis=-1)
            ri = ri_ref[rows, :]
            moe = y0 * ri[:, 4:5] + y1 * ri[:, 5:6]
            o_ref[rows, :] = _layer_norm(alpha * x_ref[rows, :] + moe, g3_ref[...], b3_ref[...])

    @pl.when(i % 2 == 0)
    def _():
        run(0)

    @pl.when(i % 2 == 1)
    def _():
        run(1)


def _combine(x2d, rinfo, dest, g, ys, g3, b3, alpha):
    n = x2d.shape[0]
    tt = _pick_tile(n, 256)
    nt = n // tt
    dest3 = dest.reshape(nt, 1, 2 * tt)
    vec = pl.BlockSpec((1, D_MODEL), lambda i: (0, 0))
    stage = pltpu.VMEM((tt * SUBLANES, LANES), F32)
    return pl.pallas_call(
        functools.partial(_combine_body, tt=tt, nt=nt, g=g, alpha=alpha),
        grid=(nt,),
        in_specs=[
            pl.BlockSpec((1, 1, 2 * tt), lambda i: (i, 0, 0), memory_space=pltpu.SMEM),
            pl.BlockSpec((1, 1, 2 * tt), lambda i: (jnp.minimum(i + 1, nt - 1), 0, 0), memory_space=pltpu.SMEM),
            pl.BlockSpec((tt, D_MODEL), lambda i: (i, 0)),
            pl.BlockSpec((tt, LANES), lambda i: (i, 0)),
            vec, vec,
            pl.BlockSpec(memory_space=pl.ANY),
        ],
        out_specs=pl.BlockSpec((tt, D_MODEL), lambda i: (i, 0)),
        out_shape=jax.ShapeDtypeStruct((n, D_MODEL), F32),
        scratch_shapes=[stage, stage, stage, stage, pltpu.SemaphoreType.DMA, pltpu.SemaphoreType.DMA],
        compiler_params=_cp(("arbitrary",)),
        name="moe_combine",
    )(dest3, dest3, x2d, rinfo, g3, b3, ys)


def _rope_tables(pos):
    half = QK_ROPE // 2
    inv = ROPE_THETA ** (-jnp.arange(half, dtype=F32) / half)
    ang = pos.astype(F32)[:, None] * inv[None, :]
    cos, sin = jnp.cos(ang), jnp.sin(ang)
    n = pos.shape[0]
    pad_r = LANES - ROPE_LANE0 - QK_ROPE
    cos_t = jnp.concatenate([jnp.ones((n, ROPE_LANE0), F32), cos, cos, jnp.zeros((n, pad_r), F32)], -1)
    sin_t = jnp.concatenate([jnp.zeros((n, ROPE_LANE0), F32), sin, sin, jnp.zeros((n, pad_r), F32)], -1)
    return cos_t, sin_t


def _swap_neg(wr):
    half = QK_ROPE // 2
    return jnp.concatenate([-wr[:, half:], wr[:, :half]], axis=1)


def _prep_weights(l, w_in, q_norm_g, kv_norm_g, w_uq, w_ukv, conv_w, w_out, ln1_g, ln1_b, w_xq, w_xk, w_xv, w_xo,
                  ln2_g, ln2_b, w_router_group, b_router_group, w_router_expert, b_router_expert,
                  w_exp_gate, w_exp_up, w_exp_down, ln3_g, ln3_b):
    wi = w_in[l]
    c0 = Q_LORA + KV_LORA
    w_kr = wi[:, c0:c0 + QK_ROPE]
    zl = jnp.zeros((D_MODEL, ROPE_LANE0), F32)
    w_in_p = jnp.concatenate([wi[:, :c0], wi[:, c0 + QK_ROPE:], zl, w_kr, _swap_neg(w_kr)], axis=1)
    wq = w_uq[l].reshape(Q_LORA, N_HEADS, QK_NOPE + QK_ROPE)
    wq_rot = jnp.concatenate([-wq[:, :, QK_NOPE + QK_ROPE // 2:], wq[:, :, QK_NOPE:QK_NOPE + QK_ROPE // 2]], axis=2)
    wq_a = jnp.concatenate([wq, wq_rot], axis=2).reshape(Q_LORA, N_HEADS * HEAD_PAD)
    wkv = w_ukv[l].reshape(KV_LORA, N_HEADS, QK_NOPE + V_HEAD)
    wk_p = jnp.concatenate([wkv[:, :, :QK_NOPE], jnp.zeros((KV_LORA, N_HEADS, HEAD_PAD - QK_NOPE), F32)], axis=2)
    wk_p = wk_p.reshape(KV_LORA, N_HEADS * HEAD_PAD)
    wv_p = jnp.concatenate([wkv[:, :, QK_NOPE:], jnp.zeros((KV_LORA, N_HEADS, HEAD_PAD - V_HEAD), F32)], axis=2)
    wv_p = wv_p.reshape(KV_LORA, N_HEADS * HEAD_PAD)
    w_router = jnp.concatenate([w_router_group[l], w_router_expert[l],
                                jnp.zeros((D_MODEL, LANES - N_GROUPS - N_EXPERTS), F32)], axis=1)
    b_router = jnp.concatenate([b_router_group[l], b_router_expert[l].reshape(-1),
                                jnp.zeros((LANES - N_GROUPS - N_EXPERTS,), F32)]).reshape(1, LANES)
    return dict(
        w_in=w_in_p.astype(BF16),
        q_norm_g=q_norm_g[l].reshape(1, Q_LORA), kv_norm_g=kv_norm_g[l].reshape(1, KV_LORA),
        w_uq=wq_a.astype(BF16),
        w_ukv=jnp.concatenate([wk_p, wv_p], axis=1).astype(BF16),
        conv_w=conv_w[l],
        w_out=w_out[l].astype(BF16), ln1_g=ln1_g[l].reshape(1, -1), ln1_b=ln1_b[l].reshape(1, -1),
        w_xq=w_xq[l].astype(BF16), w_xk=w_xk[l].astype(BF16), w_xv=w_xv[l].astype(BF16),
        w_xo=w_xo[l].astype(BF16), ln2_g=ln2_g[l].reshape(1, -1), ln2_b=ln2_b[l].reshape(1, -1),
        w_router=w_router.astype(BF16), b_router=b_router,
        w_exp_gate=w_exp_gate[l], w_exp_up=w_exp_up[l], w_exp_down=w_exp_down[l],
        ln3_g=ln3_g[l].reshape(1, -1), ln3_b=ln3_b[l].reshape(1, -1),
    )


def _slots(rt, pstarts):
    g = rt.shape[-1]
    rt = rt.reshape(-1, SUBLANES, g)
    dest = pstarts[rt[:, 0:2, :].astype(I32)] + rt[:, 2:4, :].astype(I32)
    return dest.reshape(-1), g


def _layer(l, depth, xp, xs, lat_past, kr_past, conv_past, mk_s, mv_s, mem_prompt, w):
    alpha = (2 * depth) ** 0.25
    b, s, _ = xp.shape
    bs, ss, _ = xs.shape
    past = lat_past.shape[1]

    cos_p, sin_p = _rope_tables(jnp.arange(s))
    q_p, k_p, v_p, yc_p, lat_p, kr_p, cst_p = _inproj(
        xp, jnp.zeros((b, CONV_K - 1, CONV_WIDTH), F32), cos_p, sin_p, w)
    attn_p = _attention(q_p, k_p, v_p, 0, None)
    mk, mv, mk_b, mv_b = _memkv(mem_prompt.reshape(b * N_MEM, D_MODEL), w['w_xk'], w['w_xv'])
    cnt0 = jnp.zeros((1, LANES), F32)
    x2_p, ri_p, cnt_p, rt_p = _mid(xp, attn_p, yc_p, mk_b.reshape(b, N_MEM, D_MODEL), mv_b.reshape(b, N_MEM, D_MODEL),
                             cnt0, w, alpha)

    cos_s, sin_s = _rope_tables(past + jnp.arange(ss))
    q_s, _, _, yc_s, lat_s, kr_s, cst_s = _inproj(xs, conv_past, cos_s, sin_s, w)
    n_keys = past + ss
    sk = -(-n_keys // LANES) * LANES
    lat_all = jnp.concatenate([lat_past, lat_s, jnp.zeros((bs, sk - n_keys, KV_LORA), F32)], axis=1)
    kr_all = jnp.concatenate([kr_past, kr_s, jnp.zeros((bs, sk - n_keys, QK_ROPE), F32)], axis=1)
    kr_all = jnp.pad(kr_all, ((0, 0), (0, 0), (ROPE_LANE0, LANES - ROPE_LANE0 - QK_ROPE)))
    attn_s = _decode_attention(q_s, lat_all, kr_all, w['w_ukv'], n_keys)
    mk_sb = mk_s.reshape(bs, N_MEM, D_MODEL).astype(BF16)
    mv_sb = mv_s.reshape(bs, N_MEM, D_MODEL).astype(BF16)
    x2_s, ri_s, cnt, rt_s = _mid(xs, attn_s, yc_s, mk_sb, mv_sb, cnt_p, w, alpha)

    n_p, n_s = b * s, bs * ss
    counts = cnt[0, ROUTER_LANE0:ROUTER_LANE0 + N_EXPERTS].astype(I32)
    padded = (counts + MOE_ROWS - 1) // MOE_ROWS * MOE_ROWS
    pends = jnp.cumsum(padded)
    pstarts = pends - padded
    nblk = -(-2 * (n_p + n_s) // MOE_ROWS) + N_EXPERTS
    blk_start = jnp.arange(nblk, dtype=I32) * MOE_ROWS
    block_e = jnp.minimum(jnp.sum((pends[None, :] <= blk_start[:, None]).astype(I32), axis=1), N_EXPERTS - 1)
    n_used = (pends[-1] // MOE_ROWS).astype(I32).reshape(1)
    ri_p2, ri_s2 = ri_p.reshape(n_p, LANES), ri_s.reshape(n_s, LANES)
    dest_p, g_p = _slots(rt_p, pstarts)
    dest_s, g_s = _slots(rt_s, pstarts)
    x2_p2, x2_s2 = x2_p.reshape(n_p, D_MODEL), x2_s.reshape(n_s, D_MODEL)
    pad_info = jnp.concatenate([pstarts + counts, padded - counts, n_used]).astype(I32)
    slots = _dispatch(x2_p2, dest_p, g_p, x2_s2, dest_s, g_s, pad_info, nblk, MOE_ROWS)
    ys = _experts(slots, block_e, n_used, w['w_exp_gate'], w['w_exp_up'], w['w_exp_down'], MOE_ROWS)
    y_p = _combine(x2_p2, ri_p2, dest_p, g_p, ys, w['ln3_g'], w['ln3_b'], alpha).reshape(b, s, D_MODEL)
    y_s = _combine(x2_s2, ri_s2, dest_s, g_s, ys, w['ln3_g'], w['ln3_b'], alpha).reshape(bs, ss, D_MODEL)
    return (y_p, y_s, lat_p, kr_p, cst_p, mk.reshape(b, N_MEM, X_HEADS, X_HEAD_DIM),
            mv.reshape(b, N_MEM, X_HEADS, X_HEAD_DIM), lat_s, kr_s, cst_s)


def kernel(x_prompt, x_sample, cache_kv_latent, cache_k_rope, cache_conv, cache_mem_k, cache_mem_v, mem_prompt,
           w_in, q_norm_g, kv_norm_g, w_uq, w_ukv, conv_w, w_out, ln1_g, ln1_b, w_xq, w_xk, w_xv, w_xo, ln2_g,
           ln2_b, w_router_group, b_router_group, w_router_expert, b_router_expert, w_exp_gate, w_exp_up,
           w_exp_down, ln3_g, ln3_b):
    depth = w_in.shape[0]
    xp, xs = x_prompt, x_sample
    outs = [[] for _ in range(8)]
    for l in range(depth):
        w = _prep_weights(l, w_in, q_norm_g, kv_norm_g, w_uq, w_ukv, conv_w, w_out, ln1_g, ln1_b, w_xq, w_xk, w_xv,
                          w_xo, ln2_g, ln2_b, w_router_group, b_router_group, w_router_expert, b_router_expert,
                          w_exp_gate, w_exp_up, w_exp_down, ln3_g, ln3_b)
        res = _layer(l, depth, xp, xs, cache_kv_latent[l], cache_k_rope[l], cache_conv[l], cache_mem_k[l],
                     cache_mem_v[l], mem_prompt, w)
        xp, xs = res[0], res[1]
        for acc, r in zip(outs, res[2:]):
            acc.append(r)
    return (xp, xs) + tuple(jnp.stack(o) for o in outs)
```

```python
import functools
import math

import numpy as np
import jax
import jax.numpy as jnp
from jax import lax
from jax.experimental import pallas as pl
from jax.experimental.pallas import tpu as pltpu

F32 = jnp.float32
BF16 = jnp.bfloat16
I32 = jnp.int32

D_MODEL = 1024
CHUNK = 64
N_HEADS = 8
QK_NOPE = 64
QK_ROPE = 32
V_HEAD = 64
Q_LORA = 256
KV_LORA = 128
ROPE_THETA = 10000.0
MLA_SCALE = (QK_NOPE + QK_ROPE) ** -0.5
Q_PRESCALE = MLA_SCALE * math.log2(math.e)
CONV_WIDTH = 512
CONV_K = 3
N_MEM = 256
X_HEADS = 4
X_HEAD_DIM = D_MODEL // X_HEADS
X_SCALE = X_HEAD_DIM ** -0.5
N_GROUPS = 4
EXPERTS_PER_GROUP = 8
N_EXPERTS = N_GROUPS * EXPERTS_PER_GROUP
D_EXPERT = 512
LN_EPS = 1e-5
RMS_EPS = 1e-6

LANES = 128
SUBLANES = 8
ROW_CHUNKS = D_MODEL // LANES
HEAD_PAD = LANES
ROPE_LANE0 = QK_NOPE
ROUTER_LANE0 = N_GROUPS
VMEM_LIMIT = 56 * 1024 * 1024
MOE_ROWS = 512
MOE_PARTS = 2
ACT_ROW_SLAB = 32
GATHER_UNROLL = 8
WCAST_ROWS = 64
ATTN_TQ = 1024
ATTN_TK = 512
ATTN_SUB = 2
ATTN_ROW_SLAB = 64
LN_ROW_SLAB = 16
XATTN_ROW_SLAB = 64
INPROJ_ROWS = 512
ROUTE_GROUP = 256
MID_SKEW = 1
MID_ROWS = 512
MID_PARTS = 2


def _cp(sem, vmem=VMEM_LIMIT):
    return pltpu.CompilerParams(dimension_semantics=sem, vmem_limit_bytes=vmem)


def _pick_tile(n, pref):
    t = min(n, pref)
    while n % t:
        t //= 2
    return t


def _with_ones_lane(v):
    lane = lax.broadcasted_iota(I32, v.shape, 1)
    return jnp.where((lane & (HEAD_PAD - 1)) == V_HEAD, 1.0, v)


def _inproj_body(x_ref, cinit_ref, cos_ref, sin_ref, win_ref, qg_ref, kvg_ref, wuq_ref, wukv_ref, cw_ref,
                 q_ref, k_ref, v_ref, yc_ref, lat_ref, kr_ref, cst_ref, u_scr, *, tt, nj):
    j = pl.program_id(1)

    @pl.when(j == 0)
    def _():
        u_scr[6:8, :] = cinit_ref[0]

    x = x_ref[0].astype(BF16)
    proj = jnp.dot(x, win_ref[...], preferred_element_type=F32)
    cq = proj[:, 0:256]
    ckv = proj[:, 256:384]
    gb = proj[:, 384:896]
    gc = proj[:, 896:1408]
    gv = proj[:, 1408:1920]
    kr_blk = proj[:, 1920:2048]
    cos_t = cos_ref[...]
    sin_t = sin_ref[...]

    def rotate(blk):
        return blk * cos_t + pltpu.roll(blk, LANES - QK_ROPE, 1) * sin_t

    cqn = cq * lax.rsqrt(jnp.mean(cq * cq, -1, keepdims=True) + RMS_EPS) * qg_ref[...]
    ckvn = ckv * lax.rsqrt(jnp.mean(ckv * ckv, -1, keepdims=True) + RMS_EPS) * kvg_ref[...]
    lat_ref[0] = ckvn
    kr_p = rotate(kr_blk)
    kr_ref[0] = kr_p[:, ROPE_LANE0:ROPE_LANE0 + QK_ROPE]
    qq = jnp.dot(cqn.astype(BF16), wuq_ref[...], preferred_element_type=F32)
    kv = jnp.dot(ckvn.astype(BF16), wukv_ref[...], preferred_element_type=F32)
    hw = N_HEADS * HEAD_PAD
    for h in range(N_HEADS):
        sl = slice(h * HEAD_PAD, (h + 1) * HEAD_PAD)
        q_ref[0, :, sl] = (rotate(qq[:, sl]) * Q_PRESCALE).astype(BF16)
        k_ref[0, :, sl] = (kv[:, sl] + kr_p).astype(BF16)
    v_ref[0] = _with_ones_lane(kv[:, hw:2 * hw]).astype(BF16)
    u = gc * gv
    u_scr[8:8 + tt, :] = u
    conv = cw_ref[0:1, :] * u_scr[6:6 + tt, :] + cw_ref[1:2, :] * u_scr[7:7 + tt, :] + cw_ref[2:3, :] * u
    yc_ref[0] = (gb * conv).astype(BF16)
    last2 = u_scr[tt + 6:tt + 8, :]
    u_scr[6:8, :] = last2

    @pl.when(j == nj - 1)
    def _():
        cst_ref[0] = last2


def _inproj(x, conv_init, cos_t, sin_t, w):
    b, s, _ = x.shape
    tt = _pick_tile(s, INPROJ_ROWS)
    nj = s // tt
    wn = w['w_in'].shape[1]
    full = lambda shape: pl.BlockSpec(shape, lambda bi, ji: (0,) * len(shape))
    out_shapes = (
        jax.ShapeDtypeStruct((b, s, N_HEADS * HEAD_PAD), BF16),
        jax.ShapeDtypeStruct((b, s, N_HEADS * HEAD_PAD), BF16),
        jax.ShapeDtypeStruct((b, s, N_HEADS * HEAD_PAD), BF16),
        jax.ShapeDtypeStruct((b, s, CONV_WIDTH), BF16),
        jax.ShapeDtypeStruct((b, s, KV_LORA), F32),
        jax.ShapeDtypeStruct((b, s, QK_ROPE), F32),
        jax.ShapeDtypeStruct((b, CONV_K - 1, CONV_WIDTH), F32),
    )
    row = lambda width: pl.BlockSpec((1, tt, width), lambda bi, ji: (bi, ji, 0))
    return pl.pallas_call(
        functools.partial(_inproj_body, tt=tt, nj=nj),
        grid=(b, nj),
        in_specs=[
            row(D_MODEL),
            pl.BlockSpec((1, CONV_K - 1, CONV_WIDTH), lambda bi, ji: (bi, 0, 0)),
            pl.BlockSpec((tt, LANES), lambda bi, ji: (ji, 0)),
            pl.BlockSpec((tt, LANES), lambda bi, ji: (ji, 0)),
            full((D_MODEL, wn)),
            full((1, Q_LORA)),
            full((1, KV_LORA)),
            full(w['w_uq'].shape),
            full(w['w_ukv'].shape),
            full((CONV_K, CONV_WIDTH)),
        ],
        out_specs=(
            row(N_HEADS * HEAD_PAD), row(N_HEADS * HEAD_PAD), row(N_HEADS * HEAD_PAD), row(CONV_WIDTH),
            row(KV_LORA), row(QK_ROPE),
            pl.BlockSpec((1, CONV_K - 1, CONV_WIDTH), lambda bi, ji: (bi, 0, 0)),
        ),
        out_shape=out_shapes,
        scratch_shapes=[pltpu.VMEM((tt + 8, CONV_WIDTH), F32)],
        compiler_params=_cp(("parallel", "arbitrary")),
        name="inproj",
    )(x, conv_init, cos_t, sin_t, w['w_in'], w['q_norm_g'], w['kv_norm_g'], w['w_uq'], w['w_ukv'], w['conv_w'])


def _attn_body(qi_ref, ki_ref, fl_ref, q_ref, k_ref, v_ref, o_ref, m_scr, acc_scr, s_scr, p_scr, a_scr,
               *, tq, tk, n_sub, rs, q_pos0, n_valid, combos):
    step = pl.program_id(1)
    qi = qi_ref[step]
    ki = ki_ref[step]
    flags = fl_ref[step]
    reps = tk // LANES
    th = tq // n_sub

    @pl.when((flags & 1) != 0)
    def _():
        m_scr[...] = jnp.full(m_scr.shape, -jnp.inf, F32)
        acc_scr[...] = jnp.zeros(acc_scr.shape, F32)

    def scores(item, buf):
        sub, h, _ = item
        hs = slice(h * HEAD_PAD, (h + 1) * HEAD_PAD)
        s_scr[buf] = lax.dot_general(q_ref[0, sub * th:(sub + 1) * th, hs], k_ref[0, :, hs],
                                     (((1,), (1,)), ((), ())), preferred_element_type=F32)

    def softmax_pv(item, buf):
        sub, h, masked = item
        hs = slice(h * HEAD_PAD, (h + 1) * HEAD_PAD)
        for r in range(th // rs):
            rows = slice(r * rs, (r + 1) * rs)
            arows = slice(sub * th + r * rs, sub * th + (r + 1) * rs)
            s_r = s_scr[buf, rows, :]
            if masked:
                qpos = q_pos0 + qi * tq + sub * th + r * rs + lax.broadcasted_iota(I32, (rs, tk), 0)
                kpos = ki * tk + lax.broadcasted_iota(I32, (rs, tk), 1)
                mask = (kpos >> 6) <= (qpos >> 6)
                if n_valid is not None:
                    mask = mask & (kpos < n_valid)
                s_r = jnp.where(mask, s_r, -jnp.inf)
            m_old = m_scr[h, arows, :]
            m_new = jnp.maximum(m_old, jnp.max(s_r, axis=-1, keepdims=True))
            a_scr[buf, rows, :] = jnp.exp2(m_old - m_new)
            m_rep = jnp.concatenate([m_new] * reps, axis=1)
            p_scr[buf, rows, :] = jnp.exp2(s_r - m_rep).astype(BF16)
            m_scr[h, arows, :] = m_new
        pv = jnp.dot(p_scr[buf], v_ref[0, :, hs], preferred_element_type=F32)
        srows = slice(sub * th, (sub + 1) * th)
        acc_scr[h, srows, :] = a_scr[buf] * acc_scr[h, srows, :] + pv

    def run(modes):
        items = [(sub, h, mode == 2) for sub, mode in enumerate(modes) if mode != 0 for h in range(N_HEADS)]
        scores(items[0], 0)
        for n, item in enumerate(items):
            if n + 1 < len(items):
                scores(items[n + 1], (n + 1) % 2)
            softmax_pv(item, n % 2)

    for code, modes in combos:
        @pl.when((flags >> 2) == code)
        def _(modes=modes):
            run(modes)

    @pl.when((flags & 2) != 0)
    def _():
        for h in range(N_HEADS):
            acc = acc_scr[h]
            o_ref[0, :, h * V_HEAD:(h + 1) * V_HEAD] = (acc[:, 0:V_HEAD] / acc[:, V_HEAD:V_HEAD + 1]).astype(BF16)


def _attn_tables(nq, nk, tq, tk, n_sub, q_pos0, n_valid):
    th = tq // n_sub
    qi_l, ki_l, fl_l, combos = [], [], [], {}
    for qi in range(nq):
        sub_lo = [q_pos0 + qi * tq + j * th for j in range(n_sub)]
        sub_last = []
        for lo in sub_lo:
            last_pos = ((lo + th - 1) // CHUNK) * CHUNK + CHUNK - 1
            if n_valid is not None:
                last_pos = min(last_pos, n_valid - 1)
            sub_last.append(min(nk - 1, last_pos // tk))
        k_last = max(sub_last)
        for ki in range(k_last + 1):
            k_hi = ki * tk + tk - 1
            modes = []
            for lo, last in zip(sub_lo, sub_last):
                if ki > last:
                    modes.append(0)
                elif (k_hi // CHUNK) > (lo // CHUNK) or (n_valid is not None and k_hi >= n_valid):
                    modes.append(2)
                else:
                    modes.append(1)
            code = sum(m * 3 ** j for j, m in enumerate(modes))
            combos[code] = tuple(modes)
            qi_l.append(qi); ki_l.append(ki)
            fl_l.append((1 if ki == 0 else 0) | (2 if ki == k_last else 0) | (code << 2))
    to_arr = lambda vals: jnp.asarray(np.array(vals, np.int32))
    return to_arr(qi_l), to_arr(ki_l), to_arr(fl_l), tuple(sorted(combos.items()))


def _attention(q, k, v, q_pos0, n_valid):
    b, sq, _ = q.shape
    sk = k.shape[1]
    tq = _pick_tile(sq, ATTN_TQ)
    tk = _pick_tile(sk, ATTN_TK)
    n_sub = ATTN_SUB if tq % (ATTN_SUB * 2 * SUBLANES) == 0 else 1
    th = tq // n_sub
    qi_t, ki_t, fl_t, combos = _attn_tables(sq // tq, sk // tk, tq, tk, n_sub, q_pos0, n_valid)
    n_steps = int(qi_t.shape[0])
    grid_spec = pltpu.PrefetchScalarGridSpec(
        num_scalar_prefetch=3,
        grid=(b, n_steps),
        in_specs=[
            pl.BlockSpec((1, tq, N_HEADS * HEAD_PAD), lambda bi, si, qt, kt, ft: (bi, qt[si], 0)),
            pl.BlockSpec((1, tk, N_HEADS * HEAD_PAD), lambda bi, si, qt, kt, ft: (bi, kt[si], 0)),
            pl.BlockSpec((1, tk, N_HEADS * HEAD_PAD), lambda bi, si, qt, kt, ft: (bi, kt[si], 0)),
        ],
        out_specs=pl.BlockSpec((1, tq, N_HEADS * V_HEAD), lambda bi, si, qt, kt, ft: (bi, qt[si], 0)),
        scratch_shapes=[
            pltpu.VMEM((N_HEADS, tq, LANES), F32),
            pltpu.VMEM((N_HEADS, tq, LANES), F32),
            pltpu.VMEM((2, th, tk), F32),
            pltpu.VMEM((2, th, tk), BF16),
            pltpu.VMEM((2, th, LANES), F32),
        ],
    )
    return pl.pallas_call(
        functools.partial(_attn_body, tq=tq, tk=tk, n_sub=n_sub, rs=min(th, ATTN_ROW_SLAB), q_pos0=q_pos0,
                          n_valid=n_valid, combos=combos),
        grid_spec=grid_spec,
        out_shape=jax.ShapeDtypeStruct((b, sq, N_HEADS * V_HEAD), BF16),
        compiler_params=_cp(("parallel", "arbitrary")),
        name="mla_attn",
    )(qi_t, ki_t, fl_t, q, k, v)


def _decode_attn_body(q_ref, lat_ref, kr_ref, wukv_ref, o_ref, qa_scr, qr_scr, *, ss, n_valid):
    hw = N_HEADS * HEAD_PAD
    latb = lat_ref[0].astype(BF16)
    kr = kr_ref[0]
    krb = jnp.concatenate([jnp.zeros((kr.shape[0], ROPE_LANE0), F32), kr,
                           jnp.zeros((kr.shape[0], LANES - ROPE_LANE0 - QK_ROPE), F32)],
                          axis=-1).astype(BF16)
    contract_last = (((1,), (1,)), ((), ()))
    for h in range(N_HEADS):
        hs = slice(h * HEAD_PAD, (h + 1) * HEAD_PAD)
        rows = slice(h * ss, (h + 1) * ss)
        q_h = q_ref[0, :, hs]
        qa_scr[rows, :] = lax.dot_general(q_h, wukv_ref[:, hs], contract_last,
                                          preferred_element_type=F32).astype(BF16)
        qr_scr[rows, :] = q_h
    s = (lax.dot_general(qa_scr[...], latb, contract_last, preferred_element_type=F32)
         + lax.dot_general(qr_scr[...], krb, contract_last, preferred_element_type=F32))
    kpos = lax.broadcasted_iota(I32, s.shape, 1)
    s = jnp.where(kpos < n_valid, s, -jnp.inf)
    p = jnp.exp2(s - jnp.max(s, -1, keepdims=True))
    ol = jnp.dot(p.astype(BF16), latb, preferred_element_type=F32) / jnp.sum(p, -1, keepdims=True)
    for h in range(N_HEADS):
        rows = slice(h * ss, (h + 1) * ss)
        o_h = jnp.dot(ol[rows, :].astype(BF16), wukv_ref[:, hw + h * HEAD_PAD:hw + (h + 1) * HEAD_PAD],
                      preferred_element_type=F32)
        o_ref[0, :, h * V_HEAD:(h + 1) * V_HEAD] = o_h[:, 0:V_HEAD].astype(BF16)


def _decode_attention(q, lat_all, kr_all, w_ukv, n_valid):
    b, ss, _ = q.shape
    sk = lat_all.shape[1]
    return pl.pallas_call(
        functools.partial(_decode_attn_body, ss=ss, n_valid=n_valid),
        grid=(b,),
        in_specs=[pl.BlockSpec((1, ss, N_HEADS * HEAD_PAD), lambda bi: (bi, 0, 0)),
                  pl.BlockSpec((1, sk, KV_LORA), lambda bi: (bi, 0, 0)),
                  pl.BlockSpec((1, sk, QK_ROPE), lambda bi: (bi, 0, 0)),
                  pl.BlockSpec(w_ukv.shape, lambda bi: (0, 0))],
        out_specs=pl.BlockSpec((1, ss, N_HEADS * V_HEAD), lambda bi: (bi, 0, 0)),
        out_shape=jax.ShapeDtypeStruct((b, ss, N_HEADS * V_HEAD), BF16),
        scratch_shapes=[pltpu.VMEM((N_HEADS * ss, KV_LORA), BF16), pltpu.VMEM((N_HEADS * ss, HEAD_PAD), BF16)],
        compiler_params=_cp(("parallel",)),
        name="decode_attn",
    )(q, lat_all, kr_all, w_ukv)


def _memkv_body(mem_ref, wk_ref, wv_ref, mk_ref, mv_ref, mkb_ref, mvb_ref):
    m = mem_ref[...].astype(BF16)
    mk = jnp.dot(m, wk_ref[...], preferred_element_type=F32)
    mv = jnp.dot(m, wv_ref[...], preferred_element_type=F32)
    mk_ref[...] = mk
    mv_ref[...] = mv
    mkb_ref[...] = mk.astype(BF16)
    mvb_ref[...] = mv.astype(BF16)


def _memkv(mem2d, w_xk, w_xv):
    n = mem2d.shape[0]
    tt = _pick_tile(n, 256)
    row = pl.BlockSpec((tt, D_MODEL), lambda i: (i, 0))
    wspec = pl.BlockSpec((D_MODEL, D_MODEL), lambda i: (0, 0))
    return pl.pallas_call(
        _memkv_body,
        grid=(n // tt,),
        in_specs=[row, wspec, wspec],
        out_specs=(row, row, row, row),
        out_shape=(jax.ShapeDtypeStruct((n, D_MODEL), F32), jax.ShapeDtypeStruct((n, D_MODEL), F32),
                   jax.ShapeDtypeStruct((n, D_MODEL), BF16), jax.ShapeDtypeStruct((n, D_MODEL), BF16)),
        compiler_params=_cp(("parallel",)),
        name="memkv",
    )(mem2d, w_xk, w_xv)


def _layer_norm(x, g, b):
    mu = jnp.mean(x, -1, keepdims=True)
    xc = x - mu
    var = jnp.mean(xc * xc, -1, keepdims=True)
    return xc * lax.rsqrt(var + LN_EPS) * g + b


def _mid_body(x_ref, at_ref, yc_ref, mk_ref, mv_ref, cnt0_ref, low_ref, wo_ref, g1_ref, b1_ref, wq_ref, wxo_ref,
              g2_ref, b2_ref, wr_ref, br_ref, x2_ref, ri_ref, cnt_ref, rt_ref,
              cnt_scr, a_scr, xb_scr, q_scr, sc_scr, p_scr, o_scr, lg_scr, *, tt, alpha):
    step = pl.program_id(0)

    @pl.when(step == 0)
    def _():
        cnt_scr[...] = cnt0_ref[...]
        lg_scr[...] = jnp.zeros(lg_scr.shape, F32)

    cur_slot = step % 2
    prev_slot = 1 - cur_slot
    routed = (step > 0).astype(F32)

    n_part = MID_PARTS if tt % (MID_PARTS * SUBLANES * 2) == 0 else 1
    pr = tt // n_part
    ln_rs = min(pr, LN_ROW_SLAB)
    sm_rs = min(pr, XATTN_ROW_SLAB)
    c_exp = X_SCALE * math.log2(math.e)
    lane = lax.broadcasted_iota(I32, (pr, LANES), 1)

    def part_rows(k):
        return slice(k * pr, (k + 1) * pr)

    def out_proj(k):
        rp = part_rows(k)
        mix = jnp.concatenate([at_ref[0, rp, :], yc_ref[0, rp, :]], axis=-1)
        a_scr[rp, :] = jnp.dot(mix, wo_ref[...], preferred_element_type=F32)

    def norm1(k):
        for r in range(pr // ln_rs):
            rows = slice(k * pr + r * ln_rs, k * pr + (r + 1) * ln_rs)
            x1 = _layer_norm(alpha * x_ref[0, rows, :] + a_scr[rows, :], g1_ref[...], b1_ref[...])
            x2_ref[0, rows, :] = x1
            xb_scr[rows, :] = x1.astype(BF16)

    def q_proj(k):
        rp = part_rows(k)
        q_scr[rp, :] = jnp.dot(xb_scr[rp, :], wq_ref[...], preferred_element_type=F32).astype(BF16)

    def mem_head(ref, h):
        if len(ref.shape) == 4:
            return ref[0, :, h, :].astype(BF16)
        return ref[0, :, h * X_HEAD_DIM:(h + 1) * X_HEAD_DIM]

    def cross_attn(k):
        rp = part_rows(k)
        for h in range(X_HEADS):
            sl = slice(h * X_HEAD_DIM, (h + 1) * X_HEAD_DIM)
            b2 = h % 2
            sc_scr[k, b2] = lax.dot_general(q_scr[rp, sl], mem_head(mk_ref, h), (((1,), (1,)), ((), ())),
                                            preferred_element_type=F32)
            for r in range(pr // sm_rs):
                rows = slice(r * sm_rs, (r + 1) * sm_rs)
                s_r = sc_scr[k, b2, rows, :]
                e = jnp.exp2((s_r - jnp.max(s_r, -1, keepdims=True)) * c_exp)
                p_scr[k, b2, rows, :] = (e / jnp.sum(e, -1, keepdims=True)).astype(BF16)
            o_scr[rp, sl] = jnp.dot(p_scr[k, b2], mem_head(mv_ref, h), preferred_element_type=F32).astype(BF16)

    def x_out_proj(k):
        rp = part_rows(k)
        a_scr[rp, :] = jnp.dot(o_scr[rp, :], wxo_ref[...], preferred_element_type=F32)

    def norm2(k):
        for r in range(pr // ln_rs):
            rows = slice(k * pr + r * ln_rs, k * pr + (r + 1) * ln_rs)
            x2 = _layer_norm(alpha * x2_ref[0, rows, :] + a_scr[rows, :], g2_ref[...], b2_ref[...])
            x2_ref[0, rows, :] = x2
            xb_scr[rows, :] = x2.astype(BF16)

    def router_logits(k):
        rp = part_rows(k)
        lg_scr[cur_slot, rp, :] = jnp.dot(xb_scr[rp, :], wr_ref[...], preferred_element_type=F32) + br_ref[...]

    def route(k):
        rp = part_rows(k)
        logits = lg_scr[prev_slot, rp, :]
        neg = -jnp.inf
        is_g = lane < N_GROUPS
        lg = jnp.where(is_g, logits, neg)
        mg = jnp.max(lg, -1, keepdims=True)
        g_idx = jnp.min(jnp.where(lg == mg, lane, LANES), -1, keepdims=True)
        pg = 1.0 / jnp.sum(jnp.where(is_g, jnp.exp(logits - mg), 0.0), -1, keepdims=True)
        in_grp = ((lane >= ROUTER_LANE0) & (lane < ROUTER_LANE0 + N_EXPERTS)
                  & (((lane - ROUTER_LANE0) >> 3) == g_idx))
        le = jnp.where(in_grp, logits, neg)
        v1 = jnp.max(le, -1, keepdims=True)
        i1 = jnp.min(jnp.where(le == v1, lane, LANES), -1, keepdims=True)
        le2 = jnp.where(lane == i1, neg, le)
        v2 = jnp.max(le2, -1, keepdims=True)
        i2 = jnp.min(jnp.where(le2 == v2, lane, LANES), -1, keepdims=True)
        e2 = jnp.exp(v2 - v1)
        den = 1.0 + e2
        gate1 = (1.0 / den) * pg
        gate2 = (e2 / den) * pg
        oh1 = (lane == i1).astype(F32)
        oh2 = (lane == i2).astype(F32)
        oh = oh1 + oh2
        base = cnt_scr[...] + jnp.dot(low_ref[...], oh.astype(BF16), preferred_element_type=F32)
        rank1 = jnp.sum(oh1 * base, -1, keepdims=True)
        rank2 = jnp.sum(oh2 * base, -1, keepdims=True)
        cnt_scr[...] = cnt_scr[...] + jnp.sum(oh, 0, keepdims=True) * routed
        e1f = (i1 - ROUTER_LANE0).astype(F32)
        e2f = (i2 - ROUTER_LANE0).astype(F32)
        ri = jnp.where(lane == 0, e1f, jnp.where(lane == 1, e2f, jnp.where(
            lane == 2, rank1, jnp.where(lane == 3, rank2, jnp.where(lane == 4, gate1, jnp.where(
                lane == 5, gate2, 0.0))))))
        ri_ref[0, rp, :] = ri
        ri_t = jnp.transpose(ri)[0:SUBLANES, :]
        grp = min(pr, ROUTE_GROUP)
        for j in range(pr // grp):
            rt_ref[0, k * (pr // grp) + j] = ri_t[:, j * grp:(j + 1) * grp]

    stages = (out_proj, norm1, q_proj, cross_attn, x_out_proj, norm2, router_logits)
    matmul_stages = (out_proj, q_proj, x_out_proj, router_logits)
    for t in range(len(stages) + MID_SKEW * (n_part - 1)):
        todo = [(stages[t - MID_SKEW * k], k) for k in range(n_part) if 0 <= t - MID_SKEW * k < len(stages)]
        for fn, k in sorted(todo, key=lambda fk: fk[0] not in matmul_stages):
            fn(k)
        if t < n_part:
            route(t)
    cnt_ref[...] = cnt_scr[...]


def _mid(x, attn, yconv, mk_b, mv_b, cnt0, w, alpha):
    b, s, _ = x.shape
    tt = _pick_tile(s, MID_ROWS)
    nj = s // tt
    n_tiles = b * nj
    cur = lambda i: jnp.minimum(i, n_tiles - 1)
    prev = lambda i: jnp.maximum(i - 1, 0)
    row = lambda width: pl.BlockSpec((1, tt, width), lambda i: (cur(i) // nj, cur(i) % nj, 0))
    row_prev = lambda width: pl.BlockSpec((1, tt, width), lambda i: (prev(i) // nj, prev(i) % nj, 0))
    full = lambda shape: pl.BlockSpec(shape, lambda i: (0,) * len(shape))
    mem = pl.BlockSpec((1,) + mk_b.shape[1:], lambda i: (cur(i) // nj,) + (0,) * (mk_b.ndim - 1))
    vec = full((1, D_MODEL))
    n_part = MID_PARTS if tt % (MID_PARTS * SUBLANES * 2) == 0 else 1
    pr = tt // n_part
    grp = min(pr, ROUTE_GROUP)
    lower = jnp.tril(jnp.ones((pr, pr), F32), -1).astype(BF16)
    return pl.pallas_call(
        functools.partial(_mid_body, tt=tt, alpha=alpha),
        grid=(n_tiles + 1,),
        in_specs=[row(D_MODEL), row(N_HEADS * V_HEAD), row(CONV_WIDTH), mem, mem, full((1, LANES)), full((pr, pr)),
                  full((D_MODEL, D_MODEL)), vec, vec, full((D_MODEL, D_MODEL)), full((D_MODEL, D_MODEL)),
                  vec, vec, full((D_MODEL, LANES)), full((1, LANES))],
        out_specs=(row(D_MODEL), row_prev(LANES), full((1, LANES)),
                   pl.BlockSpec((1, tt // grp, SUBLANES, grp), lambda i: (prev(i) // nj, prev(i) % nj, 0, 0))),
        out_shape=(jax.ShapeDtypeStruct((b, s, D_MODEL), F32), jax.ShapeDtypeStruct((b, s, LANES), F32),
                   jax.ShapeDtypeStruct((1, LANES), F32),
                   jax.ShapeDtypeStruct((b, s // grp, SUBLANES, grp), F32)),
        scratch_shapes=[
            pltpu.VMEM((1, LANES), F32),
            pltpu.VMEM((tt, D_MODEL), F32),
            pltpu.VMEM((tt, D_MODEL), BF16),
            pltpu.VMEM((tt, D_MODEL), BF16),
            pltpu.VMEM((n_part, 2, pr, N_MEM), F32),
            pltpu.VMEM((n_part, 2, pr, N_MEM), BF16),
            pltpu.VMEM((tt, D_MODEL), BF16),
            pltpu.VMEM((2, tt, LANES), F32),
        ],
        compiler_params=_cp(("arbitrary",)),
        name="mid",
    )(x, attn, yconv, mk_b, mv_b, cnt0, lower, w['w_out'], w['ln1_g'], w['ln1_b'], w['w_xq'], w['w_xo'],
      w['ln2_g'], w['ln2_b'], w['w_router'], w['b_router'])


def _row_slice(ref, row):
    return ref.at[pl.ds(pl.multiple_of(row * SUBLANES, SUBLANES), SUBLANES), :]


def _slot_index(t, kk, g, n):
    if g >= n:
        return kk * g + t
    shift = g.bit_length() - 1
    return ((t >> shift) << (shift + 1)) + kk * g + (t & (g - 1))


def _rows_slice(ref, row, n_rows):
    return ref.at[pl.ds(pl.multiple_of(row * SUBLANES, SUBLANES), n_rows * SUBLANES), :]


def _dispatch_body(pad_ref, dest_ref, dest_s_ref, x_ref, x_s_ref, xs_ref, buf0, buf1, zbuf, sem0, sem1, zsem,
                   *, tt, ts, nt, nblk, rows, g_p, g_s):
    i = pl.program_id(0)
    bufs = (buf0, buf1)
    sems = (sem0, sem1)
    len_bits = rows.bit_length() - 1

    def scatter_rows(buf, sem, src_ref, d_ref, n, g):
        for c in range(ROW_CHUNKS):
            buf[pl.ds(c, n, stride=SUBLANES), :] = src_ref[:, c * LANES:(c + 1) * LANES]
        unroll = min(n, GATHER_UNROLL)

        def issue(j, carry):
            for u in range(unroll):
                t = j * unroll + u
                src = _row_slice(buf, t)
                for kk in range(2):
                    d = d_ref[0, 0, _slot_index(t, kk, g, n)]
                    pltpu.make_async_copy(src, _row_slice(xs_ref, d), sem).start(priority=kk)
            return carry

        lax.fori_loop(0, n // unroll, issue, 0)

    def wait_rows(buf, sem, n):
        for _ in range(2):
            pltpu.make_async_copy(_rows_slice(buf, 0, n), _rows_slice(xs_ref, 0, n), sem).wait()

    def zero_fill(wait):
        def fire(copy):
            if wait:
                copy.wait()
            else:
                copy.start()

        def per_expert(e, carry):
            first = pad_ref[e]
            n_pad = pad_ref[N_EXPERTS + e]
            for bit in range(len_bits):
                size = 1 << bit
                off = (n_pad >> (bit + 1)) << (bit + 1)

                @pl.when(((n_pad >> bit) & 1) == 1)
                def _():
                    fire(pltpu.make_async_copy(_rows_slice(zbuf, 0, size), _rows_slice(xs_ref, first + off, size),
                                               zsem))
            return carry

        lax.fori_loop(0, N_EXPERTS, per_expert, 0)

        def per_block(j, carry):
            fire(pltpu.make_async_copy(zbuf, _rows_slice(xs_ref, j * rows, rows), zsem))
            return carry

        lax.fori_loop(pad_ref[2 * N_EXPERTS], nblk, per_block, 0)

    def run(slot):
        @pl.when(i < nt)
        def _():
            scatter_rows(bufs[slot], sems[slot], x_ref, dest_ref, tt, g_p)

        @pl.when(i == nt)
        def _():
            zbuf[...] = jnp.zeros(zbuf.shape, F32)
            scatter_rows(bufs[slot], sems[slot], x_s_ref, dest_s_ref, ts, g_s)
            zero_fill(False)

        @pl.when(i > 0)
        def _():
            wait_rows(bufs[1 - slot], sems[1 - slot], tt)

        @pl.when(i == nt)
        def _():
            wait_rows(bufs[slot], sems[slot], ts)
            zero_fill(True)

    @pl.when(i % 2 == 0)
    def _():
        run(0)

    @pl.when(i % 2 == 1)
    def _():
        run(1)


def _dispatch(x_p, dest_p, g_p, x_s, dest_s, g_s, pad_info, nblk, rows):
    n_p, n_s = x_p.shape[0], x_s.shape[0]
    tt = _pick_tile(n_p, 256)
    nt = n_p // tt
    assert n_s <= tt
    last = lambda i, pad: jnp.minimum(i, nt - 1)
    grid_spec = pltpu.PrefetchScalarGridSpec(
        num_scalar_prefetch=1,
        grid=(nt + 1,),
        in_specs=[
            pl.BlockSpec((1, 1, 2 * tt), lambda i, pad: (last(i, pad), 0, 0), memory_space=pltpu.SMEM),
            pl.BlockSpec((1, 1, 2 * n_s), lambda i, pad: (0, 0, 0), memory_space=pltpu.SMEM),
            pl.BlockSpec((tt, D_MODEL), lambda i, pad: (last(i, pad), 0)),
            pl.BlockSpec((n_s, D_MODEL), lambda i, pad: (0, 0)),
        ],
        out_specs=pl.BlockSpec(memory_space=pl.ANY),
        scratch_shapes=[pltpu.VMEM((tt * SUBLANES, LANES), F32), pltpu.VMEM((tt * SUBLANES, LANES), F32),
                        pltpu.VMEM((rows * SUBLANES, LANES), F32),
                        pltpu.SemaphoreType.DMA, pltpu.SemaphoreType.DMA, pltpu.SemaphoreType.DMA],
    )
    return pl.pallas_call(
        functools.partial(_dispatch_body, tt=tt, ts=n_s, nt=nt, nblk=nblk, rows=rows, g_p=g_p, g_s=g_s),
        grid_spec=grid_spec,
        out_shape=jax.ShapeDtypeStruct((nblk * rows * SUBLANES, LANES), F32),
        compiler_params=_cp(("arbitrary",)),
        name="moe_dispatch",
    )(pad_info, dest_p.reshape(nt, 1, 2 * tt), dest_s.reshape(1, 1, 2 * n_s), x_p, x_s)


def _experts_body(be_ref, nu_ref, x_ref, wg_ref, wu_ref, wd_ref, y_ref, xb_scr, g_scr, u_scr, h_scr,
                  wgb_scr, wub_scr, wdb_scr, *, rows):
    i = pl.program_id(0)
    cur = jnp.minimum(i, nu_ref[0] - 1)
    new_expert = (i == 0) | (be_ref[cur] != be_ref[jnp.maximum(cur - 1, 0)])

    @pl.when((i < nu_ref[0]) & new_expert)
    def _():
        for r in range(0, D_MODEL, WCAST_ROWS):
            wgb_scr[r:r + WCAST_ROWS, :] = wg_ref[0, r:r + WCAST_ROWS, :].astype(BF16)
            wub_scr[r:r + WCAST_ROWS, :] = wu_ref[0, r:r + WCAST_ROWS, :].astype(BF16)
        for r in range(0, D_EXPERT, WCAST_ROWS // 2):
            wdb_scr[r:r + WCAST_ROWS // 2, :] = wd_ref[0, r:r + WCAST_ROWS // 2, :].astype(BF16)
    n_part = MOE_PARTS
    pr = rows // n_part
    act_rs = min(pr, ACT_ROW_SLAB)

    def load(k):
        base = k * pr * SUBLANES
        xb_scr[k] = jnp.concatenate([x_ref[pl.ds(base + c, pr, stride=SUBLANES), :] for c in range(ROW_CHUNKS)],
                                    axis=-1).astype(BF16)

    def gate_up(k):
        g_scr[k] = jnp.dot(xb_scr[k], wgb_scr[...], preferred_element_type=F32)
        u_scr[k] = jnp.dot(xb_scr[k], wub_scr[...], preferred_element_type=F32)

    def act(k):
        for r in range(pr // act_rs):
            rows_r = slice(r * act_rs, (r + 1) * act_rs)
            g = g_scr[k, rows_r, :]
            h_scr[k, rows_r, :] = ((g * jax.nn.sigmoid(g)) * u_scr[k, rows_r, :]).astype(BF16)

    def down(k):
        y = jnp.dot(h_scr[k], wdb_scr[...], preferred_element_type=F32)
        base = k * pr * SUBLANES
        for c in range(ROW_CHUNKS):
            y_ref[pl.ds(base + c, pr, stride=SUBLANES), :] = y[:, c * LANES:(c + 1) * LANES]

    @pl.when(i < nu_ref[0])
    def _():
        stages = (load, gate_up, act, down)
        matmul_stages = (gate_up, down)
        for t in range(len(stages) + n_part - 1):
            todo = [(stages[t - k], k) for k in range(n_part) if 0 <= t - k < len(stages)]
            for fn, k in sorted(todo, key=lambda fk: fk[0] not in matmul_stages):
                fn(k)

    @pl.when(i >= nu_ref[0])
    def _():
        y_ref[...] = jnp.zeros(y_ref.shape, F32)


def _experts(xs, block_e, n_used, wg, wu, wd, rows):
    nblk = xs.shape[0] // (rows * SUBLANES)
    clamp = lambda i, nu: jnp.minimum(i, nu[0] - 1)
    grid_spec = pltpu.PrefetchScalarGridSpec(
        num_scalar_prefetch=2,
        grid=(nblk,),
        in_specs=[
            pl.BlockSpec((rows * SUBLANES, LANES), lambda i, be, nu: (clamp(i, nu), 0)),
            pl.BlockSpec((1, D_MODEL, D_EXPERT), lambda i, be, nu: (be[clamp(i, nu)], 0, 0)),
            pl.BlockSpec((1, D_MODEL, D_EXPERT), lambda i, be, nu: (be[clamp(i, nu)], 0, 0)),
            pl.BlockSpec((1, D_EXPERT, D_MODEL), lambda i, be, nu: (be[clamp(i, nu)], 0, 0)),
        ],
        out_specs=pl.BlockSpec((rows * SUBLANES, LANES), lambda i, be, nu: (i, 0)),
        scratch_shapes=[
            pltpu.VMEM((MOE_PARTS, rows // MOE_PARTS, D_MODEL), BF16),
            pltpu.VMEM((MOE_PARTS, rows // MOE_PARTS, D_EXPERT), F32),
            pltpu.VMEM((MOE_PARTS, rows // MOE_PARTS, D_EXPERT), F32),
            pltpu.VMEM((MOE_PARTS, rows // MOE_PARTS, D_EXPERT), BF16),
            pltpu.VMEM((D_MODEL, D_EXPERT), BF16),
            pltpu.VMEM((D_MODEL, D_EXPERT), BF16),
            pltpu.VMEM((D_EXPERT, D_MODEL), BF16),
        ],
    )
    return pl.pallas_call(
        functools.partial(_experts_body, rows=rows),
        grid_spec=grid_spec,
        out_shape=jax.ShapeDtypeStruct(xs.shape, F32),
        compiler_params=_cp(("arbitrary",)),
        name="moe_experts",
    )(block_e, n_used, xs, wg, wu, wd)


def _combine_body(dest_ref, destn_ref, x_ref, ri_ref, g3_ref, b3_ref, ys_ref, o_ref,
                  b00, b01, b10, b11, sem0, sem1, *, tt, nt, g, alpha):
    i = pl.program_id(0)
    bufs = ((b00, b01), (b10, b11))
    sems = (sem0, sem1)
    rs = min(tt, LN_ROW_SLAB)
    unroll = min(tt, GATHER_UNROLL)

    def gather_rows(dref, slot):
        def copy(t, kk):
            d = dref[0, 0, _slot_index(t, kk, g, tt)]
            return pltpu.make_async_copy(_row_slice(ys_ref, d), _row_slice(bufs[slot][kk], t), sems[slot])

        def body(j, carry):
            for u in range(unroll):
                for kk in range(2):
                    copy(j * unroll + u, kk).start(priority=kk)
            return carry

        lax.fori_loop(0, tt // unroll, body, 0)

    def wait_rows(slot):
        for kk in range(2):
            pltpu.make_async_copy(ys_ref.at[pl.ds(0, tt * SUBLANES), :], bufs[slot][kk], sems[slot]).wait()

    def run(slot):
        if slot == 0:
            @pl.when(i == 0)
            def _():
                gather_rows(dest_ref, 0)

        @pl.when(i + 1 < nt)
        def _():
            gather_rows(destn_ref, 1 - slot)

        wait_rows(slot)
        for r in range(tt // rs):
            rows = slice(r * rs, (r + 1) * rs)
            base = r * rs * SUBLANES
            y0 = jnp.concatenate([bufs[slot][0][pl.ds(base + c, rs, stride=SUBLANES), :]
                                  for c in range(ROW_CHUNKS)], axis=-1)
            y1 = jnp.concatenate([bufs[slot][1][pl.ds(base + c, rs, stride=SUBLANES), :]
                                  for c in range(ROW_CHUNKS)], axis=-1)
            ri = ri_ref[rows, :]
            moe = y0 * ri[:, 4:5] + y1 * ri[:, 5:6]
            o_ref[rows, :] = _layer_norm(alpha * x_ref[rows, :] + moe, g3_ref[...], b3_ref[...])

    @pl.when(i % 2 == 0)
    def _():
        run(0)

    @pl.when(i % 2 == 1)
    def _():
        run(1)


def _combine(x2d, rinfo, dest, g, ys, g3, b3, alpha):
    n = x2d.shape[0]
    tt = _pick_tile(n, 256)
    nt = n // tt
    dest3 = dest.reshape(nt, 1, 2 * tt)
    vec = pl.BlockSpec((1, D_MODEL), lambda i: (0, 0))
    stage = pltpu.VMEM((tt * SUBLANES, LANES), F32)
    return pl.pallas_call(
        functools.partial(_combine_body, tt=tt, nt=nt, g=g, alpha=alpha),
        grid=(nt,),
        in_specs=[
            pl.BlockSpec((1, 1, 2 * tt), lambda i: (i, 0, 0), memory_space=pltpu.SMEM),
            pl.BlockSpec((1, 1, 2 * tt), lambda i: (jnp.minimum(i + 1, nt - 1), 0, 0), memory_space=pltpu.SMEM),
            pl.BlockSpec((tt, D_MODEL), lambda i: (i, 0)),
            pl.BlockSpec((tt, LANES), lambda i: (i, 0)),
            vec, vec,
            pl.BlockSpec(memory_space=pl.ANY),
        ],
        out_specs=pl.BlockSpec((tt, D_MODEL), lambda i: (i, 0)),
        out_shape=jax.ShapeDtypeStruct((n, D_MODEL), F32),
        scratch_shapes=[stage, stage, stage, stage, pltpu.SemaphoreType.DMA, pltpu.SemaphoreType.DMA],
        compiler_params=_cp(("arbitrary",)),
        name="moe_combine",
    )(dest3, dest3, x2d, rinfo, g3, b3, ys)


def _rope_tables(pos):
    half = QK_ROPE // 2
    inv = ROPE_THETA ** (-jnp.arange(half, dtype=F32) / half)
    ang = pos.astype(F32)[:, None] * inv[None, :]
    cos, sin = jnp.cos(ang), jnp.sin(ang)
    n = pos.shape[0]
    pad_r = LANES - ROPE_LANE0 - QK_ROPE
    cos_t = jnp.concatenate([jnp.ones((n, ROPE_LANE0), F32), cos, cos, jnp.zeros((n, pad_r), F32)], -1)
    sin_t = jnp.concatenate([jnp.zeros((n, ROPE_LANE0), F32), sin, sin, jnp.zeros((n, pad_r), F32)], -1)
    return cos_t, sin_t


def _swap_neg(wr):
    half = QK_ROPE // 2
    return jnp.concatenate([-wr[:, half:], wr[:, :half]], axis=1)


def _prep_weights(l, w_in, q_norm_g, kv_norm_g, w_uq, w_ukv, conv_w, w_out, ln1_g, ln1_b, w_xq, w_xk, w_xv, w_xo,
                  ln2_g, ln2_b, w_router_group, b_router_group, w_router_expert, b_router_expert,
                  w_exp_gate, w_exp_up, w_exp_down, ln3_g, ln3_b):
    wi = w_in[l]
    c0 = Q_LORA + KV_LORA
    w_kr = wi[:, c0:c0 + QK_ROPE]
    zl = jnp.zeros((D_MODEL, ROPE_LANE0), F32)
    w_in_p = jnp.concatenate([wi[:, :c0], wi[:, c0 + QK_ROPE:], zl, w_kr, _swap_neg(w_kr)], axis=1)
    wq = w_uq[l].reshape(Q_LORA, N_HEADS, QK_NOPE + QK_ROPE)
    wq_rot = jnp.concatenate([-wq[:, :, QK_NOPE + QK_ROPE // 2:], wq[:, :, QK_NOPE:QK_NOPE + QK_ROPE // 2]], axis=2)
    wq_a = jnp.concatenate([wq, wq_rot], axis=2).reshape(Q_LORA, N_HEADS * HEAD_PAD)
    wkv = w_ukv[l].reshape(KV_LORA, N_HEADS, QK_NOPE + V_HEAD)
    wk_p = jnp.concatenate([wkv[:, :, :QK_NOPE], jnp.zeros((KV_LORA, N_HEADS, HEAD_PAD - QK_NOPE), F32)], axis=2)
    wk_p = wk_p.reshape(KV_LORA, N_HEADS * HEAD_PAD)
    wv_p = jnp.concatenate([wkv[:, :, QK_NOPE:], jnp.zeros((KV_LORA, N_HEADS, HEAD_PAD - V_HEAD), F32)], axis=2)
    wv_p = wv_p.reshape(KV_LORA, N_HEADS * HEAD_PAD)
    w_router = jnp.concatenate([w_router_group[l], w_router_expert[l],
                                jnp.zeros((D_MODEL, LANES - N_GROUPS - N_EXPERTS), F32)], axis=1)
    b_router = jnp.concatenate([b_router_group[l], b_router_expert[l].reshape(-1),
                                jnp.zeros((LANES - N_GROUPS - N_EXPERTS,), F32)]).reshape(1, LANES)
    return dict(
        w_in=w_in_p.astype(BF16),
        q_norm_g=q_norm_g[l].reshape(1, Q_LORA), kv_norm_g=kv_norm_g[l].reshape(1, KV_LORA),
        w_uq=wq_a.astype(BF16),
        w_ukv=jnp.concatenate([wk_p, wv_p], axis=1).astype(BF16),
        conv_w=conv_w[l],
        w_out=w_out[l].astype(BF16), ln1_g=ln1_g[l].reshape(1, -1), ln1_b=ln1_b[l].reshape(1, -1),
        w_xq=w_xq[l].astype(BF16), w_xk=w_xk[l].astype(BF16), w_xv=w_xv[l].astype(BF16),
        w_xo=w_xo[l].astype(BF16), ln2_g=ln2_g[l].reshape(1, -1), ln2_b=ln2_b[l].reshape(1, -1),
        w_router=w_router.astype(BF16), b_router=b_router,
        w_exp_gate=w_exp_gate[l], w_exp_up=w_exp_up[l], w_exp_down=w_exp_down[l],
        ln3_g=ln3_g[l].reshape(1, -1), ln3_b=ln3_b[l].reshape(1, -1),
    )


def _slots(rt, pstarts):
    g = rt.shape[-1]
    rt = rt.reshape(-1, SUBLANES, g)
    dest = pstarts[rt[:, 0:2, :].astype(I32)] + rt[:, 2:4, :].astype(I32)
    return dest.reshape(-1), g


def _layer(l, depth, xp, xs, lat_past, kr_past, conv_past, mk_s, mv_s, mem_prompt, w):
    alpha = (2 * depth) ** 0.25
    b, s, _ = xp.shape
    bs, ss, _ = xs.shape
    past = lat_past.shape[1]

    cos_p, sin_p = _rope_tables(jnp.arange(s))
    q_p, k_p, v_p, yc_p, lat_p, kr_p, cst_p = _inproj(
        xp, jnp.zeros((b, CONV_K - 1, CONV_WIDTH), F32), cos_p, sin_p, w)
    attn_p = _attention(q_p, k_p, v_p, 0, None)
    mk, mv, mk_b, mv_b = _memkv(mem_prompt.reshape(b * N_MEM, D_MODEL), w['w_xk'], w['w_xv'])
    cnt0 = jnp.zeros((1, LANES), F32)
    x2_p, ri_p, cnt_p, rt_p = _mid(xp, attn_p, yc_p, mk_b.reshape(b, N_MEM, D_MODEL), mv_b.reshape(b, N_MEM, D_MODEL),
                             cnt0, w, alpha)

    cos_s, sin_s = _rope_tables(past + jnp.arange(ss))
    q_s, _, _, yc_s, lat_s, kr_s, cst_s = _inproj(xs, conv_past, cos_s, sin_s, w)
    n_keys = past + ss
    sk = -(-n_keys // LANES) * LANES
    lat_all = jnp.concatenate([lat_past, lat_s, jnp.zeros((bs, sk - n_keys, KV_LORA), F32)], axis=1)
    kr_all = jnp.concatenate([kr_past, kr_s, jnp.zeros((bs, sk - n_keys, QK_ROPE), F32)], axis=1)
    attn_s = _decode_attention(q_s, lat_all, kr_all, w['w_ukv'], n_keys)
    x2_s, ri_s, cnt, rt_s = _mid(xs, attn_s, yc_s, mk_s, mv_s, cnt_p, w, alpha)

    n_p, n_s = b * s, bs * ss
    counts = cnt[0, ROUTER_LANE0:ROUTER_LANE0 + N_EXPERTS].astype(I32)
    padded = (counts + MOE_ROWS - 1) // MOE_ROWS * MOE_ROWS
    pends = jnp.cumsum(padded)
    pstarts = pends - padded
    nblk = -(-2 * (n_p + n_s) // MOE_ROWS) + N_EXPERTS
    blk_start = jnp.arange(nblk, dtype=I32) * MOE_ROWS
    block_e = jnp.minimum(jnp.sum((pends[None, :] <= blk_start[:, None]).astype(I32), axis=1), N_EXPERTS - 1)
    n_used = (pends[-1] // MOE_ROWS).astype(I32).reshape(1)
    ri_p2, ri_s2 = ri_p.reshape(n_p, LANES), ri_s.reshape(n_s, LANES)
    dest_p, g_p = _slots(rt_p, pstarts)
    dest_s, g_s = _slots(rt_s, pstarts)
    x2_p2, x2_s2 = x2_p.reshape(n_p, D_MODEL), x2_s.reshape(n_s, D_MODEL)
    pad_info = jnp.concatenate([pstarts + counts, padded - counts, n_used]).astype(I32)
    slots = _dispatch(x2_p2, dest_p, g_p, x2_s2, dest_s, g_s, pad_info, nblk, MOE_ROWS)
    ys = _experts(slots, block_e, n_used, w['w_exp_gate'], w['w_exp_up'], w['w_exp_down'], MOE_ROWS)
    y_p = _combine(x2_p2, ri_p2, dest_p, g_p, ys, w['ln3_g'], w['ln3_b'], alpha).reshape(b, s, D_MODEL)
    y_s = _combine(x2_s2, ri_s2, dest_s, g_s, ys, w['ln3_g'], w['ln3_b'], alpha).reshape(bs, ss, D_MODEL)
    return (y_p, y_s, lat_p, kr_p, cst_p, mk.reshape(b, N_MEM, X_HEADS, X_HEAD_DIM),
            mv.reshape(b, N_MEM, X_HEADS, X_HEAD_DIM), lat_s, kr_s, cst_s)


def kernel(x_prompt, x_sample, cache_kv_latent, cache_k_rope, cache_conv, cache_mem_k, cache_mem_v, mem_prompt,
           w_in, q_norm_g, kv_norm_g, w_uq, w_ukv, conv_w, w_out, ln1_g, ln1_b, w_xq, w_xk, w_xv, w_xo, ln2_g,
           ln2_b, w_router_group, b_router_group, w_router_expert, b_router_expert, w_exp_gate, w_exp_up,
           w_exp_down, ln3_g, ln3_b):
    depth = w_in.shape[0]
    xp, xs = x_prompt, x_sample
    outs = [[] for _ in range(8)]
    for l in range(depth):
        w = _prep_weights(l, w_in, q_norm_g, kv_norm_g, w_uq, w_ukv, conv_w, w_out, ln1_g, ln1_b, w_xq, w_xk, w_xv,
                          w_xo, ln2_g, ln2_b, w_router_group, b_router_group, w_router_expert, b_router_expert,
                          w_exp_gate, w_exp_up, w_exp_down, ln3_g, ln3_b)
        res = _layer(l, depth, xp, xs, cache_kv_latent[l], cache_k_rope[l], cache_conv[l], cache_mem_k[l],
                     cache_mem_v[l], mem_prompt, w)
        xp, xs = res[0], res[1]
        for acc, r in zip(outs, res[2:]):
            acc.append(r)
    return (xp, xs) + tuple(jnp.stack(o) for o in outs)
```

```python
import functools
import math

import numpy as np
import jax
import jax.numpy as jnp
from jax import lax
from jax.experimental import pallas as pl
from jax.experimental.pallas import tpu as pltpu

F32 = jnp.float32
BF16 = jnp.bfloat16
I32 = jnp.int32

D_MODEL = 1024
CHUNK = 64
N_HEADS = 8
QK_NOPE = 64
QK_ROPE = 32
V_HEAD = 64
Q_LORA = 256
KV_LORA = 128
ROPE_THETA = 10000.0
MLA_SCALE = (QK_NOPE + QK_ROPE) ** -0.5
Q_PRESCALE = MLA_SCALE * math.log2(math.e)
CONV_WIDTH = 512
CONV_K = 3
N_MEM = 256
X_HEADS = 4
X_HEAD_DIM = D_MODEL // X_HEADS
X_SCALE = X_HEAD_DIM ** -0.5
N_GROUPS = 4
EXPERTS_PER_GROUP = 8
N_EXPERTS = N_GROUPS * EXPERTS_PER_GROUP
D_EXPERT = 512
LN_EPS = 1e-5
RMS_EPS = 1e-6

LANES = 128
SUBLANES = 8
ROW_CHUNKS = D_MODEL // LANES
HEAD_PAD = LANES
ROPE_LANE0 = QK_NOPE
ROUTER_LANE0 = N_GROUPS
VMEM_LIMIT = 56 * 1024 * 1024
MOE_ROWS = 512
MOE_PARTS = 2
ACT_ROW_SLAB = 32
GATHER_UNROLL = 8
WCAST_ROWS = 64
ATTN_TQ = 1024
ATTN_TK = 512
ATTN_SUB = 2
ATTN_ROW_SLAB = 64
LN_ROW_SLAB = 16
XATTN_ROW_SLAB = 64
INPROJ_ROWS = 512
ROUTE_GROUP = 256
MID_SKEW = 1
MID_ROWS = 512
MID_PARTS = 2


def _cp(sem, vmem=VMEM_LIMIT):
    return pltpu.CompilerParams(dimension_semantics=sem, vmem_limit_bytes=vmem)


def _pick_tile(n, pref):
    t = min(n, pref)
    while n % t:
        t //= 2
    return t


def _with_ones_lane(v):
    lane = lax.broadcasted_iota(I32, v.shape, 1)
    return jnp.where((lane & (HEAD_PAD - 1)) == V_HEAD, 1.0, v)


def _inproj_body(x_ref, cinit_ref, cos_ref, sin_ref, win_ref, qg_ref, kvg_ref, wuq_ref, wukv_ref, cw_ref,
                 q_ref, k_ref, v_ref, yc_ref, lat_ref, kr_ref, cst_ref, u_scr, *, tt, nj):
    j = pl.program_id(1)

    @pl.when(j == 0)
    def _():
        u_scr[6:8, :] = cinit_ref[0]

    x = x_ref[0].astype(BF16)
    proj = jnp.dot(x, win_ref[...], preferred_element_type=F32)
    cq = proj[:, 0:256]
    ckv = proj[:, 256:384]
    gb = proj[:, 384:896]
    gc = proj[:, 896:1408]
    gv = proj[:, 1408:1920]
    kr_blk = proj[:, 1920:2048]
    cos_t = cos_ref[...]
    sin_t = sin_ref[...]

    def rotate(blk):
        return blk * cos_t + pltpu.roll(blk, LANES - QK_ROPE, 1) * sin_t

    cqn = cq * lax.rsqrt(jnp.mean(cq * cq, -1, keepdims=True) + RMS_EPS) * qg_ref[...]
    ckvn = ckv * lax.rsqrt(jnp.mean(ckv * ckv, -1, keepdims=True) + RMS_EPS) * kvg_ref[...]
    lat_ref[0] = ckvn
    kr_p = rotate(kr_blk)
    kr_ref[0] = kr_p[:, ROPE_LANE0:ROPE_LANE0 + QK_ROPE]
    qq = jnp.dot(cqn.astype(BF16), wuq_ref[...], preferred_element_type=F32)
    kv = jnp.dot(ckvn.astype(BF16), wukv_ref[...], preferred_element_type=F32)
    hw = N_HEADS * HEAD_PAD
    for h in range(N_HEADS):
        sl = slice(h * HEAD_PAD, (h + 1) * HEAD_PAD)
        q_ref[0, :, sl] = (rotate(qq[:, sl]) * Q_PRESCALE).astype(BF16)
        k_ref[0, :, sl] = (kv[:, sl] + kr_p).astype(BF16)
    v_ref[0] = _with_ones_lane(kv[:, hw:2 * hw]).astype(BF16)
    u = gc * gv
    u_scr[8:8 + tt, :] = u
    conv = cw_ref[0:1, :] * u_scr[6:6 + tt, :] + cw_ref[1:2, :] * u_scr[7:7 + tt, :] + cw_ref[2:3, :] * u
    yc_ref[0] = (gb * conv).astype(BF16)
    last2 = u_scr[tt + 6:tt + 8, :]
    u_scr[6:8, :] = last2

    @pl.when(j == nj - 1)
    def _():
        cst_ref[0] = last2


def _inproj(x, conv_init, cos_t, sin_t, w):
    b, s, _ = x.shape
    tt = _pick_tile(s, INPROJ_ROWS)
    nj = s // tt
    wn = w['w_in'].shape[1]
    full = lambda shape: pl.BlockSpec(shape, lambda bi, ji: (0,) * len(shape))
    out_shapes = (
        jax.ShapeDtypeStruct((b, s, N_HEADS * HEAD_PAD), BF16),
        jax.ShapeDtypeStruct((b, s, N_HEADS * HEAD_PAD), BF16),
        jax.ShapeDtypeStruct((b, s, N_HEADS * HEAD_PAD), BF16),
        jax.ShapeDtypeStruct((b, s, CONV_WIDTH), BF16),
        jax.ShapeDtypeStruct((b, s, KV_LORA), F32),
        jax.ShapeDtypeStruct((b, s, QK_ROPE), F32),
        jax.ShapeDtypeStruct((b, CONV_K - 1, CONV_WIDTH), F32),
    )
    row = lambda width: pl.BlockSpec((1, tt, width), lambda bi, ji: (bi, ji, 0))
    return pl.pallas_call(
        functools.partial(_inproj_body, tt=tt, nj=nj),
        grid=(b, nj),
        in_specs=[
            row(D_MODEL),
            pl.BlockSpec((1, CONV_K - 1, CONV_WIDTH), lambda bi, ji: (bi, 0, 0)),
            pl.BlockSpec((tt, LANES), lambda bi, ji: (ji, 0)),
            pl.BlockSpec((tt, LANES), lambda bi, ji: (ji, 0)),
            full((D_MODEL, wn)),
            full((1, Q_LORA)),
            full((1, KV_LORA)),
            full(w['w_uq'].shape),
            full(w['w_ukv'].shape),
            full((CONV_K, CONV_WIDTH)),
        ],
        out_specs=(
            row(N_HEADS * HEAD_PAD), row(N_HEADS * HEAD_PAD), row(N_HEADS * HEAD_PAD), row(CONV_WIDTH),
            row(KV_LORA), row(QK_ROPE),
            pl.BlockSpec((1, CONV_K - 1, CONV_WIDTH), lambda bi, ji: (bi, 0, 0)),
        ),
        out_shape=out_shapes,
        scratch_shapes=[pltpu.VMEM((tt + 8, CONV_WIDTH), F32)],
        compiler_params=_cp(("parallel", "arbitrary")),
        name="inproj",
    )(x, conv_init, cos_t, sin_t, w['w_in'], w['q_norm_g'], w['kv_norm_g'], w['w_uq'], w['w_ukv'], w['conv_w'])


def _attn_body(qi_ref, ki_ref, fl_ref, q_ref, k_ref, v_ref, o_ref, m_scr, acc_scr, s_scr, p_scr, a_scr,
               *, tq, tk, n_sub, rs, q_pos0, n_valid, combos):
    step = pl.program_id(1)
    qi = qi_ref[step]
    ki = ki_ref[step]
    flags = fl_ref[step]
    reps = tk // LANES
    th = tq // n_sub

    @pl.when((flags & 1) != 0)
    def _():
        m_scr[...] = jnp.full(m_scr.shape, -jnp.inf, F32)
        acc_scr[...] = jnp.zeros(acc_scr.shape, F32)

    def scores(item, buf):
        sub, h, _ = item
        hs = slice(h * HEAD_PAD, (h + 1) * HEAD_PAD)
        s_scr[buf] = lax.dot_general(q_ref[0, sub * th:(sub + 1) * th, hs], k_ref[0, :, hs],
                                     (((1,), (1,)), ((), ())), preferred_element_type=F32)

    def softmax_pv(item, buf):
        sub, h, masked = item
        hs = slice(h * HEAD_PAD, (h + 1) * HEAD_PAD)
        for r in range(th // rs):
            rows = slice(r * rs, (r + 1) * rs)
            arows = slice(sub * th + r * rs, sub * th + (r + 1) * rs)
            s_r = s_scr[buf, rows, :]
            if masked:
                qpos = q_pos0 + qi * tq + sub * th + r * rs + lax.broadcasted_iota(I32, (rs, tk), 0)
                kpos = ki * tk + lax.broadcasted_iota(I32, (rs, tk), 1)
                mask = (kpos >> 6) <= (qpos >> 6)
                if n_valid is not None:
                    mask = mask & (kpos < n_valid)
                s_r = jnp.where(mask, s_r, -jnp.inf)
            m_old = m_scr[h, arows, :]
            m_new = jnp.maximum(m_old, jnp.max(s_r, axis=-1, keepdims=True))
            a_scr[buf, rows, :] = jnp.exp2(m_old - m_new)
            m_rep = jnp.concatenate([m_new] * reps, axis=1)
            p_scr[buf, rows, :] = jnp.exp2(s_r - m_rep).astype(BF16)
            m_scr[h, arows, :] = m_new
        pv = jnp.dot(p_scr[buf], v_ref[0, :, hs], preferred_element_type=F32)
        srows = slice(sub * th, (sub + 1) * th)
        acc_scr[h, srows, :] = a_scr[buf] * acc_scr[h, srows, :] + pv

    def run(modes):
        items = [(sub, h, mode == 2) for sub, mode in enumerate(modes) if mode != 0 for h in range(N_HEADS)]
        scores(items[0], 0)
        for n, item in enumerate(items):
            if n + 1 < len(items):
                scores(items[n + 1], (n + 1) % 2)
            softmax_pv(item, n % 2)

    for code, modes in combos:
        @pl.when((flags >> 2) == code)
        def _(modes=modes):
            run(modes)

    @pl.when((flags & 2) != 0)
    def _():
        for h in range(N_HEADS):
            acc = acc_scr[h]
            o_ref[0, :, h * V_HEAD:(h + 1) * V_HEAD] = (acc[:, 0:V_HEAD] / acc[:, V_HEAD:V_HEAD + 1]).astype(BF16)


def _attn_tables(nq, nk, tq, tk, n_sub, q_pos0, n_valid):
    th = tq // n_sub
    qi_l, ki_l, fl_l, combos = [], [], [], {}
    for qi in range(nq):
        sub_lo = [q_pos0 + qi * tq + j * th for j in range(n_sub)]
        sub_last = []
        for lo in sub_lo:
            last_pos = ((lo + th - 1) // CHUNK) * CHUNK + CHUNK - 1
            if n_valid is not None:
                last_pos = min(last_pos, n_valid - 1)
            sub_last.append(min(nk - 1, last_pos // tk))
        k_last = max(sub_last)
        for ki in range(k_last + 1):
            k_hi = ki * tk + tk - 1
            modes = []
            for lo, last in zip(sub_lo, sub_last):
                if ki > last:
                    modes.append(0)
                elif (k_hi // CHUNK) > (lo // CHUNK) or (n_valid is not None and k_hi >= n_valid):
                    modes.append(2)
                else:
                    modes.append(1)
            code = sum(m * 3 ** j for j, m in enumerate(modes))
            combos[code] = tuple(modes)
            qi_l.append(qi); ki_l.append(ki)
            fl_l.append((1 if ki == 0 else 0) | (2 if ki == k_last else 0) | (code << 2))
    to_arr = lambda vals: jnp.asarray(np.array(vals, np.int32))
    return to_arr(qi_l), to_arr(ki_l), to_arr(fl_l), tuple(sorted(combos.items()))


def _attention(q, k, v, q_pos0, n_valid):
    b, sq, _ = q.shape
    sk = k.shape[1]
    tq = _pick_tile(sq, ATTN_TQ)
    tk = _pick_tile(sk, ATTN_TK)
    n_sub = ATTN_SUB if tq % (ATTN_SUB * 2 * SUBLANES) == 0 else 1
    th = tq // n_sub
    qi_t, ki_t, fl_t, combos = _attn_tables(sq // tq, sk // tk, tq, tk, n_sub, q_pos0, n_valid)
    n_steps = int(qi_t.shape[0])
    grid_spec = pltpu.PrefetchScalarGridSpec(
        num_scalar_prefetch=3,
        grid=(b, n_steps),
        in_specs=[
            pl.BlockSpec((1, tq, N_HEADS * HEAD_PAD), lambda bi, si, qt, kt, ft: (bi, qt[si], 0)),
            pl.BlockSpec((1, tk, N_HEADS * HEAD_PAD), lambda bi, si, qt, kt, ft: (bi, kt[si], 0)),
            pl.BlockSpec((1, tk, N_HEADS * HEAD_PAD), lambda bi, si, qt, kt, ft: (bi, kt[si], 0)),
        ],
        out_specs=pl.BlockSpec((1, tq, N_HEADS * V_HEAD), lambda bi, si, qt, kt, ft: (bi, qt[si], 0)),
        scratch_shapes=[
            pltpu.VMEM((N_HEADS, tq, LANES), F32),
            pltpu.VMEM((N_HEADS, tq, LANES), F32),
            pltpu.VMEM((2, th, tk), F32),
            pltpu.VMEM((2, th, tk), BF16),
            pltpu.VMEM((2, th, LANES), F32),
        ],
    )
    return pl.pallas_call(
        functools.partial(_attn_body, tq=tq, tk=tk, n_sub=n_sub, rs=min(th, ATTN_ROW_SLAB), q_pos0=q_pos0,
                          n_valid=n_valid, combos=combos),
        grid_spec=grid_spec,
        out_shape=jax.ShapeDtypeStruct((b, sq, N_HEADS * V_HEAD), BF16),
        compiler_params=_cp(("parallel", "arbitrary")),
        name="mla_attn",
    )(qi_t, ki_t, fl_t, q, k, v)


def _decode_attn_body(q_ref, lat_ref, krp_ref, wukv_ref, o_ref, qa_scr, qr_scr, *, ss, n_valid):
    hw = N_HEADS * HEAD_PAD
    latb = lat_ref[0].astype(BF16)
    krb = krp_ref[0].astype(BF16)
    contract_last = (((1,), (1,)), ((), ()))
    for h in range(N_HEADS):
        hs = slice(h * HEAD_PAD, (h + 1) * HEAD_PAD)
        rows = slice(h * ss, (h + 1) * ss)
        q_h = q_ref[0, :, hs]
        qa_scr[rows, :] = lax.dot_general(q_h, wukv_ref[:, hs], contract_last,
                                          preferred_element_type=F32).astype(BF16)
        qr_scr[rows, :] = q_h
    s = (lax.dot_general(qa_scr[...], latb, contract_last, preferred_element_type=F32)
         + lax.dot_general(qr_scr[...], krb, contract_last, preferred_element_type=F32))
    kpos = lax.broadcasted_iota(I32, s.shape, 1)
    s = jnp.where(kpos < n_valid, s, -jnp.inf)
    p = jnp.exp2(s - jnp.max(s, -1, keepdims=True))
    ol = jnp.dot(p.astype(BF16), latb, preferred_element_type=F32) / jnp.sum(p, -1, keepdims=True)
    for h in range(N_HEADS):
        rows = slice(h * ss, (h + 1) * ss)
        o_h = jnp.dot(ol[rows, :].astype(BF16), wukv_ref[:, hw + h * HEAD_PAD:hw + (h + 1) * HEAD_PAD],
                      preferred_element_type=F32)
        o_ref[0, :, h * V_HEAD:(h + 1) * V_HEAD] = o_h[:, 0:V_HEAD].astype(BF16)


def _decode_attention(q, lat_all, kr_padded, w_ukv, n_valid):
    b, ss, _ = q.shape
    sk = lat_all.shape[1]
    return pl.pallas_call(
        functools.partial(_decode_attn_body, ss=ss, n_valid=n_valid),
        grid=(b,),
        in_specs=[pl.BlockSpec((1, ss, N_HEADS * HEAD_PAD), lambda bi: (bi, 0, 0)),
                  pl.BlockSpec((1, sk, KV_LORA), lambda bi: (bi, 0, 0)),
                  pl.BlockSpec((1, sk, LANES), lambda bi: (bi, 0, 0)),
                  pl.BlockSpec(w_ukv.shape, lambda bi: (0, 0))],
        out_specs=pl.BlockSpec((1, ss, N_HEADS * V_HEAD), lambda bi: (bi, 0, 0)),
        out_shape=jax.ShapeDtypeStruct((b, ss, N_HEADS * V_HEAD), BF16),
        scratch_shapes=[pltpu.VMEM((N_HEADS * ss, KV_LORA), BF16), pltpu.VMEM((N_HEADS * ss, HEAD_PAD), BF16)],
        compiler_params=_cp(("parallel",)),
        name="decode_attn",
    )(q, lat_all, kr_padded, w_ukv)


def _memkv_body(mem_ref, wk_ref, wv_ref, mk_ref, mv_ref, mkb_ref, mvb_ref):
    m = mem_ref[...].astype(BF16)
    mk = jnp.dot(m, wk_ref[...], preferred_element_type=F32)
    mv = jnp.dot(m, wv_ref[...], preferred_element_type=F32)
    mk_ref[...] = mk
    mv_ref[...] = mv
    mkb_ref[...] = mk.astype(BF16)
    mvb_ref[...] = mv.astype(BF16)


def _memkv(mem2d, w_xk, w_xv):
    n = mem2d.shape[0]
    tt = _pick_tile(n, 256)
    row = pl.BlockSpec((tt, D_MODEL), lambda i: (i, 0))
    wspec = pl.BlockSpec((D_MODEL, D_MODEL), lambda i: (0, 0))
    return pl.pallas_call(
        _memkv_body,
        grid=(n // tt,),
        in_specs=[row, wspec, wspec],
        out_specs=(row, row, row, row),
        out_shape=(jax.ShapeDtypeStruct((n, D_MODEL), F32), jax.ShapeDtypeStruct((n, D_MODEL), F32),
                   jax.ShapeDtypeStruct((n, D_MODEL), BF16), jax.ShapeDtypeStruct((n, D_MODEL), BF16)),
        compiler_params=_cp(("parallel",)),
        name="memkv",
    )(mem2d, w_xk, w_xv)


def _layer_norm(x, g, b):
    mu = jnp.mean(x, -1, keepdims=True)
    xc = x - mu
    var = jnp.mean(xc * xc, -1, keepdims=True)
    return xc * lax.rsqrt(var + LN_EPS) * g + b


def _mid_body(x_ref, at_ref, yc_ref, mk_ref, mv_ref, cnt0_ref, low_ref, wo_ref, g1_ref, b1_ref, wq_ref, wxo_ref,
              g2_ref, b2_ref, wr_ref, br_ref, x2_ref, ri_ref, cnt_ref, rt_ref,
              cnt_scr, a_scr, xb_scr, q_scr, sc_scr, p_scr, o_scr, lg_scr, *, tt, alpha):
    step = pl.program_id(0)

    @pl.when(step == 0)
    def _():
        cnt_scr[...] = cnt0_ref[...]
        lg_scr[...] = jnp.zeros(lg_scr.shape, F32)

    cur_slot = step % 2
    prev_slot = 1 - cur_slot
    routed = (step > 0).astype(F32)

    n_part = MID_PARTS if tt % (MID_PARTS * SUBLANES * 2) == 0 else 1
    pr = tt // n_part
    ln_rs = min(pr, LN_ROW_SLAB)
    sm_rs = min(pr, XATTN_ROW_SLAB)
    c_exp = X_SCALE * math.log2(math.e)
    lane = lax.broadcasted_iota(I32, (pr, LANES), 1)

    def part_rows(k):
        return slice(k * pr, (k + 1) * pr)

    def out_proj(k):
        rp = part_rows(k)
        mix = jnp.concatenate([at_ref[0, rp, :], yc_ref[0, rp, :]], axis=-1)
        a_scr[rp, :] = jnp.dot(mix, wo_ref[...], preferred_element_type=F32)

    def norm1(k):
        for r in range(pr // ln_rs):
            rows = slice(k * pr + r * ln_rs, k * pr + (r + 1) * ln_rs)
            x1 = _layer_norm(alpha * x_ref[0, rows, :] + a_scr[rows, :], g1_ref[...], b1_ref[...])
            x2_ref[0, rows, :] = x1
            xb_scr[rows, :] = x1.astype(BF16)

    def q_proj(k):
        rp = part_rows(k)
        q_scr[rp, :] = jnp.dot(xb_scr[rp, :], wq_ref[...], preferred_element_type=F32).astype(BF16)

    def cross_attn(k):
        rp = part_rows(k)
        for h in range(X_HEADS):
            sl = slice(h * X_HEAD_DIM, (h + 1) * X_HEAD_DIM)
            b2 = h % 2
            sc_scr[k, b2] = lax.dot_general(q_scr[rp, sl], mk_ref[0, :, sl], (((1,), (1,)), ((), ())),
                                            preferred_element_type=F32)
            for r in range(pr // sm_rs):
                rows = slice(r * sm_rs, (r + 1) * sm_rs)
                s_r = sc_scr[k, b2, rows, :]
                e = jnp.exp2((s_r - jnp.max(s_r, -1, keepdims=True)) * c_exp)
                p_scr[k, b2, rows, :] = (e / jnp.sum(e, -1, keepdims=True)).astype(BF16)
            o_scr[rp, sl] = jnp.dot(p_scr[k, b2], mv_ref[0, :, sl], preferred_element_type=F32).astype(BF16)

    def x_out_proj(k):
        rp = part_rows(k)
        a_scr[rp, :] = jnp.dot(o_scr[rp, :], wxo_ref[...], preferred_element_type=F32)

    def norm2(k):
        for r in range(pr // ln_rs):
            rows = slice(k * pr + r * ln_rs, k * pr + (r + 1) * ln_rs)
            x2 = _layer_norm(alpha * x2_ref[0, rows, :] + a_scr[rows, :], g2_ref[...], b2_ref[...])
            x2_ref[0, rows, :] = x2
            xb_scr[rows, :] = x2.astype(BF16)

    def router_logits(k):
        rp = part_rows(k)
        lg_scr[cur_slot, rp, :] = jnp.dot(xb_scr[rp, :], wr_ref[...], preferred_element_type=F32) + br_ref[...]

    def route(k):
        rp = part_rows(k)
        logits = lg_scr[prev_slot, rp, :]
        neg = -jnp.inf
        is_g = lane < N_GROUPS
        lg = jnp.where(is_g, logits, neg)
        mg = jnp.max(lg, -1, keepdims=True)
        g_idx = jnp.min(jnp.where(lg == mg, lane, LANES), -1, keepdims=True)
        pg = 1.0 / jnp.sum(jnp.where(is_g, jnp.exp(logits - mg), 0.0), -1, keepdims=True)
        in_grp = ((lane >= ROUTER_LANE0) & (lane < ROUTER_LANE0 + N_EXPERTS)
                  & (((lane - ROUTER_LANE0) >> 3) == g_idx))
        le = jnp.where(in_grp, logits, neg)
        v1 = jnp.max(le, -1, keepdims=True)
        i1 = jnp.min(jnp.where(le == v1, lane, LANES), -1, keepdims=True)
        le2 = jnp.where(lane == i1, neg, le)
        v2 = jnp.max(le2, -1, keepdims=True)
        i2 = jnp.min(jnp.where(le2 == v2, lane, LANES), -1, keepdims=True)
        e2 = jnp.exp(v2 - v1)
        den = 1.0 + e2
        gate1 = (1.0 / den) * pg
        gate2 = (e2 / den) * pg
        oh1 = (lane == i1).astype(F32)
        oh2 = (lane == i2).astype(F32)
        oh = oh1 + oh2
        base = cnt_scr[...] + jnp.dot(low_ref[...], oh.astype(BF16), preferred_element_type=F32)
        rank1 = jnp.sum(oh1 * base, -1, keepdims=True)
        rank2 = jnp.sum(oh2 * base, -1, keepdims=True)
        cnt_scr[...] = cnt_scr[...] + jnp.sum(oh, 0, keepdims=True) * routed
        e1f = (i1 - ROUTER_LANE0).astype(F32)
        e2f = (i2 - ROUTER_LANE0).astype(F32)
        ri = jnp.where(lane == 0, e1f, jnp.where(lane == 1, e2f, jnp.where(
            lane == 2, rank1, jnp.where(lane == 3, rank2, jnp.where(lane == 4, gate1, jnp.where(
                lane == 5, gate2, 0.0))))))
        ri_ref[0, rp, :] = ri
        ri_t = jnp.transpose(ri)[0:SUBLANES, :]
        grp = min(pr, ROUTE_GROUP)
        for j in range(pr // grp):
            rt_ref[0, k * (pr // grp) + j] = ri_t[:, j * grp:(j + 1) * grp]

    stages = (out_proj, norm1, q_proj, cross_attn, x_out_proj, norm2, router_logits)
    matmul_stages = (out_proj, q_proj, x_out_proj, router_logits)
    for t in range(len(stages) + MID_SKEW * (n_part - 1)):
        todo = [(stages[t - MID_SKEW * k], k) for k in range(n_part) if 0 <= t - MID_SKEW * k < len(stages)]
        for fn, k in sorted(todo, key=lambda fk: fk[0] not in matmul_stages):
            fn(k)
        if t < n_part:
            route(t)
    cnt_ref[...] = cnt_scr[...]


def _mid(x, attn, yconv, mk_b, mv_b, cnt0, w, alpha):
    b, s, _ = x.shape
    tt = _pick_tile(s, MID_ROWS)
    nj = s // tt
    n_tiles = b * nj
    cur = lambda i: jnp.minimum(i, n_tiles - 1)
    prev = lambda i: jnp.maximum(i - 1, 0)
    row = lambda width: pl.BlockSpec((1, tt, width), lambda i: (cur(i) // nj, cur(i) % nj, 0))
    row_prev = lambda width: pl.BlockSpec((1, tt, width), lambda i: (prev(i) // nj, prev(i) % nj, 0))
    full = lambda shape: pl.BlockSpec(shape, lambda i: (0,) * len(shape))
    mem = pl.BlockSpec((1, N_MEM, D_MODEL), lambda i: (cur(i) // nj, 0, 0))
    vec = full((1, D_MODEL))
    n_part = MID_PARTS if tt % (MID_PARTS * SUBLANES * 2) == 0 else 1
    pr = tt // n_part
    grp = min(pr, ROUTE_GROUP)
    lower = jnp.tril(jnp.ones((pr, pr), F32), -1).astype(BF16)
    return pl.pallas_call(
        functools.partial(_mid_body, tt=tt, alpha=alpha),
        grid=(n_tiles + 1,),
        in_specs=[row(D_MODEL), row(N_HEADS * V_HEAD), row(CONV_WIDTH), mem, mem, full((1, LANES)), full((pr, pr)),
                  full((D_MODEL, D_MODEL)), vec, vec, full((D_MODEL, D_MODEL)), full((D_MODEL, D_MODEL)),
                  vec, vec, full((D_MODEL, LANES)), full((1, LANES))],
        out_specs=(row(D_MODEL), row_prev(LANES), full((1, LANES)),
                   pl.BlockSpec((1, tt // grp, SUBLANES, grp), lambda i: (prev(i) // nj, prev(i) % nj, 0, 0))),
        out_shape=(jax.ShapeDtypeStruct((b, s, D_MODEL), F32), jax.ShapeDtypeStruct((b, s, LANES), F32),
                   jax.ShapeDtypeStruct((1, LANES), F32),
                   jax.ShapeDtypeStruct((b, s // grp, SUBLANES, grp), F32)),
        scratch_shapes=[
            pltpu.VMEM((1, LANES), F32),
            pltpu.VMEM((tt, D_MODEL), F32),
            pltpu.VMEM((tt, D_MODEL), BF16),
            pltpu.VMEM((tt, D_MODEL), BF16),
            pltpu.VMEM((n_part, 2, pr, N_MEM), F32),
            pltpu.VMEM((n_part, 2, pr, N_MEM), BF16),
            pltpu.VMEM((tt, D_MODEL), BF16),
            pltpu.VMEM((2, tt, LANES), F32),
        ],
        compiler_params=_cp(("arbitrary",)),
        name="mid",
    )(x, attn, yconv, mk_b, mv_b, cnt0, lower, w['w_out'], w['ln1_g'], w['ln1_b'], w['w_xq'], w['w_xo'],
      w['ln2_g'], w['ln2_b'], w['w_router'], w['b_router'])


def _row_slice(ref, row):
    return ref.at[pl.ds(pl.multiple_of(row * SUBLANES, SUBLANES), SUBLANES), :]


def _slot_index(t, kk, g, n):
    if g >= n:
        return kk * g + t
    shift = g.bit_length() - 1
    return ((t >> shift) << (shift + 1)) + kk * g + (t & (g - 1))


def _rows_slice(ref, row, n_rows):
    return ref.at[pl.ds(pl.multiple_of(row * SUBLANES, SUBLANES), n_rows * SUBLANES), :]


def _dispatch_body(pad_ref, dest_ref, dest_s_ref, x_ref, x_s_ref, xs_ref, buf0, buf1, zbuf, sem0, sem1, zsem,
                   *, tt, ts, nt, nblk, rows, g_p, g_s):
    i = pl.program_id(0)
    bufs = (buf0, buf1)
    sems = (sem0, sem1)
    len_bits = rows.bit_length() - 1

    def scatter_rows(buf, sem, src_ref, d_ref, n, g):
        for c in range(ROW_CHUNKS):
            buf[pl.ds(c, n, stride=SUBLANES), :] = src_ref[:, c * LANES:(c + 1) * LANES]
        unroll = min(n, GATHER_UNROLL)

        def issue(j, carry):
            for u in range(unroll):
                t = j * unroll + u
                src = _row_slice(buf, t)
                for kk in range(2):
                    d = d_ref[0, 0, _slot_index(t, kk, g, n)]
                    pltpu.make_async_copy(src, _row_slice(xs_ref, d), sem).start(priority=kk)
            return carry

        lax.fori_loop(0, n // unroll, issue, 0)

    def wait_rows(buf, sem, n):
        for _ in range(2):
            pltpu.make_async_copy(_rows_slice(buf, 0, n), _rows_slice(xs_ref, 0, n), sem).wait()

    def zero_fill(wait):
        def fire(copy):
            if wait:
                copy.wait()
            else:
                copy.start()

        def per_expert(e, carry):
            first = pad_ref[e]
            n_pad = pad_ref[N_EXPERTS + e]
            for bit in range(len_bits):
                size = 1 << bit
                off = (n_pad >> (bit + 1)) << (bit + 1)

                @pl.when(((n_pad >> bit) & 1) == 1)
                def _():
                    fire(pltpu.make_async_copy(_rows_slice(zbuf, 0, size), _rows_slice(xs_ref, first + off, size),
                                               zsem))
            return carry

        lax.fori_loop(0, N_EXPERTS, per_expert, 0)

        def per_block(j, carry):
            fire(pltpu.make_async_copy(zbuf, _rows_slice(xs_ref, j * rows, rows), zsem))
            return carry

        lax.fori_loop(pad_ref[2 * N_EXPERTS], nblk, per_block, 0)

    def run(slot):
        @pl.when(i < nt)
        def _():
            scatter_rows(bufs[slot], sems[slot], x_ref, dest_ref, tt, g_p)

        @pl.when(i == nt)
        def _():
            zbuf[...] = jnp.zeros(zbuf.shape, F32)
            scatter_rows(bufs[slot], sems[slot], x_s_ref, dest_s_ref, ts, g_s)
            zero_fill(False)

        @pl.when(i > 0)
        def _():
            wait_rows(bufs[1 - slot], sems[1 - slot], tt)

        @pl.when(i == nt)
        def _():
            wait_rows(bufs[slot], sems[slot], ts)
            zero_fill(True)

    @pl.when(i % 2 == 0)
    def _():
        run(0)

    @pl.when(i % 2 == 1)
    def _():
        run(1)


def _dispatch(x_p, dest_p, g_p, x_s, dest_s, g_s, pad_info, nblk, rows):
    n_p, n_s = x_p.shape[0], x_s.shape[0]
    tt = _pick_tile(n_p, 512)
    nt = n_p // tt
    assert n_s <= tt
    last = lambda i, pad: jnp.minimum(i, nt - 1)
    grid_spec = pltpu.PrefetchScalarGridSpec(
        num_scalar_prefetch=1,
        grid=(nt + 1,),
        in_specs=[
            pl.BlockSpec((1, 1, 2 * tt), lambda i, pad: (last(i, pad), 0, 0), memory_space=pltpu.SMEM),
            pl.BlockSpec((1, 1, 2 * n_s), lambda i, pad: (0, 0, 0), memory_space=pltpu.SMEM),
            pl.BlockSpec((tt, D_MODEL), lambda i, pad: (last(i, pad), 0)),
            pl.BlockSpec((n_s, D_MODEL), lambda i, pad: (0, 0)),
        ],
        out_specs=pl.BlockSpec(memory_space=pl.ANY),
        scratch_shapes=[pltpu.VMEM((tt * SUBLANES, LANES), F32), pltpu.VMEM((tt * SUBLANES, LANES), F32),
                        pltpu.VMEM((rows * SUBLANES, LANES), F32),
                        pltpu.SemaphoreType.DMA, pltpu.SemaphoreType.DMA, pltpu.SemaphoreType.DMA],
    )
    return pl.pallas_call(
        functools.partial(_dispatch_body, tt=tt, ts=n_s, nt=nt, nblk=nblk, rows=rows, g_p=g_p, g_s=g_s),
        grid_spec=grid_spec,
        out_shape=jax.ShapeDtypeStruct((nblk * rows * SUBLANES, LANES), F32),
        compiler_params=_cp(("arbitrary",)),
        name="moe_dispatch",
    )(pad_info, dest_p.reshape(nt, 1, 2 * tt), dest_s.reshape(1, 1, 2 * n_s), x_p, x_s)


def _experts_body(be_ref, nu_ref, x_ref, wg_ref, wu_ref, wd_ref, y_ref, xb_scr, g_scr, u_scr, h_scr,
                  wgb_scr, wub_scr, wdb_scr, *, rows):
    i = pl.program_id(0)
    cur = jnp.minimum(i, nu_ref[0] - 1)
    new_expert = (i == 0) | (be_ref[cur] != be_ref[jnp.maximum(cur - 1, 0)])

    @pl.when((i < nu_ref[0]) & new_expert)
    def _():
        for r in range(0, D_MODEL, WCAST_ROWS):
            wgb_scr[r:r + WCAST_ROWS, :] = wg_ref[0, r:r + WCAST_ROWS, :].astype(BF16)
            wub_scr[r:r + WCAST_ROWS, :] = wu_ref[0, r:r + WCAST_ROWS, :].astype(BF16)
        for r in range(0, D_EXPERT, WCAST_ROWS // 2):
            wdb_scr[r:r + WCAST_ROWS // 2, :] = wd_ref[0, r:r + WCAST_ROWS // 2, :].astype(BF16)
    n_part = MOE_PARTS
    pr = rows // n_part
    act_rs = min(pr, ACT_ROW_SLAB)

    def load(k):
        base = k * pr * SUBLANES
        xb_scr[k] = jnp.concatenate([x_ref[pl.ds(base + c, pr, stride=SUBLANES), :] for c in range(ROW_CHUNKS)],
                                    axis=-1).astype(BF16)

    def gate_up(k):
        g_scr[k] = jnp.dot(xb_scr[k], wgb_scr[...], preferred_element_type=F32)
        u_scr[k] = jnp.dot(xb_scr[k], wub_scr[...], preferred_element_type=F32)

    def act(k):
        for r in range(pr // act_rs):
            rows_r = slice(r * act_rs, (r + 1) * act_rs)
            g = g_scr[k, rows_r, :]
            h_scr[k, rows_r, :] = ((g * jax.nn.sigmoid(g)) * u_scr[k, rows_r, :]).astype(BF16)

    def down(k):
        y = jnp.dot(h_scr[k], wdb_scr[...], preferred_element_type=F32)
        base = k * pr * SUBLANES
        for c in range(ROW_CHUNKS):
            y_ref[pl.ds(base + c, pr, stride=SUBLANES), :] = y[:, c * LANES:(c + 1) * LANES]

    @pl.when(i < nu_ref[0])
    def _():
        stages = (load, gate_up, act, down)
        matmul_stages = (gate_up, down)
        for t in range(len(stages) + n_part - 1):
            todo = [(stages[t - k], k) for k in range(n_part) if 0 <= t - k < len(stages)]
            for fn, k in sorted(todo, key=lambda fk: fk[0] not in matmul_stages):
                fn(k)

    @pl.when(i >= nu_ref[0])
    def _():
        y_ref[...] = jnp.zeros(y_ref.shape, F32)


def _experts(xs, block_e, n_used, wg, wu, wd, rows):
    nblk = xs.shape[0] // (rows * SUBLANES)
    clamp = lambda i, nu: jnp.minimum(i, nu[0] - 1)
    grid_spec = pltpu.PrefetchScalarGridSpec(
        num_scalar_prefetch=2,
        grid=(nblk,),
        in_specs=[
            pl.BlockSpec((rows * SUBLANES, LANES), lambda i, be, nu: (clamp(i, nu), 0)),
            pl.BlockSpec((1, D_MODEL, D_EXPERT), lambda i, be, nu: (be[clamp(i, nu)], 0, 0)),
            pl.BlockSpec((1, D_MODEL, D_EXPERT), lambda i, be, nu: (be[clamp(i, nu)], 0, 0)),
            pl.BlockSpec((1, D_EXPERT, D_MODEL), lambda i, be, nu: (be[clamp(i, nu)], 0, 0)),
        ],
        out_specs=pl.BlockSpec((rows * SUBLANES, LANES), lambda i, be, nu: (i, 0)),
        scratch_shapes=[
            pltpu.VMEM((MOE_PARTS, rows // MOE_PARTS, D_MODEL), BF16),
            pltpu.VMEM((MOE_PARTS, rows // MOE_PARTS, D_EXPERT), F32),
            pltpu.VMEM((MOE_PARTS, rows // MOE_PARTS, D_EXPERT), F32),
            pltpu.VMEM((MOE_PARTS, rows // MOE_PARTS, D_EXPERT), BF16),
            pltpu.VMEM((D_MODEL, D_EXPERT), BF16),
            pltpu.VMEM((D_MODEL, D_EXPERT), BF16),
            pltpu.VMEM((D_EXPERT, D_MODEL), BF16),
        ],
    )
    return pl.pallas_call(
        functools.partial(_experts_body, rows=rows),
        grid_spec=grid_spec,
        out_shape=jax.ShapeDtypeStruct(xs.shape, F32),
        compiler_params=_cp(("arbitrary",)),
        name="moe_experts",
    )(block_e, n_used, xs, wg, wu, wd)


def _combine_body(dest_ref, destn_ref, x_ref, ri_ref, g3_ref, b3_ref, ys_ref, o_ref,
                  b00, b01, b10, b11, sem0, sem1, *, tt, nt, g, alpha):
    i = pl.program_id(0)
    bufs = ((b00, b01), (b10, b11))
    sems = (sem0, sem1)
    rs = min(tt, LN_ROW_SLAB)
    unroll = min(tt, GATHER_UNROLL)

    def gather_rows(dref, slot):
        def copy(t, kk):
            d = dref[0, 0, _slot_index(t, kk, g, tt)]
            return pltpu.make_async_copy(_row_slice(ys_ref, d), _row_slice(bufs[slot][kk], t), sems[slot])

        def body(j, carry):
            for u in range(unroll):
                for kk in range(2):
                    copy(j * unroll + u, kk).start(priority=kk)
            return carry

        lax.fori_loop(0, tt // unroll, body, 0)

    def wait_rows(slot):
        for kk in range(2):
            pltpu.make_async_copy(ys_ref.at[pl.ds(0, tt * SUBLANES), :], bufs[slot][kk], sems[slot]).wait()

    def run(slot):
        if slot == 0:
            @pl.when(i == 0)
            def _():
                gather_rows(dest_ref, 0)

        @pl.when(i + 1 < nt)
        def _():
            gather_rows(destn_ref, 1 - slot)

        wait_rows(slot)
        for r in range(tt // rs):
            rows = slice(r * rs, (r + 1) * rs)
            base = r * rs * SUBLANES
            y0 = jnp.concatenate([bufs[slot][0][pl.ds(base + c, rs, stride=SUBLANES), :]
                                  for c in range(ROW_CHUNKS)], axis=-1)
            y1 = jnp.concatenate([bufs[slot][1][pl.ds(base + c, rs, stride=SUBLANES), :]
                                  for c in range(ROW_CHUNKS)], axis=-1)
            ri = ri_ref[rows, :]
            moe = y0 * ri[:, 4:5] + y1 * ri[:, 5:6]
            o_ref[rows, :] = _layer_norm(alpha * x_ref[rows, :] + moe, g3_ref[...], b3_ref[...])

    @pl.when(i % 2 == 0)
    def _():
        run(0)

    @pl.when(i % 2 == 1)
    def _():
        run(1)


def _combine(x2d, rinfo, dest, g, ys, g3, b3, alpha):
    n = x2d.shape[0]
    tt = _pick_tile(n, 512)
    nt = n // tt
    dest3 = dest.reshape(nt, 1, 2 * tt)
    vec = pl.BlockSpec((1, D_MODEL), lambda i: (0, 0))
    stage = pltpu.VMEM((tt * SUBLANES, LANES), F32)
    return pl.pallas_call(
        functools.partial(_combine_body, tt=tt, nt=nt, g=g, alpha=alpha),
        grid=(nt,),
        in_specs=[
            pl.BlockSpec((1, 1, 2 * tt), lambda i: (i, 0, 0), memory_space=pltpu.SMEM),
            pl.BlockSpec((1, 1, 2 * tt), lambda i: (jnp.minimum(i + 1, nt - 1), 0, 0), memory_space=pltpu.SMEM),
            pl.BlockSpec((tt, D_MODEL), lambda i: (i, 0)),
            pl.BlockSpec((tt, LANES), lambda i: (i, 0)),
            vec, vec,
            pl.BlockSpec(memory_space=pl.ANY),
        ],
        out_specs=pl.BlockSpec((tt, D_MODEL), lambda i: (i, 0)),
        out_shape=jax.ShapeDtypeStruct((n, D_MODEL), F32),
        scratch_shapes=[stage, stage, stage, stage, pltpu.SemaphoreType.DMA, pltpu.SemaphoreType.DMA],
        compiler_params=_cp(("arbitrary",)),
        name="moe_combine",
    )(dest3, dest3, x2d, rinfo, g3, b3, ys)


def _rope_tables(pos):
    half = QK_ROPE // 2
    inv = ROPE_THETA ** (-jnp.arange(half, dtype=F32) / half)
    ang = pos.astype(F32)[:, None] * inv[None, :]
    cos, sin = jnp.cos(ang), jnp.sin(ang)
    n = pos.shape[0]
    pad_r = LANES - ROPE_LANE0 - QK_ROPE
    cos_t = jnp.concatenate([jnp.ones((n, ROPE_LANE0), F32), cos, cos, jnp.zeros((n, pad_r), F32)], -1)
    sin_t = jnp.concatenate([jnp.zeros((n, ROPE_LANE0), F32), sin, sin, jnp.zeros((n, pad_r), F32)], -1)
    return cos_t, sin_t


def _swap_neg(wr):
    half = QK_ROPE // 2
    return jnp.concatenate([-wr[:, half:], wr[:, :half]], axis=1)


def _prep_weights(l, w_in, q_norm_g, kv_norm_g, w_uq, w_ukv, conv_w, w_out, ln1_g, ln1_b, w_xq, w_xk, w_xv, w_xo,
                  ln2_g, ln2_b, w_router_group, b_router_group, w_router_expert, b_router_expert,
                  w_exp_gate, w_exp_up, w_exp_down, ln3_g, ln3_b):
    wi = w_in[l]
    c0 = Q_LORA + KV_LORA
    w_kr = wi[:, c0:c0 + QK_ROPE]
    zl = jnp.zeros((D_MODEL, ROPE_LANE0), F32)
    w_in_p = jnp.concatenate([wi[:, :c0], wi[:, c0 + QK_ROPE:], zl, w_kr, _swap_neg(w_kr)], axis=1)
    wq = w_uq[l].reshape(Q_LORA, N_HEADS, QK_NOPE + QK_ROPE)
    wq_rot = jnp.concatenate([-wq[:, :, QK_NOPE + QK_ROPE // 2:], wq[:, :, QK_NOPE:QK_NOPE + QK_ROPE // 2]], axis=2)
    wq_a = jnp.concatenate([wq, wq_rot], axis=2).reshape(Q_LORA, N_HEADS * HEAD_PAD)
    wkv = w_ukv[l].reshape(KV_LORA, N_HEADS, QK_NOPE + V_HEAD)
    wk_p = jnp.concatenate([wkv[:, :, :QK_NOPE], jnp.zeros((KV_LORA, N_HEADS, HEAD_PAD - QK_NOPE), F32)], axis=2)
    wk_p = wk_p.reshape(KV_LORA, N_HEADS * HEAD_PAD)
    wv_p = jnp.concatenate([wkv[:, :, QK_NOPE:], jnp.zeros((KV_LORA, N_HEADS, HEAD_PAD - V_HEAD), F32)], axis=2)
    wv_p = wv_p.reshape(KV_LORA, N_HEADS * HEAD_PAD)
    w_router = jnp.concatenate([w_router_group[l], w_router_expert[l],
                                jnp.zeros((D_MODEL, LANES - N_GROUPS - N_EXPERTS), F32)], axis=1)
    b_router = jnp.concatenate([b_router_group[l], b_router_expert[l].reshape(-1),
                                jnp.zeros((LANES - N_GROUPS - N_EXPERTS,), F32)]).reshape(1, LANES)
    return dict(
        w_in=w_in_p.astype(BF16),
        q_norm_g=q_norm_g[l].reshape(1, Q_LORA), kv_norm_g=kv_norm_g[l].reshape(1, KV_LORA),
        w_uq=wq_a.astype(BF16),
        w_ukv=jnp.concatenate([wk_p, wv_p], axis=1).astype(BF16),
        conv_w=conv_w[l],
        w_out=w_out[l].astype(BF16), ln1_g=ln1_g[l].reshape(1, -1), ln1_b=ln1_b[l].reshape(1, -1),
        w_xq=w_xq[l].astype(BF16), w_xk=w_xk[l].astype(BF16), w_xv=w_xv[l].astype(BF16),
        w_xo=w_xo[l].astype(BF16), ln2_g=ln2_g[l].reshape(1, -1), ln2_b=ln2_b[l].reshape(1, -1),
        w_router=w_router.astype(BF16), b_router=b_router,
        w_exp_gate=w_exp_gate[l], w_exp_up=w_exp_up[l], w_exp_down=w_exp_down[l],
        ln3_g=ln3_g[l].reshape(1, -1), ln3_b=ln3_b[l].reshape(1, -1),
    )


def _slots(rt, pstarts):
    g = rt.shape[-1]
    rt = rt.reshape(-1, SUBLANES, g)
    dest = pstarts[rt[:, 0:2, :].astype(I32)] + rt[:, 2:4, :].astype(I32)
    return dest.reshape(-1), g


def _layer(l, depth, xp, xs, lat_past, kr_past, conv_past, mk_s, mv_s, mem_prompt, w):
    alpha = (2 * depth) ** 0.25
    b, s, _ = xp.shape
    bs, ss, _ = xs.shape
    past = lat_past.shape[1]

    cos_p, sin_p = _rope_tables(jnp.arange(s))
    q_p, k_p, v_p, yc_p, lat_p, kr_p, cst_p = _inproj(
        xp, jnp.zeros((b, CONV_K - 1, CONV_WIDTH), F32), cos_p, sin_p, w)
    attn_p = _attention(q_p, k_p, v_p, 0, None)
    mk, mv, mk_b, mv_b = _memkv(mem_prompt.reshape(b * N_MEM, D_MODEL), w['w_xk'], w['w_xv'])
    cnt0 = jnp.zeros((1, LANES), F32)
    x2_p, ri_p, cnt_p, rt_p = _mid(xp, attn_p, yc_p, mk_b.reshape(b, N_MEM, D_MODEL), mv_b.reshape(b, N_MEM, D_MODEL),
                             cnt0, w, alpha)

    cos_s, sin_s = _rope_tables(past + jnp.arange(ss))
    q_s, _, _, yc_s, lat_s, kr_s, cst_s = _inproj(xs, conv_past, cos_s, sin_s, w)
    n_keys = past + ss
    sk = -(-n_keys // LANES) * LANES
    lat_all = jnp.concatenate([lat_past, lat_s, jnp.zeros((bs, sk - n_keys, KV_LORA), F32)], axis=1)
    kr_all = jnp.concatenate([kr_past, kr_s, jnp.zeros((bs, sk - n_keys, QK_ROPE), F32)], axis=1)
    kr_all = jnp.pad(kr_all, ((0, 0), (0, 0), (ROPE_LANE0, LANES - ROPE_LANE0 - QK_ROPE)))
    attn_s = _decode_attention(q_s, lat_all, kr_all, w['w_ukv'], n_keys)
    mk_sb = mk_s.reshape(bs, N_MEM, D_MODEL).astype(BF16)
    mv_sb = mv_s.reshape(bs, N_MEM, D_MODEL).astype(BF16)
    x2_s, ri_s, cnt, rt_s = _mid(xs, attn_s, yc_s, mk_sb, mv_sb, cnt_p, w, alpha)

    n_p, n_s = b * s, bs * ss
    counts = cnt[0, ROUTER_LANE0:ROUTER_LANE0 + N_EXPERTS].astype(I32)
    padded = (counts + MOE_ROWS - 1) // MOE_ROWS * MOE_ROWS
    pends = jnp.cumsum(padded)
    pstarts = pends - padded
    nblk = -(-2 * (n_p + n_s) // MOE_ROWS) + N_EXPERTS
    blk_start = jnp.arange(nblk, dtype=I32) * MOE_ROWS
    block_e = jnp.minimum(jnp.sum((pends[None, :] <= blk_start[:, None]).astype(I32), axis=1), N_EXPERTS - 1)
    n_used = (pends[-1] // MOE_ROWS).astype(I32).reshape(1)
    ri_p2, ri_s2 = ri_p.reshape(n_p, LANES), ri_s.reshape(n_s, LANES)
    dest_p, g_p = _slots(rt_p, pstarts)
    dest_s, g_s = _slots(rt_s, pstarts)
    x2_p2, x2_s2 = x2_p.reshape(n_p, D_MODEL), x2_s.reshape(n_s, D_MODEL)
    pad_info = jnp.concatenate([pstarts + counts, padded - counts, n_used]).astype(I32)
    slots = _dispatch(x2_p2, dest_p, g_p, x2_s2, dest_s, g_s, pad_info, nblk, MOE_ROWS)
    ys = _experts(slots, block_e, n_used, w['w_exp_gate'], w['w_exp_up'], w['w_exp_down'], MOE_ROWS)
    y_p = _combine(x2_p2, ri_p2, dest_p, g_p, ys, w['ln3_g'], w['ln3_b'], alpha).reshape(b, s, D_MODEL)
    y_s = _combine(x2_s2, ri_s2, dest_s, g_s, ys, w['ln3_g'], w['ln3_b'], alpha).reshape(bs, ss, D_MODEL)
    return (y_p, y_s, lat_p, kr_p, cst_p, mk.reshape(b, N_MEM, X_HEADS, X_HEAD_DIM),
            mv.reshape(b, N_MEM, X_HEADS, X_HEAD_DIM), lat_s, kr_s, cst_s)


def kernel(x_prompt, x_sample, cache_kv_latent, cache_k_rope, cache_conv, cache_mem_k, cache_mem_v, mem_prompt,
           w_in, q_norm_g, kv_norm_g, w_uq, w_ukv, conv_w, w_out, ln1_g, ln1_b, w_xq, w_xk, w_xv, w_xo, ln2_g,
           ln2_b, w_router_group, b_router_group, w_router_expert, b_router_expert, w_exp_gate, w_exp_up,
           w_exp_down, ln3_g, ln3_b):
    depth = w_in.shape[0]
    xp, xs = x_prompt, x_sample
    outs = [[] for _ in range(8)]
    for l in range(depth):
        w = _prep_weights(l, w_in, q_norm_g, kv_norm_g, w_uq, w_ukv, conv_w, w_out, ln1_g, ln1_b, w_xq, w_xk, w_xv,
                          w_xo, ln2_g, ln2_b, w_router_group, b_router_group, w_router_expert, b_router_expert,
                          w_exp_gate, w_exp_up, w_exp_down, ln3_g, ln3_b)
        res = _layer(l, depth, xp, xs, cache_kv_latent[l], cache_k_rope[l], cache_conv[l], cache_mem_k[l],
                     cache_mem_v[l], mem_prompt, w)
        xp, xs = res[0], res[1]
        for acc, r in zip(outs, res[2:]):
            acc.append(r)
    return (xp, xs) + tuple(jnp.stack(o) for o in outs)
```
